```python
import math
import jax, jax.numpy as jnp
from jax import lax
import numpy as np

D_MODEL = 2048
BATCH = 1
SEQ = 8192
DEPTH = 4

N_META = 16
CHUNK = 128
PAD = CHUNK - N_META
D_FF = 5632
EPS = 1e-6
ROPE_THETA = 10000.0
NEG_INF = -1e30

MLA_HEADS = 4
MLA_NOPE = 128
MLA_ROPE = 64
MLA_QK = MLA_NOPE + MLA_ROPE
MLA_V = 128
MLA_Q_LORA = 384
MLA_KV_LORA = 128
MLA_WIDTH = MLA_HEADS * MLA_V

SSD_HEADS = 8
SSD_HEAD_DIM = 64
SSD_WIDTH = SSD_HEADS * SSD_HEAD_DIM
SSD_GROUPS = 2
SSD_STATE = 128
SSD_CONV = 4
SSD_CONV_CH = SSD_WIDTH + 2 * SSD_GROUPS * SSD_STATE

RET_HEADS = 4
RET_DK = 64
RET_DV = 128
RET_WIDTH = RET_HEADS * RET_DV

RWKV_HEADS = 8
RWKV_HEAD_DIM = 64
RWKV_WIDTH = RWKV_HEADS * RWKV_HEAD_DIM
RWKV_DECAY_LORA = 32
RWKV_A_LORA = 32
RWKV_GATE_LORA = 64
RWKV_LN_EPS = 64e-5

MIX_WIDTH = MLA_WIDTH + SSD_WIDTH + RET_WIDTH + RWKV_WIDTH
MLA_IN = MLA_Q_LORA + MLA_KV_LORA + MLA_ROPE
SSD_IN = SSD_WIDTH + SSD_CONV_CH + SSD_HEADS
RET_IN = 2 * RET_HEADS * RET_DK + 2 * RET_WIDTH
RWKV_IN = 3 * RWKV_WIDTH + RWKV_DECAY_LORA + RWKV_A_LORA + RWKV_GATE_LORA
IN_WIDTH = MLA_IN + SSD_IN + RET_IN + RWKV_IN

kernel_name = 'hybrid_parallel_heads_mla_ssd_ret_rwkv7'


def rms_norm(x, g, eps=EPS):
    xf = x.astype(jnp.float32)
    y = xf * lax.rsqrt(jnp.mean(xf * xf, axis=-1, keepdims=True) + eps)
    return (y * g.astype(jnp.float32)).astype(x.dtype)


def layer_norm_heads(x, g, eps):
    xf = x.astype(jnp.float32)
    mu = jnp.mean(xf, axis=-1, keepdims=True)
    var = jnp.mean(jnp.square(xf - mu), axis=-1, keepdims=True)
    return ((xf - mu) * lax.rsqrt(var + eps) * g.astype(jnp.float32)).astype(x.dtype)


def swiglu(h, w_gate, w_up, w_down):
    return (jax.nn.silu(h @ w_gate) * (h @ w_up)) @ w_down


def rope(x, pos):
    half = x.shape[-1] // 2
    inv = ROPE_THETA ** (-jnp.arange(half, dtype=jnp.float32) / half)
    ang = pos[:, None] * inv[None, :]
    cos = jnp.cos(ang)[None, :, None, :]
    sin = jnp.sin(ang)[None, :, None, :]
    xf = x.astype(jnp.float32)
    x1, x2 = xf[..., :half], xf[..., half:]
    return jnp.concatenate([x1 * cos - x2 * sin, x1 * sin + x2 * cos], axis=-1).astype(x.dtype)


def causal_tri():
    return jnp.arange(CHUNK)[:, None] >= jnp.arange(CHUNK)[None, :]


def chunk_state_scan(decay, states):
    def step(s, inp):
        dcy, st = inp
        return dcy[:, :, None, None] * s + st, s
    s0 = jnp.zeros_like(states[:, 0])
    _, prev = lax.scan(step, s0, (jnp.moveaxis(decay, 1, 0), jnp.moveaxis(states, 1, 0)))
    return jnp.moveaxis(prev, 0, 1)


def mla_mixer(u, pos, valid, q_norm, w_q_up, kv_norm, w_kv_up, qk_norm_q, qk_norm_k):
    b, lp, _ = u.shape
    cq, ckv, k_pe = jnp.split(u, [MLA_Q_LORA, MLA_Q_LORA + MLA_KV_LORA], axis=-1)
    q = (rms_norm(cq, q_norm) @ w_q_up).reshape(b, lp, MLA_HEADS, MLA_QK)
    kv = (rms_norm(ckv, kv_norm) @ w_kv_up).reshape(b, lp, MLA_HEADS, MLA_NOPE + MLA_V)
    k_nope, v = kv[..., :MLA_NOPE], kv[..., MLA_NOPE:]
    k_pe = jnp.broadcast_to(k_pe[:, :, None, :], (b, lp, MLA_HEADS, MLA_ROPE))
    k = jnp.concatenate([k_nope, k_pe], axis=-1)
    q = rms_norm(q, qk_norm_q)
    k = rms_norm(k, qk_norm_k)
    q = jnp.concatenate([q[..., :MLA_NOPE], rope(q[..., MLA_NOPE:], pos)], axis=-1)
    k = jnp.concatenate([k[..., :MLA_NOPE], rope(k[..., MLA_NOPE:], pos)], axis=-1)
    nb = lp // CHUNK
    qb = jnp.moveaxis(q.reshape(b, nb, CHUNK, MLA_HEADS, MLA_QK), 1, 0)
    kpos = jnp.arange(lp)
    scale = MLA_QK ** -0.5

    def block(args):
        qi, i = args
        s = jnp.einsum('bqhd,bkhd->bhqk', qi, k, preferred_element_type=jnp.float32) * scale
        qpos = i * CHUNK + jnp.arange(CHUNK)
        mask = (kpos[None, :] <= qpos[:, None]) & valid[None, :]
        p = jax.nn.softmax(jnp.where(mask, s, NEG_INF), axis=-1).astype(v.dtype)
        return jnp.einsum('bhqk,bkhd->bqhd', p, v)

    o = lax.map(block, (qb, jnp.arange(nb)))
    return jnp.moveaxis(o, 0, 1).reshape(b, lp, MLA_WIDTH)


def ssd_chunked(x, log_a, bm, cm):
    b, lp, h, p = x.shape
    c = lp // CHUNK
    xf = x.astype(jnp.float32).reshape(b, c, CHUNK, h, p)
    bf = bm.astype(jnp.float32).reshape(b, c, CHUNK, h, -1)
    cf = cm.astype(jnp.float32).reshape(b, c, CHUNK, h, -1)
    cs = jnp.cumsum(log_a.astype(jnp.float32).reshape(b, c, CHUNK, h), axis=2)
    seg = cs[:, :, :, None, :] - cs[:, :, None, :, :]
    lmat = jnp.exp(jnp.where(causal_tri()[None, None, :, :, None], seg, -jnp.inf))
    scores = jnp.einsum('bcihn,bcjhn->bcijh', cf, bf) * lmat
    y_diag = jnp.einsum('bcijh,bcjhp->bcihp', scores, xf)
    decay_to_end = jnp.exp(cs[:, :, -1:, :] - cs)
    states = jnp.einsum('bcjhn,bcjh,bcjhp->bchpn', bf, decay_to_end, xf)
    s_prev = chunk_state_scan(jnp.exp(cs[:, :, -1, :]), states)
    y_off = jnp.einsum('bcihn,bchpn,bcih->bcihp', cf, s_prev, jnp.exp(cs))
    return (y_diag + y_off).reshape(b, lp, h, p).astype(x.dtype)


def ssd_mixer(u, valid, conv_w, conv_b, dt_bias, a_log, d_skip, norm_g):
    b, lp, _ = u.shape
    z, xbc, dt = jnp.split(u, [SSD_WIDTH, SSD_WIDTH + SSD_CONV_CH], axis=-1)
    xbc = lax.conv_general_dilated(xbc, conv_w[:, None, :], window_strides=(1,),
                                   padding=[(SSD_CONV - 1, 0)],
                                   dimension_numbers=('NWC', 'WIO', 'NWC'),
                                   feature_group_count=SSD_CONV_CH)
    xbc = jax.nn.silu(xbc + conv_b)
    xs, bm, cm = jnp.split(xbc, [SSD_WIDTH, SSD_WIDTH + SSD_GROUPS * SSD_STATE], axis=-1)
    xs = xs.reshape(b, lp, SSD_HEADS, SSD_HEAD_DIM)
    rep = SSD_HEADS // SSD_GROUPS
    bm = jnp.repeat(bm.reshape(b, lp, SSD_GROUPS, SSD_STATE), rep, axis=2)
    cm = jnp.repeat(cm.reshape(b, lp, SSD_GROUPS, SSD_STATE), rep, axis=2)
    dt = jax.nn.softplus(dt.astype(jnp.float32) + dt_bias.astype(jnp.float32))
    dt = dt * valid.astype(jnp.float32)[None, :, None]
    a = -jnp.exp(a_log.astype(jnp.float32))
    y = ssd_chunked(xs * dt[..., None].astype(xs.dtype), dt * a, bm, cm)
    y = y + d_skip[:, None] * xs
    y = y.reshape(b, lp, SSD_WIDTH) * jax.nn.silu(z)
    return rms_norm(y, norm_g)


def retention_mixer(u, pos, norm_g):
    b, lp, _ = u.shape
    qkw = RET_HEADS * RET_DK
    q, k, v, g = jnp.split(u, [qkw, 2 * qkw, 2 * qkw + RET_WIDTH], axis=-1)
    q = rope(q.reshape(b, lp, RET_HEADS, RET_DK), pos)
    k = rope(k.reshape(b, lp, RET_HEADS, RET_DK), pos) * (RET_DK ** -0.5)
    c = lp // CHUNK
    qf = q.astype(jnp.float32).reshape(b, c, CHUNK, RET_HEADS, RET_DK)
    kf = k.astype(jnp.float32).reshape(b, c, CHUNK, RET_HEADS, RET_DK)
    vf = v.astype(jnp.float32).reshape(b, c, CHUNK, RET_HEADS, RET_DV)
    log_g = jnp.log(1.0 - 2.0 ** (-5.0 - jnp.arange(RET_HEADS, dtype=jnp.float32)))
    idx = jnp.arange(CHUNK, dtype=jnp.float32)
    diff = idx[:, None] - idx[None, :]
    dmat = jnp.exp(jnp.where(causal_tri()[..., None], diff[..., None] * log_g, -jnp.inf))
    y_in = jnp.einsum('bcijh,bcjhv->bcihv', jnp.einsum('bcihk,bcjhk->bcijh', qf, kf) * dmat, vf)
    k_dec = jnp.exp((CHUNK - 1 - idx)[:, None] * log_g)
    states = jnp.einsum('bcjhk,jh,bcjhv->bchvk', kf, k_dec, vf)
    chunk_dec = jnp.broadcast_to(jnp.exp(CHUNK * log_g), (b, c, RET_HEADS))
    s_prev = chunk_state_scan(chunk_dec, states)
    q_dec = jnp.exp((idx + 1.0)[:, None] * log_g)
    y_cross = jnp.einsum('bcihk,bchvk,ih->bcihv', qf, s_prev, q_dec)
    y = (y_in + y_cross).reshape(b, lp, RET_HEADS, RET_DV)
    y = layer_norm_heads(y, norm_g, EPS).astype(u.dtype).reshape(b, lp, RET_WIDTH)
    return jax.nn.silu(g) * y


def rwkv7_mixer(u, mu, w0, w2, a0, a2, g2, k_k, k_a, r_k, ln_g):
    b, lp, _ = u.shape
    u_prev = jnp.pad(u, ((0, 0), (1, 0), (0, 0)))[:, :-1]
    u = u + (u_prev - u) * mu
    r, k, v, wd, ad, gd = jnp.split(
        u, [RWKV_WIDTH, 2 * RWKV_WIDTH, 3 * RWKV_WIDTH, 3 * RWKV_WIDTH + RWKV_DECAY_LORA,
            3 * RWKV_WIDTH + RWKV_DECAY_LORA + RWKV_A_LORA], axis=-1)
    w = (w0 + jnp.tanh(wd) @ w2).astype(jnp.float32)
    w = -jax.nn.softplus(-w) - 0.5
    decay = jnp.exp(-jnp.exp(w))
    a = jax.nn.sigmoid(a0 + ad @ a2)
    g = jax.nn.sigmoid(gd) @ g2
    kk = k * k_k
    k = k * (1.0 + (a - 1.0) * k_a)

    def heads(t):
        return t.astype(jnp.float32).reshape(b, lp, RWKV_HEADS, RWKV_HEAD_DIM)

    rh, kh, vh, ah, dh, kkh = heads(r), heads(k), heads(v), heads(a), heads(decay), heads(kk)
    kkh = kkh / jnp.maximum(jnp.linalg.norm(kkh, axis=-1, keepdims=True), 1e-12)

    def step(s, inp):
        r_t, w_t, k_t, v_t, kk_t, b_t = inp
        sa = jnp.einsum('bhvk,bhk->bhv', s, -kk_t)
        s = s * w_t[:, :, None, :] + sa[..., None] * b_t[:, :, None, :] + v_t[..., None] * k_t[:, :, None, :]
        return s, jnp.einsum('bhvk,bhk->bhv', s, r_t)

    s0 = jnp.zeros((b, RWKV_HEADS, RWKV_HEAD_DIM, RWKV_HEAD_DIM), jnp.float32)
    xs = tuple(jnp.moveaxis(t, 1, 0) for t in (rh, dh, kh, vh, kkh, kkh * ah))
    _, out = lax.scan(step, s0, xs)
    out = jnp.moveaxis(out, 0, 1)
    out = layer_norm_heads(out, ln_g, RWKV_LN_EPS)
    bonus = jnp.sum(rh * kh * r_k.astype(jnp.float32), axis=-1, keepdims=True) * vh
    out = (out + bonus).astype(u.dtype).reshape(b, lp, RWKV_WIDTH)
    return out * g


def setup_inputs(seed: int = 0) -> dict:
    key = jax.random.key(seed)
    ks = iter(jax.random.split(key, 48))
    L = DEPTH

    def nrm(shape, scale):
        return jax.random.normal(next(ks), shape, jnp.float32) * scale

    def gain(shape):
        return 1.0 + nrm(shape, 0.02)

    def unif(shape, lo, hi):
        return jax.random.uniform(next(ks), shape, jnp.float32, minval=lo, maxval=hi)

    dt0 = jnp.exp(unif((L, SSD_HEADS), math.log(1e-3), math.log(1e-1)))
    return {
        'x': nrm((BATCH, SEQ, D_MODEL), 1.0),
        'meta_tokens': nrm((N_META, D_MODEL), 1.0),
        'ffn1_norm': gain((L, D_MODEL)),
        'ffn1_w_gate': nrm((L, D_MODEL, D_FF), D_MODEL ** -0.5),
        'ffn1_w_up': nrm((L, D_MODEL, D_FF), D_MODEL ** -0.5),
        'ffn1_w_down': nrm((L, D_FF, D_MODEL), D_FF ** -0.5),
        'mix_norm': gain((L, D_MODEL)),
        'w_in': nrm((L, D_MODEL, IN_WIDTH), D_MODEL ** -0.5),
        'w_out': nrm((L, MIX_WIDTH, D_MODEL), MIX_WIDTH ** -0.5),
        'mla_q_norm': gain((L, MLA_Q_LORA)),
        'mla_w_q_up': nrm((L, MLA_Q_LORA, MLA_HEADS * MLA_QK), MLA_Q_LORA ** -0.5),
        'mla_kv_norm': gain((L, MLA_KV_LORA)),
        'mla_w_kv_up': nrm((L, MLA_KV_LORA, MLA_HEADS * (MLA_NOPE + MLA_V)), MLA_KV_LORA ** -0.5),
        'mla_qk_norm_q': gain((L, MLA_QK)),
        'mla_qk_norm_k': gain((L, MLA_QK)),
        'ssd_conv_w': nrm((L, SSD_CONV, SSD_CONV_CH), SSD_CONV ** -0.5),
        'ssd_conv_b': nrm((L, SSD_CONV_CH), 0.01),
        'ssd_dt_bias': dt0 + jnp.log(-jnp.expm1(-dt0)),
        'ssd_a_log': jnp.log(unif((L, SSD_HEADS), 1.0, 16.0)),
        'ssd_d': gain((L, SSD_HEADS)),
        'ssd_norm': gain((L, SSD_WIDTH)),
        'ret_norm': gain((L, RET_HEADS, RET_DV)),
        'rwkv_mu': unif((L, RWKV_IN), 0.0, 1.0),
        'rwkv_w0': unif((L, RWKV_WIDTH), -6.0, 0.0),
        'rwkv_w2': nrm((L, RWKV_DECAY_LORA, RWKV_WIDTH), RWKV_DECAY_LORA ** -0.5),
        'rwkv_a0': nrm((L, RWKV_WIDTH), 0.1),
        'rwkv_a2': nrm((L, RWKV_A_LORA, RWKV_WIDTH), RWKV_A_LORA ** -0.5),
        'rwkv_g2': nrm((L, RWKV_GATE_LORA, RWKV_WIDTH), RWKV_GATE_LORA ** -0.5),
        'rwkv_k_k': 0.85 + nrm((L, RWKV_WIDTH), 0.02),
        'rwkv_k_a': gain((L, RWKV_WIDTH)),
        'rwkv_r_k': nrm((L, RWKV_HEADS, RWKV_HEAD_DIM), 0.1),
        'rwkv_ln': gain((L, RWKV_HEADS, RWKV_HEAD_DIM)),
        'ffn2_norm': gain((L, D_MODEL)),
        'ffn2_w_gate': nrm((L, D_MODEL, D_FF), D_MODEL ** -0.5),
        'ffn2_w_up': nrm((L, D_MODEL, D_FF), D_MODEL ** -0.5),
        'ffn2_w_down': nrm((L, D_FF, D_MODEL), D_FF ** -0.5),
    }


def reference(x, meta_tokens, ffn1_norm, ffn1_w_gate, ffn1_w_up, ffn1_w_down, mix_norm, w_in, w_out,
              mla_q_norm, mla_w_q_up, mla_kv_norm, mla_w_kv_up, mla_qk_norm_q, mla_qk_norm_k,
              ssd_conv_w, ssd_conv_b, ssd_dt_bias, ssd_a_log, ssd_d, ssd_norm, ret_norm,
              rwkv_mu, rwkv_w0, rwkv_w2, rwkv_a0, rwkv_a2, rwkv_g2, rwkv_k_k, rwkv_k_a, rwkv_r_k, rwkv_ln,
              ffn2_norm, ffn2_w_gate, ffn2_w_up, ffn2_w_down):
    b, seq, _ = x.shape
    meta = jnp.broadcast_to(meta_tokens.astype(x.dtype)[None], (b, N_META, D_MODEL))
    h = jnp.concatenate([meta, x], axis=1)
    lp = seq + CHUNK
    pos = jnp.arange(lp, dtype=jnp.float32) - PAD
    valid = jnp.arange(lp) >= PAD
    for l in range(DEPTH):
        h = h + 0.5 * swiglu(rms_norm(h, ffn1_norm[l]), ffn1_w_gate[l], ffn1_w_up[l], ffn1_w_down[l])
        u = rms_norm(h, mix_norm[l]) @ w_in[l]
        u = jnp.pad(u, ((0, 0), (PAD, 0), (0, 0)))
        u_mla, u_ssd, u_ret, u_rwkv = jnp.split(
            u, [MLA_IN, MLA_IN + SSD_IN, MLA_IN + SSD_IN + RET_IN], axis=-1)
        y = jnp.concatenate([
            mla_mixer(u_mla, pos, valid, mla_q_norm[l], mla_w_q_up[l], mla_kv_norm[l], mla_w_kv_up[l],
                      mla_qk_norm_q[l], mla_qk_norm_k[l]),
            ssd_mixer(u_ssd, valid, ssd_conv_w[l], ssd_conv_b[l], ssd_dt_bias[l], ssd_a_log[l],
                      ssd_d[l], ssd_norm[l]),
            retention_mixer(u_ret, pos, ret_norm[l]),
            rwkv7_mixer(u_rwkv, rwkv_mu[l], rwkv_w0[l], rwkv_w2[l], rwkv_a0[l], rwkv_a2[l], rwkv_g2[l],
                        rwkv_k_k[l], rwkv_k_a[l], rwkv_r_k[l], rwkv_ln[l]),
        ], axis=-1)[:, PAD:]
        h = h + y @ w_out[l]
        h = h + 0.5 * swiglu(rms_norm(h, ffn2_norm[l]), ffn2_w_gate[l], ffn2_w_up[l], ffn2_w_down[l])
    return h[:, N_META:]
```

```python
import functools
import math

import jax
import jax.numpy as jnp
from jax import lax
from jax.experimental import pallas as pl
from jax.experimental.pallas import tpu as pltpu

F32 = jnp.float32
BF16 = jnp.bfloat16

D_MODEL = 2048
DEPTH = 4
N_META = 16
CHUNK = 128
PAD = CHUNK - N_META
D_FF = 5632
EPS = 1e-6
ROPE_THETA = 10000.0
NEG_INF = -1e30

MLA_HEADS = 4
MLA_NOPE = 128
MLA_ROPE = 64
MLA_QK = MLA_NOPE + MLA_ROPE
MLA_V = 128
MLA_Q_LORA = 384
MLA_KV_LORA = 128
MLA_QK_PAD = 256
MLA_IN = MLA_Q_LORA + MLA_KV_LORA + MLA_ROPE
MLA_IN_PAD = 640

SSD_HEADS = 8
SSD_HEAD_DIM = 64
SSD_WIDTH = 512
SSD_GROUPS = 2
SSD_STATE = 128
SSD_CONV = 4
SSD_CONV_CH = SSD_WIDTH + 2 * SSD_GROUPS * SSD_STATE
SSD_IN = SSD_WIDTH + SSD_CONV_CH + SSD_HEADS
SSD_IN_PAD = SSD_WIDTH + SSD_CONV_CH + 128

RET_HEADS = 4
RET_DK = 64
RET_DV = 128
RET_WIDTH = 512
RET_IN = 2 * RET_HEADS * RET_DK + 2 * RET_WIDTH

RWKV_HEADS = 8
RWKV_HEAD_DIM = 64
RWKV_WIDTH = 512
RWKV_DECAY_LORA = 32
RWKV_A_LORA = 32
RWKV_GATE_LORA = 64
RWKV_LN_EPS = 64e-5
RWKV_IN = 3 * RWKV_WIDTH + RWKV_DECAY_LORA + RWKV_A_LORA + RWKV_GATE_LORA
RWKV_SUB = 64
RWKV_INV_BLOCK = 16

MIX_WIDTH = 2048

V7X_VMEM_LIMIT_BYTES = 56 * 1024 * 1024
TOKEN_TILE = 640
FF_TILE = 512


def _cparams(*sem):
    return pltpu.CompilerParams(dimension_semantics=sem, vmem_limit_bytes=V7X_VMEM_LIMIT_BYTES)


def _sigmoid(x):
    return 1.0 / (1.0 + jnp.exp(-x))


def _silu(x):
    return x * _sigmoid(x)


def _softplus(x):
    return jnp.maximum(x, 0.0) + jnp.log(1.0 + jnp.exp(-jnp.abs(x)))


def _rms(x, g, eps=EPS):
    return x * lax.rsqrt(jnp.mean(x * x, axis=-1, keepdims=True) + eps) * g


def _mm(a, b):
    return jnp.dot(a.astype(BF16), b.astype(BF16), preferred_element_type=F32)


def _mm_nt(a, b):
    return lax.dot_general(a.astype(BF16), b.astype(BF16), (((1,), (1,)), ((), ())),
                           preferred_element_type=F32)


def _split3(x):
    x1 = x.astype(BF16)
    r1 = x - x1.astype(F32)
    x2 = r1.astype(BF16)
    x3 = (r1 - x2.astype(F32)).astype(BF16)
    return x1, x2, x3


def _dot01_right(x, m01):
    p1, p2, p3 = _split3(x)
    return (jnp.dot(p1, m01, preferred_element_type=F32)
            + jnp.dot(p2, m01, preferred_element_type=F32)
            + jnp.dot(p3, m01, preferred_element_type=F32))


def _dot01_left(m01, x):
    p1, p2, p3 = _split3(x)
    return (jnp.dot(m01, p1, preferred_element_type=F32)
            + jnp.dot(m01, p2, preferred_element_type=F32)
            + jnp.dot(m01, p3, preferred_element_type=F32))


def _row_ids(rows, cols, base):
    return base + lax.broadcasted_iota(jnp.int32, (rows, cols), 0)


def _rope_table_kernel(inv_ref, cos_ref, sin_ref, sinm_ref):
    i = pl.program_id(0)
    pos = (_row_ids(CHUNK, 128, i * CHUNK) - PAD).astype(F32)
    lane = lax.broadcasted_iota(jnp.int32, (CHUNK, 128), 1)
    ang = pos * inv_ref[...]
    c = jnp.cos(ang)
    s = jnp.sin(ang)
    s = jnp.where((lane % 64) < 32, -s, s)
    cos_ref[...] = c
    sin_ref[...] = s
    sinm_ref[...] = jnp.where(lane < 64, s, 0.0)


def _rope_tables(lp):
    half = 32
    inv = ROPE_THETA ** (-jnp.arange(half, dtype=F32) / half)
    inv = jnp.tile(inv, 4)[None, :]
    out = jax.ShapeDtypeStruct((lp, 128), F32)
    spec = pl.BlockSpec((CHUNK, 128), lambda i: (i, 0))
    return pl.pallas_call(
        _rope_table_kernel,
        grid=(lp // CHUNK,),
        in_specs=[pl.BlockSpec((1, 128), lambda i: (0, 0))],
        out_specs=[spec, spec, spec],
        out_shape=[out, out, out],
        compiler_params=_cparams("parallel"),
        name="rope_tables",
    )(inv)


def _rope_lanes(x, cos, sin_signed):
    lane = lax.broadcasted_iota(jnp.int32, x.shape, 1)
    fwd = pltpu.roll(x, 32, axis=1)
    bwd = pltpu.roll(x, 96, axis=1)
    rot = jnp.where((lane % 64) < 32, bwd, fwd)
    return x * cos + rot * sin_signed


def _ffn_kernel(x_ref, g_ref, wg_ref, wu_ref, wd_ref, o_ref, xn_ref, acc_ref):
    j = pl.program_id(1)

    @pl.when(j == 0)
    def _():
        xn_ref[...] = _rms(x_ref[...], g_ref[...]).astype(BF16)
        acc_ref[...] = jnp.zeros_like(acc_ref)

    xn = xn_ref[...]
    a = jnp.dot(xn, wg_ref[...], preferred_element_type=F32)
    b = jnp.dot(xn, wu_ref[...], preferred_element_type=F32)
    mid = (_silu(a) * b).astype(BF16)
    acc_ref[...] += jnp.dot(mid, wd_ref[...], preferred_element_type=F32)

    @pl.when(j == pl.num_programs(1) - 1)
    def _():
        o_ref[...] = x_ref[...] + 0.5 * acc_ref[...]


def _ffn(h, g, wg, wu, wd):
    lp = h.shape[0]
    tm, tf = TOKEN_TILE, FF_TILE
    return pl.pallas_call(
        _ffn_kernel,
        grid=(lp // tm, D_FF // tf),
        in_specs=[
            pl.BlockSpec((tm, D_MODEL), lambda i, j: (i, 0)),
            pl.BlockSpec((1, D_MODEL), lambda i, j: (0, 0)),
            pl.BlockSpec((D_MODEL, tf), lambda i, j: (0, j)),
            pl.BlockSpec((D_MODEL, tf), lambda i, j: (0, j)),
            pl.BlockSpec((tf, D_MODEL), lambda i, j: (j, 0)),
        ],
        out_specs=pl.BlockSpec((tm, D_MODEL), lambda i, j: (i, 0)),
        out_shape=jax.ShapeDtypeStruct((lp, D_MODEL), F32),
        scratch_shapes=[pltpu.VMEM((tm, D_MODEL), BF16), pltpu.VMEM((tm, D_MODEL), F32)],
        compiler_params=_cparams("parallel", "arbitrary"),
        name="ffn",
    )(h, g, wg, wu, wd)


def _inproj_kernel(x_ref, g_ref, w_ref, o_ref):
    i = pl.program_id(0)
    tm = x_ref.shape[0]
    xn = _rms(x_ref[...], g_ref[...]).astype(BF16)
    u = jnp.dot(xn, w_ref[...], preferred_element_type=F32)
    row = _row_ids(tm, 1, i * tm)
    o_ref[...] = jnp.where(row >= PAD, u, 0.0)


def _inproj(h, g, w):
    lp = h.shape[0]
    n = w.shape[1]
    tm = TOKEN_TILE
    return pl.pallas_call(
        _inproj_kernel,
        grid=(lp // tm,),
        in_specs=[
            pl.BlockSpec((tm, D_MODEL), lambda i: (i, 0)),
            pl.BlockSpec((1, D_MODEL), lambda i: (0, 0)),
            pl.BlockSpec((D_MODEL, n), lambda i: (0, 0)),
        ],
        out_specs=pl.BlockSpec((tm, n), lambda i: (i, 0)),
        out_shape=jax.ShapeDtypeStruct((lp, n), F32),
        compiler_params=_cparams("parallel"),
        name="inproj",
    )(h, g, w)


def _outproj_kernel(h_ref, y0_ref, y1_ref, y2_ref, y3_ref, w_ref, o_ref):
    acc = h_ref[...]
    for m, y_ref in enumerate((y0_ref, y1_ref, y2_ref, y3_ref)):
        acc = acc + jnp.dot(y_ref[...], w_ref[m], preferred_element_type=F32)
    o_ref[...] = acc


def _outproj(h, ys, w):
    lp = h.shape[0]
    tm = TOKEN_TILE
    yspec = pl.BlockSpec((tm, 512), lambda i: (i, 0))
    return pl.pallas_call(
        _outproj_kernel,
        grid=(lp // tm,),
        in_specs=[pl.BlockSpec((tm, D_MODEL), lambda i: (i, 0)), yspec, yspec, yspec, yspec,
                  pl.BlockSpec((4, 512, D_MODEL), lambda i: (0, 0, 0))],
        out_specs=pl.BlockSpec((tm, D_MODEL), lambda i: (i, 0)),
        out_shape=jax.ShapeDtypeStruct((lp, D_MODEL), F32),
        compiler_params=_cparams("parallel"),
        name="outproj",
    )(h, *ys, w)


def _mla_prep_kernel(x_ref, g_ref, win_ref, qn_ref, wq_ref, kvn_ref, wkv_ref, gq_ref, gk_ref,
                     cos_ref, sinm_ref, q_ref, k_ref, v_ref):
    i = pl.program_id(0)
    tm = x_ref.shape[0]
    xn = _rms(x_ref[...], g_ref[...]).astype(BF16)
    u = jnp.dot(xn, win_ref[...], preferred_element_type=F32)
    row = _row_ids(tm, 1, i * tm)
    u = jnp.where(row >= PAD, u, 0.0)
    cq = u[:, :MLA_Q_LORA]
    ckv = u[:, MLA_Q_LORA:MLA_Q_LORA + MLA_KV_LORA]
    kpe = u[:, MLA_Q_LORA + MLA_KV_LORA:]
    q_all = _mm(_rms(cq, qn_ref[...]), wq_ref[...])
    kv_all = _mm(_rms(ckv, kvn_ref[...]), wkv_ref[...])
    cos = cos_ref[...]
    sinm = sinm_ref[...]
    gq = gq_ref[...]
    gk = gk_ref[...]
    scale = MLA_QK ** -0.5
    for h in range(MLA_HEADS):
        qh = q_all[:, h * MLA_QK_PAD:(h + 1) * MLA_QK_PAD]
        ss = jnp.sum(qh * qh, axis=-1, keepdims=True) * (1.0 / MLA_QK)
        qh = qh * lax.rsqrt(ss + EPS) * gq
        q_rot = _rope_lanes(qh[:, MLA_NOPE:], cos, sinm)
        q_ref[h, :, :MLA_NOPE] = (qh[:, :MLA_NOPE] * scale).astype(BF16)
        q_ref[h, :, MLA_NOPE:] = (q_rot * scale).astype(BF16)

        kn = kv_all[:, h * MLA_NOPE:(h + 1) * MLA_NOPE]
        ss = (jnp.sum(kn * kn, axis=-1, keepdims=True)
              + jnp.sum(kpe * kpe, axis=-1, keepdims=True)) * (1.0 / MLA_QK)
        rs = lax.rsqrt(ss + EPS)
        k_rot = _rope_lanes(kpe * rs * gk[:, MLA_NOPE:], cos, sinm)
        k_ref[h, :, :MLA_NOPE] = (kn * rs * gk[:, :MLA_NOPE]).astype(BF16)
        k_ref[h, :, MLA_NOPE:] = k_rot.astype(BF16)
        v_ref[h] = kv_all[:, 4 * MLA_NOPE + h * MLA_V: 4 * MLA_NOPE + (h + 1) * MLA_V].astype(BF16)


def _mla_prep(h, g, win, qn, wq, kvn, wkv, gq, gk, cos, sinm):
    lp = h.shape[0]
    tm = TOKEN_TILE
    full = lambda shape: pl.BlockSpec(shape, lambda i: (0,) * len(shape))
    return pl.pallas_call(
        _mla_prep_kernel,
        grid=(lp // tm,),
        in_specs=[
            pl.BlockSpec((tm, D_MODEL), lambda i: (i, 0)),
            full((1, D_MODEL)), full((D_MODEL, MLA_IN_PAD)),
            full((1, MLA_Q_LORA)), full((MLA_Q_LORA, MLA_HEADS * MLA_QK_PAD)),
            full((1, MLA_KV_LORA)), full((MLA_KV_LORA, 2 * MLA_HEADS * MLA_NOPE)),
            full((1, MLA_QK_PAD)), full((1, MLA_QK_PAD)),
            pl.BlockSpec((tm, 128), lambda i: (i, 0)),
            pl.BlockSpec((tm, 128), lambda i: (i, 0)),
        ],
        out_specs=[
            pl.BlockSpec((MLA_HEADS, tm, MLA_QK_PAD), lambda i: (0, i, 0)),
            pl.BlockSpec((MLA_HEADS, tm, MLA_QK_PAD), lambda i: (0, i, 0)),
            pl.BlockSpec((MLA_HEADS, tm, MLA_V), lambda i: (0, i, 0)),
        ],
        out_shape=[
            jax.ShapeDtypeStruct((MLA_HEADS, lp, MLA_QK_PAD), BF16),
            jax.ShapeDtypeStruct((MLA_HEADS, lp, MLA_QK_PAD), BF16),
            jax.ShapeDtypeStruct((MLA_HEADS, lp, MLA_V), BF16),
        ],
        compiler_params=_cparams("parallel"),
        name="mla_prep",
    )(h, g, win, qn, wq, kvn, wkv, gq, gk, cos, sinm)


def _attn_kernel(q_ref, k_ref, v_ref, o_ref):
    i = pl.program_id(1)
    tq = q_ref.shape[1]
    q = q_ref[0]
    row = _row_ids(tq, tq, i * tq)
    col0 = lax.broadcasted_iota(jnp.int32, (tq, tq), 1)

    def block(j, carry):
        m, l, acc = carry
        start = pl.multiple_of(j * tq, tq)
        k = k_ref[0, pl.ds(start, tq), :]
        v = v_ref[0, pl.ds(start, tq), :]
        s = lax.dot_general(q, k, (((1,), (1,)), ((), ())), preferred_element_type=F32)
        col = col0 + j * tq
        s = jnp.where(col <= row, jnp.where(col >= PAD, s, NEG_INF), NEG_INF)
        m_new = jnp.maximum(m, jnp.max(s, axis=-1, keepdims=True))
        alpha = jnp.exp(m - m_new)
        p = jnp.exp(s - m_new)
        l = alpha * l + jnp.sum(p, axis=-1, keepdims=True)
        acc = alpha * acc + jnp.dot(p.astype(BF16), v, preferred_element_type=F32)
        return m_new, l, acc

    init = (jnp.full((tq, 1), NEG_INF, F32), jnp.zeros((tq, 1), F32), jnp.zeros((tq, MLA_V), F32))
    m, l, acc = lax.fori_loop(0, i + 1, block, init)
    o_ref[...] = (acc / l).astype(BF16)


def _attention(q, k, v):
    lp = q.shape[1]
    tq = TOKEN_TILE
    return pl.pallas_call(
        _attn_kernel,
        grid=(MLA_HEADS, lp // tq),
        in_specs=[
            pl.BlockSpec((1, tq, MLA_QK_PAD), lambda h, i: (h, i, 0)),
            pl.BlockSpec((1, lp, MLA_QK_PAD), lambda h, i: (h, 0, 0)),
            pl.BlockSpec((1, lp, MLA_V), lambda h, i: (h, 0, 0)),
        ],
        out_specs=pl.BlockSpec((tq, MLA_V), lambda h, i: (i, h)),
        out_shape=jax.ShapeDtypeStruct((lp, MLA_HEADS * MLA_V), BF16),
        compiler_params=_cparams("parallel", "arbitrary"),
        name="mla_attention",
    )(q, k, v)


def _shift_rows(x, carry, s):
    rolled = pltpu.roll(x, s, axis=0)
    head = pltpu.roll(carry, s, axis=0)
    r8 = lax.broadcasted_iota(jnp.int32, (8, 1), 0)
    return jnp.concatenate([jnp.where(r8 < s, head, rolled[:8]), rolled[8:]], axis=0)


def _ssd_kernel(u_ref, cw_ref, cb_ref, dtb_ref, alog_ref, dsk_ref, ng_ref, tri_ref, e8_ref,
                y_ref, carry_ref, state_ref):
    i = pl.program_id(0)

    @pl.when(i == 0)
    def _():
        carry_ref[...] = jnp.zeros_like(carry_ref)
        state_ref[...] = jnp.zeros_like(state_ref)

    z = u_ref[:, :SSD_WIDTH]
    xbc = u_ref[:, SSD_WIDTH:SSD_WIDTH + SSD_CONV_CH]
    dt_raw = u_ref[:, SSD_WIDTH + SSD_CONV_CH:]
    carry = carry_ref[...]
    cw = cw_ref[...]
    conv = xbc * cw[3:4, :]
    for s in (1, 2, 3):
        conv = conv + _shift_rows(xbc, carry, s) * cw[3 - s:4 - s, :]
    carry_ref[...] = xbc[CHUNK - 8:, :]
    xbc = _silu(conv + cb_ref[...])
    xs = xbc[:, :SSD_WIDTH]
    bm = xbc[:, SSD_WIDTH:SSD_WIDTH + SSD_GROUPS * SSD_STATE]
    cm = xbc[:, SSD_WIDTH + SSD_GROUPS * SSD_STATE:]

    row = _row_ids(CHUNK, 1, i * CHUNK)
    dt = _softplus(dt_raw + dtb_ref[...]) * jnp.where(row >= PAD, 1.0, 0.0)
    la = dt * (-jnp.exp(alog_ref[...]))
    tri = tri_ref[...]
    e8 = e8_ref[...]
    cs = _dot01_left(tri, la)
    cs_e = _dot01_right(cs, e8)
    dt_e = _dot01_right(dt, e8)
    cs_t = cs.T
    cs_last_e = cs_e[CHUNK - 1:CHUNK, :]
    x = xs * dt_e
    xd = x * jnp.exp(cs_last_e - cs_e)
    ecs = jnp.exp(cs_e)
    dec = jnp.exp(cs_last_e)

    r_i = lax.broadcasted_iota(jnp.int32, (CHUNK, CHUNK), 0)
    c_i = lax.broadcasted_iota(jnp.int32, (CHUNK, CHUNK), 1)
    causal = r_i >= c_i
    per = SSD_HEADS // SSD_GROUPS
    gw = per * SSD_HEAD_DIM
    y_parts = []
    for g in range(SSD_GROUPS):
        b_g = bm[:, g * SSD_STATE:(g + 1) * SSD_STATE]
        c_g = cm[:, g * SSD_STATE:(g + 1) * SSD_STATE]
        scores = _mm_nt(c_g, b_g)
        s_prev = state_ref[g]
        y_off = _mm(c_g, s_prev) * ecs[:, g * gw:(g + 1) * gw]
        state_ref[g] = s_prev * dec[:, g * gw:(g + 1) * gw] + _mm(b_g.T, xd[:, g * gw:(g + 1) * gw])
        for hh in range(per):
            h = g * per + hh
            seg = cs[:, h:h + 1] - cs_t[h:h + 1, :]
            lmat = jnp.where(causal, jnp.exp(jnp.where(causal, seg, 0.0)), 0.0)
            y_parts.append(_mm(scores * lmat, x[:, h * SSD_HEAD_DIM:(h + 1) * SSD_HEAD_DIM])
                           + y_off[:, hh * SSD_HEAD_DIM:(hh + 1) * SSD_HEAD_DIM])
    y = jnp.concatenate(y_parts, axis=1) + dsk_ref[...] * xs
    y = y * _silu(z)
    y_ref[...] = _rms(y, ng_ref[...]).astype(BF16)


def _ssd(u, cw, cb, dtb, alog, dsk, ng, tri, e8):
    lp = u.shape[0]
    full = lambda shape: pl.BlockSpec(shape, lambda i: (0,) * len(shape))
    return pl.pallas_call(
        _ssd_kernel,
        grid=(lp // CHUNK,),
        in_specs=[pl.BlockSpec((CHUNK, SSD_IN_PAD), lambda i: (i, 0)),
                  full((SSD_CONV, SSD_CONV_CH)), full((1, SSD_CONV_CH)), full((1, 128)), full((1, 128)),
                  full((1, SSD_WIDTH)), full((1, SSD_WIDTH)), full((CHUNK, CHUNK)), full((128, SSD_WIDTH))],
        out_specs=pl.BlockSpec((CHUNK, SSD_WIDTH), lambda i: (i, 0)),
        out_shape=jax.ShapeDtypeStruct((lp, SSD_WIDTH), BF16),
        scratch_shapes=[pltpu.VMEM((8, SSD_CONV_CH), F32),
                        pltpu.VMEM((SSD_GROUPS, SSD_STATE, SSD_WIDTH // SSD_GROUPS), F32)],
        compiler_params=_cparams("arbitrary"),
        name="ssd",
    )(u, cw, cb, dtb, alog, dsk, ng, tri, e8)


def _ret_kernel(u_ref, cos_ref, sin_ref, ng_ref, y_ref, dmat_ref, state_ref):
    i = pl.program_id(0)
    log_g = [math.log(1.0 - 2.0 ** (-5.0 - h)) for h in range(RET_HEADS)]

    @pl.when(i == 0)
    def _():
        state_ref[...] = jnp.zeros_like(state_ref)
        r_i = lax.broadcasted_iota(jnp.int32, (CHUNK, CHUNK), 0)
        c_i = lax.broadcasted_iota(jnp.int32, (CHUNK, CHUNK), 1)
        diff = (r_i - c_i).astype(F32)
        for h in range(RET_HEADS):
            dmat_ref[h] = jnp.where(r_i >= c_i, jnp.exp(jnp.where(r_i >= c_i, diff, 0.0) * log_g[h]), 0.0)

    qkw = RET_HEADS * RET_DK
    cos = cos_ref[...]
    sin = sin_ref[...]
    q = jnp.concatenate([_rope_lanes(u_ref[:, c * 128:(c + 1) * 128], cos, sin) for c in range(2)], axis=1)
    k = jnp.concatenate([_rope_lanes(u_ref[:, qkw + c * 128:qkw + (c + 1) * 128], cos, sin)
                         for c in range(2)], axis=1) * (RET_DK ** -0.5)
    idx = lax.broadcasted_iota(jnp.int32, (CHUNK, 1), 0).astype(F32)
    k_t = k.T
    idx_row = lax.broadcasted_iota(jnp.int32, (1, CHUNK), 1).astype(F32)
    for h in range(RET_HEADS):
        q_h = q[:, h * RET_DK:(h + 1) * RET_DK]
        k_h = k[:, h * RET_DK:(h + 1) * RET_DK]
        v_h = u_ref[:, 2 * qkw + h * RET_DV: 2 * qkw + (h + 1) * RET_DV]
        g_h = u_ref[:, 2 * qkw + RET_WIDTH + h * RET_DV: 2 * qkw + RET_WIDTH + (h + 1) * RET_DV]
        sc = _mm_nt(q_h, k_h) * dmat_ref[h]
        s_prev = state_ref[h]
        q_dec = jnp.exp((idx + 1.0) * log_g[h])
        y = _mm(sc, v_h) + _mm(q_h * q_dec, s_prev)
        k_dec_row = jnp.exp((CHUNK - 1 - idx_row) * log_g[h])
        kd_t = k_t[h * RET_DK:(h + 1) * RET_DK, :] * k_dec_row
        state_ref[h] = s_prev * math.exp(CHUNK * log_g[h]) + _mm(kd_t, v_h)
        mu = jnp.mean(y, axis=-1, keepdims=True)
        var = jnp.mean(jnp.square(y - mu), axis=-1, keepdims=True)
        yn = (y - mu) * lax.rsqrt(var + EPS) * ng_ref[:, h * RET_DV:(h + 1) * RET_DV]
        y_ref[:, h * RET_DV:(h + 1) * RET_DV] = (_silu(g_h) * yn).astype(BF16)


def _retention(u, cos, sin, ng):
    lp = u.shape[0]
    return pl.pallas_call(
        _ret_kernel,
        grid=(lp // CHUNK,),
        in_specs=[pl.BlockSpec((CHUNK, RET_IN), lambda i: (i, 0)),
                  pl.BlockSpec((CHUNK, 128), lambda i: (i, 0)),
                  pl.BlockSpec((CHUNK, 128), lambda i: (i, 0)),
                  pl.BlockSpec((1, RET_WIDTH), lambda i: (0, 0))],
        out_specs=pl.BlockSpec((CHUNK, RET_WIDTH), lambda i: (i, 0)),
        out_shape=jax.ShapeDtypeStruct((lp, RET_WIDTH), BF16),
        scratch_shapes=[pltpu.VMEM((RET_HEADS, CHUNK, CHUNK), F32),
                        pltpu.VMEM((RET_HEADS, RET_DK, RET_DV), F32)],
        compiler_params=_cparams("arbitrary"),
        name="retention",
    )(u, cos, sin, ng)


def _inv_unit_upper(a, in_block, eye):
    d = jnp.where(in_block, a, 0.0)
    f = a - d
    d2 = _mm(d, d)
    d4 = _mm(d2, d2)
    d8 = _mm(d4, d4)
    td = _mm(eye + d, eye + d2)
    td = _mm(td, eye + d4)
    td = _mm(td, eye + d8)
    g = _mm(f, td)
    g2 = _mm(g, g)
    return _mm(td, _mm(eye + g, eye + g2))


def _rwkv_kernel(u_ref, mu_ref, w0_ref, a0_ref, kk_ref, ka_ref, rk_ref, ln_ref, lora_ref, tri_ref,
                 ones_ref, y_ref, prev_ref, state_ref, o_ref):
    i = pl.program_id(0)

    @pl.when(i == 0)
    def _():
        prev_ref[...] = jnp.zeros_like(prev_ref)
        state_ref[...] = jnp.zeros_like(state_ref)

    u = u_ref[...]
    rows = lax.broadcasted_iota(jnp.int32, (CHUNK, 1), 0)
    u_prev = jnp.where(rows == 0, prev_ref[7:8, :], pltpu.roll(u, 1, axis=0))
    prev_ref[...] = u[CHUNK - 8:, :]
    us = u + (u_prev - u) * mu_ref[...]
    w3 = RWKV_WIDTH
    r = us[:, :w3]
    k = us[:, w3:2 * w3]
    v = us[:, 2 * w3:3 * w3]
    lo = us[:, 3 * w3:]
    w = w0_ref[...] + _mm(jnp.tanh(lo), lora_ref[0])
    w = -_softplus(-w) - 0.5
    ld = -jnp.exp(w)
    a = _sigmoid(a0_ref[...] + _mm(lo, lora_ref[1]))
    g = _mm(_sigmoid(lo), lora_ref[2])
    ones = ones_ref[...]
    kk = k * kk_ref[...]
    k2 = k * (1.0 + (a - 1.0) * ka_ref[...])
    kk = kk / jnp.maximum(jnp.sqrt(_dot01_right(kk * kk, ones)), 1e-12)
    b = kk * a

    lc_full = _dot01_left(tri_ref[...], ld)
    v_t = v.T

    n = RWKV_SUB
    r_i = lax.broadcasted_iota(jnp.int32, (n, n), 0)
    c_i = lax.broadcasted_iota(jnp.int32, (n, n), 1)
    strict_upper = r_i < c_i
    lower_incl = r_i >= c_i
    in_block = (r_i // RWKV_INV_BLOCK) == (c_i // RWKV_INV_BLOCK)
    eye = jnp.where(r_i == c_i, 1.0, 0.0)
    hd = RWKV_HEAD_DIM

    for sub in range(CHUNK // n):
        lo_r, hi_r = sub * n, (sub + 1) * n
        lc = lc_full[lo_r:hi_r, :]
        if sub > 0:
            lc = lc - lc_full[lo_r - 1:lo_r, :]
        lc_last = lc[n - 1:n, :]
        e_pos = jnp.exp(lc)
        e_neg = jnp.exp(-lc)
        e_prev = jnp.exp(lc - ld[lo_r:hi_r, :])
        e_end = jnp.exp(lc_last - lc)
        gam = jnp.exp(lc_last)
        a_t = (-kk[lo_r:hi_r, :] * e_prev).astype(BF16)
        r_t = (r[lo_r:hi_r, :] * e_pos).astype(BF16)
        b_t = (b[lo_r:hi_r, :] * e_neg).astype(BF16)
        k_t = (k2[lo_r:hi_r, :] * e_neg).astype(BF16)
        b_h = (b[lo_r:hi_r, :] * e_end).astype(BF16)
        k_h = (k2[lo_r:hi_r, :] * e_end).astype(BF16)
        v_s = v[lo_r:hi_r, :].astype(BF16)
        for h in range(RWKV_HEADS):
            sl = slice(h * hd, (h + 1) * hd)
            bk = jnp.concatenate([b_t[:, sl], k_t[:, sl]], axis=0)
            at = _mm_nt(bk, a_t[:, sl])
            ar = _mm_nt(r_t[:, sl], bk)
            aab_t = jnp.where(strict_upper, at[:n, :], 0.0)
            aak_t = jnp.where(strict_upper, at[n:, :], 0.0)
            a_rb = jnp.where(lower_incl, ar[:, :n], 0.0)
            a_rk = jnp.where(lower_incl, ar[:, n:], 0.0)
            t_t = _inv_unit_upper(aab_t, in_block, eye)
            s0 = state_ref[h]
            vt_h = v_t[h * hd:(h + 1) * hd, lo_r:hi_r]
            u_t = _mm(_mm_nt(s0, a_t[:, sl]) + _mm(vt_h, aak_t), t_t)
            o_h = _mm_nt(r_t[:, sl], s0) + _mm_nt(a_rb, u_t) + _mm(a_rk, v_s[:, sl])
            state_ref[h] = s0 * gam[:, sl] + _mm(u_t, b_h[:, sl]) + _mm(vt_h, k_h[:, sl])
            o_ref[lo_r:hi_r, sl] = o_h

    out = o_ref[...]
    inv_n = 1.0 / RWKV_HEAD_DIM
    mean = _dot01_right(out, ones) * inv_n
    cen = out - mean
    var = _dot01_right(cen * cen, ones) * inv_n
    out = cen * lax.rsqrt(var + RWKV_LN_EPS) * ln_ref[...]
    bonus = _dot01_right(r * k2 * rk_ref[...], ones) * v
    y_ref[...] = ((out + bonus) * g).astype(BF16)


def _rwkv(u, mu, w0, a0, k_k, k_a, r_k, ln, lora, tri, ones):
    lp = u.shape[0]
    full = lambda shape: pl.BlockSpec(shape, lambda i: (0,) * len(shape))
    vec = full((1, RWKV_WIDTH))
    return pl.pallas_call(
        _rwkv_kernel,
        grid=(lp // CHUNK,),
        in_specs=[pl.BlockSpec((CHUNK, RWKV_IN), lambda i: (i, 0)), full((1, RWKV_IN)),
                  vec, vec, vec, vec, vec, vec, full((3, 128, RWKV_WIDTH)), full((CHUNK, CHUNK)),
                  full((RWKV_WIDTH, RWKV_WIDTH))],
        out_specs=pl.BlockSpec((CHUNK, RWKV_WIDTH), lambda i: (i, 0)),
        out_shape=jax.ShapeDtypeStruct((lp, RWKV_WIDTH), BF16),
        scratch_shapes=[pltpu.VMEM((8, RWKV_IN), F32),
                        pltpu.VMEM((RWKV_HEADS, RWKV_HEAD_DIM, RWKV_HEAD_DIM), F32),
                        pltpu.VMEM((CHUNK, RWKV_WIDTH), F32)],
        compiler_params=_cparams("arbitrary"),
        name="rwkv7",
    )(u, mu, w0, a0, k_k, k_a, r_k, ln, lora, tri, ones)


def _constants():
    r = jnp.arange(CHUNK)
    tri = (r[:, None] >= r[None, :]).astype(BF16)
    lane = jnp.arange(RWKV_WIDTH)
    ones_bd = ((lane[:, None] // RWKV_HEAD_DIM) == (lane[None, :] // RWKV_HEAD_DIM)).astype(BF16)
    e8 = ((jnp.arange(128)[:, None] == (lane[None, :] // SSD_HEAD_DIM))).astype(BF16)
    return tri, ones_bd, e8


def kernel(x, meta_tokens, ffn1_norm, ffn1_w_gate, ffn1_w_up, ffn1_w_down, mix_norm, w_in, w_out, mla_q_norm, mla_w_q_up, mla_kv_norm, mla_w_kv_up, mla_qk_norm_q, mla_qk_norm_k, ssd_conv_w, ssd_conv_b, ssd_dt_bias, ssd_a_log, ssd_d, ssd_norm, ret_norm, rwkv_mu, rwkv_w0, rwkv_w2, rwkv_a0, rwkv_a2, rwkv_g2, rwkv_k_k, rwkv_k_a, rwkv_r_k, rwkv_ln, ffn2_norm, ffn2_w_gate, ffn2_w_up, ffn2_w_down):
    b, seq, d = x.shape
    assert b == 1 and d == D_MODEL and seq % CHUNK == 0
    nl = w_in.shape[0]
    lp = seq + CHUNK
    assert lp % TOKEN_TILE == 0

    hst = jnp.concatenate([jnp.zeros((PAD, d), x.dtype), meta_tokens.astype(x.dtype), x[0]], axis=0)

    o0 = MLA_IN
    o1 = o0 + SSD_IN
    o2 = o1 + RET_IN
    w_mla = jnp.pad(w_in[:, :, :o0], ((0, 0), (0, 0), (0, MLA_IN_PAD - MLA_IN))).astype(BF16)
    w_ssd = jnp.pad(w_in[:, :, o0:o1], ((0, 0), (0, 0), (0, SSD_IN_PAD - SSD_IN))).astype(BF16)
    w_ret = w_in[:, :, o1:o2].astype(BF16)
    w_rwkv = w_in[:, :, o2:].astype(BF16)
    w_o = w_out.reshape(nl, 4, 512, D_MODEL).astype(BF16)
    wq = jnp.pad(mla_w_q_up.reshape(nl, MLA_Q_LORA, MLA_HEADS, MLA_QK),
                 ((0, 0), (0, 0), (0, 0), (0, MLA_QK_PAD - MLA_QK))
                 ).reshape(nl, MLA_Q_LORA, MLA_HEADS * MLA_QK_PAD).astype(BF16)
    wkv = mla_w_kv_up.reshape(nl, MLA_KV_LORA, MLA_HEADS, 2, MLA_NOPE).transpose(0, 1, 3, 2, 4
                              ).reshape(nl, MLA_KV_LORA, 2 * MLA_HEADS * MLA_NOPE).astype(BF16)
    gq = jnp.pad(mla_qk_norm_q, ((0, 0), (0, MLA_QK_PAD - MLA_QK)))[:, None, :]
    gk = jnp.pad(mla_qk_norm_k, ((0, 0), (0, MLA_QK_PAD - MLA_QK)))[:, None, :]
    dtb = jnp.pad(ssd_dt_bias, ((0, 0), (0, 128 - SSD_HEADS)))[:, None, :]
    alog = jnp.pad(ssd_a_log, ((0, 0), (0, 128 - SSD_HEADS)))[:, None, :]
    dsk = jnp.repeat(ssd_d, SSD_HEAD_DIM, axis=1)[:, None, :]
    lora = jnp.zeros((nl, 3, 128, RWKV_WIDTH), F32)
    lora = lora.at[:, 0, :RWKV_DECAY_LORA].set(rwkv_w2)
    lora = lora.at[:, 1, RWKV_DECAY_LORA:RWKV_DECAY_LORA + RWKV_A_LORA].set(rwkv_a2)
    lora = lora.at[:, 2, RWKV_DECAY_LORA + RWKV_A_LORA:].set(rwkv_g2)
    lora = lora.astype(BF16)
    f1g, f1u, f1d = ffn1_w_gate.astype(BF16), ffn1_w_up.astype(BF16), ffn1_w_down.astype(BF16)
    f2g, f2u, f2d = ffn2_w_gate.astype(BF16), ffn2_w_up.astype(BF16), ffn2_w_down.astype(BF16)

    tri, ones_bd, e8 = _constants()
    cos, sin, sinm = _rope_tables(lp)

    for l in range(nl):
        hst = _ffn(hst, ffn1_norm[l][None, :], f1g[l], f1u[l], f1d[l])
        gmix = mix_norm[l][None, :]
        q, k, v = _mla_prep(hst, gmix, w_mla[l], mla_q_norm[l][None, :], wq[l], mla_kv_norm[l][None, :],
                            wkv[l], gq[l], gk[l], cos, sinm)
        y_mla = _attention(q, k, v)
        u_ssd = _inproj(hst, gmix, w_ssd[l])
        y_ssd = _ssd(u_ssd, ssd_conv_w[l], ssd_conv_b[l][None, :], dtb[l], alog[l], dsk[l],
                     ssd_norm[l][None, :], tri, e8)
        u_ret = _inproj(hst, gmix, w_ret[l])
        y_ret = _retention(u_ret, cos, sin, ret_norm[l].reshape(1, RET_WIDTH))
        u_rwkv = _inproj(hst, gmix, w_rwkv[l])
        y_rwkv = _rwkv(u_rwkv, rwkv_mu[l][None, :], rwkv_w0[l][None, :], rwkv_a0[l][None, :],
                       rwkv_k_k[l][None, :], rwkv_k_a[l][None, :], rwkv_r_k[l].reshape(1, RWKV_WIDTH),
                       rwkv_ln[l].reshape(1, RWKV_WIDTH), lora[l], tri, ones_bd)
        hst = _outproj(hst, (y_mla, y_ssd, y_ret, y_rwkv), w_o[l])
        hst = _ffn(hst, ffn2_norm[l][None, :], f2g[l], f2u[l], f2d[l])
    return hst[CHUNK:][None]
```

```python
import functools
import math

import jax
import jax.numpy as jnp
from jax import lax
from jax.experimental import pallas as pl
from jax.experimental.pallas import tpu as pltpu

F32 = jnp.float32
BF16 = jnp.bfloat16

D_MODEL = 2048
DEPTH = 4
N_META = 16
CHUNK = 128
PAD = CHUNK - N_META
D_FF = 5632
EPS = 1e-6
ROPE_THETA = 10000.0
NEG_INF = -1e30

MLA_HEADS = 4
MLA_NOPE = 128
MLA_ROPE = 64
MLA_QK = MLA_NOPE + MLA_ROPE
MLA_V = 128
MLA_Q_LORA = 384
MLA_KV_LORA = 128
MLA_QK_PAD = 256
MLA_IN = MLA_Q_LORA + MLA_KV_LORA + MLA_ROPE
MLA_IN_PAD = 640

SSD_HEADS = 8
SSD_HEAD_DIM = 64
SSD_WIDTH = 512
SSD_GROUPS = 2
SSD_STATE = 128
SSD_CONV = 4
SSD_CONV_CH = SSD_WIDTH + 2 * SSD_GROUPS * SSD_STATE
SSD_IN = SSD_WIDTH + SSD_CONV_CH + SSD_HEADS
SSD_IN_PAD = SSD_WIDTH + SSD_CONV_CH + 128

RET_HEADS = 4
RET_DK = 64
RET_DV = 128
RET_WIDTH = 512
RET_IN = 2 * RET_HEADS * RET_DK + 2 * RET_WIDTH

RWKV_HEADS = 8
RWKV_HEAD_DIM = 64
RWKV_WIDTH = 512
RWKV_DECAY_LORA = 32
RWKV_A_LORA = 32
RWKV_GATE_LORA = 64
RWKV_LN_EPS = 64e-5
RWKV_IN = 3 * RWKV_WIDTH + RWKV_DECAY_LORA + RWKV_A_LORA + RWKV_GATE_LORA
RWKV_SUB = 64
RWKV_INV_BLOCK = 16

MIX_WIDTH = 2048

V7X_VMEM_LIMIT_BYTES = 56 * 1024 * 1024
TOKEN_TILE = 640
FF_TILE = 512


def _cparams(*sem):
    return pltpu.CompilerParams(dimension_semantics=sem, vmem_limit_bytes=V7X_VMEM_LIMIT_BYTES)


def _sigmoid(x):
    return 1.0 / (1.0 + jnp.exp(-x))


def _silu(x):
    return x * _sigmoid(x)


def _softplus(x):
    return jnp.maximum(x, 0.0) + jnp.log(1.0 + jnp.exp(-jnp.abs(x)))


def _rms(x, g, eps=EPS):
    return x * lax.rsqrt(jnp.mean(x * x, axis=-1, keepdims=True) + eps) * g


def _mm(a, b):
    return jnp.dot(a.astype(BF16), b.astype(BF16), preferred_element_type=F32)


def _mm_nt(a, b):
    return lax.dot_general(a.astype(BF16), b.astype(BF16), (((1,), (1,)), ((), ())),
                           preferred_element_type=F32)


def _split3(x):
    x1 = x.astype(BF16)
    r1 = x - x1.astype(F32)
    x2 = r1.astype(BF16)
    x3 = (r1 - x2.astype(F32)).astype(BF16)
    return x1, x2, x3


def _dot01_right(x, m01):
    p1, p2, p3 = _split3(x)
    return (jnp.dot(p1, m01, preferred_element_type=F32)
            + jnp.dot(p2, m01, preferred_element_type=F32)
            + jnp.dot(p3, m01, preferred_element_type=F32))


def _dot01_left(m01, x):
    p1, p2, p3 = _split3(x)
    return (jnp.dot(m01, p1, preferred_element_type=F32)
            + jnp.dot(m01, p2, preferred_element_type=F32)
            + jnp.dot(m01, p3, preferred_element_type=F32))


def _row_ids(rows, cols, base):
    return base + lax.broadcasted_iota(jnp.int32, (rows, cols), 0)


def _rope_table_kernel(inv_ref, cos_ref, sin_ref, sinm_ref):
    i = pl.program_id(0)
    pos = (_row_ids(CHUNK, 128, i * CHUNK) - PAD).astype(F32)
    lane = lax.broadcasted_iota(jnp.int32, (CHUNK, 128), 1)
    ang = pos * inv_ref[...]
    c = jnp.cos(ang)
    s = jnp.sin(ang)
    s = jnp.where((lane % 64) < 32, -s, s)
    cos_ref[...] = c
    sin_ref[...] = s
    sinm_ref[...] = jnp.where(lane < 64, s, 0.0)


def _rope_tables(lp):
    half = 32
    inv = ROPE_THETA ** (-jnp.arange(half, dtype=F32) / half)
    inv = jnp.tile(inv, 4)[None, :]
    out = jax.ShapeDtypeStruct((lp, 128), F32)
    spec = pl.BlockSpec((CHUNK, 128), lambda i: (i, 0))
    return pl.pallas_call(
        _rope_table_kernel,
        grid=(lp // CHUNK,),
        in_specs=[pl.BlockSpec((1, 128), lambda i: (0, 0))],
        out_specs=[spec, spec, spec],
        out_shape=[out, out, out],
        compiler_params=_cparams("parallel"),
        name="rope_tables",
    )(inv)


def _rope_lanes(x, cos, sin_signed):
    lane = lax.broadcasted_iota(jnp.int32, x.shape, 1)
    fwd = pltpu.roll(x, 32, axis=1)
    bwd = pltpu.roll(x, 96, axis=1)
    rot = jnp.where((lane % 64) < 32, bwd, fwd)
    return x * cos + rot * sin_signed


def _ffn_kernel(x_ref, g_ref, wg_ref, wu_ref, wd_ref, o_ref, xn_ref, acc_ref):
    j = pl.program_id(1)

    @pl.when(j == 0)
    def _():
        xn_ref[...] = _rms(x_ref[...], g_ref[...]).astype(BF16)
        acc_ref[...] = jnp.zeros_like(acc_ref)

    xn = xn_ref[...]
    a = jnp.dot(xn, wg_ref[...], preferred_element_type=F32)
    b = jnp.dot(xn, wu_ref[...], preferred_element_type=F32)
    mid = (_silu(a) * b).astype(BF16)
    acc_ref[...] += jnp.dot(mid, wd_ref[...], preferred_element_type=F32)

    @pl.when(j == pl.num_programs(1) - 1)
    def _():
        o_ref[...] = x_ref[...] + 0.5 * acc_ref[...]


def _ffn(h, g, wg, wu, wd):
    lp = h.shape[0]
    tm, tf = TOKEN_TILE, FF_TILE
    return pl.pallas_call(
        _ffn_kernel,
        grid=(lp // tm, D_FF // tf),
        in_specs=[
            pl.BlockSpec((tm, D_MODEL), lambda i, j: (i, 0)),
            pl.BlockSpec((1, D_MODEL), lambda i, j: (0, 0)),
            pl.BlockSpec((D_MODEL, tf), lambda i, j: (0, j)),
            pl.BlockSpec((D_MODEL, tf), lambda i, j: (0, j)),
            pl.BlockSpec((tf, D_MODEL), lambda i, j: (j, 0)),
        ],
        out_specs=pl.BlockSpec((tm, D_MODEL), lambda i, j: (i, 0)),
        out_shape=jax.ShapeDtypeStruct((lp, D_MODEL), F32),
        scratch_shapes=[pltpu.VMEM((tm, D_MODEL), BF16), pltpu.VMEM((tm, D_MODEL), F32)],
        compiler_params=_cparams("parallel", "arbitrary"),
        name="ffn",
    )(h, g, wg, wu, wd)


def _inproj_kernel(x_ref, g_ref, w_ref, o_ref):
    i = pl.program_id(0)
    tm = x_ref.shape[0]
    xn = _rms(x_ref[...], g_ref[...]).astype(BF16)
    u = jnp.dot(xn, w_ref[...], preferred_element_type=F32)
    row = _row_ids(tm, 1, i * tm)
    o_ref[...] = jnp.where(row >= PAD, u, 0.0)


def _inproj(h, g, w):
    lp = h.shape[0]
    n = w.shape[1]
    tm = TOKEN_TILE
    return pl.pallas_call(
        _inproj_kernel,
        grid=(lp // tm,),
        in_specs=[
            pl.BlockSpec((tm, D_MODEL), lambda i: (i, 0)),
            pl.BlockSpec((1, D_MODEL), lambda i: (0, 0)),
            pl.BlockSpec((D_MODEL, n), lambda i: (0, 0)),
        ],
        out_specs=pl.BlockSpec((tm, n), lambda i: (i, 0)),
        out_shape=jax.ShapeDtypeStruct((lp, n), F32),
        compiler_params=_cparams("parallel"),
        name="inproj",
    )(h, g, w)


def _outproj_kernel(h_ref, y0_ref, y1_ref, y2_ref, y3_ref, w_ref, o_ref):
    acc = h_ref[...]
    for m, y_ref in enumerate((y0_ref, y1_ref, y2_ref, y3_ref)):
        acc = acc + jnp.dot(y_ref[...], w_ref[m], preferred_element_type=F32)
    o_ref[...] = acc


def _outproj(h, ys, w):
    lp = h.shape[0]
    tm = TOKEN_TILE
    yspec = pl.BlockSpec((tm, 512), lambda i: (i, 0))
    return pl.pallas_call(
        _outproj_kernel,
        grid=(lp // tm,),
        in_specs=[pl.BlockSpec((tm, D_MODEL), lambda i: (i, 0)), yspec, yspec, yspec, yspec,
                  pl.BlockSpec((4, 512, D_MODEL), lambda i: (0, 0, 0))],
        out_specs=pl.BlockSpec((tm, D_MODEL), lambda i: (i, 0)),
        out_shape=jax.ShapeDtypeStruct((lp, D_MODEL), F32),
        compiler_params=_cparams("parallel"),
        name="outproj",
    )(h, *ys, w)


def _mla_prep_kernel(x_ref, g_ref, win_ref, qn_ref, wq_ref, kvn_ref, wkv_ref, gq_ref, gk_ref,
                     cos_ref, sinm_ref, q_ref, k_ref, v_ref):
    i = pl.program_id(0)
    tm = x_ref.shape[0]
    xn = _rms(x_ref[...], g_ref[...]).astype(BF16)
    u = jnp.dot(xn, win_ref[...], preferred_element_type=F32)
    row = _row_ids(tm, 1, i * tm)
    u = jnp.where(row >= PAD, u, 0.0)
    cq = u[:, :MLA_Q_LORA]
    ckv = u[:, MLA_Q_LORA:MLA_Q_LORA + MLA_KV_LORA]
    kpe = u[:, MLA_Q_LORA + MLA_KV_LORA:]
    q_all = _mm(_rms(cq, qn_ref[...]), wq_ref[...])
    kv_all = _mm(_rms(ckv, kvn_ref[...]), wkv_ref[...])
    cos = cos_ref[...]
    sinm = sinm_ref[...]
    gq = gq_ref[...]
    gk = gk_ref[...]
    scale = MLA_QK ** -0.5
    for h in range(MLA_HEADS):
        qh = q_all[:, h * MLA_QK_PAD:(h + 1) * MLA_QK_PAD]
        ss = jnp.sum(qh * qh, axis=-1, keepdims=True) * (1.0 / MLA_QK)
        qh = qh * lax.rsqrt(ss + EPS) * gq
        q_rot = _rope_lanes(qh[:, MLA_NOPE:], cos, sinm)
        q_ref[h, :, :MLA_NOPE] = (qh[:, :MLA_NOPE] * scale).astype(BF16)
        q_ref[h, :, MLA_NOPE:] = (q_rot * scale).astype(BF16)

        kn = kv_all[:, h * MLA_NOPE:(h + 1) * MLA_NOPE]
        ss = (jnp.sum(kn * kn, axis=-1, keepdims=True)
              + jnp.sum(kpe * kpe, axis=-1, keepdims=True)) * (1.0 / MLA_QK)
        rs = lax.rsqrt(ss + EPS)
        k_rot = _rope_lanes(kpe * rs * gk[:, MLA_NOPE:], cos, sinm)
        k_ref[h, :, :MLA_NOPE] = (kn * rs * gk[:, :MLA_NOPE]).astype(BF16)
        k_ref[h, :, MLA_NOPE:] = k_rot.astype(BF16)
        v_ref[h] = kv_all[:, 4 * MLA_NOPE + h * MLA_V: 4 * MLA_NOPE + (h + 1) * MLA_V].astype(BF16)


def _mla_prep(h, g, win, qn, wq, kvn, wkv, gq, gk, cos, sinm):
    lp = h.shape[0]
    tm = TOKEN_TILE
    full = lambda shape: pl.BlockSpec(shape, lambda i: (0,) * len(shape))
    return pl.pallas_call(
        _mla_prep_kernel,
        grid=(lp // tm,),
        in_specs=[
            pl.BlockSpec((tm, D_MODEL), lambda i: (i, 0)),
            full((1, D_MODEL)), full((D_MODEL, MLA_IN_PAD)),
            full((1, MLA_Q_LORA)), full((MLA_Q_LORA, MLA_HEADS * MLA_QK_PAD)),
            full((1, MLA_KV_LORA)), full((MLA_KV_LORA, 2 * MLA_HEADS * MLA_NOPE)),
            full((1, MLA_QK_PAD)), full((1, MLA_QK_PAD)),
            pl.BlockSpec((tm, 128), lambda i: (i, 0)),
            pl.BlockSpec((tm, 128), lambda i: (i, 0)),
        ],
        out_specs=[
            pl.BlockSpec((MLA_HEADS, tm, MLA_QK_PAD), lambda i: (0, i, 0)),
            pl.BlockSpec((MLA_HEADS, tm, MLA_QK_PAD), lambda i: (0, i, 0)),
            pl.BlockSpec((MLA_HEADS, tm, MLA_V), lambda i: (0, i, 0)),
        ],
        out_shape=[
            jax.ShapeDtypeStruct((MLA_HEADS, lp, MLA_QK_PAD), BF16),
            jax.ShapeDtypeStruct((MLA_HEADS, lp, MLA_QK_PAD), BF16),
            jax.ShapeDtypeStruct((MLA_HEADS, lp, MLA_V), BF16),
        ],
        compiler_params=_cparams("parallel"),
        name="mla_prep",
    )(h, g, win, qn, wq, kvn, wkv, gq, gk, cos, sinm)


def _attn_kernel(q_ref, k_ref, v_ref, o_ref):
    i = pl.program_id(1)
    tq = q_ref.shape[1]
    q = q_ref[0]
    row = _row_ids(tq, tq, i * tq)
    col0 = lax.broadcasted_iota(jnp.int32, (tq, tq), 1)

    def block(j, carry, masked):
        m, l, acc = carry
        start = pl.multiple_of(j * tq, tq)
        k = k_ref[0, pl.ds(start, tq), :]
        v = v_ref[0, pl.ds(start, tq), :]
        s = lax.dot_general(q, k, (((1,), (1,)), ((), ())), preferred_element_type=F32)
        if masked:
            col = col0 + j * tq
            s = jnp.where(col <= row, jnp.where(col >= PAD, s, NEG_INF), NEG_INF)
        m_new = jnp.maximum(m, jnp.max(s, axis=-1, keepdims=True))
        alpha = jnp.exp(m - m_new)
        p = jnp.exp(s - m_new)
        l = alpha * l + jnp.sum(p, axis=-1, keepdims=True)
        acc = alpha * acc + jnp.dot(p.astype(BF16), v, preferred_element_type=F32)
        return m_new, l, acc

    init = (jnp.full((tq, 1), NEG_INF, F32), jnp.zeros((tq, 1), F32), jnp.zeros((tq, MLA_V), F32))
    carry = block(0, init, True)
    carry = lax.fori_loop(1, i, lambda j, c: block(j, c, False), carry)
    m, l, acc = lax.cond(i > 0, lambda c: block(i, c, True), lambda c: c, carry)
    o_ref[...] = (acc / l).astype(BF16)


def _attention(q, k, v):
    lp = q.shape[1]
    tq = TOKEN_TILE
    return pl.pallas_call(
        _attn_kernel,
        grid=(MLA_HEADS, lp // tq),
        in_specs=[
            pl.BlockSpec((1, tq, MLA_QK_PAD), lambda h, i: (h, i, 0)),
            pl.BlockSpec((1, lp, MLA_QK_PAD), lambda h, i: (h, 0, 0)),
            pl.BlockSpec((1, lp, MLA_V), lambda h, i: (h, 0, 0)),
        ],
        out_specs=pl.BlockSpec((tq, MLA_V), lambda h, i: (i, h)),
        out_shape=jax.ShapeDtypeStruct((lp, MLA_HEADS * MLA_V), BF16),
        compiler_params=_cparams("parallel", "arbitrary"),
        name="mla_attention",
    )(q, k, v)


def _shift_rows(x, carry, s):
    rolled = pltpu.roll(x, s, axis=0)
    head = pltpu.roll(carry, s, axis=0)
    r8 = lax.broadcasted_iota(jnp.int32, (8, 1), 0)
    return jnp.concatenate([jnp.where(r8 < s, head, rolled[:8]), rolled[8:]], axis=0)


def _ssd_kernel(u_ref, cw_ref, cb_ref, dtb_ref, alog_ref, dsk_ref, ng_ref, tri_ref, e8_ref,
                y_ref, carry_ref, state_ref):
    i = pl.program_id(0)

    @pl.when(i == 0)
    def _():
        carry_ref[...] = jnp.zeros_like(carry_ref)
        state_ref[...] = jnp.zeros_like(state_ref)

    z = u_ref[:, :SSD_WIDTH]
    xbc = u_ref[:, SSD_WIDTH:SSD_WIDTH + SSD_CONV_CH]
    dt_raw = u_ref[:, SSD_WIDTH + SSD_CONV_CH:]
    carry = carry_ref[...]
    cw = cw_ref[...]
    conv = xbc * cw[3:4, :]
    for s in (1, 2, 3):
        conv = conv + _shift_rows(xbc, carry, s) * cw[3 - s:4 - s, :]
    carry_ref[...] = xbc[CHUNK - 8:, :]
    xbc = _silu(conv + cb_ref[...])
    xs = xbc[:, :SSD_WIDTH]
    bm = xbc[:, SSD_WIDTH:SSD_WIDTH + SSD_GROUPS * SSD_STATE]
    cm = xbc[:, SSD_WIDTH + SSD_GROUPS * SSD_STATE:]

    row = _row_ids(CHUNK, 1, i * CHUNK)
    dt = _softplus(dt_raw + dtb_ref[...]) * jnp.where(row >= PAD, 1.0, 0.0)
    la = dt * (-jnp.exp(alog_ref[...]))
    tri = tri_ref[...]
    e8 = e8_ref[...]
    cs = _dot01_left(tri, la)
    cs_e = _dot01_right(cs, e8)
    dt_e = _dot01_right(dt, e8)
    cs_t = cs.T
    cs_last_e = cs_e[CHUNK - 1:CHUNK, :]
    x = xs * dt_e
    xd = x * jnp.exp(cs_last_e - cs_e)
    ecs = jnp.exp(cs_e)
    dec = jnp.exp(cs_last_e)

    r_i = lax.broadcasted_iota(jnp.int32, (CHUNK, CHUNK), 0)
    c_i = lax.broadcasted_iota(jnp.int32, (CHUNK, CHUNK), 1)
    causal = r_i >= c_i
    per = SSD_HEADS // SSD_GROUPS
    gw = per * SSD_HEAD_DIM
    y_parts = []
    for g in range(SSD_GROUPS):
        b_g = bm[:, g * SSD_STATE:(g + 1) * SSD_STATE]
        c_g = cm[:, g * SSD_STATE:(g + 1) * SSD_STATE]
        scores = _mm_nt(c_g, b_g)
        s_prev = state_ref[g]
        y_off = _mm(c_g, s_prev) * ecs[:, g * gw:(g + 1) * gw]
        state_ref[g] = s_prev * dec[:, g * gw:(g + 1) * gw] + _mm(b_g.T, xd[:, g * gw:(g + 1) * gw])
        for hh in range(per):
            h = g * per + hh
            seg = cs[:, h:h + 1] - cs_t[h:h + 1, :]
            lmat = jnp.where(causal, jnp.exp(jnp.where(causal, seg, 0.0)), 0.0)
            y_parts.append(_mm(scores * lmat, x[:, h * SSD_HEAD_DIM:(h + 1) * SSD_HEAD_DIM])
                           + y_off[:, hh * SSD_HEAD_DIM:(hh + 1) * SSD_HEAD_DIM])
    y = jnp.concatenate(y_parts, axis=1) + dsk_ref[...] * xs
    y = y * _silu(z)
    y_ref[...] = _rms(y, ng_ref[...]).astype(BF16)


def _ssd(u, cw, cb, dtb, alog, dsk, ng, tri, e8):
    lp = u.shape[0]
    full = lambda shape: pl.BlockSpec(shape, lambda i: (0,) * len(shape))
    return pl.pallas_call(
        _ssd_kernel,
        grid=(lp // CHUNK,),
        in_specs=[pl.BlockSpec((CHUNK, SSD_IN_PAD), lambda i: (i, 0)),
                  full((SSD_CONV, SSD_CONV_CH)), full((1, SSD_CONV_CH)), full((1, 128)), full((1, 128)),
                  full((1, SSD_WIDTH)), full((1, SSD_WIDTH)), full((CHUNK, CHUNK)), full((128, SSD_WIDTH))],
        out_specs=pl.BlockSpec((CHUNK, SSD_WIDTH), lambda i: (i, 0)),
        out_shape=jax.ShapeDtypeStruct((lp, SSD_WIDTH), BF16),
        scratch_shapes=[pltpu.VMEM((8, SSD_CONV_CH), F32),
                        pltpu.VMEM((SSD_GROUPS, SSD_STATE, SSD_WIDTH // SSD_GROUPS), F32)],
        compiler_params=_cparams("arbitrary"),
        name="ssd",
    )(u, cw, cb, dtb, alog, dsk, ng, tri, e8)


def _ret_kernel(u_ref, cos_ref, sin_ref, ng_ref, y_ref, dmat_ref, state_ref):
    i = pl.program_id(0)
    log_g = [math.log(1.0 - 2.0 ** (-5.0 - h)) for h in range(RET_HEADS)]

    @pl.when(i == 0)
    def _():
        state_ref[...] = jnp.zeros_like(state_ref)
        r_i = lax.broadcasted_iota(jnp.int32, (CHUNK, CHUNK), 0)
        c_i = lax.broadcasted_iota(jnp.int32, (CHUNK, CHUNK), 1)
        diff = (r_i - c_i).astype(F32)
        for h in range(RET_HEADS):
            dmat_ref[h] = jnp.where(r_i >= c_i, jnp.exp(jnp.where(r_i >= c_i, diff, 0.0) * log_g[h]), 0.0)

    qkw = RET_HEADS * RET_DK
    cos = cos_ref[...]
    sin = sin_ref[...]
    q = jnp.concatenate([_rope_lanes(u_ref[:, c * 128:(c + 1) * 128], cos, sin) for c in range(2)], axis=1)
    k = jnp.concatenate([_rope_lanes(u_ref[:, qkw + c * 128:qkw + (c + 1) * 128], cos, sin)
                         for c in range(2)], axis=1) * (RET_DK ** -0.5)
    idx = lax.broadcasted_iota(jnp.int32, (CHUNK, 1), 0).astype(F32)
    k_t = k.T
    idx_row = lax.broadcasted_iota(jnp.int32, (1, CHUNK), 1).astype(F32)
    for h in range(RET_HEADS):
        q_h = q[:, h * RET_DK:(h + 1) * RET_DK]
        k_h = k[:, h * RET_DK:(h + 1) * RET_DK]
        v_h = u_ref[:, 2 * qkw + h * RET_DV: 2 * qkw + (h + 1) * RET_DV]
        g_h = u_ref[:, 2 * qkw + RET_WIDTH + h * RET_DV: 2 * qkw + RET_WIDTH + (h + 1) * RET_DV]
        sc = _mm_nt(q_h, k_h) * dmat_ref[h]
        s_prev = state_ref[h]
        q_dec = jnp.exp((idx + 1.0) * log_g[h])
        y = _mm(sc, v_h) + _mm(q_h * q_dec, s_prev)
        k_dec_row = jnp.exp((CHUNK - 1 - idx_row) * log_g[h])
        kd_t = k_t[h * RET_DK:(h + 1) * RET_DK, :] * k_dec_row
        state_ref[h] = s_prev * math.exp(CHUNK * log_g[h]) + _mm(kd_t, v_h)
        mu = jnp.mean(y, axis=-1, keepdims=True)
        var = jnp.mean(jnp.square(y - mu), axis=-1, keepdims=True)
        yn = (y - mu) * lax.rsqrt(var + EPS) * ng_ref[:, h * RET_DV:(h + 1) * RET_DV]
        y_ref[:, h * RET_DV:(h + 1) * RET_DV] = (_silu(g_h) * yn).astype(BF16)


def _retention(u, cos, sin, ng):
    lp = u.shape[0]
    return pl.pallas_call(
        _ret_kernel,
        grid=(lp // CHUNK,),
        in_specs=[pl.BlockSpec((CHUNK, RET_IN), lambda i: (i, 0)),
                  pl.BlockSpec((CHUNK, 128), lambda i: (i, 0)),
                  pl.BlockSpec((CHUNK, 128), lambda i: (i, 0)),
                  pl.BlockSpec((1, RET_WIDTH), lambda i: (0, 0))],
        out_specs=pl.BlockSpec((CHUNK, RET_WIDTH), lambda i: (i, 0)),
        out_shape=jax.ShapeDtypeStruct((lp, RET_WIDTH), BF16),
        scratch_shapes=[pltpu.VMEM((RET_HEADS, CHUNK, CHUNK), F32),
                        pltpu.VMEM((RET_HEADS, RET_DK, RET_DV), F32)],
        compiler_params=_cparams("arbitrary"),
        name="retention",
    )(u, cos, sin, ng)


def _inv_unit_upper_many(mats, in_block, eye):
    d = [jnp.where(in_block, a, 0.0) for a in mats]
    f = [a - x for a, x in zip(mats, d)]
    d2 = [_mm(x, x) for x in d]
    d4 = [_mm(x, x) for x in d2]
    p = [_mm(eye + x, eye + y) for x, y in zip(d, d2)]
    d8 = [_mm(x, x) for x in d4]
    p = [_mm(x, eye + y) for x, y in zip(p, d4)]
    td = [_mm(x, eye + y) for x, y in zip(p, d8)]
    g = [_mm(x, y) for x, y in zip(f, td)]
    g2 = [_mm(x, x) for x in g]
    tg = [_mm(x, eye + y) for x, y in zip(td, g)]
    return [_mm(x, eye + y) for x, y in zip(tg, g2)]


def _rwkv_kernel(u_ref, mu_ref, w0_ref, a0_ref, kk_ref, ka_ref, rk_ref, ln_ref, lora_ref, tri_ref,
                 ones_ref, y_ref, prev_ref, state_ref, o_ref):
    i = pl.program_id(0)

    @pl.when(i == 0)
    def _():
        prev_ref[...] = jnp.zeros_like(prev_ref)
        state_ref[...] = jnp.zeros_like(state_ref)

    u = u_ref[...]
    rows = lax.broadcasted_iota(jnp.int32, (CHUNK, 1), 0)
    u_prev = jnp.where(rows == 0, prev_ref[7:8, :], pltpu.roll(u, 1, axis=0))
    prev_ref[...] = u[CHUNK - 8:, :]
    us = u + (u_prev - u) * mu_ref[...]
    w3 = RWKV_WIDTH
    r = us[:, :w3]
    k = us[:, w3:2 * w3]
    v = us[:, 2 * w3:3 * w3]
    lo = us[:, 3 * w3:]
    w = w0_ref[...] + _mm(jnp.tanh(lo), lora_ref[0])
    w = -_softplus(-w) - 0.5
    ld = -jnp.exp(w)
    a = _sigmoid(a0_ref[...] + _mm(lo, lora_ref[1]))
    g = _mm(_sigmoid(lo), lora_ref[2])
    ones = ones_ref[...]
    kk = k * kk_ref[...]
    k2 = k * (1.0 + (a - 1.0) * ka_ref[...])
    kk = kk / jnp.maximum(jnp.sqrt(_dot01_right(kk * kk, ones)), 1e-12)
    b = kk * a

    lc_full = _dot01_left(tri_ref[...], ld)
    v_t = v.T

    n = RWKV_SUB
    r_i = lax.broadcasted_iota(jnp.int32, (n, n), 0)
    c_i = lax.broadcasted_iota(jnp.int32, (n, n), 1)
    strict_upper = r_i < c_i
    lower_incl = r_i >= c_i
    in_block = (r_i // RWKV_INV_BLOCK) == (c_i // RWKV_INV_BLOCK)
    eye = jnp.where(r_i == c_i, 1.0, 0.0)
    hd = RWKV_HEAD_DIM
    n_sub = CHUNK // n
    heads = range(RWKV_HEADS)
    hs = [slice(h * hd, (h + 1) * hd) for h in heads]

    sc = []
    for sub in range(n_sub):
        lo_r, hi_r = sub * n, (sub + 1) * n
        lc = lc_full[lo_r:hi_r, :]
        if sub > 0:
            lc = lc - lc_full[lo_r - 1:lo_r, :]
        lc_last = lc[n - 1:n, :]
        e_pos = jnp.exp(lc)
        e_neg = jnp.exp(-lc)
        e_prev = jnp.exp(lc - ld[lo_r:hi_r, :])
        e_end = jnp.exp(lc_last - lc)
        sc.append(dict(
            gam=jnp.exp(lc_last),
            a_t=(-kk[lo_r:hi_r, :] * e_prev).astype(BF16),
            r_t=(r[lo_r:hi_r, :] * e_pos).astype(BF16),
            b_t=(b[lo_r:hi_r, :] * e_neg).astype(BF16),
            k_t=(k2[lo_r:hi_r, :] * e_neg).astype(BF16),
            b_h=(b[lo_r:hi_r, :] * e_end).astype(BF16),
            k_h=(k2[lo_r:hi_r, :] * e_end).astype(BF16),
            v_s=v[lo_r:hi_r, :].astype(BF16),
            v_t=v_t[:, lo_r:hi_r].astype(BF16),
        ))

    pairs = [(sub, h) for sub in range(n_sub) for h in heads]
    bk = [jnp.concatenate([sc[sub]["b_t"][:, hs[h]], sc[sub]["k_t"][:, hs[h]]], axis=0) for sub, h in pairs]
    at = [_mm_nt(x, sc[sub]["a_t"][:, hs[h]]) for x, (sub, h) in zip(bk, pairs)]
    ar = [_mm_nt(sc[sub]["r_t"][:, hs[h]], x) for x, (sub, h) in zip(bk, pairs)]
    aab_t = [jnp.where(strict_upper, x[:n, :], 0.0) for x in at]
    aak_t = [jnp.where(strict_upper, x[n:, :], 0.0) for x in at]
    a_rb = [jnp.where(lower_incl, x[:, :n], 0.0) for x in ar]
    a_rk = [jnp.where(lower_incl, x[:, n:], 0.0) for x in ar]
    xv = [_mm(sc[sub]["v_t"][hs[h], :], x) for x, (sub, h) in zip(aak_t, pairs)]
    ov = [_mm(x, sc[sub]["v_s"][:, hs[h]]) for x, (sub, h) in zip(a_rk, pairs)]
    sv = [_mm(sc[sub]["v_t"][hs[h], :], sc[sub]["k_h"][:, hs[h]]) for sub, h in pairs]
    t_t = _inv_unit_upper_many(aab_t, in_block, eye)

    state = [state_ref[h] for h in heads]
    for sub in range(n_sub):
        c = sc[sub]
        base = sub * RWKV_HEADS
        x = [_mm_nt(state[h], c["a_t"][:, hs[h]]) + xv[base + h] for h in heads]
        u_t = [_mm(x[h], t_t[base + h]) for h in heads]
        o_rs = [_mm_nt(c["r_t"][:, hs[h]], state[h]) for h in heads]
        o_bu = [_mm_nt(a_rb[base + h], u_t[h]) for h in heads]
        new_state = [state[h] * c["gam"][:, hs[h]] + _mm(u_t[h], c["b_h"][:, hs[h]]) + sv[base + h]
                     for h in heads]
        for h in heads:
            o_ref[sub * n:(sub + 1) * n, hs[h]] = o_rs[h] + o_bu[h] + ov[base + h]
        state = new_state
    for h in heads:
        state_ref[h] = state[h]

    out = o_ref[...]
    inv_n = 1.0 / RWKV_HEAD_DIM
    mean = _dot01_right(out, ones) * inv_n
    cen = out - mean
    var = _dot01_right(cen * cen, ones) * inv_n
    out = cen * lax.rsqrt(var + RWKV_LN_EPS) * ln_ref[...]
    bonus = _dot01_right(r * k2 * rk_ref[...], ones) * v
    y_ref[...] = ((out + bonus) * g).astype(BF16)


def _rwkv(u, mu, w0, a0, k_k, k_a, r_k, ln, lora, tri, ones):
    lp = u.shape[0]
    full = lambda shape: pl.BlockSpec(shape, lambda i: (0,) * len(shape))
    vec = full((1, RWKV_WIDTH))
    return pl.pallas_call(
        _rwkv_kernel,
        grid=(lp // CHUNK,),
        in_specs=[pl.BlockSpec((CHUNK, RWKV_IN), lambda i: (i, 0)), full((1, RWKV_IN)),
                  vec, vec, vec, vec, vec, vec, full((3, 128, RWKV_WIDTH)), full((CHUNK, CHUNK)),
                  full((RWKV_WIDTH, RWKV_WIDTH))],
        out_specs=pl.BlockSpec((CHUNK, RWKV_WIDTH), lambda i: (i, 0)),
        out_shape=jax.ShapeDtypeStruct((lp, RWKV_WIDTH), BF16),
        scratch_shapes=[pltpu.VMEM((8, RWKV_IN), F32),
                        pltpu.VMEM((RWKV_HEADS, RWKV_HEAD_DIM, RWKV_HEAD_DIM), F32),
                        pltpu.VMEM((CHUNK, RWKV_WIDTH), F32)],
        compiler_params=_cparams("arbitrary"),
        name="rwkv7",
    )(u, mu, w0, a0, k_k, k_a, r_k, ln, lora, tri, ones)


def _constants():
    r = jnp.arange(CHUNK)
    tri = (r[:, None] >= r[None, :]).astype(BF16)
    lane = jnp.arange(RWKV_WIDTH)
    ones_bd = ((lane[:, None] // RWKV_HEAD_DIM) == (lane[None, :] // RWKV_HEAD_DIM)).astype(BF16)
    e8 = ((jnp.arange(128)[:, None] == (lane[None, :] // SSD_HEAD_DIM))).astype(BF16)
    return tri, ones_bd, e8


def kernel(x, meta_tokens, ffn1_norm, ffn1_w_gate, ffn1_w_up, ffn1_w_down, mix_norm, w_in, w_out, mla_q_norm, mla_w_q_up, mla_kv_norm, mla_w_kv_up, mla_qk_norm_q, mla_qk_norm_k, ssd_conv_w, ssd_conv_b, ssd_dt_bias, ssd_a_log, ssd_d, ssd_norm, ret_norm, rwkv_mu, rwkv_w0, rwkv_w2, rwkv_a0, rwkv_a2, rwkv_g2, rwkv_k_k, rwkv_k_a, rwkv_r_k, rwkv_ln, ffn2_norm, ffn2_w_gate, ffn2_w_up, ffn2_w_down):
    b, seq, d = x.shape
    assert b == 1 and d == D_MODEL and seq % CHUNK == 0
    nl = w_in.shape[0]
    lp = seq + CHUNK
    assert lp % TOKEN_TILE == 0

    hst = jnp.concatenate([jnp.zeros((PAD, d), x.dtype), meta_tokens.astype(x.dtype), x[0]], axis=0)

    o0 = MLA_IN
    o1 = o0 + SSD_IN
    o2 = o1 + RET_IN
    w_mla = jnp.pad(w_in[:, :, :o0], ((0, 0), (0, 0), (0, MLA_IN_PAD - MLA_IN))).astype(BF16)
    w_ssd = jnp.pad(w_in[:, :, o0:o1], ((0, 0), (0, 0), (0, SSD_IN_PAD - SSD_IN))).astype(BF16)
    w_ret = w_in[:, :, o1:o2].astype(BF16)
    w_rwkv = w_in[:, :, o2:].astype(BF16)
    w_o = w_out.reshape(nl, 4, 512, D_MODEL).astype(BF16)
    wq = jnp.pad(mla_w_q_up.reshape(nl, MLA_Q_LORA, MLA_HEADS, MLA_QK),
                 ((0, 0), (0, 0), (0, 0), (0, MLA_QK_PAD - MLA_QK))
                 ).reshape(nl, MLA_Q_LORA, MLA_HEADS * MLA_QK_PAD).astype(BF16)
    wkv = mla_w_kv_up.reshape(nl, MLA_KV_LORA, MLA_HEADS, 2, MLA_NOPE).transpose(0, 1, 3, 2, 4
                              ).reshape(nl, MLA_KV_LORA, 2 * MLA_HEADS * MLA_NOPE).astype(BF16)
    gq = jnp.pad(mla_qk_norm_q, ((0, 0), (0, MLA_QK_PAD - MLA_QK)))[:, None, :]
    gk = jnp.pad(mla_qk_norm_k, ((0, 0), (0, MLA_QK_PAD - MLA_QK)))[:, None, :]
    dtb = jnp.pad(ssd_dt_bias, ((0, 0), (0, 128 - SSD_HEADS)))[:, None, :]
    alog = jnp.pad(ssd_a_log, ((0, 0), (0, 128 - SSD_HEADS)))[:, None, :]
    dsk = jnp.repeat(ssd_d, SSD_HEAD_DIM, axis=1)[:, None, :]
    lora = jnp.zeros((nl, 3, 128, RWKV_WIDTH), F32)
    lora = lora.at[:, 0, :RWKV_DECAY_LORA].set(rwkv_w2)
    lora = lora.at[:, 1, RWKV_DECAY_LORA:RWKV_DECAY_LORA + RWKV_A_LORA].set(rwkv_a2)
    lora = lora.at[:, 2, RWKV_DECAY_LORA + RWKV_A_LORA:].set(rwkv_g2)
    lora = lora.astype(BF16)
    f1g, f1u, f1d = ffn1_w_gate.astype(BF16), ffn1_w_up.astype(BF16), ffn1_w_down.astype(BF16)
    f2g, f2u, f2d = ffn2_w_gate.astype(BF16), ffn2_w_up.astype(BF16), ffn2_w_down.astype(BF16)

    tri, ones_bd, e8 = _constants()
    cos, sin, sinm = _rope_tables(lp)

    for l in range(nl):
        hst = _ffn(hst, ffn1_norm[l][None, :], f1g[l], f1u[l], f1d[l])
        gmix = mix_norm[l][None, :]
        q, k, v = _mla_prep(hst, gmix, w_mla[l], mla_q_norm[l][None, :], wq[l], mla_kv_norm[l][None, :],
                            wkv[l], gq[l], gk[l], cos, sinm)
        y_mla = _attention(q, k, v)
        u_ssd = _inproj(hst, gmix, w_ssd[l])
        y_ssd = _ssd(u_ssd, ssd_conv_w[l], ssd_conv_b[l][None, :], dtb[l], alog[l], dsk[l],
                     ssd_norm[l][None, :], tri, e8)
        u_ret = _inproj(hst, gmix, w_ret[l])
        y_ret = _retention(u_ret, cos, sin, ret_norm[l].reshape(1, RET_WIDTH))
        u_rwkv = _inproj(hst, gmix, w_rwkv[l])
        y_rwkv = _rwkv(u_rwkv, rwkv_mu[l][None, :], rwkv_w0[l][None, :], rwkv_a0[l][None, :],
                       rwkv_k_k[l][None, :], rwkv_k_a[l][None, :], rwkv_r_k[l].reshape(1, RWKV_WIDTH),
                       rwkv_ln[l].reshape(1, RWKV_WIDTH), lora[l], tri, ones_bd)
        hst = _outproj(hst, (y_mla, y_ssd, y_ret, y_rwkv), w_o[l])
        hst = _ffn(hst, ffn2_norm[l][None, :], f2g[l], f2u[l], f2d[l])
    return hst[CHUNK:][None]
```

```python
import functools
import math

import jax
import jax.numpy as jnp
from jax import lax
from jax.experimental import pallas as pl
from jax.experimental.pallas import tpu as pltpu

F32 = jnp.float32
BF16 = jnp.bfloat16

D_MODEL = 2048
DEPTH = 4
N_META = 16
CHUNK = 128
PAD = CHUNK - N_META
D_FF = 5632
EPS = 1e-6
ROPE_THETA = 10000.0
NEG_INF = -1e30

MLA_HEADS = 4
MLA_NOPE = 128
MLA_ROPE = 64
MLA_QK = MLA_NOPE + MLA_ROPE
MLA_V = 128
MLA_Q_LORA = 384
MLA_KV_LORA = 128
MLA_QK_PAD = 256
MLA_V_AUG = MLA_V + 16
MLA_IN = MLA_Q_LORA + MLA_KV_LORA + MLA_ROPE
MLA_IN_PAD = 640

SSD_HEADS = 8
SSD_HEAD_DIM = 64
SSD_WIDTH = 512
SSD_GROUPS = 2
SSD_STATE = 128
SSD_CONV = 4
SSD_CONV_CH = SSD_WIDTH + 2 * SSD_GROUPS * SSD_STATE
SSD_IN = SSD_WIDTH + SSD_CONV_CH + SSD_HEADS
SSD_IN_PAD = SSD_WIDTH + SSD_CONV_CH + 128

RET_HEADS = 4
RET_DK = 64
RET_DV = 128
RET_WIDTH = 512
RET_IN = 2 * RET_HEADS * RET_DK + 2 * RET_WIDTH

RWKV_HEADS = 8
RWKV_HEAD_DIM = 64
RWKV_WIDTH = 512
RWKV_DECAY_LORA = 32
RWKV_A_LORA = 32
RWKV_GATE_LORA = 64
RWKV_LN_EPS = 64e-5
RWKV_IN = 3 * RWKV_WIDTH + RWKV_DECAY_LORA + RWKV_A_LORA + RWKV_GATE_LORA
RWKV_SUB = 64
RWKV_INV_BLOCK = 16

MIX_WIDTH = 2048

V7X_VMEM_LIMIT_BYTES = 56 * 1024 * 1024
TOKEN_TILE = 640
FF_TILE = 512


def _cparams(*sem):
    return pltpu.CompilerParams(dimension_semantics=sem, vmem_limit_bytes=V7X_VMEM_LIMIT_BYTES)


def _sigmoid(x):
    return 1.0 / (1.0 + jnp.exp(-x))


def _silu(x):
    return x * _sigmoid(x)


def _softplus(x):
    return jnp.maximum(x, 0.0) + jnp.log(1.0 + jnp.exp(-jnp.abs(x)))


def _rms(x, g, eps=EPS):
    return x * lax.rsqrt(jnp.mean(x * x, axis=-1, keepdims=True) + eps) * g


def _mm(a, b):
    return jnp.dot(a.astype(BF16), b.astype(BF16), preferred_element_type=F32)


def _mm_nt(a, b):
    return lax.dot_general(a.astype(BF16), b.astype(BF16), (((1,), (1,)), ((), ())),
                           preferred_element_type=F32)


def _split3(x):
    x1 = x.astype(BF16)
    r1 = x - x1.astype(F32)
    x2 = r1.astype(BF16)
    x3 = (r1 - x2.astype(F32)).astype(BF16)
    return x1, x2, x3


def _dot01_right(x, m01):
    p1, p2, p3 = _split3(x)
    return (jnp.dot(p1, m01, preferred_element_type=F32)
            + jnp.dot(p2, m01, preferred_element_type=F32)
            + jnp.dot(p3, m01, preferred_element_type=F32))


def _dot01_left(m01, x):
    p1, p2, p3 = _split3(x)
    return (jnp.dot(m01, p1, preferred_element_type=F32)
            + jnp.dot(m01, p2, preferred_element_type=F32)
            + jnp.dot(m01, p3, preferred_element_type=F32))


def _row_ids(rows, cols, base):
    return base + lax.broadcasted_iota(jnp.int32, (rows, cols), 0)


def _rope_table_kernel(inv_ref, cos_ref, sin_ref, sinm_ref):
    i = pl.program_id(0)
    pos = (_row_ids(CHUNK, 128, i * CHUNK) - PAD).astype(F32)
    lane = lax.broadcasted_iota(jnp.int32, (CHUNK, 128), 1)
    ang = pos * inv_ref[...]
    c = jnp.cos(ang)
    s = jnp.sin(ang)
    s = jnp.where((lane % 64) < 32, -s, s)
    cos_ref[...] = c
    sin_ref[...] = s
    sinm_ref[...] = jnp.where(lane < 64, s, 0.0)


def _rope_tables(lp):
    half = 32
    inv = ROPE_THETA ** (-jnp.arange(half, dtype=F32) / half)
    inv = jnp.tile(inv, 4)[None, :]
    out = jax.ShapeDtypeStruct((lp, 128), F32)
    spec = pl.BlockSpec((CHUNK, 128), lambda i: (i, 0))
    return pl.pallas_call(
        _rope_table_kernel,
        grid=(lp // CHUNK,),
        in_specs=[pl.BlockSpec((1, 128), lambda i: (0, 0))],
        out_specs=[spec, spec, spec],
        out_shape=[out, out, out],
        compiler_params=_cparams("parallel"),
        name="rope_tables",
    )(inv)


def _rope_lanes(x, cos, sin_signed):
    lane = lax.broadcasted_iota(jnp.int32, x.shape, 1)
    fwd = pltpu.roll(x, 32, axis=1)
    bwd = pltpu.roll(x, 96, axis=1)
    rot = jnp.where((lane % 64) < 32, bwd, fwd)
    return x * cos + rot * sin_signed


def _ffn_kernel(x_ref, g_ref, wg_ref, wu_ref, wd_ref, o_ref, xn_ref, acc_ref):
    j = pl.program_id(1)

    @pl.when(j == 0)
    def _():
        xn_ref[...] = _rms(x_ref[...], g_ref[...]).astype(BF16)
        acc_ref[...] = jnp.zeros_like(acc_ref)

    xn = xn_ref[...]
    a = jnp.dot(xn, wg_ref[...], preferred_element_type=F32)
    b = jnp.dot(xn, wu_ref[...], preferred_element_type=F32)
    mid = (_silu(a) * b).astype(BF16)
    acc_ref[...] += jnp.dot(mid, wd_ref[...], preferred_element_type=F32)

    @pl.when(j == pl.num_programs(1) - 1)
    def _():
        o_ref[...] = x_ref[...] + 0.5 * acc_ref[...]


def _ffn(h, g, wg, wu, wd):
    lp = h.shape[0]
    tm, tf = TOKEN_TILE, FF_TILE
    return pl.pallas_call(
        _ffn_kernel,
        grid=(lp // tm, D_FF // tf),
        in_specs=[
            pl.BlockSpec((tm, D_MODEL), lambda i, j: (i, 0)),
            pl.BlockSpec((1, D_MODEL), lambda i, j: (0, 0)),
            pl.BlockSpec((D_MODEL, tf), lambda i, j: (0, j)),
            pl.BlockSpec((D_MODEL, tf), lambda i, j: (0, j)),
            pl.BlockSpec((tf, D_MODEL), lambda i, j: (j, 0)),
        ],
        out_specs=pl.BlockSpec((tm, D_MODEL), lambda i, j: (i, 0)),
        out_shape=jax.ShapeDtypeStruct((lp, D_MODEL), F32),
        scratch_shapes=[pltpu.VMEM((tm, D_MODEL), BF16), pltpu.VMEM((tm, D_MODEL), F32)],
        compiler_params=_cparams("parallel", "arbitrary"),
        name="ffn",
    )(h, g, wg, wu, wd)


def _inproj_kernel(x_ref, g_ref, w_ref, o_ref):
    i = pl.program_id(0)
    tm = x_ref.shape[0]
    xn = _rms(x_ref[...], g_ref[...]).astype(BF16)
    u = jnp.dot(xn, w_ref[...], preferred_element_type=F32)
    row = _row_ids(tm, 1, i * tm)
    o_ref[...] = jnp.where(row >= PAD, u, 0.0)


def _inproj(h, g, w):
    lp = h.shape[0]
    n = w.shape[1]
    tm = TOKEN_TILE
    return pl.pallas_call(
        _inproj_kernel,
        grid=(lp // tm,),
        in_specs=[
            pl.BlockSpec((tm, D_MODEL), lambda i: (i, 0)),
            pl.BlockSpec((1, D_MODEL), lambda i: (0, 0)),
            pl.BlockSpec((D_MODEL, n), lambda i: (0, 0)),
        ],
        out_specs=pl.BlockSpec((tm, n), lambda i: (i, 0)),
        out_shape=jax.ShapeDtypeStruct((lp, n), F32),
        compiler_params=_cparams("parallel"),
        name="inproj",
    )(h, g, w)


def _outproj_kernel(h_ref, y0_ref, y1_ref, y2_ref, y3_ref, w_ref, o_ref):
    acc = h_ref[...]
    for m, y_ref in enumerate((y0_ref, y1_ref, y2_ref, y3_ref)):
        acc = acc + jnp.dot(y_ref[...], w_ref[m], preferred_element_type=F32)
    o_ref[...] = acc


def _outproj(h, ys, w):
    lp = h.shape[0]
    tm = TOKEN_TILE
    yspec = pl.BlockSpec((tm, 512), lambda i: (i, 0))
    return pl.pallas_call(
        _outproj_kernel,
        grid=(lp // tm,),
        in_specs=[pl.BlockSpec((tm, D_MODEL), lambda i: (i, 0)), yspec, yspec, yspec, yspec,
                  pl.BlockSpec((4, 512, D_MODEL), lambda i: (0, 0, 0))],
        out_specs=pl.BlockSpec((tm, D_MODEL), lambda i: (i, 0)),
        out_shape=jax.ShapeDtypeStruct((lp, D_MODEL), F32),
        compiler_params=_cparams("parallel"),
        name="outproj",
    )(h, *ys, w)


def _mla_prep_kernel(x_ref, g_ref, win_ref, qn_ref, wq_ref, kvn_ref, wkv_ref, gq_ref, gk_ref,
                     cos_ref, sinm_ref, q_ref, k_ref, v_ref):
    i = pl.program_id(0)
    tm = x_ref.shape[0]
    xn = _rms(x_ref[...], g_ref[...]).astype(BF16)
    u = jnp.dot(xn, win_ref[...], preferred_element_type=F32)
    row = _row_ids(tm, 1, i * tm)
    u = jnp.where(row >= PAD, u, 0.0)
    cq = u[:, :MLA_Q_LORA]
    ckv = u[:, MLA_Q_LORA:MLA_Q_LORA + MLA_KV_LORA]
    kpe = u[:, MLA_Q_LORA + MLA_KV_LORA:]
    q_all = _mm(_rms(cq, qn_ref[...]), wq_ref[...])
    kv_all = _mm(_rms(ckv, kvn_ref[...]), wkv_ref[...])
    cos = cos_ref[...]
    sinm = sinm_ref[...]
    gq = gq_ref[...]
    gk = gk_ref[...]
    scale = MLA_QK ** -0.5
    for h in range(MLA_HEADS):
        qh = q_all[:, h * MLA_QK_PAD:(h + 1) * MLA_QK_PAD]
        ss = jnp.sum(qh * qh, axis=-1, keepdims=True) * (1.0 / MLA_QK)
        qh = qh * lax.rsqrt(ss + EPS) * gq
        q_rot = _rope_lanes(qh[:, MLA_NOPE:], cos, sinm)
        q_ref[h, :, :MLA_NOPE] = (qh[:, :MLA_NOPE] * scale).astype(BF16)
        q_ref[h, :, MLA_NOPE:] = (q_rot * scale).astype(BF16)

        kn = kv_all[:, h * MLA_NOPE:(h + 1) * MLA_NOPE]
        ss = (jnp.sum(kn * kn, axis=-1, keepdims=True)
              + jnp.sum(kpe * kpe, axis=-1, keepdims=True)) * (1.0 / MLA_QK)
        rs = lax.rsqrt(ss + EPS)
        k_rot = _rope_lanes(kpe * rs * gk[:, MLA_NOPE:], cos, sinm)
        k_ref[h, :, :MLA_NOPE] = (kn * rs * gk[:, :MLA_NOPE]).astype(BF16)
        k_ref[h, :, MLA_NOPE:] = k_rot.astype(BF16)
        v_ref[h, 0, :MLA_V, :] = kv_all[:, 4 * MLA_NOPE + h * MLA_V: 4 * MLA_NOPE + (h + 1) * MLA_V].T.astype(BF16)
        tail = lax.broadcasted_iota(jnp.int32, (MLA_V_AUG - MLA_V, tm), 0)
        v_ref[h, 0, MLA_V:, :] = jnp.where(tail == 0, 1.0, 0.0).astype(BF16)


def _mla_prep(h, g, win, qn, wq, kvn, wkv, gq, gk, cos, sinm):
    lp = h.shape[0]
    tm = TOKEN_TILE
    full = lambda shape: pl.BlockSpec(shape, lambda i: (0,) * len(shape))
    return pl.pallas_call(
        _mla_prep_kernel,
        grid=(lp // tm,),
        in_specs=[
            pl.BlockSpec((tm, D_MODEL), lambda i: (i, 0)),
            full((1, D_MODEL)), full((D_MODEL, MLA_IN_PAD)),
            full((1, MLA_Q_LORA)), full((MLA_Q_LORA, MLA_HEADS * MLA_QK_PAD)),
            full((1, MLA_KV_LORA)), full((MLA_KV_LORA, 2 * MLA_HEADS * MLA_NOPE)),
            full((1, MLA_QK_PAD)), full((1, MLA_QK_PAD)),
            pl.BlockSpec((tm, 128), lambda i: (i, 0)),
            pl.BlockSpec((tm, 128), lambda i: (i, 0)),
        ],
        out_specs=[
            pl.BlockSpec((MLA_HEADS, tm, MLA_QK_PAD), lambda i: (0, i, 0)),
            pl.BlockSpec((MLA_HEADS, tm, MLA_QK_PAD), lambda i: (0, i, 0)),
            pl.BlockSpec((MLA_HEADS, 1, MLA_V_AUG, tm), lambda i: (0, i, 0, 0)),
        ],
        out_shape=[
            jax.ShapeDtypeStruct((MLA_HEADS, lp, MLA_QK_PAD), BF16),
            jax.ShapeDtypeStruct((MLA_HEADS, lp, MLA_QK_PAD), BF16),
            jax.ShapeDtypeStruct((MLA_HEADS, lp // tm, MLA_V_AUG, tm), BF16),
        ],
        compiler_params=_cparams("parallel"),
        name="mla_prep",
    )(h, g, win, qn, wq, kvn, wkv, gq, gk, cos, sinm)


def _attn_kernel(q_ref, k_ref, vt_ref, o_ref):
    i = pl.program_id(1)
    tq = q_ref.shape[1]
    q = q_ref[0]
    key0 = lax.broadcasted_iota(jnp.int32, (tq, tq), 0)
    qry = i * tq + lax.broadcasted_iota(jnp.int32, (tq, tq), 1)

    def scores(j, masked):
        start = pl.multiple_of(j * tq, tq)
        k = k_ref[0, pl.ds(start, tq), :]
        s = lax.dot_general(k, q, (((1,), (1,)), ((), ())), preferred_element_type=F32)
        if masked:
            key = key0 + j * tq
            s = jnp.where(key <= qry, jnp.where(key >= PAD, s, NEG_INF), NEG_INF)
        return s

    def update(j, s, carry):
        m, acc = carry
        m_new = jnp.maximum(m, jnp.max(s, axis=0, keepdims=True))
        alpha = jnp.exp(m - m_new)
        p = jnp.exp((s - m_new).astype(BF16))
        acc = alpha * acc + jnp.dot(vt_ref[0, j], p, preferred_element_type=F32)
        return m_new, acc

    init = (jnp.full((1, tq), NEG_INF, F32), jnp.zeros((MLA_V_AUG, tq), F32))
    s_first = scores(0, True)

    def body(j, c):
        s_prev, carry = c
        s_new = scores(j, False)
        return s_new, update(j - 1, s_prev, carry)

    s_last, carry = lax.fori_loop(1, i, body, (s_first, init))
    j_last = jnp.maximum(i - 1, 0)
    m, acc = lax.cond(i > 0,
                      lambda c: update(i, scores(i, True), update(j_last, s_last, c)),
                      lambda c: update(j_last, s_last, c), carry)
    o_ref[...] = (acc[:MLA_V] / acc[MLA_V:MLA_V + 1]).T.astype(BF16)


def _attention(q, k, v):
    lp = q.shape[1]
    tq = TOKEN_TILE
    return pl.pallas_call(
        _attn_kernel,
        grid=(MLA_HEADS, lp // tq),
        in_specs=[
            pl.BlockSpec((1, tq, MLA_QK_PAD), lambda h, i: (h, i, 0)),
            pl.BlockSpec((1, lp, MLA_QK_PAD), lambda h, i: (h, 0, 0)),
            pl.BlockSpec((1, lp // tq, MLA_V_AUG, tq), lambda h, i: (h, 0, 0, 0)),
        ],
        out_specs=pl.BlockSpec((tq, MLA_V), lambda h, i: (i, h)),
        out_shape=jax.ShapeDtypeStruct((lp, MLA_HEADS * MLA_V), BF16),
        compiler_params=_cparams("parallel", "arbitrary"),
        name="mla_attention",
    )(q, k, v)


def _shift_rows(x, carry, s):
    rolled = pltpu.roll(x, s, axis=0)
    head = pltpu.roll(carry, s, axis=0)
    r8 = lax.broadcasted_iota(jnp.int32, (8, 1), 0)
    return jnp.concatenate([jnp.where(r8 < s, head, rolled[:8]), rolled[8:]], axis=0)


def _ssd_kernel(u_ref, cw_ref, cb_ref, dtb_ref, alog_ref, dsk_ref, ng_ref, tri_ref, e8_ref,
                y_ref, carry_ref, state_ref):
    i = pl.program_id(0)

    @pl.when(i == 0)
    def _():
        carry_ref[...] = jnp.zeros_like(carry_ref)
        state_ref[...] = jnp.zeros_like(state_ref)

    z = u_ref[:, :SSD_WIDTH]
    xbc = u_ref[:, SSD_WIDTH:SSD_WIDTH + SSD_CONV_CH]
    dt_raw = u_ref[:, SSD_WIDTH + SSD_CONV_CH:]
    carry = carry_ref[...]
    cw = cw_ref[...]
    conv = xbc * cw[3:4, :]
    for s in (1, 2, 3):
        conv = conv + _shift_rows(xbc, carry, s) * cw[3 - s:4 - s, :]
    carry_ref[...] = xbc[CHUNK - 8:, :]
    xbc = _silu(conv + cb_ref[...])
    xs = xbc[:, :SSD_WIDTH]
    bm = xbc[:, SSD_WIDTH:SSD_WIDTH + SSD_GROUPS * SSD_STATE]
    cm = xbc[:, SSD_WIDTH + SSD_GROUPS * SSD_STATE:]

    row = _row_ids(CHUNK, 1, i * CHUNK)
    dt = _softplus(dt_raw + dtb_ref[...]) * jnp.where(row >= PAD, 1.0, 0.0)
    la = dt * (-jnp.exp(alog_ref[...]))
    tri = tri_ref[...]
    e8 = e8_ref[...]
    cs = _dot01_left(tri, la)
    cs_e = _dot01_right(cs, e8)
    dt_e = _dot01_right(dt, e8)
    cs_t = cs.T
    cs_last_e = cs_e[CHUNK - 1:CHUNK, :]
    x = xs * dt_e
    xd = x * jnp.exp(cs_last_e - cs_e)
    ecs = jnp.exp(cs_e)
    dec = jnp.exp(cs_last_e)

    r_i = lax.broadcasted_iota(jnp.int32, (CHUNK, CHUNK), 0)
    c_i = lax.broadcasted_iota(jnp.int32, (CHUNK, CHUNK), 1)
    causal = r_i >= c_i
    per = SSD_HEADS // SSD_GROUPS
    gw = per * SSD_HEAD_DIM
    groups = range(SSD_GROUPS)
    b_gs = [bm[:, g * SSD_STATE:(g + 1) * SSD_STATE] for g in groups]
    c_gs = [cm[:, g * SSD_STATE:(g + 1) * SSD_STATE] for g in groups]
    s_prev = [state_ref[g] for g in groups]
    scores = [_mm_nt(c_gs[g], b_gs[g]) for g in groups]
    y_off = [_mm(c_gs[g], s_prev[g]) for g in groups]
    s_add = [_mm(b_gs[g].T, xd[:, g * gw:(g + 1) * gw]) for g in groups]
    lmats = []
    for h in range(SSD_HEADS):
        seg = cs[:, h:h + 1] - cs_t[h:h + 1, :]
        lmats.append(jnp.where(causal, jnp.exp(jnp.where(causal, seg, 0.0)), 0.0))
    y_diag = [_mm(scores[h // per] * lmats[h], x[:, h * SSD_HEAD_DIM:(h + 1) * SSD_HEAD_DIM])
              for h in range(SSD_HEADS)]
    for g in groups:
        state_ref[g] = s_prev[g] * dec[:, g * gw:(g + 1) * gw] + s_add[g]
    y = (jnp.concatenate(y_diag, axis=1) + jnp.concatenate(y_off, axis=1) * ecs) + dsk_ref[...] * xs
    y = y * _silu(z)
    y_ref[...] = _rms(y, ng_ref[...]).astype(BF16)


def _ssd(u, cw, cb, dtb, alog, dsk, ng, tri, e8):
    lp = u.shape[0]
    full = lambda shape: pl.BlockSpec(shape, lambda i: (0,) * len(shape))
    return pl.pallas_call(
        _ssd_kernel,
        grid=(lp // CHUNK,),
        in_specs=[pl.BlockSpec((CHUNK, SSD_IN_PAD), lambda i: (i, 0)),
                  full((SSD_CONV, SSD_CONV_CH)), full((1, SSD_CONV_CH)), full((1, 128)), full((1, 128)),
                  full((1, SSD_WIDTH)), full((1, SSD_WIDTH)), full((CHUNK, CHUNK)), full((128, SSD_WIDTH))],
        out_specs=pl.BlockSpec((CHUNK, SSD_WIDTH), lambda i: (i, 0)),
        out_shape=jax.ShapeDtypeStruct((lp, SSD_WIDTH), BF16),
        scratch_shapes=[pltpu.VMEM((8, SSD_CONV_CH), F32),
                        pltpu.VMEM((SSD_GROUPS, SSD_STATE, SSD_WIDTH // SSD_GROUPS), F32)],
        compiler_params=_cparams("arbitrary"),
        name="ssd",
    )(u, cw, cb, dtb, alog, dsk, ng, tri, e8)


def _ret_kernel(u_ref, cos_ref, sin_ref, ng_ref, y_ref, dmat_ref, state_ref):
    i = pl.program_id(0)
    log_g = [math.log(1.0 - 2.0 ** (-5.0 - h)) for h in range(RET_HEADS)]

    @pl.when(i == 0)
    def _():
        state_ref[...] = jnp.zeros_like(state_ref)
        r_i = lax.broadcasted_iota(jnp.int32, (CHUNK, CHUNK), 0)
        c_i = lax.broadcasted_iota(jnp.int32, (CHUNK, CHUNK), 1)
        diff = (r_i - c_i).astype(F32)
        for h in range(RET_HEADS):
            dmat_ref[h] = jnp.where(r_i >= c_i, jnp.exp(jnp.where(r_i >= c_i, diff, 0.0) * log_g[h]), 0.0)

    qkw = RET_HEADS * RET_DK
    cos = cos_ref[...]
    sin = sin_ref[...]
    q = jnp.concatenate([_rope_lanes(u_ref[:, c * 128:(c + 1) * 128], cos, sin) for c in range(2)], axis=1)
    k = jnp.concatenate([_rope_lanes(u_ref[:, qkw + c * 128:qkw + (c + 1) * 128], cos, sin)
                         for c in range(2)], axis=1) * (RET_DK ** -0.5)
    idx = lax.broadcasted_iota(jnp.int32, (CHUNK, 1), 0).astype(F32)
    k_t = k.T
    idx_row = lax.broadcasted_iota(jnp.int32, (1, CHUNK), 1).astype(F32)
    heads = range(RET_HEADS)
    q_hs = [q[:, h * RET_DK:(h + 1) * RET_DK] for h in heads]
    k_hs = [k[:, h * RET_DK:(h + 1) * RET_DK] for h in heads]
    v_hs = [u_ref[:, 2 * qkw + h * RET_DV: 2 * qkw + (h + 1) * RET_DV].astype(BF16) for h in heads]
    s_prevs = [state_ref[h] for h in heads]
    scs = [_mm_nt(q_hs[h], k_hs[h]) for h in heads]
    y_cross = [_mm(q_hs[h] * jnp.exp((idx + 1.0) * log_g[h]), s_prevs[h]) for h in heads]
    s_adds = [_mm(k_t[h * RET_DK:(h + 1) * RET_DK, :] * jnp.exp((CHUNK - 1 - idx_row) * log_g[h]), v_hs[h])
              for h in heads]
    y_in = [_mm(scs[h] * dmat_ref[h], v_hs[h]) for h in heads]
    for h in heads:
        g_h = u_ref[:, 2 * qkw + RET_WIDTH + h * RET_DV: 2 * qkw + RET_WIDTH + (h + 1) * RET_DV]
        y = y_in[h] + y_cross[h]
        state_ref[h] = s_prevs[h] * math.exp(CHUNK * log_g[h]) + s_adds[h]
        mu = jnp.mean(y, axis=-1, keepdims=True)
        var = jnp.mean(jnp.square(y - mu), axis=-1, keepdims=True)
        yn = (y - mu) * lax.rsqrt(var + EPS) * ng_ref[:, h * RET_DV:(h + 1) * RET_DV]
        y_ref[:, h * RET_DV:(h + 1) * RET_DV] = (_silu(g_h) * yn).astype(BF16)


def _retention(u, cos, sin, ng):
    lp = u.shape[0]
    return pl.pallas_call(
        _ret_kernel,
        grid=(lp // CHUNK,),
        in_specs=[pl.BlockSpec((CHUNK, RET_IN), lambda i: (i, 0)),
                  pl.BlockSpec((CHUNK, 128), lambda i: (i, 0)),
                  pl.BlockSpec((CHUNK, 128), lambda i: (i, 0)),
                  pl.BlockSpec((1, RET_WIDTH), lambda i: (0, 0))],
        out_specs=pl.BlockSpec((CHUNK, RET_WIDTH), lambda i: (i, 0)),
        out_shape=jax.ShapeDtypeStruct((lp, RET_WIDTH), BF16),
        scratch_shapes=[pltpu.VMEM((RET_HEADS, CHUNK, CHUNK), F32),
                        pltpu.VMEM((RET_HEADS, RET_DK, RET_DV), F32)],
        compiler_params=_cparams("arbitrary"),
        name="retention",
    )(u, cos, sin, ng)


def _inv_unit_upper_many(mats, in_block, eye):
    d = [jnp.where(in_block, a, 0.0) for a in mats]
    f = [a - x for a, x in zip(mats, d)]
    d2 = [_mm(x, x) for x in d]
    d4 = [_mm(x, x) for x in d2]
    p = [_mm(eye + x, eye + y) for x, y in zip(d, d2)]
    d8 = [_mm(x, x) for x in d4]
    p = [_mm(x, eye + y) for x, y in zip(p, d4)]
    td = [_mm(x, eye + y) for x, y in zip(p, d8)]
    g = [_mm(x, y) for x, y in zip(f, td)]
    g2 = [_mm(x, x) for x in g]
    tg = [_mm(x, eye + y) for x, y in zip(td, g)]
    return [_mm(x, eye + y) for x, y in zip(tg, g2)]


def _rwkv_kernel(u_ref, mu_ref, w0_ref, a0_ref, kk_ref, ka_ref, rk_ref, ln_ref, lora_ref, tri_ref,
                 ones_ref, y_ref, prev_ref, state_ref, o_ref):
    i = pl.program_id(0)

    @pl.when(i == 0)
    def _():
        prev_ref[...] = jnp.zeros_like(prev_ref)
        state_ref[...] = jnp.zeros_like(state_ref)

    u = u_ref[...]
    rows = lax.broadcasted_iota(jnp.int32, (CHUNK, 1), 0)
    u_prev = jnp.where(rows == 0, prev_ref[7:8, :], pltpu.roll(u, 1, axis=0))
    prev_ref[...] = u[CHUNK - 8:, :]
    us = u + (u_prev - u) * mu_ref[...]
    w3 = RWKV_WIDTH
    r = us[:, :w3]
    k = us[:, w3:2 * w3]
    v = us[:, 2 * w3:3 * w3]
    lo = us[:, 3 * w3:]
    w = w0_ref[...] + _mm(jnp.tanh(lo), lora_ref[0])
    w = -_softplus(-w) - 0.5
    ld = -jnp.exp(w)
    a = _sigmoid(a0_ref[...] + _mm(lo, lora_ref[1]))
    g = _mm(_sigmoid(lo), lora_ref[2])
    ones = ones_ref[...]
    kk = k * kk_ref[...]
    k2 = k * (1.0 + (a - 1.0) * ka_ref[...])
    kk = kk / jnp.maximum(jnp.sqrt(_dot01_right(kk * kk, ones)), 1e-12)
    b = kk * a

    lc_full = _dot01_left(tri_ref[...], ld)
    v_t = v.T

    n = RWKV_SUB
    r_i = lax.broadcasted_iota(jnp.int32, (n, n), 0)
    c_i = lax.broadcasted_iota(jnp.int32, (n, n), 1)
    strict_upper = r_i < c_i
    lower_incl = r_i >= c_i
    in_block = (r_i // RWKV_INV_BLOCK) == (c_i // RWKV_INV_BLOCK)
    eye = jnp.where(r_i == c_i, 1.0, 0.0)
    hd = RWKV_HEAD_DIM
    n_sub = CHUNK // n
    heads = range(RWKV_HEADS)
    hs = [slice(h * hd, (h + 1) * hd) for h in heads]

    sc = []
    for sub in range(n_sub):
        lo_r, hi_r = sub * n, (sub + 1) * n
        lc = lc_full[lo_r:hi_r, :]
        if sub > 0:
            lc = lc - lc_full[lo_r - 1:lo_r, :]
        lc_last = lc[n - 1:n, :]
        e_pos = jnp.exp(lc)
        e_neg = jnp.exp(-lc)
        e_prev = jnp.exp(lc - ld[lo_r:hi_r, :])
        e_end = jnp.exp(lc_last - lc)
        sc.append(dict(
            gam=jnp.exp(lc_last),
            a_t=(-kk[lo_r:hi_r, :] * e_prev).astype(BF16),
            r_t=(r[lo_r:hi_r, :] * e_pos).astype(BF16),
            b_t=(b[lo_r:hi_r, :] * e_neg).astype(BF16),
            k_t=(k2[lo_r:hi_r, :] * e_neg).astype(BF16),
            b_h=(b[lo_r:hi_r, :] * e_end).astype(BF16),
            k_h=(k2[lo_r:hi_r, :] * e_end).astype(BF16),
            v_s=v[lo_r:hi_r, :].astype(BF16),
            v_t=v_t[:, lo_r:hi_r].astype(BF16),
        ))

    pairs = [(sub, h) for sub in range(n_sub) for h in heads]
    bk = [jnp.concatenate([sc[sub]["b_t"][:, hs[h]], sc[sub]["k_t"][:, hs[h]]], axis=0) for sub, h in pairs]
    at = [_mm_nt(x, sc[sub]["a_t"][:, hs[h]]) for x, (sub, h) in zip(bk, pairs)]
    ar = [_mm_nt(sc[sub]["r_t"][:, hs[h]], x) for x, (sub, h) in zip(bk, pairs)]
    aab_t = [jnp.where(strict_upper, x[:n, :], 0.0) for x in at]
    aak_t = [jnp.where(strict_upper, x[n:, :], 0.0) for x in at]
    a_rb = [jnp.where(lower_incl, x[:, :n], 0.0) for x in ar]
    a_rk = [jnp.where(lower_incl, x[:, n:], 0.0) for x in ar]
    xv = [_mm(sc[sub]["v_t"][hs[h], :], x) for x, (sub, h) in zip(aak_t, pairs)]
    ov = [_mm(x, sc[sub]["v_s"][:, hs[h]]) for x, (sub, h) in zip(a_rk, pairs)]
    sv = [_mm(sc[sub]["v_t"][hs[h], :], sc[sub]["k_h"][:, hs[h]]) for sub, h in pairs]
    t_t = _inv_unit_upper_many(aab_t, in_block, eye)

    state = [state_ref[h] for h in heads]
    for sub in range(n_sub):
        c = sc[sub]
        base = sub * RWKV_HEADS
        x = [_mm_nt(state[h], c["a_t"][:, hs[h]]) + xv[base + h] for h in heads]
        u_t = [_mm(x[h], t_t[base + h]) for h in heads]
        o_rs = [_mm_nt(c["r_t"][:, hs[h]], state[h]) for h in heads]
        o_bu = [_mm_nt(a_rb[base + h], u_t[h]) for h in heads]
        new_state = [state[h] * c["gam"][:, hs[h]] + _mm(u_t[h], c["b_h"][:, hs[h]]) + sv[base + h]
                     for h in heads]
        for h in heads:
            o_ref[sub * n:(sub + 1) * n, hs[h]] = o_rs[h] + o_bu[h] + ov[base + h]
        state = new_state
    for h in heads:
        state_ref[h] = state[h]

    out = o_ref[...]
    inv_n = 1.0 / RWKV_HEAD_DIM
    mean = _dot01_right(out, ones) * inv_n
    cen = out - mean
    var = _dot01_right(cen * cen, ones) * inv_n
    out = cen * lax.rsqrt(var + RWKV_LN_EPS) * ln_ref[...]
    bonus = _dot01_right(r * k2 * rk_ref[...], ones) * v
    y_ref[...] = ((out + bonus) * g).astype(BF16)


def _rwkv(u, mu, w0, a0, k_k, k_a, r_k, ln, lora, tri, ones):
    lp = u.shape[0]
    full = lambda shape: pl.BlockSpec(shape, lambda i: (0,) * len(shape))
    vec = full((1, RWKV_WIDTH))
    return pl.pallas_call(
        _rwkv_kernel,
        grid=(lp // CHUNK,),
        in_specs=[pl.BlockSpec((CHUNK, RWKV_IN), lambda i: (i, 0)), full((1, RWKV_IN)),
                  vec, vec, vec, vec, vec, vec, full((3, 128, RWKV_WIDTH)), full((CHUNK, CHUNK)),
                  full((RWKV_WIDTH, RWKV_WIDTH))],
        out_specs=pl.BlockSpec((CHUNK, RWKV_WIDTH), lambda i: (i, 0)),
        out_shape=jax.ShapeDtypeStruct((lp, RWKV_WIDTH), BF16),
        scratch_shapes=[pltpu.VMEM((8, RWKV_IN), F32),
                        pltpu.VMEM((RWKV_HEADS, RWKV_HEAD_DIM, RWKV_HEAD_DIM), F32),
                        pltpu.VMEM((CHUNK, RWKV_WIDTH), F32)],
        compiler_params=_cparams("arbitrary"),
        name="rwkv7",
    )(u, mu, w0, a0, k_k, k_a, r_k, ln, lora, tri, ones)


def _constants():
    r = jnp.arange(CHUNK)
    tri = (r[:, None] >= r[None, :]).astype(BF16)
    lane = jnp.arange(RWKV_WIDTH)
    ones_bd = ((lane[:, None] // RWKV_HEAD_DIM) == (lane[None, :] // RWKV_HEAD_DIM)).astype(BF16)
    e8 = ((jnp.arange(128)[:, None] == (lane[None, :] // SSD_HEAD_DIM))).astype(BF16)
    return tri, ones_bd, e8


def kernel(x, meta_tokens, ffn1_norm, ffn1_w_gate, ffn1_w_up, ffn1_w_down, mix_norm, w_in, w_out, mla_q_norm, mla_w_q_up, mla_kv_norm, mla_w_kv_up, mla_qk_norm_q, mla_qk_norm_k, ssd_conv_w, ssd_conv_b, ssd_dt_bias, ssd_a_log, ssd_d, ssd_norm, ret_norm, rwkv_mu, rwkv_w0, rwkv_w2, rwkv_a0, rwkv_a2, rwkv_g2, rwkv_k_k, rwkv_k_a, rwkv_r_k, rwkv_ln, ffn2_norm, ffn2_w_gate, ffn2_w_up, ffn2_w_down):
    b, seq, d = x.shape
    assert b == 1 and d == D_MODEL and seq % CHUNK == 0
    nl = w_in.shape[0]
    lp = seq + CHUNK
    assert lp % TOKEN_TILE == 0

    hst = jnp.concatenate([jnp.zeros((PAD, d), x.dtype), meta_tokens.astype(x.dtype), x[0]], axis=0)

    o0 = MLA_IN
    o1 = o0 + SSD_IN
    o2 = o1 + RET_IN
    w_mla = jnp.pad(w_in[:, :, :o0], ((0, 0), (0, 0), (0, MLA_IN_PAD - MLA_IN))).astype(BF16)
    w_ssd = jnp.pad(w_in[:, :, o0:o1], ((0, 0), (0, 0), (0, SSD_IN_PAD - SSD_IN))).astype(BF16)
    w_ret = w_in[:, :, o1:o2].astype(BF16)
    w_rwkv = w_in[:, :, o2:].astype(BF16)
    w_o = w_out.reshape(nl, 4, 512, D_MODEL).astype(BF16)
    wq = jnp.pad(mla_w_q_up.reshape(nl, MLA_Q_LORA, MLA_HEADS, MLA_QK),
                 ((0, 0), (0, 0), (0, 0), (0, MLA_QK_PAD - MLA_QK))
                 ).reshape(nl, MLA_Q_LORA, MLA_HEADS * MLA_QK_PAD).astype(BF16)
    wkv = mla_w_kv_up.reshape(nl, MLA_KV_LORA, MLA_HEADS, 2, MLA_NOPE).transpose(0, 1, 3, 2, 4
                              ).reshape(nl, MLA_KV_LORA, 2 * MLA_HEADS * MLA_NOPE).astype(BF16)
    gq = jnp.pad(mla_qk_norm_q, ((0, 0), (0, MLA_QK_PAD - MLA_QK)))[:, None, :]
    gk = jnp.pad(mla_qk_norm_k, ((0, 0), (0, MLA_QK_PAD - MLA_QK)))[:, None, :]
    dtb = jnp.pad(ssd_dt_bias, ((0, 0), (0, 128 - SSD_HEADS)))[:, None, :]
    alog = jnp.pad(ssd_a_log, ((0, 0), (0, 128 - SSD_HEADS)))[:, None, :]
    dsk = jnp.repeat(ssd_d, SSD_HEAD_DIM, axis=1)[:, None, :]
    lora = jnp.zeros((nl, 3, 128, RWKV_WIDTH), F32)
    lora = lora.at[:, 0, :RWKV_DECAY_LORA].set(rwkv_w2)
    lora = lora.at[:, 1, RWKV_DECAY_LORA:RWKV_DECAY_LORA + RWKV_A_LORA].set(rwkv_a2)
    lora = lora.at[:, 2, RWKV_DECAY_LORA + RWKV_A_LORA:].set(rwkv_g2)
    lora = lora.astype(BF16)
    f1g, f1u, f1d = ffn1_w_gate.astype(BF16), ffn1_w_up.astype(BF16), ffn1_w_down.astype(BF16)
    f2g, f2u, f2d = ffn2_w_gate.astype(BF16), ffn2_w_up.astype(BF16), ffn2_w_down.astype(BF16)

    tri, ones_bd, e8 = _constants()
    cos, sin, sinm = _rope_tables(lp)

    for l in range(nl):
        hst = _ffn(hst, ffn1_norm[l][None, :], f1g[l], f1u[l], f1d[l])
        gmix = mix_norm[l][None, :]
        q, k, v = _mla_prep(hst, gmix, w_mla[l], mla_q_norm[l][None, :], wq[l], mla_kv_norm[l][None, :],
                            wkv[l], gq[l], gk[l], cos, sinm)
        y_mla = _attention(q, k, v)
        u_ssd = _inproj(hst, gmix, w_ssd[l])
        y_ssd = _ssd(u_ssd, ssd_conv_w[l], ssd_conv_b[l][None, :], dtb[l], alog[l], dsk[l],
                     ssd_norm[l][None, :], tri, e8)
        u_ret = _inproj(hst, gmix, w_ret[l])
        y_ret = _retention(u_ret, cos, sin, ret_norm[l].reshape(1, RET_WIDTH))
        u_rwkv = _inproj(hst, gmix, w_rwkv[l])
        y_rwkv = _rwkv(u_rwkv, rwkv_mu[l][None, :], rwkv_w0[l][None, :], rwkv_a0[l][None, :],
                       rwkv_k_k[l][None, :], rwkv_k_a[l][None, :], rwkv_r_k[l].reshape(1, RWKV_WIDTH),
                       rwkv_ln[l].reshape(1, RWKV_WIDTH), lora[l], tri, ones_bd)
        hst = _outproj(hst, (y_mla, y_ssd, y_ret, y_rwkv), w_o[l])
        hst = _ffn(hst, ffn2_norm[l][None, :], f2g[l], f2u[l], f2d[l])
    return hst[CHUNK:][None]
```

```python
import functools
import math

import jax
import jax.numpy as jnp
from jax import lax
from jax.experimental import pallas as pl
from jax.experimental.pallas import tpu as pltpu

F32 = jnp.float32
BF16 = jnp.bfloat16

D_MODEL = 2048
DEPTH = 4
N_META = 16
CHUNK = 128
PAD = CHUNK - N_META
D_FF = 5632
EPS = 1e-6
ROPE_THETA = 10000.0
NEG_INF = -1e30

MLA_HEADS = 4
MLA_NOPE = 128
MLA_ROPE = 64
MLA_QK = MLA_NOPE + MLA_ROPE
MLA_V = 128
MLA_Q_LORA = 384
MLA_KV_LORA = 128
MLA_QK_PAD = 256
MLA_V_AUG = MLA_V + 16
MLA_IN = MLA_Q_LORA + MLA_KV_LORA + MLA_ROPE
MLA_IN_PAD = 640

SSD_HEADS = 8
SSD_HEAD_DIM = 64
SSD_WIDTH = 512
SSD_GROUPS = 2
SSD_STATE = 128
SSD_CONV = 4
SSD_CONV_CH = SSD_WIDTH + 2 * SSD_GROUPS * SSD_STATE
SSD_IN = SSD_WIDTH + SSD_CONV_CH + SSD_HEADS
SSD_IN_PAD = SSD_WIDTH + SSD_CONV_CH + 128

RET_HEADS = 4
RET_DK = 64
RET_DV = 128
RET_WIDTH = 512
RET_IN = 2 * RET_HEADS * RET_DK + 2 * RET_WIDTH

RWKV_HEADS = 8
RWKV_HEAD_DIM = 64
RWKV_WIDTH = 512
RWKV_DECAY_LORA = 32
RWKV_A_LORA = 32
RWKV_GATE_LORA = 64
RWKV_LN_EPS = 64e-5
RWKV_IN = 3 * RWKV_WIDTH + RWKV_DECAY_LORA + RWKV_A_LORA + RWKV_GATE_LORA
RWKV_SUB = 64
RWKV_INV_BLOCK = 16

MIX_WIDTH = 2048

V7X_VMEM_LIMIT_BYTES = 56 * 1024 * 1024
TOKEN_TILE = 640
FF_TILE = 512
FF_TILE_CAST = 256


def _cparams(*sem):
    return pltpu.CompilerParams(dimension_semantics=sem, vmem_limit_bytes=V7X_VMEM_LIMIT_BYTES)


def _sigmoid(x):
    return 1.0 / (1.0 + jnp.exp(-x))


def _silu(x):
    return x * _sigmoid(x)


def _softplus(x):
    return jnp.maximum(x, 0.0) + jnp.log(1.0 + jnp.exp(-jnp.abs(x)))


def _rms(x, g, eps=EPS):
    return x * lax.rsqrt(jnp.mean(x * x, axis=-1, keepdims=True) + eps) * g


def _mm(a, b):
    return jnp.dot(a.astype(BF16), b.astype(BF16), preferred_element_type=F32)


def _mm_nt(a, b):
    return lax.dot_general(a.astype(BF16), b.astype(BF16), (((1,), (1,)), ((), ())),
                           preferred_element_type=F32)


def _split3(x):
    x1 = x.astype(BF16)
    r1 = x - x1.astype(F32)
    x2 = r1.astype(BF16)
    x3 = (r1 - x2.astype(F32)).astype(BF16)
    return x1, x2, x3


def _dot01_right(x, m01):
    p1, p2, p3 = _split3(x)
    return (jnp.dot(p1, m01, preferred_element_type=F32)
            + jnp.dot(p2, m01, preferred_element_type=F32)
            + jnp.dot(p3, m01, preferred_element_type=F32))


def _dot01_left(m01, x):
    p1, p2, p3 = _split3(x)
    return (jnp.dot(m01, p1, preferred_element_type=F32)
            + jnp.dot(m01, p2, preferred_element_type=F32)
            + jnp.dot(m01, p3, preferred_element_type=F32))


def _row_ids(rows, cols, base):
    return base + lax.broadcasted_iota(jnp.int32, (rows, cols), 0)


def _rope_table_kernel(inv_ref, cos_ref, sin_ref, sinm_ref):
    i = pl.program_id(0)
    pos = (_row_ids(CHUNK, 128, i * CHUNK) - PAD).astype(F32)
    lane = lax.broadcasted_iota(jnp.int32, (CHUNK, 128), 1)
    ang = pos * inv_ref[...]
    c = jnp.cos(ang)
    s = jnp.sin(ang)
    s = jnp.where((lane % 64) < 32, -s, s)
    cos_ref[...] = c
    sin_ref[...] = s
    sinm_ref[...] = jnp.where(lane < 64, s, 0.0)


def _rope_tables(lp):
    half = 32
    inv = ROPE_THETA ** (-jnp.arange(half, dtype=F32) / half)
    inv = jnp.tile(inv, 4)[None, :]
    out = jax.ShapeDtypeStruct((lp, 128), F32)
    spec = pl.BlockSpec((CHUNK, 128), lambda i: (i, 0))
    return pl.pallas_call(
        _rope_table_kernel,
        grid=(lp // CHUNK,),
        in_specs=[pl.BlockSpec((1, 128), lambda i: (0, 0))],
        out_specs=[spec, spec, spec],
        out_shape=[out, out, out],
        compiler_params=_cparams("parallel"),
        name="rope_tables",
    )(inv)


def _rope_lanes(x, cos, sin_signed):
    lane = lax.broadcasted_iota(jnp.int32, x.shape, 1)
    fwd = pltpu.roll(x, 32, axis=1)
    bwd = pltpu.roll(x, 96, axis=1)
    rot = jnp.where((lane % 64) < 32, bwd, fwd)
    return x * cos + rot * sin_signed


def _ffn_kernel(*refs, cast_weights):
    if cast_weights:
        x_ref, g_ref, wg_ref, wu_ref, wd_ref, o_ref, wgb_ref, wub_ref, wdb_ref, xn_ref, acc_ref = refs
    else:
        x_ref, g_ref, wg_ref, wu_ref, wd_ref, _, o_ref, xn_ref, acc_ref = refs
    j = pl.program_id(1)

    @pl.when(j == 0)
    def _():
        xn_ref[...] = _rms(x_ref[...], g_ref[...]).astype(BF16)
        acc_ref[...] = jnp.zeros_like(acc_ref)

    wg, wu, wd = wg_ref[...], wu_ref[...], wd_ref[...]
    if cast_weights:
        wg, wu, wd = wg.astype(BF16), wu.astype(BF16), wd.astype(BF16)
        wgb_ref[...] = wg
        wub_ref[...] = wu
        wdb_ref[...] = wd
    xn = xn_ref[...]
    a = jnp.dot(xn, wg, preferred_element_type=F32)
    b = jnp.dot(xn, wu, preferred_element_type=F32)
    mid = (_silu(a) * b).astype(BF16)
    acc_ref[...] += jnp.dot(mid, wd, preferred_element_type=F32)

    @pl.when(j == pl.num_programs(1) - 1)
    def _():
        o_ref[...] = x_ref[...] + 0.5 * acc_ref[...]


def _ffn(h, g, wg, wu, wd, l):
    lp = h.shape[0]
    tm = TOKEN_TILE
    scratch = [pltpu.VMEM((tm, D_MODEL), BF16), pltpu.VMEM((tm, D_MODEL), F32)]
    tf = FF_TILE_CAST
    first, wgb, wub, wdb = pl.pallas_call(
        functools.partial(_ffn_kernel, cast_weights=True),
        grid=(1, D_FF // tf),
        in_specs=[
            pl.BlockSpec((tm, D_MODEL), lambda i, j: (0, 0)),
            pl.BlockSpec((1, D_MODEL), lambda i, j: (0, 0)),
            pl.BlockSpec((None, D_MODEL, tf), lambda i, j: (l, 0, j)),
            pl.BlockSpec((None, D_MODEL, tf), lambda i, j: (l, 0, j)),
            pl.BlockSpec((None, tf, D_MODEL), lambda i, j: (l, j, 0)),
        ],
        out_specs=[
            pl.BlockSpec((tm, D_MODEL), lambda i, j: (0, 0)),
            pl.BlockSpec((D_MODEL, tf), lambda i, j: (0, j)),
            pl.BlockSpec((D_MODEL, tf), lambda i, j: (0, j)),
            pl.BlockSpec((tf, D_MODEL), lambda i, j: (j, 0)),
        ],
        out_shape=[
            jax.ShapeDtypeStruct((lp, D_MODEL), F32),
            jax.ShapeDtypeStruct((D_MODEL, D_FF), BF16),
            jax.ShapeDtypeStruct((D_MODEL, D_FF), BF16),
            jax.ShapeDtypeStruct((D_FF, D_MODEL), BF16),
        ],
        scratch_shapes=scratch,
        compiler_params=_cparams("arbitrary", "arbitrary"),
        name="ffn_first",
    )(h, g, wg, wu, wd)
    tf = FF_TILE
    return pl.pallas_call(
        functools.partial(_ffn_kernel, cast_weights=False),
        grid=(lp // tm - 1, D_FF // tf),
        in_specs=[
            pl.BlockSpec((tm, D_MODEL), lambda i, j: (i + 1, 0)),
            pl.BlockSpec((1, D_MODEL), lambda i, j: (0, 0)),
            pl.BlockSpec((D_MODEL, tf), lambda i, j: (0, j)),
            pl.BlockSpec((D_MODEL, tf), lambda i, j: (0, j)),
            pl.BlockSpec((tf, D_MODEL), lambda i, j: (j, 0)),
            pl.BlockSpec(memory_space=pl.ANY),
        ],
        out_specs=pl.BlockSpec((tm, D_MODEL), lambda i, j: (i + 1, 0)),
        out_shape=jax.ShapeDtypeStruct((lp, D_MODEL), F32),
        input_output_aliases={5: 0},
        scratch_shapes=scratch,
        compiler_params=_cparams("parallel", "arbitrary"),
        name="ffn",
    )(h, g, wgb, wub, wdb, first)


def _inproj_kernel(x_ref, g_ref, w_ref, o_ref):
    i = pl.program_id(0)
    tm = x_ref.shape[0]
    xn = _rms(x_ref[...], g_ref[...]).astype(BF16)
    u = jnp.dot(xn, w_ref[...], preferred_element_type=F32)
    row = _row_ids(tm, 1, i * tm)
    o_ref[...] = jnp.where(row >= PAD, u, 0.0)


def _inproj(h, g, w, l):
    lp = h.shape[0]
    n = w.shape[2]
    tm = TOKEN_TILE
    return pl.pallas_call(
        _inproj_kernel,
        grid=(lp // tm,),
        in_specs=[
            pl.BlockSpec((tm, D_MODEL), lambda i: (i, 0)),
            pl.BlockSpec((1, D_MODEL), lambda i: (0, 0)),
            pl.BlockSpec((None, D_MODEL, n), lambda i: (l, 0, 0)),
        ],
        out_specs=pl.BlockSpec((tm, n), lambda i: (i, 0)),
        out_shape=jax.ShapeDtypeStruct((lp, n), F32),
        compiler_params=_cparams("parallel"),
        name="inproj",
    )(h, g, w)


def _outproj_kernel(h_ref, y0_ref, y1_ref, y2_ref, y3_ref, w_ref, o_ref):
    acc = h_ref[...]
    for m, y_ref in enumerate((y0_ref, y1_ref, y2_ref, y3_ref)):
        acc = acc + jnp.dot(y_ref[...], w_ref[m], preferred_element_type=F32)
    o_ref[...] = acc


def _outproj(h, ys, w, l):
    lp = h.shape[0]
    tm = TOKEN_TILE
    yspec = pl.BlockSpec((tm, 512), lambda i: (i, 0))
    return pl.pallas_call(
        _outproj_kernel,
        grid=(lp // tm,),
        in_specs=[pl.BlockSpec((tm, D_MODEL), lambda i: (i, 0)), yspec, yspec, yspec, yspec,
                  pl.BlockSpec((None, 4, 512, D_MODEL), lambda i: (l, 0, 0, 0))],
        out_specs=pl.BlockSpec((tm, D_MODEL), lambda i: (i, 0)),
        out_shape=jax.ShapeDtypeStruct((lp, D_MODEL), F32),
        compiler_params=_cparams("parallel"),
        name="outproj",
    )(h, *ys, w)


def _mla_prep_kernel(x_ref, g_ref, win_ref, qn_ref, wq_ref, kvn_ref, wkv_ref, gq_ref, gk_ref,
                     cos_ref, sinm_ref, q_ref, k_ref, v_ref):
    i = pl.program_id(0)
    tm = x_ref.shape[0]
    xn = _rms(x_ref[...], g_ref[...]).astype(BF16)
    u = jnp.dot(xn, win_ref[...], preferred_element_type=F32)
    row = _row_ids(tm, 1, i * tm)
    u = jnp.where(row >= PAD, u, 0.0)
    cq = u[:, :MLA_Q_LORA]
    ckv = u[:, MLA_Q_LORA:MLA_Q_LORA + MLA_KV_LORA]
    kpe = u[:, MLA_Q_LORA + MLA_KV_LORA:]
    q_all = _mm(_rms(cq, qn_ref[...]), wq_ref[...])
    kv_all = _mm(_rms(ckv, kvn_ref[...]), wkv_ref[...])
    cos = cos_ref[...]
    sinm = sinm_ref[...]
    gq = gq_ref[...]
    gk = gk_ref[...]
    scale = MLA_QK ** -0.5
    for h in range(MLA_HEADS):
        qh = q_all[:, h * MLA_QK_PAD:(h + 1) * MLA_QK_PAD]
        ss = jnp.sum(qh * qh, axis=-1, keepdims=True) * (1.0 / MLA_QK)
        qh = qh * lax.rsqrt(ss + EPS) * gq
        q_rot = _rope_lanes(qh[:, MLA_NOPE:], cos, sinm)
        q_ref[h, :, :MLA_NOPE] = (qh[:, :MLA_NOPE] * scale).astype(BF16)
        q_ref[h, :, MLA_NOPE:] = (q_rot * scale).astype(BF16)

        kn = kv_all[:, h * MLA_NOPE:(h + 1) * MLA_NOPE]
        ss = (jnp.sum(kn * kn, axis=-1, keepdims=True)
              + jnp.sum(kpe * kpe, axis=-1, keepdims=True)) * (1.0 / MLA_QK)
        rs = lax.rsqrt(ss + EPS)
        k_rot = _rope_lanes(kpe * rs * gk[:, MLA_NOPE:], cos, sinm)
        k_ref[h, :, :MLA_NOPE] = (kn * rs * gk[:, :MLA_NOPE]).astype(BF16)
        k_ref[h, :, MLA_NOPE:] = k_rot.astype(BF16)
        v_ref[h, 0, :MLA_V, :] = kv_all[:, 4 * MLA_NOPE + h * MLA_V: 4 * MLA_NOPE + (h + 1) * MLA_V].T.astype(BF16)
        tail = lax.broadcasted_iota(jnp.int32, (MLA_V_AUG - MLA_V, tm), 0)
        v_ref[h, 0, MLA_V:, :] = jnp.where(tail == 0, 1.0, 0.0).astype(BF16)


def _mla_prep(h, g, win, qn, wq, kvn, wkv, gq, gk, cos, sinm, l):
    lp = h.shape[0]
    tm = TOKEN_TILE
    full = lambda shape: pl.BlockSpec(shape, lambda i: (0,) * len(shape))
    layer = lambda shape: pl.BlockSpec((None,) + shape, lambda i: (l,) + (0,) * len(shape))
    return pl.pallas_call(
        _mla_prep_kernel,
        grid=(lp // tm,),
        in_specs=[
            pl.BlockSpec((tm, D_MODEL), lambda i: (i, 0)),
            full((1, D_MODEL)), layer((D_MODEL, MLA_IN_PAD)),
            full((1, MLA_Q_LORA)), layer((MLA_Q_LORA, MLA_HEADS * MLA_QK_PAD)),
            full((1, MLA_KV_LORA)), layer((MLA_KV_LORA, 2 * MLA_HEADS * MLA_NOPE)),
            full((1, MLA_QK_PAD)), full((1, MLA_QK_PAD)),
            pl.BlockSpec((tm, 128), lambda i: (i, 0)),
            pl.BlockSpec((tm, 128), lambda i: (i, 0)),
        ],
        out_specs=[
            pl.BlockSpec((MLA_HEADS, tm, MLA_QK_PAD), lambda i: (0, i, 0)),
            pl.BlockSpec((MLA_HEADS, tm, MLA_QK_PAD), lambda i: (0, i, 0)),
            pl.BlockSpec((MLA_HEADS, 1, MLA_V_AUG, tm), lambda i: (0, i, 0, 0)),
        ],
        out_shape=[
            jax.ShapeDtypeStruct((MLA_HEADS, lp, MLA_QK_PAD), BF16),
            jax.ShapeDtypeStruct((MLA_HEADS, lp, MLA_QK_PAD), BF16),
            jax.ShapeDtypeStruct((MLA_HEADS, lp // tm, MLA_V_AUG, tm), BF16),
        ],
        compiler_params=_cparams("parallel"),
        name="mla_prep",
    )(h, g, win, qn, wq, kvn, wkv, gq, gk, cos, sinm)


def _attn_kernel(q_ref, k_ref, vt_ref, o_ref):
    i = pl.program_id(1)
    tq = q_ref.shape[1]
    q = q_ref[0]
    key0 = lax.broadcasted_iota(jnp.int32, (tq, tq), 0)
    qry = i * tq + lax.broadcasted_iota(jnp.int32, (tq, tq), 1)

    def scores(j, masked):
        start = pl.multiple_of(j * tq, tq)
        k = k_ref[0, pl.ds(start, tq), :]
        s = lax.dot_general(k, q, (((1,), (1,)), ((), ())), preferred_element_type=F32)
        if masked:
            key = key0 + j * tq
            s = jnp.where(key <= qry, jnp.where(key >= PAD, s, NEG_INF), NEG_INF)
        return s

    def update(j, s, carry):
        m, acc = carry
        m_new = jnp.maximum(m, jnp.max(s, axis=0, keepdims=True))
        alpha = jnp.exp(m - m_new)
        p = jnp.exp((s - m_new).astype(BF16))
        acc = alpha * acc + jnp.dot(vt_ref[0, j], p, preferred_element_type=F32)
        return m_new, acc

    init = (jnp.full((1, tq), NEG_INF, F32), jnp.zeros((MLA_V_AUG, tq), F32))
    s_first = scores(0, True)

    def body(j, c):
        s_prev, carry = c
        s_new = scores(j, False)
        return s_new, update(j - 1, s_prev, carry)

    s_last, carry = lax.fori_loop(1, i, body, (s_first, init))
    j_last = jnp.maximum(i - 1, 0)
    m, acc = lax.cond(i > 0,
                      lambda c: update(i, scores(i, True), update(j_last, s_last, c)),
                      lambda c: update(j_last, s_last, c), carry)
    o_ref[...] = (acc[:MLA_V] / acc[MLA_V:MLA_V + 1]).T.astype(BF16)


def _attention(q, k, v):
    lp = q.shape[1]
    tq = TOKEN_TILE
    return pl.pallas_call(
        _attn_kernel,
        grid=(MLA_HEADS, lp // tq),
        in_specs=[
            pl.BlockSpec((1, tq, MLA_QK_PAD), lambda h, i: (h, i, 0)),
            pl.BlockSpec((1, lp, MLA_QK_PAD), lambda h, i: (h, 0, 0)),
            pl.BlockSpec((1, lp // tq, MLA_V_AUG, tq), lambda h, i: (h, 0, 0, 0)),
        ],
        out_specs=pl.BlockSpec((tq, MLA_V), lambda h, i: (i, h)),
        out_shape=jax.ShapeDtypeStruct((lp, MLA_HEADS * MLA_V), BF16),
        compiler_params=_cparams("parallel", "arbitrary"),
        name="mla_attention",
    )(q, k, v)


def _shift_rows(x, carry, s):
    rolled = pltpu.roll(x, s, axis=0)
    head = pltpu.roll(carry, s, axis=0)
    r8 = lax.broadcasted_iota(jnp.int32, (8, 1), 0)
    return jnp.concatenate([jnp.where(r8 < s, head, rolled[:8]), rolled[8:]], axis=0)


def _ssd_kernel(u_ref, cw_ref, cb_ref, dtb_ref, alog_ref, dsk_ref, ng_ref, tri_ref, e8_ref,
                y_ref, carry_ref, state_ref):
    i = pl.program_id(0)

    @pl.when(i == 0)
    def _():
        carry_ref[...] = jnp.zeros_like(carry_ref)
        state_ref[...] = jnp.zeros_like(state_ref)

    z = u_ref[:, :SSD_WIDTH]
    xbc = u_ref[:, SSD_WIDTH:SSD_WIDTH + SSD_CONV_CH]
    dt_raw = u_ref[:, SSD_WIDTH + SSD_CONV_CH:]
    carry = carry_ref[...]
    cw = cw_ref[...]
    conv = xbc * cw[3:4, :]
    for s in (1, 2, 3):
        conv = conv + _shift_rows(xbc, carry, s) * cw[3 - s:4 - s, :]
    carry_ref[...] = xbc[CHUNK - 8:, :]
    xbc = _silu(conv + cb_ref[...])
    xs = xbc[:, :SSD_WIDTH]
    bm = xbc[:, SSD_WIDTH:SSD_WIDTH + SSD_GROUPS * SSD_STATE]
    cm = xbc[:, SSD_WIDTH + SSD_GROUPS * SSD_STATE:]

    row = _row_ids(CHUNK, 1, i * CHUNK)
    dt = _softplus(dt_raw + dtb_ref[...]) * jnp.where(row >= PAD, 1.0, 0.0)
    la = dt * (-jnp.exp(alog_ref[...]))
    tri = tri_ref[...]
    e8 = e8_ref[...]
    cs = _dot01_left(tri, la)
    cs_e = _dot01_right(cs, e8)
    dt_e = _dot01_right(dt, e8)
    cs_t = cs.T
    cs_last_e = cs_e[CHUNK - 1:CHUNK, :]
    x = xs * dt_e
    xd = x * jnp.exp(cs_last_e - cs_e)
    ecs = jnp.exp(cs_e)
    dec = jnp.exp(cs_last_e)

    r_i = lax.broadcasted_iota(jnp.int32, (CHUNK, CHUNK), 0)
    c_i = lax.broadcasted_iota(jnp.int32, (CHUNK, CHUNK), 1)
    causal = r_i >= c_i
    per = SSD_HEADS // SSD_GROUPS
    gw = per * SSD_HEAD_DIM
    groups = range(SSD_GROUPS)
    b_gs = [bm[:, g * SSD_STATE:(g + 1) * SSD_STATE] for g in groups]
    c_gs = [cm[:, g * SSD_STATE:(g + 1) * SSD_STATE] for g in groups]
    s_prev = [state_ref[g] for g in groups]
    scores = [_mm_nt(c_gs[g], b_gs[g]) for g in groups]
    y_off = [_mm(c_gs[g], s_prev[g]) for g in groups]
    s_add = [_mm(b_gs[g].T, xd[:, g * gw:(g + 1) * gw]) for g in groups]
    lmats = []
    for h in range(SSD_HEADS):
        seg = cs[:, h:h + 1] - cs_t[h:h + 1, :]
        lmats.append(jnp.where(causal, jnp.exp(jnp.where(causal, seg, 0.0)), 0.0))
    y_diag = [_mm(scores[h // per] * lmats[h], x[:, h * SSD_HEAD_DIM:(h + 1) * SSD_HEAD_DIM])
              for h in range(SSD_HEADS)]
    for g in groups:
        state_ref[g] = s_prev[g] * dec[:, g * gw:(g + 1) * gw] + s_add[g]
    y = (jnp.concatenate(y_diag, axis=1) + jnp.concatenate(y_off, axis=1) * ecs) + dsk_ref[...] * xs
    y = y * _silu(z)
    y_ref[...] = _rms(y, ng_ref[...]).astype(BF16)


def _ssd(u, cw, cb, dtb, alog, dsk, ng, tri, e8):
    lp = u.shape[0]
    full = lambda shape: pl.BlockSpec(shape, lambda i: (0,) * len(shape))
    return pl.pallas_call(
        _ssd_kernel,
        grid=(lp // CHUNK,),
        in_specs=[pl.BlockSpec((CHUNK, SSD_IN_PAD), lambda i: (i, 0)),
                  full((SSD_CONV, SSD_CONV_CH)), full((1, SSD_CONV_CH)), full((1, 128)), full((1, 128)),
                  full((1, SSD_WIDTH)), full((1, SSD_WIDTH)), full((CHUNK, CHUNK)), full((128, SSD_WIDTH))],
        out_specs=pl.BlockSpec((CHUNK, SSD_WIDTH), lambda i: (i, 0)),
        out_shape=jax.ShapeDtypeStruct((lp, SSD_WIDTH), BF16),
        scratch_shapes=[pltpu.VMEM((8, SSD_CONV_CH), F32),
                        pltpu.VMEM((SSD_GROUPS, SSD_STATE, SSD_WIDTH // SSD_GROUPS), F32)],
        compiler_params=_cparams("arbitrary"),
        name="ssd",
    )(u, cw, cb, dtb, alog, dsk, ng, tri, e8)


def _ret_kernel(u_ref, cos_ref, sin_ref, ng_ref, y_ref, dmat_ref, state_ref):
    i = pl.program_id(0)
    log_g = [math.log(1.0 - 2.0 ** (-5.0 - h)) for h in range(RET_HEADS)]

    @pl.when(i == 0)
    def _():
        state_ref[...] = jnp.zeros_like(state_ref)
        r_i = lax.broadcasted_iota(jnp.int32, (CHUNK, CHUNK), 0)
        c_i = lax.broadcasted_iota(jnp.int32, (CHUNK, CHUNK), 1)
        diff = (r_i - c_i).astype(F32)
        for h in range(RET_HEADS):
            dmat_ref[h] = jnp.where(r_i >= c_i, jnp.exp(jnp.where(r_i >= c_i, diff, 0.0) * log_g[h]), 0.0)

    qkw = RET_HEADS * RET_DK
    cos = cos_ref[...]
    sin = sin_ref[...]
    q = jnp.concatenate([_rope_lanes(u_ref[:, c * 128:(c + 1) * 128], cos, sin) for c in range(2)], axis=1)
    k = jnp.concatenate([_rope_lanes(u_ref[:, qkw + c * 128:qkw + (c + 1) * 128], cos, sin)
                         for c in range(2)], axis=1) * (RET_DK ** -0.5)
    idx = lax.broadcasted_iota(jnp.int32, (CHUNK, 1), 0).astype(F32)
    k_t = k.T
    idx_row = lax.broadcasted_iota(jnp.int32, (1, CHUNK), 1).astype(F32)
    heads = range(RET_HEADS)
    q_hs = [q[:, h * RET_DK:(h + 1) * RET_DK] for h in heads]
    k_hs = [k[:, h * RET_DK:(h + 1) * RET_DK] for h in heads]
    v_hs = [u_ref[:, 2 * qkw + h * RET_DV: 2 * qkw + (h + 1) * RET_DV].astype(BF16) for h in heads]
    s_prevs = [state_ref[h] for h in heads]
    scs = [_mm_nt(q_hs[h], k_hs[h]) for h in heads]
    y_cross = [_mm(q_hs[h] * jnp.exp((idx + 1.0) * log_g[h]), s_prevs[h]) for h in heads]
    s_adds = [_mm(k_t[h * RET_DK:(h + 1) * RET_DK, :] * jnp.exp((CHUNK - 1 - idx_row) * log_g[h]), v_hs[h])
              for h in heads]
    y_in = [_mm(scs[h] * dmat_ref[h], v_hs[h]) for h in heads]
    for h in heads:
        g_h = u_ref[:, 2 * qkw + RET_WIDTH + h * RET_DV: 2 * qkw + RET_WIDTH + (h + 1) * RET_DV]
        y = y_in[h] + y_cross[h]
        state_ref[h] = s_prevs[h] * math.exp(CHUNK * log_g[h]) + s_adds[h]
        mu = jnp.mean(y, axis=-1, keepdims=True)
        var = jnp.mean(jnp.square(y - mu), axis=-1, keepdims=True)
        yn = (y - mu) * lax.rsqrt(var + EPS) * ng_ref[:, h * RET_DV:(h + 1) * RET_DV]
        y_ref[:, h * RET_DV:(h + 1) * RET_DV] = (_silu(g_h) * yn).astype(BF16)


def _retention(u, cos, sin, ng):
    lp = u.shape[0]
    return pl.pallas_call(
        _ret_kernel,
        grid=(lp // CHUNK,),
        in_specs=[pl.BlockSpec((CHUNK, RET_IN), lambda i: (i, 0)),
                  pl.BlockSpec((CHUNK, 128), lambda i: (i, 0)),
                  pl.BlockSpec((CHUNK, 128), lambda i: (i, 0)),
                  pl.BlockSpec((1, RET_WIDTH), lambda i: (0, 0))],
        out_specs=pl.BlockSpec((CHUNK, RET_WIDTH), lambda i: (i, 0)),
        out_shape=jax.ShapeDtypeStruct((lp, RET_WIDTH), BF16),
        scratch_shapes=[pltpu.VMEM((RET_HEADS, CHUNK, CHUNK), F32),
                        pltpu.VMEM((RET_HEADS, RET_DK, RET_DV), F32)],
        compiler_params=_cparams("arbitrary"),
        name="retention",
    )(u, cos, sin, ng)


def _inv_unit_upper_many(mats, in_block, eye):
    d = [jnp.where(in_block, a, 0.0) for a in mats]
    f = [a - x for a, x in zip(mats, d)]
    d2 = [_mm(x, x) for x in d]
    d4 = [_mm(x, x) for x in d2]
    p = [_mm(eye + x, eye + y) for x, y in zip(d, d2)]
    d8 = [_mm(x, x) for x in d4]
    p = [_mm(x, eye + y) for x, y in zip(p, d4)]
    td = [_mm(x, eye + y) for x, y in zip(p, d8)]
    g = [_mm(x, y) for x, y in zip(f, td)]
    g2 = [_mm(x, x) for x in g]
    tg = [_mm(x, eye + y) for x, y in zip(td, g)]
    return [_mm(x, eye + y) for x, y in zip(tg, g2)]


def _rwkv_kernel(u_ref, mu_ref, w0_ref, a0_ref, kk_ref, ka_ref, rk_ref, ln_ref, lora_ref, tri_ref,
                 ones_ref, y_ref, prev_ref, state_ref, o_ref):
    i = pl.program_id(0)

    @pl.when(i == 0)
    def _():
        prev_ref[...] = jnp.zeros_like(prev_ref)
        state_ref[...] = jnp.zeros_like(state_ref)

    u = u_ref[...]
    rows = lax.broadcasted_iota(jnp.int32, (CHUNK, 1), 0)
    u_prev = jnp.where(rows == 0, prev_ref[7:8, :], pltpu.roll(u, 1, axis=0))
    prev_ref[...] = u[CHUNK - 8:, :]
    us = u + (u_prev - u) * mu_ref[...]
    w3 = RWKV_WIDTH
    r = us[:, :w3]
    k = us[:, w3:2 * w3]
    v = us[:, 2 * w3:3 * w3]
    lo = us[:, 3 * w3:]
    w = w0_ref[...] + _mm(jnp.tanh(lo), lora_ref[0])
    w = -_softplus(-w) - 0.5
    ld = -jnp.exp(w)
    a = _sigmoid(a0_ref[...] + _mm(lo, lora_ref[1]))
    g = _mm(_sigmoid(lo), lora_ref[2])
    ones = ones_ref[...]
    kk = k * kk_ref[...]
    k2 = k * (1.0 + (a - 1.0) * ka_ref[...])
    kk = kk / jnp.maximum(jnp.sqrt(_dot01_right(kk * kk, ones)), 1e-12)
    b = kk * a

    lc_full = _dot01_left(tri_ref[...], ld)
    v_t = v.T

    n = RWKV_SUB
    r_i = lax.broadcasted_iota(jnp.int32, (n, n), 0)
    c_i = lax.broadcasted_iota(jnp.int32, (n, n), 1)
    strict_upper = r_i < c_i
    lower_incl = r_i >= c_i
    in_block = (r_i // RWKV_INV_BLOCK) == (c_i // RWKV_INV_BLOCK)
    eye = jnp.where(r_i == c_i, 1.0, 0.0)
    hd = RWKV_HEAD_DIM
    n_sub = CHUNK // n
    heads = range(RWKV_HEADS)
    hs = [slice(h * hd, (h + 1) * hd) for h in heads]

    sc = []
    for sub in range(n_sub):
        lo_r, hi_r = sub * n, (sub + 1) * n
        lc = lc_full[lo_r:hi_r, :]
        if sub > 0:
            lc = lc - lc_full[lo_r - 1:lo_r, :]
        lc_last = lc[n - 1:n, :]
        e_pos = jnp.exp(lc)
        e_neg = jnp.exp(-lc)
        e_prev = jnp.exp(lc - ld[lo_r:hi_r, :])
        e_end = jnp.exp(lc_last - lc)
        sc.append(dict(
            gam=jnp.exp(lc_last),
            a_t=(-kk[lo_r:hi_r, :] * e_prev).astype(BF16),
            r_t=(r[lo_r:hi_r, :] * e_pos).astype(BF16),
            b_t=(b[lo_r:hi_r, :] * e_neg).astype(BF16),
            k_t=(k2[lo_r:hi_r, :] * e_neg).astype(BF16),
            b_h=(b[lo_r:hi_r, :] * e_end).astype(BF16),
            k_h=(k2[lo_r:hi_r, :] * e_end).astype(BF16),
            v_s=v[lo_r:hi_r, :].astype(BF16),
            v_t=v_t[:, lo_r:hi_r].astype(BF16),
        ))

    pairs = [(sub, h) for sub in range(n_sub) for h in heads]
    bk = [jnp.concatenate([sc[sub]["b_t"][:, hs[h]], sc[sub]["k_t"][:, hs[h]]], axis=0) for sub, h in pairs]
    at = [_mm_nt(x, sc[sub]["a_t"][:, hs[h]]) for x, (sub, h) in zip(bk, pairs)]
    ar = [_mm_nt(sc[sub]["r_t"][:, hs[h]], x) for x, (sub, h) in zip(bk, pairs)]
    aab_t = [jnp.where(strict_upper, x[:n, :], 0.0) for x in at]
    aak_t = [jnp.where(strict_upper, x[n:, :], 0.0) for x in at]
    a_rb = [jnp.where(lower_incl, x[:, :n], 0.0) for x in ar]
    a_rk = [jnp.where(lower_incl, x[:, n:], 0.0) for x in ar]
    xv = [_mm(sc[sub]["v_t"][hs[h], :], x) for x, (sub, h) in zip(aak_t, pairs)]
    ov = [_mm(x, sc[sub]["v_s"][:, hs[h]]) for x, (sub, h) in zip(a_rk, pairs)]
    sv = [_mm(sc[sub]["v_t"][hs[h], :], sc[sub]["k_h"][:, hs[h]]) for sub, h in pairs]
    t_t = _inv_unit_upper_many(aab_t, in_block, eye)

    state = [state_ref[h] for h in heads]
    for sub in range(n_sub):
        c = sc[sub]
        base = sub * RWKV_HEADS
        x = [_mm_nt(state[h], c["a_t"][:, hs[h]]) + xv[base + h] for h in heads]
        u_t = [_mm(x[h], t_t[base + h]) for h in heads]
        o_rs = [_mm_nt(c["r_t"][:, hs[h]], state[h]) for h in heads]
        o_bu = [_mm_nt(a_rb[base + h], u_t[h]) for h in heads]
        new_state = [state[h] * c["gam"][:, hs[h]] + _mm(u_t[h], c["b_h"][:, hs[h]]) + sv[base + h]
                     for h in heads]
        for h in heads:
            o_ref[sub * n:(sub + 1) * n, hs[h]] = o_rs[h] + o_bu[h] + ov[base + h]
        state = new_state
    for h in heads:
        state_ref[h] = state[h]

    out = o_ref[...]
    inv_n = 1.0 / RWKV_HEAD_DIM
    mean = _dot01_right(out, ones) * inv_n
    cen = out - mean
    var = _dot01_right(cen * cen, ones) * inv_n
    out = cen * lax.rsqrt(var + RWKV_LN_EPS) * ln_ref[...]
    bonus = _dot01_right(r * k2 * rk_ref[...], ones) * v
    y_ref[...] = ((out + bonus) * g).astype(BF16)


def _rwkv(u, mu, w0, a0, k_k, k_a, r_k, ln, lora, tri, ones):
    lp = u.shape[0]
    full = lambda shape: pl.BlockSpec(shape, lambda i: (0,) * len(shape))
    vec = full((1, RWKV_WIDTH))
    return pl.pallas_call(
        _rwkv_kernel,
        grid=(lp // CHUNK,),
        in_specs=[pl.BlockSpec((CHUNK, RWKV_IN), lambda i: (i, 0)), full((1, RWKV_IN)),
                  vec, vec, vec, vec, vec, vec, full((3, 128, RWKV_WIDTH)), full((CHUNK, CHUNK)),
                  full((RWKV_WIDTH, RWKV_WIDTH))],
        out_specs=pl.BlockSpec((CHUNK, RWKV_WIDTH), lambda i: (i, 0)),
        out_shape=jax.ShapeDtypeStruct((lp, RWKV_WIDTH), BF16),
        scratch_shapes=[pltpu.VMEM((8, RWKV_IN), F32),
                        pltpu.VMEM((RWKV_HEADS, RWKV_HEAD_DIM, RWKV_HEAD_DIM), F32),
                        pltpu.VMEM((CHUNK, RWKV_WIDTH), F32)],
        compiler_params=_cparams("arbitrary"),
        name="rwkv7",
    )(u, mu, w0, a0, k_k, k_a, r_k, ln, lora, tri, ones)


def _constants():
    r = jnp.arange(CHUNK)
    tri = (r[:, None] >= r[None, :]).astype(BF16)
    lane = jnp.arange(RWKV_WIDTH)
    ones_bd = ((lane[:, None] // RWKV_HEAD_DIM) == (lane[None, :] // RWKV_HEAD_DIM)).astype(BF16)
    e8 = ((jnp.arange(128)[:, None] == (lane[None, :] // SSD_HEAD_DIM))).astype(BF16)
    return tri, ones_bd, e8


def kernel(x, meta_tokens, ffn1_norm, ffn1_w_gate, ffn1_w_up, ffn1_w_down, mix_norm, w_in, w_out, mla_q_norm, mla_w_q_up, mla_kv_norm, mla_w_kv_up, mla_qk_norm_q, mla_qk_norm_k, ssd_conv_w, ssd_conv_b, ssd_dt_bias, ssd_a_log, ssd_d, ssd_norm, ret_norm, rwkv_mu, rwkv_w0, rwkv_w2, rwkv_a0, rwkv_a2, rwkv_g2, rwkv_k_k, rwkv_k_a, rwkv_r_k, rwkv_ln, ffn2_norm, ffn2_w_gate, ffn2_w_up, ffn2_w_down):
    b, seq, d = x.shape
    assert b == 1 and d == D_MODEL and seq % CHUNK == 0
    nl = w_in.shape[0]
    lp = seq + CHUNK
    assert lp % TOKEN_TILE == 0

    hst = jnp.concatenate([jnp.zeros((PAD, d), x.dtype), meta_tokens.astype(x.dtype), x[0]], axis=0)

    o0 = MLA_IN
    o1 = o0 + SSD_IN
    o2 = o1 + RET_IN
    w_mla = jnp.pad(w_in[:, :, :o0], ((0, 0), (0, 0), (0, MLA_IN_PAD - MLA_IN))).astype(BF16)
    w_ssd = jnp.pad(w_in[:, :, o0:o1], ((0, 0), (0, 0), (0, SSD_IN_PAD - SSD_IN))).astype(BF16)
    w_ret = w_in[:, :, o1:o2].astype(BF16)
    w_rwkv = w_in[:, :, o2:].astype(BF16)
    w_o = w_out.reshape(nl, 4, 512, D_MODEL).astype(BF16)
    wq = jnp.pad(mla_w_q_up.reshape(nl, MLA_Q_LORA, MLA_HEADS, MLA_QK),
                 ((0, 0), (0, 0), (0, 0), (0, MLA_QK_PAD - MLA_QK))
                 ).reshape(nl, MLA_Q_LORA, MLA_HEADS * MLA_QK_PAD).astype(BF16)
    wkv = mla_w_kv_up.reshape(nl, MLA_KV_LORA, MLA_HEADS, 2, MLA_NOPE).transpose(0, 1, 3, 2, 4
                              ).reshape(nl, MLA_KV_LORA, 2 * MLA_HEADS * MLA_NOPE).astype(BF16)
    gq = jnp.pad(mla_qk_norm_q, ((0, 0), (0, MLA_QK_PAD - MLA_QK)))[:, None, :]
    gk = jnp.pad(mla_qk_norm_k, ((0, 0), (0, MLA_QK_PAD - MLA_QK)))[:, None, :]
    dtb = jnp.pad(ssd_dt_bias, ((0, 0), (0, 128 - SSD_HEADS)))[:, None, :]
    alog = jnp.pad(ssd_a_log, ((0, 0), (0, 128 - SSD_HEADS)))[:, None, :]
    dsk = jnp.repeat(ssd_d, SSD_HEAD_DIM, axis=1)[:, None, :]
    lora = jnp.zeros((nl, 3, 128, RWKV_WIDTH), F32)
    lora = lora.at[:, 0, :RWKV_DECAY_LORA].set(rwkv_w2)
    lora = lora.at[:, 1, RWKV_DECAY_LORA:RWKV_DECAY_LORA + RWKV_A_LORA].set(rwkv_a2)
    lora = lora.at[:, 2, RWKV_DECAY_LORA + RWKV_A_LORA:].set(rwkv_g2)
    lora = lora.astype(BF16)

    tri, ones_bd, e8 = _constants()
    cos, sin, sinm = _rope_tables(lp)

    for l in range(nl):
        hst = _ffn(hst, ffn1_norm[l][None, :], ffn1_w_gate, ffn1_w_up, ffn1_w_down, l)
        gmix = mix_norm[l][None, :]
        q, k, v = _mla_prep(hst, gmix, w_mla, mla_q_norm[l][None, :], wq, mla_kv_norm[l][None, :],
                            wkv, gq[l], gk[l], cos, sinm, l)
        y_mla = _attention(q, k, v)
        u_ssd = _inproj(hst, gmix, w_ssd, l)
        y_ssd = _ssd(u_ssd, ssd_conv_w[l], ssd_conv_b[l][None, :], dtb[l], alog[l], dsk[l],
                     ssd_norm[l][None, :], tri, e8)
        u_ret = _inproj(hst, gmix, w_ret, l)
        y_ret = _retention(u_ret, cos, sin, ret_norm[l].reshape(1, RET_WIDTH))
        u_rwkv = _inproj(hst, gmix, w_rwkv, l)
        y_rwkv = _rwkv(u_rwkv, rwkv_mu[l][None, :], rwkv_w0[l][None, :], rwkv_a0[l][None, :],
                       rwkv_k_k[l][None, :], rwkv_k_a[l][None, :], rwkv_r_k[l].reshape(1, RWKV_WIDTH),
                       rwkv_ln[l].reshape(1, RWKV_WIDTH), lora[l], tri, ones_bd)
        hst = _outproj(hst, (y_mla, y_ssd, y_ret, y_rwkv), w_o, l)
        hst = _ffn(hst, ffn2_norm[l][None, :], ffn2_w_gate, ffn2_w_up, ffn2_w_down, l)
    return hst[CHUNK:][None]
```

```python
import functools
import math

import jax
import jax.numpy as jnp
from jax import lax
from jax.experimental import pallas as pl
from jax.experimental.pallas import tpu as pltpu

F32 = jnp.float32
BF16 = jnp.bfloat16

D_MODEL = 2048
DEPTH = 4
N_META = 16
CHUNK = 128
PAD = CHUNK - N_META
D_FF = 5632
EPS = 1e-6
ROPE_THETA = 10000.0
NEG_INF = -1e30

MLA_HEADS = 4
MLA_NOPE = 128
MLA_ROPE = 64
MLA_QK = MLA_NOPE + MLA_ROPE
MLA_V = 128
MLA_Q_LORA = 384
MLA_KV_LORA = 128
MLA_QK_PAD = 256
MLA_V_AUG = MLA_V + 16
MLA_IN = MLA_Q_LORA + MLA_KV_LORA + MLA_ROPE
MLA_IN_PAD = 640

SSD_HEADS = 8
SSD_HEAD_DIM = 64
SSD_WIDTH = 512
SSD_GROUPS = 2
SSD_STATE = 128
SSD_CONV = 4
SSD_CONV_CH = SSD_WIDTH + 2 * SSD_GROUPS * SSD_STATE
SSD_IN = SSD_WIDTH + SSD_CONV_CH + SSD_HEADS
SSD_IN_PAD = SSD_WIDTH + SSD_CONV_CH + 128

RET_HEADS = 4
RET_DK = 64
RET_DV = 128
RET_WIDTH = 512
RET_IN = 2 * RET_HEADS * RET_DK + 2 * RET_WIDTH

RWKV_HEADS = 8
RWKV_HEAD_DIM = 64
RWKV_WIDTH = 512
RWKV_DECAY_LORA = 32
RWKV_A_LORA = 32
RWKV_GATE_LORA = 64
RWKV_LN_EPS = 64e-5
RWKV_IN = 3 * RWKV_WIDTH + RWKV_DECAY_LORA + RWKV_A_LORA + RWKV_GATE_LORA
RWKV_SUB = 64
RWKV_INV_BLOCK = 16

MIX_WIDTH = 2048

V7X_VMEM_LIMIT_BYTES = 56 * 1024 * 1024
TOKEN_TILE = 640
FF_TILE = 512
FF_TILE_CAST = 256


def _cparams(*sem):
    return pltpu.CompilerParams(dimension_semantics=sem, vmem_limit_bytes=V7X_VMEM_LIMIT_BYTES)


def _sigmoid(x):
    return 1.0 / (1.0 + jnp.exp(-x))


def _silu(x):
    return x * _sigmoid(x)


def _softplus(x):
    return jnp.maximum(x, 0.0) + jnp.log(1.0 + jnp.exp(-jnp.abs(x)))


def _rms(x, g, eps=EPS):
    return x * lax.rsqrt(jnp.mean(x * x, axis=-1, keepdims=True) + eps) * g


def _mm(a, b):
    return jnp.dot(a.astype(BF16), b.astype(BF16), preferred_element_type=F32)


def _mm_nt(a, b):
    return lax.dot_general(a.astype(BF16), b.astype(BF16), (((1,), (1,)), ((), ())),
                           preferred_element_type=F32)


def _split3(x):
    x1 = x.astype(BF16)
    r1 = x - x1.astype(F32)
    x2 = r1.astype(BF16)
    x3 = (r1 - x2.astype(F32)).astype(BF16)
    return x1, x2, x3


def _dot01_right(x, m01):
    p1, p2, p3 = _split3(x)
    return (jnp.dot(p1, m01, preferred_element_type=F32)
            + jnp.dot(p2, m01, preferred_element_type=F32)
            + jnp.dot(p3, m01, preferred_element_type=F32))


def _dot01_left(m01, x):
    p1, p2, p3 = _split3(x)
    return (jnp.dot(m01, p1, preferred_element_type=F32)
            + jnp.dot(m01, p2, preferred_element_type=F32)
            + jnp.dot(m01, p3, preferred_element_type=F32))


def _row_ids(rows, cols, base):
    return base + lax.broadcasted_iota(jnp.int32, (rows, cols), 0)


def _rope_table_kernel(inv_ref, cos_ref, sin_ref, sinm_ref):
    i = pl.program_id(0)
    pos = (_row_ids(CHUNK, 128, i * CHUNK) - PAD).astype(F32)
    lane = lax.broadcasted_iota(jnp.int32, (CHUNK, 128), 1)
    ang = pos * inv_ref[...]
    c = jnp.cos(ang)
    s = jnp.sin(ang)
    s = jnp.where((lane % 64) < 32, -s, s)
    cos_ref[...] = c
    sin_ref[...] = s
    sinm_ref[...] = jnp.where(lane < 64, s, 0.0)


def _rope_tables(lp):
    half = 32
    inv = ROPE_THETA ** (-jnp.arange(half, dtype=F32) / half)
    inv = jnp.tile(inv, 4)[None, :]
    out = jax.ShapeDtypeStruct((lp, 128), F32)
    spec = pl.BlockSpec((CHUNK, 128), lambda i: (i, 0))
    return pl.pallas_call(
        _rope_table_kernel,
        grid=(lp // CHUNK,),
        in_specs=[pl.BlockSpec((1, 128), lambda i: (0, 0))],
        out_specs=[spec, spec, spec],
        out_shape=[out, out, out],
        compiler_params=_cparams("parallel"),
        name="rope_tables",
    )(inv)


def _rope_lanes(x, cos, sin_signed):
    lane = lax.broadcasted_iota(jnp.int32, x.shape, 1)
    fwd = pltpu.roll(x, 32, axis=1)
    bwd = pltpu.roll(x, 96, axis=1)
    rot = jnp.where((lane % 64) < 32, bwd, fwd)
    return x * cos + rot * sin_signed


def _ffn_kernel(*refs, cast_weights):
    if cast_weights:
        x_ref, g_ref, wg_ref, wu_ref, wd_ref, o_ref, wgb_ref, wub_ref, wdb_ref, xn_ref, acc_ref = refs
    else:
        x_ref, g_ref, wg_ref, wu_ref, wd_ref, _, o_ref, xn_ref, acc_ref = refs
    j = pl.program_id(1)

    @pl.when(j == 0)
    def _():
        xn_ref[...] = _rms(x_ref[...], g_ref[...]).astype(BF16)
        acc_ref[...] = jnp.zeros_like(acc_ref)

    wg, wu, wd = wg_ref[...], wu_ref[...], wd_ref[...]
    if cast_weights:
        wg, wu, wd = wg.astype(BF16), wu.astype(BF16), wd.astype(BF16)
        wgb_ref[...] = wg
        wub_ref[...] = wu
        wdb_ref[...] = wd
    xn = xn_ref[...]
    a = jnp.dot(xn, wg, preferred_element_type=F32)
    b = jnp.dot(xn, wu, preferred_element_type=F32)
    mid = (_silu(a) * b).astype(BF16)
    acc_ref[...] += jnp.dot(mid, wd, preferred_element_type=F32)

    @pl.when(j == pl.num_programs(1) - 1)
    def _():
        o_ref[...] = x_ref[...] + 0.5 * acc_ref[...]


def _ffn(h, g, wg, wu, wd, l):
    lp = h.shape[0]
    tm = TOKEN_TILE
    scratch = [pltpu.VMEM((tm, D_MODEL), BF16), pltpu.VMEM((tm, D_MODEL), F32)]
    tf = FF_TILE_CAST
    first, wgb, wub, wdb = pl.pallas_call(
        functools.partial(_ffn_kernel, cast_weights=True),
        grid=(1, D_FF // tf),
        in_specs=[
            pl.BlockSpec((tm, D_MODEL), lambda i, j: (0, 0)),
            pl.BlockSpec((1, D_MODEL), lambda i, j: (0, 0)),
            pl.BlockSpec((None, D_MODEL, tf), lambda i, j: (l, 0, j)),
            pl.BlockSpec((None, D_MODEL, tf), lambda i, j: (l, 0, j)),
            pl.BlockSpec((None, tf, D_MODEL), lambda i, j: (l, j, 0)),
        ],
        out_specs=[
            pl.BlockSpec((tm, D_MODEL), lambda i, j: (0, 0)),
            pl.BlockSpec((D_MODEL, tf), lambda i, j: (0, j)),
            pl.BlockSpec((D_MODEL, tf), lambda i, j: (0, j)),
            pl.BlockSpec((tf, D_MODEL), lambda i, j: (j, 0)),
        ],
        out_shape=[
            jax.ShapeDtypeStruct((lp, D_MODEL), F32),
            jax.ShapeDtypeStruct((D_MODEL, D_FF), BF16),
            jax.ShapeDtypeStruct((D_MODEL, D_FF), BF16),
            jax.ShapeDtypeStruct((D_FF, D_MODEL), BF16),
        ],
        scratch_shapes=scratch,
        compiler_params=_cparams("arbitrary", "arbitrary"),
        name="ffn_first",
    )(h, g, wg, wu, wd)
    tf = FF_TILE
    return pl.pallas_call(
        functools.partial(_ffn_kernel, cast_weights=False),
        grid=(lp // tm - 1, D_FF // tf),
        in_specs=[
            pl.BlockSpec((tm, D_MODEL), lambda i, j: (i + 1, 0)),
            pl.BlockSpec((1, D_MODEL), lambda i, j: (0, 0)),
            pl.BlockSpec((D_MODEL, tf), lambda i, j: (0, j)),
            pl.BlockSpec((D_MODEL, tf), lambda i, j: (0, j)),
            pl.BlockSpec((tf, D_MODEL), lambda i, j: (j, 0)),
            pl.BlockSpec(memory_space=pl.ANY),
        ],
        out_specs=pl.BlockSpec((tm, D_MODEL), lambda i, j: (i + 1, 0)),
        out_shape=jax.ShapeDtypeStruct((lp, D_MODEL), F32),
        input_output_aliases={5: 0},
        scratch_shapes=scratch,
        compiler_params=_cparams("parallel", "arbitrary"),
        name="ffn",
    )(h, g, wgb, wub, wdb, first)


def _inproj_kernel(x_ref, g_ref, w_ref, o_ref):
    i = pl.program_id(0)
    tm = x_ref.shape[0]
    xn = _rms(x_ref[...], g_ref[...]).astype(BF16)
    u = jnp.dot(xn, w_ref[...], preferred_element_type=F32)
    row = _row_ids(tm, 1, i * tm)
    o_ref[...] = jnp.where(row >= PAD, u, 0.0)


def _inproj(h, g, w, l):
    lp = h.shape[0]
    n = w.shape[2]
    tm = TOKEN_TILE
    return pl.pallas_call(
        _inproj_kernel,
        grid=(lp // tm,),
        in_specs=[
            pl.BlockSpec((tm, D_MODEL), lambda i: (i, 0)),
            pl.BlockSpec((1, D_MODEL), lambda i: (0, 0)),
            pl.BlockSpec((None, D_MODEL, n), lambda i: (l, 0, 0)),
        ],
        out_specs=pl.BlockSpec((tm, n), lambda i: (i, 0)),
        out_shape=jax.ShapeDtypeStruct((lp, n), F32),
        compiler_params=_cparams("parallel"),
        name="inproj",
    )(h, g, w)


def _outproj_kernel(h_ref, y0_ref, y1_ref, y2_ref, y3_ref, w_ref, o_ref):
    acc = h_ref[...]
    for m, y_ref in enumerate((y0_ref, y1_ref, y2_ref, y3_ref)):
        acc = acc + jnp.dot(y_ref[...], w_ref[m], preferred_element_type=F32)
    o_ref[...] = acc


def _outproj(h, ys, w, l):
    lp = h.shape[0]
    tm = TOKEN_TILE
    yspec = pl.BlockSpec((tm, 512), lambda i: (i, 0))
    return pl.pallas_call(
        _outproj_kernel,
        grid=(lp // tm,),
        in_specs=[pl.BlockSpec((tm, D_MODEL), lambda i: (i, 0)), yspec, yspec, yspec, yspec,
                  pl.BlockSpec((None, 4, 512, D_MODEL), lambda i: (l, 0, 0, 0))],
        out_specs=pl.BlockSpec((tm, D_MODEL), lambda i: (i, 0)),
        out_shape=jax.ShapeDtypeStruct((lp, D_MODEL), F32),
        compiler_params=_cparams("parallel"),
        name="outproj",
    )(h, *ys, w)


def _mla_prep_kernel(x_ref, g_ref, win_ref, qn_ref, wq_ref, kvn_ref, wkv_ref, gq_ref, gk_ref,
                     cos_ref, sinm_ref, q_ref, k_ref, v_ref):
    i = pl.program_id(0)
    tm = x_ref.shape[0]
    xn = _rms(x_ref[...], g_ref[...]).astype(BF16)
    u = jnp.dot(xn, win_ref[...], preferred_element_type=F32)
    row = _row_ids(tm, 1, i * tm)
    u = jnp.where(row >= PAD, u, 0.0)
    cq = u[:, :MLA_Q_LORA]
    ckv = u[:, MLA_Q_LORA:MLA_Q_LORA + MLA_KV_LORA]
    kpe = u[:, MLA_Q_LORA + MLA_KV_LORA:]
    q_all = _mm(_rms(cq, qn_ref[...]), wq_ref[...])
    kv_all = _mm(_rms(ckv, kvn_ref[...]), wkv_ref[...])
    cos = cos_ref[...]
    sinm = sinm_ref[...]
    gq = gq_ref[...]
    gk = gk_ref[...]
    scale = MLA_QK ** -0.5
    for h in range(MLA_HEADS):
        qh = q_all[:, h * MLA_QK_PAD:(h + 1) * MLA_QK_PAD]
        ss = jnp.sum(qh * qh, axis=-1, keepdims=True) * (1.0 / MLA_QK)
        qh = qh * lax.rsqrt(ss + EPS) * gq
        q_rot = _rope_lanes(qh[:, MLA_NOPE:], cos, sinm)
        q_ref[h, :, :MLA_NOPE] = (qh[:, :MLA_NOPE] * scale).astype(BF16)
        q_ref[h, :, MLA_NOPE:] = (q_rot * scale).astype(BF16)

        kn = kv_all[:, h * MLA_NOPE:(h + 1) * MLA_NOPE]
        ss = (jnp.sum(kn * kn, axis=-1, keepdims=True)
              + jnp.sum(kpe * kpe, axis=-1, keepdims=True)) * (1.0 / MLA_QK)
        rs = lax.rsqrt(ss + EPS)
        k_rot = _rope_lanes(kpe * rs * gk[:, MLA_NOPE:], cos, sinm)
        k_ref[h, :, :MLA_NOPE] = (kn * rs * gk[:, :MLA_NOPE]).astype(BF16)
        k_ref[h, :, MLA_NOPE:] = k_rot.astype(BF16)
        v_ref[h, 0, :MLA_V, :] = kv_all[:, 4 * MLA_NOPE + h * MLA_V: 4 * MLA_NOPE + (h + 1) * MLA_V].T.astype(BF16)
        tail = lax.broadcasted_iota(jnp.int32, (MLA_V_AUG - MLA_V, tm), 0)
        v_ref[h, 0, MLA_V:, :] = jnp.where(tail == 0, 1.0, 0.0).astype(BF16)


def _mla_prep(h, g, win, qn, wq, kvn, wkv, gq, gk, cos, sinm, l):
    lp = h.shape[0]
    tm = TOKEN_TILE
    full = lambda shape: pl.BlockSpec(shape, lambda i: (0,) * len(shape))
    layer = lambda shape: pl.BlockSpec((None,) + shape, lambda i: (l,) + (0,) * len(shape))
    return pl.pallas_call(
        _mla_prep_kernel,
        grid=(lp // tm,),
        in_specs=[
            pl.BlockSpec((tm, D_MODEL), lambda i: (i, 0)),
            full((1, D_MODEL)), layer((D_MODEL, MLA_IN_PAD)),
            full((1, MLA_Q_LORA)), layer((MLA_Q_LORA, MLA_HEADS * MLA_QK_PAD)),
            full((1, MLA_KV_LORA)), layer((MLA_KV_LORA, 2 * MLA_HEADS * MLA_NOPE)),
            full((1, MLA_QK_PAD)), full((1, MLA_QK_PAD)),
            pl.BlockSpec((tm, 128), lambda i: (i, 0)),
            pl.BlockSpec((tm, 128), lambda i: (i, 0)),
        ],
        out_specs=[
            pl.BlockSpec((MLA_HEADS, tm, MLA_QK_PAD), lambda i: (0, i, 0)),
            pl.BlockSpec((MLA_HEADS, tm, MLA_QK_PAD), lambda i: (0, i, 0)),
            pl.BlockSpec((MLA_HEADS, 1, MLA_V_AUG, tm), lambda i: (0, i, 0, 0)),
        ],
        out_shape=[
            jax.ShapeDtypeStruct((MLA_HEADS, lp, MLA_QK_PAD), BF16),
            jax.ShapeDtypeStruct((MLA_HEADS, lp, MLA_QK_PAD), BF16),
            jax.ShapeDtypeStruct((MLA_HEADS, lp // tm, MLA_V_AUG, tm), BF16),
        ],
        compiler_params=_cparams("parallel"),
        name="mla_prep",
    )(h, g, win, qn, wq, kvn, wkv, gq, gk, cos, sinm)


def _attn_kernel(q_ref, k_ref, vt_ref, o_ref, s_ref, p_ref):
    i = pl.program_id(1)
    tq = q_ref.shape[1]
    q = q_ref[0]
    key0 = lax.broadcasted_iota(jnp.int32, (tq, tq), 0)
    qry = i * tq + lax.broadcasted_iota(jnp.int32, (tq, tq), 1)

    def scores(j, masked):
        start = pl.multiple_of(j * tq, tq)
        k = k_ref[0, pl.ds(start, tq), :]
        s = lax.dot_general(k, q, (((1,), (1,)), ((), ())), preferred_element_type=F32)
        if masked:
            key = key0 + j * tq
            s = jnp.where(key <= qry, jnp.where(key >= PAD, s, NEG_INF), NEG_INF)
        return s

    def softmax(s, m):
        m_new = jnp.maximum(m, jnp.max(s, axis=0, keepdims=True))
        return m_new, jnp.exp(m - m_new), jnp.exp((s - m_new).astype(BF16))

    def pv(j, p, alpha, acc):
        return alpha * acc + jnp.dot(vt_ref[0, j], p, preferred_element_type=F32)

    m0 = jnp.full((1, tq), NEG_INF, F32)
    acc0 = jnp.zeros((MLA_V_AUG, tq), F32)

    def diagonal(c):
        m, acc = c
        m, a, p = softmax(scores(i, True), m)
        return pv(i, p, a, acc)

    def short(_):
        m, a, p = softmax(scores(0, True), m0)
        return lax.cond(i > 0, diagonal, lambda c: c[1], (m, pv(0, p, a, acc0)))

    def pipelined(_):
        def stage_scores(b, masked):
            s_ref[b % 2] = scores(b, masked)

        def stage_softmax(b, m):
            m, a, p = softmax(s_ref[b % 2], m)
            p_ref[b % 2] = p
            return m, a

        def stage_pv(b, a, acc):
            return pv(b, p_ref[b % 2], a, acc)

        stage_scores(0, True)
        stage_scores(1, False)
        m, a = stage_softmax(0, m0)

        def body(b, c):
            a_prev, m, acc = c
            acc = stage_pv(b - 2, a_prev, acc)
            m, a = stage_softmax(b - 1, m)
            stage_scores(b, False)
            return a, m, acc

        a_prev, m, acc = lax.fori_loop(2, i, body, (a, m, acc0))
        acc = stage_pv(i - 2, a_prev, acc)
        m, a = stage_softmax(i - 1, m)
        stage_scores(i, True)
        acc = stage_pv(i - 1, a, acc)
        m, a = stage_softmax(i, m)
        return stage_pv(i, a, acc)

    acc = lax.cond(i < 2, short, pipelined, 0)
    o_ref[...] = (acc[:MLA_V] / acc[MLA_V:MLA_V + 1]).T.astype(BF16)


def _attention(q, k, v):
    lp = q.shape[1]
    tq = TOKEN_TILE
    return pl.pallas_call(
        _attn_kernel,
        grid=(MLA_HEADS, lp // tq),
        in_specs=[
            pl.BlockSpec((1, tq, MLA_QK_PAD), lambda h, i: (h, i, 0)),
            pl.BlockSpec((1, lp, MLA_QK_PAD), lambda h, i: (h, 0, 0)),
            pl.BlockSpec((1, lp // tq, MLA_V_AUG, tq), lambda h, i: (h, 0, 0, 0)),
        ],
        out_specs=pl.BlockSpec((tq, MLA_V), lambda h, i: (i, h)),
        out_shape=jax.ShapeDtypeStruct((lp, MLA_HEADS * MLA_V), BF16),
        scratch_shapes=[pltpu.VMEM((2, tq, tq), F32), pltpu.VMEM((2, tq, tq), BF16)],
        compiler_params=_cparams("parallel", "arbitrary"),
        name="mla_attention",
    )(q, k, v)


def _shift_rows(x, carry, s):
    rolled = pltpu.roll(x, s, axis=0)
    head = pltpu.roll(carry, s, axis=0)
    r8 = lax.broadcasted_iota(jnp.int32, (8, 1), 0)
    return jnp.concatenate([jnp.where(r8 < s, head, rolled[:8]), rolled[8:]], axis=0)


def _ssd_kernel(u_ref, cw_ref, cb_ref, dtb_ref, alog_ref, dsk_ref, ng_ref, tri_ref, e8_ref,
                y_ref, carry_ref, state_ref):
    i = pl.program_id(0)

    @pl.when(i == 0)
    def _():
        carry_ref[...] = jnp.zeros_like(carry_ref)
        state_ref[...] = jnp.zeros_like(state_ref)

    z = u_ref[:, :SSD_WIDTH]
    xbc = u_ref[:, SSD_WIDTH:SSD_WIDTH + SSD_CONV_CH]
    dt_raw = u_ref[:, SSD_WIDTH + SSD_CONV_CH:]
    carry = carry_ref[...]
    cw = cw_ref[...]
    conv = xbc * cw[3:4, :]
    for s in (1, 2, 3):
        conv = conv + _shift_rows(xbc, carry, s) * cw[3 - s:4 - s, :]
    carry_ref[...] = xbc[CHUNK - 8:, :]
    xbc = _silu(conv + cb_ref[...])
    xs = xbc[:, :SSD_WIDTH]
    bm = xbc[:, SSD_WIDTH:SSD_WIDTH + SSD_GROUPS * SSD_STATE]
    cm = xbc[:, SSD_WIDTH + SSD_GROUPS * SSD_STATE:]

    row = _row_ids(CHUNK, 1, i * CHUNK)
    dt = _softplus(dt_raw + dtb_ref[...]) * jnp.where(row >= PAD, 1.0, 0.0)
    la = dt * (-jnp.exp(alog_ref[...]))
    tri = tri_ref[...]
    e8 = e8_ref[...]
    cs = _dot01_left(tri, la)
    cs_e = _dot01_right(cs, e8)
    dt_e = _dot01_right(dt, e8)
    cs_t = cs.T
    cs_last_e = cs_e[CHUNK - 1:CHUNK, :]
    x = xs * dt_e
    xd = x * jnp.exp(cs_last_e - cs_e)
    ecs = jnp.exp(cs_e)
    dec = jnp.exp(cs_last_e)

    r_i = lax.broadcasted_iota(jnp.int32, (CHUNK, CHUNK), 0)
    c_i = lax.broadcasted_iota(jnp.int32, (CHUNK, CHUNK), 1)
    causal = r_i >= c_i
    per = SSD_HEADS // SSD_GROUPS
    gw = per * SSD_HEAD_DIM
    groups = range(SSD_GROUPS)
    b_gs = [bm[:, g * SSD_STATE:(g + 1) * SSD_STATE] for g in groups]
    c_gs = [cm[:, g * SSD_STATE:(g + 1) * SSD_STATE] for g in groups]
    s_prev = [state_ref[g] for g in groups]
    scores = [_mm_nt(c_gs[g], b_gs[g]) for g in groups]
    y_off = [_mm(c_gs[g], s_prev[g]) for g in groups]
    s_add = [_mm(b_gs[g].T, xd[:, g * gw:(g + 1) * gw]) for g in groups]
    lmats = []
    for h in range(SSD_HEADS):
        seg = cs[:, h:h + 1] - cs_t[h:h + 1, :]
        lmats.append(jnp.where(causal, jnp.exp(jnp.where(causal, seg, 0.0)), 0.0))
    y_diag = [_mm(scores[h // per] * lmats[h], x[:, h * SSD_HEAD_DIM:(h + 1) * SSD_HEAD_DIM])
              for h in range(SSD_HEADS)]
    for g in groups:
        state_ref[g] = s_prev[g] * dec[:, g * gw:(g + 1) * gw] + s_add[g]
    y = (jnp.concatenate(y_diag, axis=1) + jnp.concatenate(y_off, axis=1) * ecs) + dsk_ref[...] * xs
    y = y * _silu(z)
    y_ref[...] = _rms(y, ng_ref[...]).astype(BF16)


def _ssd(u, cw, cb, dtb, alog, dsk, ng, tri, e8):
    lp = u.shape[0]
    full = lambda shape: pl.BlockSpec(shape, lambda i: (0,) * len(shape))
    return pl.pallas_call(
        _ssd_kernel,
        grid=(lp // CHUNK,),
        in_specs=[pl.BlockSpec((CHUNK, SSD_IN_PAD), lambda i: (i, 0)),
                  full((SSD_CONV, SSD_CONV_CH)), full((1, SSD_CONV_CH)), full((1, 128)), full((1, 128)),
                  full((1, SSD_WIDTH)), full((1, SSD_WIDTH)), full((CHUNK, CHUNK)), full((128, SSD_WIDTH))],
        out_specs=pl.BlockSpec((CHUNK, SSD_WIDTH), lambda i: (i, 0)),
        out_shape=jax.ShapeDtypeStruct((lp, SSD_WIDTH), BF16),
        scratch_shapes=[pltpu.VMEM((8, SSD_CONV_CH), F32),
                        pltpu.VMEM((SSD_GROUPS, SSD_STATE, SSD_WIDTH // SSD_GROUPS), F32)],
        compiler_params=_cparams("arbitrary"),
        name="ssd",
    )(u, cw, cb, dtb, alog, dsk, ng, tri, e8)


def _ret_kernel(u_ref, cos_ref, sin_ref, ng_ref, y_ref, dmat_ref, state_ref):
    i = pl.program_id(0)
    log_g = [math.log(1.0 - 2.0 ** (-5.0 - h)) for h in range(RET_HEADS)]

    @pl.when(i == 0)
    def _():
        state_ref[...] = jnp.zeros_like(state_ref)
        r_i = lax.broadcasted_iota(jnp.int32, (CHUNK, CHUNK), 0)
        c_i = lax.broadcasted_iota(jnp.int32, (CHUNK, CHUNK), 1)
        diff = (r_i - c_i).astype(F32)
        for h in range(RET_HEADS):
            dmat_ref[h] = jnp.where(r_i >= c_i, jnp.exp(jnp.where(r_i >= c_i, diff, 0.0) * log_g[h]), 0.0)

    qkw = RET_HEADS * RET_DK
    cos = cos_ref[...]
    sin = sin_ref[...]
    q = jnp.concatenate([_rope_lanes(u_ref[:, c * 128:(c + 1) * 128], cos, sin) for c in range(2)], axis=1)
    k = jnp.concatenate([_rope_lanes(u_ref[:, qkw + c * 128:qkw + (c + 1) * 128], cos, sin)
                         for c in range(2)], axis=1) * (RET_DK ** -0.5)
    idx = lax.broadcasted_iota(jnp.int32, (CHUNK, 1), 0).astype(F32)
    k_t = k.T
    idx_row = lax.broadcasted_iota(jnp.int32, (1, CHUNK), 1).astype(F32)
    heads = range(RET_HEADS)
    q_hs = [q[:, h * RET_DK:(h + 1) * RET_DK] for h in heads]
    k_hs = [k[:, h * RET_DK:(h + 1) * RET_DK] for h in heads]
    v_hs = [u_ref[:, 2 * qkw + h * RET_DV: 2 * qkw + (h + 1) * RET_DV].astype(BF16) for h in heads]
    s_prevs = [state_ref[h] for h in heads]
    scs = [_mm_nt(q_hs[h], k_hs[h]) for h in heads]
    y_cross = [_mm(q_hs[h] * jnp.exp((idx + 1.0) * log_g[h]), s_prevs[h]) for h in heads]
    s_adds = [_mm(k_t[h * RET_DK:(h + 1) * RET_DK, :] * jnp.exp((CHUNK - 1 - idx_row) * log_g[h]), v_hs[h])
              for h in heads]
    y_in = [_mm(scs[h] * dmat_ref[h], v_hs[h]) for h in heads]
    for h in heads:
        g_h = u_ref[:, 2 * qkw + RET_WIDTH + h * RET_DV: 2 * qkw + RET_WIDTH + (h + 1) * RET_DV]
        y = y_in[h] + y_cross[h]
        state_ref[h] = s_prevs[h] * math.exp(CHUNK * log_g[h]) + s_adds[h]
        mu = jnp.mean(y, axis=-1, keepdims=True)
        var = jnp.mean(jnp.square(y - mu), axis=-1, keepdims=True)
        yn = (y - mu) * lax.rsqrt(var + EPS) * ng_ref[:, h * RET_DV:(h + 1) * RET_DV]
        y_ref[:, h * RET_DV:(h + 1) * RET_DV] = (_silu(g_h) * yn).astype(BF16)


def _retention(u, cos, sin, ng):
    lp = u.shape[0]
    return pl.pallas_call(
        _ret_kernel,
        grid=(lp // CHUNK,),
        in_specs=[pl.BlockSpec((CHUNK, RET_IN), lambda i: (i, 0)),
                  pl.BlockSpec((CHUNK, 128), lambda i: (i, 0)),
                  pl.BlockSpec((CHUNK, 128), lambda i: (i, 0)),
                  pl.BlockSpec((1, RET_WIDTH), lambda i: (0, 0))],
        out_specs=pl.BlockSpec((CHUNK, RET_WIDTH), lambda i: (i, 0)),
        out_shape=jax.ShapeDtypeStruct((lp, RET_WIDTH), BF16),
        scratch_shapes=[pltpu.VMEM((RET_HEADS, CHUNK, CHUNK), F32),
                        pltpu.VMEM((RET_HEADS, RET_DK, RET_DV), F32)],
        compiler_params=_cparams("arbitrary"),
        name="retention",
    )(u, cos, sin, ng)


def _inv_unit_upper_many(mats, in_block, eye):
    d = [jnp.where(in_block, a, 0.0) for a in mats]
    f = [a - x for a, x in zip(mats, d)]
    d2 = [_mm(x, x) for x in d]
    d4 = [_mm(x, x) for x in d2]
    p = [_mm(eye + x, eye + y) for x, y in zip(d, d2)]
    d8 = [_mm(x, x) for x in d4]
    p = [_mm(x, eye + y) for x, y in zip(p, d4)]
    td = [_mm(x, eye + y) for x, y in zip(p, d8)]
    g = [_mm(x, y) for x, y in zip(f, td)]
    g2 = [_mm(x, x) for x in g]
    tg = [_mm(x, eye + y) for x, y in zip(td, g)]
    return [_mm(x, eye + y) for x, y in zip(tg, g2)]


def _rwkv_kernel(u_ref, mu_ref, w0_ref, a0_ref, kk_ref, ka_ref, rk_ref, ln_ref, lora_ref, tri_ref,
                 ones_ref, y_ref, prev_ref, state_ref, o_ref):
    i = pl.program_id(0)

    @pl.when(i == 0)
    def _():
        prev_ref[...] = jnp.zeros_like(prev_ref)
        state_ref[...] = jnp.zeros_like(state_ref)

    u = u_ref[...]
    rows = lax.broadcasted_iota(jnp.int32, (CHUNK, 1), 0)
    u_prev = jnp.where(rows == 0, prev_ref[7:8, :], pltpu.roll(u, 1, axis=0))
    prev_ref[...] = u[CHUNK - 8:, :]
    us = u + (u_prev - u) * mu_ref[...]
    w3 = RWKV_WIDTH
    r = us[:, :w3]
    k = us[:, w3:2 * w3]
    v = us[:, 2 * w3:3 * w3]
    lo = us[:, 3 * w3:]
    w = w0_ref[...] + _mm(jnp.tanh(lo), lora_ref[0])
    w = -_softplus(-w) - 0.5
    ld = -jnp.exp(w)
    a = _sigmoid(a0_ref[...] + _mm(lo, lora_ref[1]))
    g = _mm(_sigmoid(lo), lora_ref[2])
    ones = ones_ref[...]
    kk = k * kk_ref[...]
    k2 = k * (1.0 + (a - 1.0) * ka_ref[...])
    kk = kk / jnp.maximum(jnp.sqrt(_dot01_right(kk * kk, ones)), 1e-12)
    b = kk * a

    lc_full = _dot01_left(tri_ref[...], ld)
    v_t = v.T

    n = RWKV_SUB
    r_i = lax.broadcasted_iota(jnp.int32, (n, n), 0)
    c_i = lax.broadcasted_iota(jnp.int32, (n, n), 1)
    strict_upper = r_i < c_i
    lower_incl = r_i >= c_i
    in_block = (r_i // RWKV_INV_BLOCK) == (c_i // RWKV_INV_BLOCK)
    eye = jnp.where(r_i == c_i, 1.0, 0.0)
    hd = RWKV_HEAD_DIM
    n_sub = CHUNK // n
    heads = range(RWKV_HEADS)
    hs = [slice(h * hd, (h + 1) * hd) for h in heads]

    sc = []
    for sub in range(n_sub):
        lo_r, hi_r = sub * n, (sub + 1) * n
        lc = lc_full[lo_r:hi_r, :]
        if sub > 0:
            lc = lc - lc_full[lo_r - 1:lo_r, :]
        lc_last = lc[n - 1:n, :]
        e_pos = jnp.exp(lc)
        e_neg = jnp.exp(-lc)
        e_prev = jnp.exp(lc - ld[lo_r:hi_r, :])
        e_end = jnp.exp(lc_last - lc)
        sc.append(dict(
            gam=jnp.exp(lc_last),
            a_t=(-kk[lo_r:hi_r, :] * e_prev).astype(BF16),
            r_t=(r[lo_r:hi_r, :] * e_pos).astype(BF16),
            b_t=(b[lo_r:hi_r, :] * e_neg).astype(BF16),
            k_t=(k2[lo_r:hi_r, :] * e_neg).astype(BF16),
            b_h=(b[lo_r:hi_r, :] * e_end).astype(BF16),
            k_h=(k2[lo_r:hi_r, :] * e_end).astype(BF16),
            v_s=v[lo_r:hi_r, :].astype(BF16),
            v_t=v_t[:, lo_r:hi_r].astype(BF16),
        ))

    pairs = [(sub, h) for sub in range(n_sub) for h in heads]
    bk = [jnp.concatenate([sc[sub]["b_t"][:, hs[h]], sc[sub]["k_t"][:, hs[h]]], axis=0) for sub, h in pairs]
    at = [_mm_nt(x, sc[sub]["a_t"][:, hs[h]]) for x, (sub, h) in zip(bk, pairs)]
    ar = [_mm_nt(sc[sub]["r_t"][:, hs[h]], x) for x, (sub, h) in zip(bk, pairs)]
    aab_t = [jnp.where(strict_upper, x[:n, :], 0.0) for x in at]
    aak_t = [jnp.where(strict_upper, x[n:, :], 0.0) for x in at]
    a_rb = [jnp.where(lower_incl, x[:, :n], 0.0) for x in ar]
    a_rk = [jnp.where(lower_incl, x[:, n:], 0.0) for x in ar]
    xv = [_mm(sc[sub]["v_t"][hs[h], :], x) for x, (sub, h) in zip(aak_t, pairs)]
    ov = [_mm(x, sc[sub]["v_s"][:, hs[h]]) for x, (sub, h) in zip(a_rk, pairs)]
    sv = [_mm(sc[sub]["v_t"][hs[h], :], sc[sub]["k_h"][:, hs[h]]) for sub, h in pairs]
    t_t = _inv_unit_upper_many(aab_t, in_block, eye)

    state = [state_ref[h] for h in heads]
    for sub in range(n_sub):
        c = sc[sub]
        base = sub * RWKV_HEADS
        x = [_mm_nt(state[h], c["a_t"][:, hs[h]]) + xv[base + h] for h in heads]
        u_t = [_mm(x[h], t_t[base + h]) for h in heads]
        o_rs = [_mm_nt(c["r_t"][:, hs[h]], state[h]) for h in heads]
        o_bu = [_mm_nt(a_rb[base + h], u_t[h]) for h in heads]
        new_state = [state[h] * c["gam"][:, hs[h]] + _mm(u_t[h], c["b_h"][:, hs[h]]) + sv[base + h]
                     for h in heads]
        for h in heads:
            o_ref[sub * n:(sub + 1) * n, hs[h]] = o_rs[h] + o_bu[h] + ov[base + h]
        state = new_state
    for h in heads:
        state_ref[h] = state[h]

    out = o_ref[...]
    inv_n = 1.0 / RWKV_HEAD_DIM
    mean = _dot01_right(out, ones) * inv_n
    cen = out - mean
    var = _dot01_right(cen * cen, ones) * inv_n
    out = cen * lax.rsqrt(var + RWKV_LN_EPS) * ln_ref[...]
    bonus = _dot01_right(r * k2 * rk_ref[...], ones) * v
    y_ref[...] = ((out + bonus) * g).astype(BF16)


def _rwkv(u, mu, w0, a0, k_k, k_a, r_k, ln, lora, tri, ones):
    lp = u.shape[0]
    full = lambda shape: pl.BlockSpec(shape, lambda i: (0,) * len(shape))
    vec = full((1, RWKV_WIDTH))
    return pl.pallas_call(
        _rwkv_kernel,
        grid=(lp // CHUNK,),
        in_specs=[pl.BlockSpec((CHUNK, RWKV_IN), lambda i: (i, 0)), full((1, RWKV_IN)),
                  vec, vec, vec, vec, vec, vec, full((3, 128, RWKV_WIDTH)), full((CHUNK, CHUNK)),
                  full((RWKV_WIDTH, RWKV_WIDTH))],
        out_specs=pl.BlockSpec((CHUNK, RWKV_WIDTH), lambda i: (i, 0)),
        out_shape=jax.ShapeDtypeStruct((lp, RWKV_WIDTH), BF16),
        scratch_shapes=[pltpu.VMEM((8, RWKV_IN), F32),
                        pltpu.VMEM((RWKV_HEADS, RWKV_HEAD_DIM, RWKV_HEAD_DIM), F32),
                        pltpu.VMEM((CHUNK, RWKV_WIDTH), F32)],
        compiler_params=_cparams("arbitrary"),
        name="rwkv7",
    )(u, mu, w0, a0, k_k, k_a, r_k, ln, lora, tri, ones)


def _constants():
    r = jnp.arange(CHUNK)
    tri = (r[:, None] >= r[None, :]).astype(BF16)
    lane = jnp.arange(RWKV_WIDTH)
    ones_bd = ((lane[:, None] // RWKV_HEAD_DIM) == (lane[None, :] // RWKV_HEAD_DIM)).astype(BF16)
    e8 = ((jnp.arange(128)[:, None] == (lane[None, :] // SSD_HEAD_DIM))).astype(BF16)
    return tri, ones_bd, e8


def kernel(x, meta_tokens, ffn1_norm, ffn1_w_gate, ffn1_w_up, ffn1_w_down, mix_norm, w_in, w_out, mla_q_norm, mla_w_q_up, mla_kv_norm, mla_w_kv_up, mla_qk_norm_q, mla_qk_norm_k, ssd_conv_w, ssd_conv_b, ssd_dt_bias, ssd_a_log, ssd_d, ssd_norm, ret_norm, rwkv_mu, rwkv_w0, rwkv_w2, rwkv_a0, rwkv_a2, rwkv_g2, rwkv_k_k, rwkv_k_a, rwkv_r_k, rwkv_ln, ffn2_norm, ffn2_w_gate, ffn2_w_up, ffn2_w_down):
    b, seq, d = x.shape
    assert b == 1 and d == D_MODEL and seq % CHUNK == 0
    nl = w_in.shape[0]
    lp = seq + CHUNK
    assert lp % TOKEN_TILE == 0

    hst = jnp.concatenate([jnp.zeros((PAD, d), x.dtype), meta_tokens.astype(x.dtype), x[0]], axis=0)

    o0 = MLA_IN
    o1 = o0 + SSD_IN
    o2 = o1 + RET_IN
    w_mla = jnp.pad(w_in[:, :, :o0], ((0, 0), (0, 0), (0, MLA_IN_PAD - MLA_IN))).astype(BF16)
    w_ssd = jnp.pad(w_in[:, :, o0:o1], ((0, 0), (0, 0), (0, SSD_IN_PAD - SSD_IN))).astype(BF16)
    w_ret = w_in[:, :, o1:o2].astype(BF16)
    w_rwkv = w_in[:, :, o2:].astype(BF16)
    w_o = w_out.reshape(nl, 4, 512, D_MODEL).astype(BF16)
    wq = jnp.pad(mla_w_q_up.reshape(nl, MLA_Q_LORA, MLA_HEADS, MLA_QK),
                 ((0, 0), (0, 0), (0, 0), (0, MLA_QK_PAD - MLA_QK))
                 ).reshape(nl, MLA_Q_LORA, MLA_HEADS * MLA_QK_PAD).astype(BF16)
    wkv = mla_w_kv_up.reshape(nl, MLA_KV_LORA, MLA_HEADS, 2, MLA_NOPE).transpose(0, 1, 3, 2, 4
                              ).reshape(nl, MLA_KV_LORA, 2 * MLA_HEADS * MLA_NOPE).astype(BF16)
    gq = jnp.pad(mla_qk_norm_q, ((0, 0), (0, MLA_QK_PAD - MLA_QK)))[:, None, :]
    gk = jnp.pad(mla_qk_norm_k, ((0, 0), (0, MLA_QK_PAD - MLA_QK)))[:, None, :]
    dtb = jnp.pad(ssd_dt_bias, ((0, 0), (0, 128 - SSD_HEADS)))[:, None, :]
    alog = jnp.pad(ssd_a_log, ((0, 0), (0, 128 - SSD_HEADS)))[:, None, :]
    dsk = jnp.repeat(ssd_d, SSD_HEAD_DIM, axis=1)[:, None, :]
    lora = jnp.zeros((nl, 3, 128, RWKV_WIDTH), F32)
    lora = lora.at[:, 0, :RWKV_DECAY_LORA].set(rwkv_w2)
    lora = lora.at[:, 1, RWKV_DECAY_LORA:RWKV_DECAY_LORA + RWKV_A_LORA].set(rwkv_a2)
    lora = lora.at[:, 2, RWKV_DECAY_LORA + RWKV_A_LORA:].set(rwkv_g2)
    lora = lora.astype(BF16)

    tri, ones_bd, e8 = _constants()
    cos, sin, sinm = _rope_tables(lp)

    for l in range(nl):
        hst = _ffn(hst, ffn1_norm[l][None, :], ffn1_w_gate, ffn1_w_up, ffn1_w_down, l)
        gmix = mix_norm[l][None, :]
        q, k, v = _mla_prep(hst, gmix, w_mla, mla_q_norm[l][None, :], wq, mla_kv_norm[l][None, :],
                            wkv, gq[l], gk[l], cos, sinm, l)
        y_mla = _attention(q, k, v)
        u_ssd = _inproj(hst, gmix, w_ssd, l)
        y_ssd = _ssd(u_ssd, ssd_conv_w[l], ssd_conv_b[l][None, :], dtb[l], alog[l], dsk[l],
                     ssd_norm[l][None, :], tri, e8)
        u_ret = _inproj(hst, gmix, w_ret, l)
        y_ret = _retention(u_ret, cos, sin, ret_norm[l].reshape(1, RET_WIDTH))
        u_rwkv = _inproj(hst, gmix, w_rwkv, l)
        y_rwkv = _rwkv(u_rwkv, rwkv_mu[l][None, :], rwkv_w0[l][None, :], rwkv_a0[l][None, :],
                       rwkv_k_k[l][None, :], rwkv_k_a[l][None, :], rwkv_r_k[l].reshape(1, RWKV_WIDTH),
                       rwkv_ln[l].reshape(1, RWKV_WIDTH), lora[l], tri, ones_bd)
        hst = _outproj(hst, (y_mla, y_ssd, y_ret, y_rwkv), w_o, l)
        hst = _ffn(hst, ffn2_norm[l][None, :], ffn2_w_gate, ffn2_w_up, ffn2_w_down, l)
    return hst[CHUNK:][None]
```

```python
import functools
import math

import jax
import jax.numpy as jnp
from jax import lax
from jax.experimental import pallas as pl
from jax.experimental.pallas import tpu as pltpu

F32 = jnp.float32
BF16 = jnp.bfloat16

D_MODEL = 2048
DEPTH = 4
N_META = 16
CHUNK = 128
PAD = CHUNK - N_META
D_FF = 5632
EPS = 1e-6
ROPE_THETA = 10000.0
NEG_INF = -1e30

MLA_HEADS = 4
MLA_NOPE = 128
MLA_ROPE = 64
MLA_QK = MLA_NOPE + MLA_ROPE
MLA_V = 128
MLA_Q_LORA = 384
MLA_KV_LORA = 128
MLA_QK_PAD = 256
MLA_V_AUG = MLA_V + 16
MLA_IN = MLA_Q_LORA + MLA_KV_LORA + MLA_ROPE
MLA_IN_PAD = 640

SSD_HEADS = 8
SSD_HEAD_DIM = 64
SSD_WIDTH = 512
SSD_GROUPS = 2
SSD_STATE = 128
SSD_CONV = 4
SSD_CONV_CH = SSD_WIDTH + 2 * SSD_GROUPS * SSD_STATE
SSD_IN = SSD_WIDTH + SSD_CONV_CH + SSD_HEADS
SSD_IN_PAD = SSD_WIDTH + SSD_CONV_CH + 128

RET_HEADS = 4
RET_DK = 64
RET_DV = 128
RET_WIDTH = 512
RET_IN = 2 * RET_HEADS * RET_DK + 2 * RET_WIDTH

RWKV_HEADS = 8
RWKV_HEAD_DIM = 64
RWKV_WIDTH = 512
RWKV_DECAY_LORA = 32
RWKV_A_LORA = 32
RWKV_GATE_LORA = 64
RWKV_LN_EPS = 64e-5
RWKV_IN = 3 * RWKV_WIDTH + RWKV_DECAY_LORA + RWKV_A_LORA + RWKV_GATE_LORA
RWKV_SUB = 64
RWKV_INV_BLOCK = 16

MIX_WIDTH = 2048

V7X_VMEM_LIMIT_BYTES = 56 * 1024 * 1024
TOKEN_TILE = 640
FF_TILE = 512
FF_TILE_CAST = 256


def _cparams(*sem):
    return pltpu.CompilerParams(dimension_semantics=sem, vmem_limit_bytes=V7X_VMEM_LIMIT_BYTES)


def _sigmoid(x):
    return 1.0 / (1.0 + jnp.exp(-x))


def _silu(x):
    return x * _sigmoid(x)


def _softplus(x):
    return jnp.maximum(x, 0.0) + jnp.log(1.0 + jnp.exp(-jnp.abs(x)))


def _rms(x, g, eps=EPS):
    return x * lax.rsqrt(jnp.mean(x * x, axis=-1, keepdims=True) + eps) * g


def _mm(a, b):
    return jnp.dot(a.astype(BF16), b.astype(BF16), preferred_element_type=F32)


def _mm_nt(a, b):
    return lax.dot_general(a.astype(BF16), b.astype(BF16), (((1,), (1,)), ((), ())),
                           preferred_element_type=F32)


def _split3(x):
    x1 = x.astype(BF16)
    r1 = x - x1.astype(F32)
    x2 = r1.astype(BF16)
    x3 = (r1 - x2.astype(F32)).astype(BF16)
    return x1, x2, x3


def _dot01_right(x, m01):
    p1, p2, p3 = _split3(x)
    return (jnp.dot(p1, m01, preferred_element_type=F32)
            + jnp.dot(p2, m01, preferred_element_type=F32)
            + jnp.dot(p3, m01, preferred_element_type=F32))


def _dot01_left(m01, x):
    p1, p2, p3 = _split3(x)
    return (jnp.dot(m01, p1, preferred_element_type=F32)
            + jnp.dot(m01, p2, preferred_element_type=F32)
            + jnp.dot(m01, p3, preferred_element_type=F32))


def _row_ids(rows, cols, base):
    return base + lax.broadcasted_iota(jnp.int32, (rows, cols), 0)


def _rope_table_kernel(inv_ref, cos_ref, sin_ref, sinm_ref):
    i = pl.program_id(0)
    pos = (_row_ids(CHUNK, 128, i * CHUNK) - PAD).astype(F32)
    lane = lax.broadcasted_iota(jnp.int32, (CHUNK, 128), 1)
    ang = pos * inv_ref[...]
    c = jnp.cos(ang)
    s = jnp.sin(ang)
    s = jnp.where((lane % 64) < 32, -s, s)
    cos_ref[...] = c
    sin_ref[...] = s
    sinm_ref[...] = jnp.where(lane < 64, s, 0.0)


def _rope_tables(lp):
    half = 32
    inv = ROPE_THETA ** (-jnp.arange(half, dtype=F32) / half)
    inv = jnp.tile(inv, 4)[None, :]
    out = jax.ShapeDtypeStruct((lp, 128), F32)
    spec = pl.BlockSpec((CHUNK, 128), lambda i: (i, 0))
    return pl.pallas_call(
        _rope_table_kernel,
        grid=(lp // CHUNK,),
        in_specs=[pl.BlockSpec((1, 128), lambda i: (0, 0))],
        out_specs=[spec, spec, spec],
        out_shape=[out, out, out],
        compiler_params=_cparams("parallel"),
        name="rope_tables",
    )(inv)


def _rope_lanes(x, cos, sin_signed):
    lane = lax.broadcasted_iota(jnp.int32, x.shape, 1)
    fwd = pltpu.roll(x, 32, axis=1)
    bwd = pltpu.roll(x, 96, axis=1)
    rot = jnp.where((lane % 64) < 32, bwd, fwd)
    return x * cos + rot * sin_signed


def _ffn_kernel(*refs, cast_weights):
    if cast_weights:
        x_ref, g_ref, wg_ref, wu_ref, wd_ref, o_ref, wgb_ref, wub_ref, wdb_ref, xn_ref = refs
    else:
        x_ref, g_ref, wg_ref, wu_ref, wd_ref, first_ref, o_ref, xn_ref = refs
    i = pl.program_id(0)
    j = pl.program_id(1)
    last = pl.num_programs(1) - 1

    def compute():
        @pl.when(j == 0)
        def _():
            xn_ref[...] = _rms(x_ref[...], g_ref[...]).astype(BF16)
            o_ref[...] = jnp.zeros_like(o_ref)

        wg, wu, wd = wg_ref[...], wu_ref[...], wd_ref[...]
        if cast_weights:
            wg, wu, wd = wg.astype(BF16), wu.astype(BF16), wd.astype(BF16)
            wgb_ref[...] = wg
            wub_ref[...] = wu
            wdb_ref[...] = wd
        xn = xn_ref[...]
        a = jnp.dot(xn, wg, preferred_element_type=F32)
        b = jnp.dot(xn, wu, preferred_element_type=F32)
        mid = (_silu(a) * b).astype(BF16)
        o_ref[...] += jnp.dot(mid, wd, preferred_element_type=F32)

        @pl.when(j == last)
        def _():
            o_ref[...] = x_ref[...] + 0.5 * o_ref[...]

    if cast_weights:
        compute()
    else:
        pl.when(i > 0)(compute)

        @pl.when(jnp.logical_and(i == 0, j == last))
        def _():
            o_ref[...] = first_ref[...]


def _ffn(h, g, wg, wu, wd, l):
    lp = h.shape[0]
    tm = TOKEN_TILE
    scratch = [pltpu.VMEM((tm, D_MODEL), BF16)]
    tf = FF_TILE_CAST
    first, wgb, wub, wdb = pl.pallas_call(
        functools.partial(_ffn_kernel, cast_weights=True),
        grid=(1, D_FF // tf),
        in_specs=[
            pl.BlockSpec((tm, D_MODEL), lambda i, j: (0, 0)),
            pl.BlockSpec((1, D_MODEL), lambda i, j: (0, 0)),
            pl.BlockSpec((None, D_MODEL, tf), lambda i, j: (l, 0, j)),
            pl.BlockSpec((None, D_MODEL, tf), lambda i, j: (l, 0, j)),
            pl.BlockSpec((None, tf, D_MODEL), lambda i, j: (l, j, 0)),
        ],
        out_specs=[
            pl.BlockSpec((tm, D_MODEL), lambda i, j: (0, 0)),
            pl.BlockSpec((D_MODEL, tf), lambda i, j: (0, j)),
            pl.BlockSpec((D_MODEL, tf), lambda i, j: (0, j)),
            pl.BlockSpec((tf, D_MODEL), lambda i, j: (j, 0)),
        ],
        out_shape=[
            jax.ShapeDtypeStruct((tm, D_MODEL), F32),
            jax.ShapeDtypeStruct((D_MODEL, D_FF), BF16),
            jax.ShapeDtypeStruct((D_MODEL, D_FF), BF16),
            jax.ShapeDtypeStruct((D_FF, D_MODEL), BF16),
        ],
        scratch_shapes=scratch,
        compiler_params=_cparams("arbitrary", "arbitrary"),
        name="ffn_first",
    )(h, g, wg, wu, wd)
    tf = FF_TILE
    return pl.pallas_call(
        functools.partial(_ffn_kernel, cast_weights=False),
        grid=(lp // tm, D_FF // tf),
        in_specs=[
            pl.BlockSpec((tm, D_MODEL), lambda i, j: (i, 0)),
            pl.BlockSpec((1, D_MODEL), lambda i, j: (0, 0)),
            pl.BlockSpec((D_MODEL, tf), lambda i, j: (0, jnp.where(i == 0, 0, j))),
            pl.BlockSpec((D_MODEL, tf), lambda i, j: (0, jnp.where(i == 0, 0, j))),
            pl.BlockSpec((tf, D_MODEL), lambda i, j: (jnp.where(i == 0, 0, j), 0)),
            pl.BlockSpec((tm, D_MODEL), lambda i, j: (0, 0)),
        ],
        out_specs=pl.BlockSpec((tm, D_MODEL), lambda i, j: (i, 0)),
        out_shape=jax.ShapeDtypeStruct((lp, D_MODEL), F32),
        scratch_shapes=scratch,
        compiler_params=_cparams("arbitrary", "arbitrary"),
        name="ffn",
    )(h, g, wgb, wub, wdb, first)


def _inproj_kernel(x_ref, g_ref, w_ref, o_ref):
    i = pl.program_id(0)
    tm = x_ref.shape[0]
    xn = _rms(x_ref[...], g_ref[...]).astype(BF16)
    u = jnp.dot(xn, w_ref[...], preferred_element_type=F32)
    row = _row_ids(tm, 1, i * tm)
    o_ref[...] = jnp.where(row >= PAD, u, 0.0)


def _inproj(h, g, w, l):
    lp = h.shape[0]
    n = w.shape[2]
    tm = TOKEN_TILE
    return pl.pallas_call(
        _inproj_kernel,
        grid=(lp // tm,),
        in_specs=[
            pl.BlockSpec((tm, D_MODEL), lambda i: (i, 0)),
            pl.BlockSpec((1, D_MODEL), lambda i: (0, 0)),
            pl.BlockSpec((None, D_MODEL, n), lambda i: (l, 0, 0)),
        ],
        out_specs=pl.BlockSpec((tm, n), lambda i: (i, 0)),
        out_shape=jax.ShapeDtypeStruct((lp, n), F32),
        compiler_params=_cparams("parallel"),
        name="inproj",
    )(h, g, w)


def _outproj_kernel(h_ref, y0_ref, y1_ref, y2_ref, y3_ref, w_ref, o_ref):
    acc = h_ref[...]
    for m, y_ref in enumerate((y0_ref, y1_ref, y2_ref, y3_ref)):
        acc = acc + jnp.dot(y_ref[...], w_ref[m], preferred_element_type=F32)
    o_ref[...] = acc


def _outproj(h, ys, w, l):
    lp = h.shape[0]
    tm = TOKEN_TILE
    yspec = pl.BlockSpec((tm, 512), lambda i: (i, 0))
    return pl.pallas_call(
        _outproj_kernel,
        grid=(lp // tm,),
        in_specs=[pl.BlockSpec((tm, D_MODEL), lambda i: (i, 0)), yspec, yspec, yspec, yspec,
                  pl.BlockSpec((None, 4, 512, D_MODEL), lambda i: (l, 0, 0, 0))],
        out_specs=pl.BlockSpec((tm, D_MODEL), lambda i: (i, 0)),
        out_shape=jax.ShapeDtypeStruct((lp, D_MODEL), F32),
        compiler_params=_cparams("parallel"),
        name="outproj",
    )(h, *ys, w)


def _mla_prep_kernel(x_ref, g_ref, win_ref, qn_ref, wq_ref, kvn_ref, wkv_ref, gq_ref, gk_ref,
                     cos_ref, sinm_ref, q_ref, k_ref, v_ref):
    i = pl.program_id(0)
    tm = x_ref.shape[0]
    xn = _rms(x_ref[...], g_ref[...]).astype(BF16)
    u = jnp.dot(xn, win_ref[...], preferred_element_type=F32)
    row = _row_ids(tm, 1, i * tm)
    u = jnp.where(row >= PAD, u, 0.0)
    cq = u[:, :MLA_Q_LORA]
    ckv = u[:, MLA_Q_LORA:MLA_Q_LORA + MLA_KV_LORA]
    kpe = u[:, MLA_Q_LORA + MLA_KV_LORA:]
    q_all = _mm(_rms(cq, qn_ref[...]), wq_ref[...])
    kv_all = _mm(_rms(ckv, kvn_ref[...]), wkv_ref[...])
    cos = cos_ref[...]
    sinm = sinm_ref[...]
    gq = gq_ref[...]
    gk = gk_ref[...]
    scale = MLA_QK ** -0.5
    for h in range(MLA_HEADS):
        qh = q_all[:, h * MLA_QK_PAD:(h + 1) * MLA_QK_PAD]
        ss = jnp.sum(qh * qh, axis=-1, keepdims=True) * (1.0 / MLA_QK)
        qh = qh * lax.rsqrt(ss + EPS) * gq
        q_rot = _rope_lanes(qh[:, MLA_NOPE:], cos, sinm)
        q_ref[h, :, :MLA_NOPE] = (qh[:, :MLA_NOPE] * scale).astype(BF16)
        q_ref[h, :, MLA_NOPE:] = (q_rot * scale).astype(BF16)

        kn = kv_all[:, h * MLA_NOPE:(h + 1) * MLA_NOPE]
        ss = (jnp.sum(kn * kn, axis=-1, keepdims=True)
              + jnp.sum(kpe * kpe, axis=-1, keepdims=True)) * (1.0 / MLA_QK)
        rs = lax.rsqrt(ss + EPS)
        k_rot = _rope_lanes(kpe * rs * gk[:, MLA_NOPE:], cos, sinm)
        k_ref[h, :, :MLA_NOPE] = (kn * rs * gk[:, :MLA_NOPE]).astype(BF16)
        k_ref[h, :, MLA_NOPE:] = k_rot.astype(BF16)
        v_ref[h, 0, :MLA_V, :] = kv_all[:, 4 * MLA_NOPE + h * MLA_V: 4 * MLA_NOPE + (h + 1) * MLA_V].T.astype(BF16)
        tail = lax.broadcasted_iota(jnp.int32, (MLA_V_AUG - MLA_V, tm), 0)
        v_ref[h, 0, MLA_V:, :] = jnp.where(tail == 0, 1.0, 0.0).astype(BF16)


def _mla_prep(h, g, win, qn, wq, kvn, wkv, gq, gk, cos, sinm, l):
    lp = h.shape[0]
    tm = TOKEN_TILE
    full = lambda shape: pl.BlockSpec(shape, lambda i: (0,) * len(shape))
    layer = lambda shape: pl.BlockSpec((None,) + shape, lambda i: (l,) + (0,) * len(shape))
    return pl.pallas_call(
        _mla_prep_kernel,
        grid=(lp // tm,),
        in_specs=[
            pl.BlockSpec((tm, D_MODEL), lambda i: (i, 0)),
            full((1, D_MODEL)), layer((D_MODEL, MLA_IN_PAD)),
            full((1, MLA_Q_LORA)), layer((MLA_Q_LORA, MLA_HEADS * MLA_QK_PAD)),
            full((1, MLA_KV_LORA)), layer((MLA_KV_LORA, 2 * MLA_HEADS * MLA_NOPE)),
            full((1, MLA_QK_PAD)), full((1, MLA_QK_PAD)),
            pl.BlockSpec((tm, 128), lambda i: (i, 0)),
            pl.BlockSpec((tm, 128), lambda i: (i, 0)),
        ],
        out_specs=[
            pl.BlockSpec((MLA_HEADS, tm, MLA_QK_PAD), lambda i: (0, i, 0)),
            pl.BlockSpec((MLA_HEADS, tm, MLA_QK_PAD), lambda i: (0, i, 0)),
            pl.BlockSpec((MLA_HEADS, 1, MLA_V_AUG, tm), lambda i: (0, i, 0, 0)),
        ],
        out_shape=[
            jax.ShapeDtypeStruct((MLA_HEADS, lp, MLA_QK_PAD), BF16),
            jax.ShapeDtypeStruct((MLA_HEADS, lp, MLA_QK_PAD), BF16),
            jax.ShapeDtypeStruct((MLA_HEADS, lp // tm, MLA_V_AUG, tm), BF16),
        ],
        compiler_params=_cparams("parallel"),
        name="mla_prep",
    )(h, g, win, qn, wq, kvn, wkv, gq, gk, cos, sinm)


def _attn_kernel(q_ref, k_ref, vt_ref, o_ref):
    i = pl.program_id(1)
    tq = q_ref.shape[1]
    q = q_ref[0]
    key0 = lax.broadcasted_iota(jnp.int32, (tq, tq), 0)
    qry = i * tq + lax.broadcasted_iota(jnp.int32, (tq, tq), 1)

    def scores(j, masked):
        start = pl.multiple_of(j * tq, tq)
        k = k_ref[0, pl.ds(start, tq), :]
        s = lax.dot_general(k, q, (((1,), (1,)), ((), ())), preferred_element_type=F32)
        if masked:
            key = key0 + j * tq
            s = jnp.where(key <= qry, jnp.where(key >= PAD, s, NEG_INF), NEG_INF)
        return s

    def update(j, s, carry):
        m, acc = carry
        m_new = jnp.maximum(m, jnp.max(s, axis=0, keepdims=True))
        alpha = jnp.exp(m - m_new)
        p = jnp.exp((s - m_new).astype(BF16))
        acc = alpha * acc + jnp.dot(vt_ref[0, j], p, preferred_element_type=F32)
        return m_new, acc

    init = (jnp.full((1, tq), NEG_INF, F32), jnp.zeros((MLA_V_AUG, tq), F32))
    s_first = scores(0, True)

    def body(j, c):
        s_prev, carry = c
        s_new = scores(j, False)
        return s_new, update(j - 1, s_prev, carry)

    s_last, carry = lax.fori_loop(1, i, body, (s_first, init))
    j_last = jnp.maximum(i - 1, 0)
    m, acc = lax.cond(i > 0,
                      lambda c: update(i, scores(i, True), update(j_last, s_last, c)),
                      lambda c: update(j_last, s_last, c), carry)
    o_ref[...] = (acc[:MLA_V] / acc[MLA_V:MLA_V + 1]).T.astype(BF16)


def _attention(q, k, v):
    lp = q.shape[1]
    tq = TOKEN_TILE
    return pl.pallas_call(
        _attn_kernel,
        grid=(MLA_HEADS, lp // tq),
        in_specs=[
            pl.BlockSpec((1, tq, MLA_QK_PAD), lambda h, i: (h, i, 0)),
            pl.BlockSpec((1, lp, MLA_QK_PAD), lambda h, i: (h, 0, 0)),
            pl.BlockSpec((1, lp // tq, MLA_V_AUG, tq), lambda h, i: (h, 0, 0, 0)),
        ],
        out_specs=pl.BlockSpec((tq, MLA_V), lambda h, i: (i, h)),
        out_shape=jax.ShapeDtypeStruct((lp, MLA_HEADS * MLA_V), BF16),
        compiler_params=_cparams("parallel", "arbitrary"),
        name="mla_attention",
    )(q, k, v)


def _shift_rows(x, carry, s):
    rolled = pltpu.roll(x, s, axis=0)
    head = pltpu.roll(carry, s, axis=0)
    r8 = lax.broadcasted_iota(jnp.int32, (8, 1), 0)
    return jnp.concatenate([jnp.where(r8 < s, head, rolled[:8]), rolled[8:]], axis=0)


def _ssd_kernel(u_ref, cw_ref, cb_ref, dtb_ref, alog_ref, dsk_ref, ng_ref, tri_ref, e8_ref,
                y_ref, carry_ref, state_ref):
    i = pl.program_id(0)

    @pl.when(i == 0)
    def _():
        carry_ref[...] = jnp.zeros_like(carry_ref)
        state_ref[...] = jnp.zeros_like(state_ref)

    z = u_ref[:, :SSD_WIDTH]
    xbc = u_ref[:, SSD_WIDTH:SSD_WIDTH + SSD_CONV_CH]
    dt_raw = u_ref[:, SSD_WIDTH + SSD_CONV_CH:]
    carry = carry_ref[...]
    cw = cw_ref[...]
    conv = xbc * cw[3:4, :]
    for s in (1, 2, 3):
        conv = conv + _shift_rows(xbc, carry, s) * cw[3 - s:4 - s, :]
    carry_ref[...] = xbc[CHUNK - 8:, :]
    xbc = _silu(conv + cb_ref[...])
    xs = xbc[:, :SSD_WIDTH]
    bm = xbc[:, SSD_WIDTH:SSD_WIDTH + SSD_GROUPS * SSD_STATE]
    cm = xbc[:, SSD_WIDTH + SSD_GROUPS * SSD_STATE:]

    row = _row_ids(CHUNK, 1, i * CHUNK)
    dt = _softplus(dt_raw + dtb_ref[...]) * jnp.where(row >= PAD, 1.0, 0.0)
    la = dt * (-jnp.exp(alog_ref[...]))
    tri = tri_ref[...]
    e8 = e8_ref[...]
    cs = _dot01_left(tri, la)
    cs_e = _dot01_right(cs, e8)
    dt_e = _dot01_right(dt, e8)
    cs_t = cs.T
    cs_last_e = cs_e[CHUNK - 1:CHUNK, :]
    x = xs * dt_e
    xd = x * jnp.exp(cs_last_e - cs_e)
    ecs = jnp.exp(cs_e)
    dec = jnp.exp(cs_last_e)

    r_i = lax.broadcasted_iota(jnp.int32, (CHUNK, CHUNK), 0)
    c_i = lax.broadcasted_iota(jnp.int32, (CHUNK, CHUNK), 1)
    causal = r_i >= c_i
    per = SSD_HEADS // SSD_GROUPS
    gw = per * SSD_HEAD_DIM
    groups = range(SSD_GROUPS)
    b_gs = [bm[:, g * SSD_STATE:(g + 1) * SSD_STATE] for g in groups]
    c_gs = [cm[:, g * SSD_STATE:(g + 1) * SSD_STATE] for g in groups]
    s_prev = [state_ref[g] for g in groups]
    scores = [_mm_nt(c_gs[g], b_gs[g]) for g in groups]
    y_off = [_mm(c_gs[g], s_prev[g]) for g in groups]
    s_add = [_mm(b_gs[g].T, xd[:, g * gw:(g + 1) * gw]) for g in groups]
    lmats = []
    for h in range(SSD_HEADS):
        seg = cs[:, h:h + 1] - cs_t[h:h + 1, :]
        lmats.append(jnp.where(causal, jnp.exp(jnp.where(causal, seg, 0.0)), 0.0))
    y_diag = [_mm(scores[h // per] * lmats[h], x[:, h * SSD_HEAD_DIM:(h + 1) * SSD_HEAD_DIM])
              for h in range(SSD_HEADS)]
    for g in groups:
        state_ref[g] = s_prev[g] * dec[:, g * gw:(g + 1) * gw] + s_add[g]
    y = (jnp.concatenate(y_diag, axis=1) + jnp.concatenate(y_off, axis=1) * ecs) + dsk_ref[...] * xs
    y = y * _silu(z)
    y_ref[...] = _rms(y, ng_ref[...]).astype(BF16)


def _ssd(u, cw, cb, dtb, alog, dsk, ng, tri, e8):
    lp = u.shape[0]
    full = lambda shape: pl.BlockSpec(shape, lambda i: (0,) * len(shape))
    return pl.pallas_call(
        _ssd_kernel,
        grid=(lp // CHUNK,),
        in_specs=[pl.BlockSpec((CHUNK, SSD_IN_PAD), lambda i: (i, 0)),
                  full((SSD_CONV, SSD_CONV_CH)), full((1, SSD_CONV_CH)), full((1, 128)), full((1, 128)),
                  full((1, SSD_WIDTH)), full((1, SSD_WIDTH)), full((CHUNK, CHUNK)), full((128, SSD_WIDTH))],
        out_specs=pl.BlockSpec((CHUNK, SSD_WIDTH), lambda i: (i, 0)),
        out_shape=jax.ShapeDtypeStruct((lp, SSD_WIDTH), BF16),
        scratch_shapes=[pltpu.VMEM((8, SSD_CONV_CH), F32),
                        pltpu.VMEM((SSD_GROUPS, SSD_STATE, SSD_WIDTH // SSD_GROUPS), F32)],
        compiler_params=_cparams("arbitrary"),
        name="ssd",
    )(u, cw, cb, dtb, alog, dsk, ng, tri, e8)


def _ret_kernel(u_ref, cos_ref, sin_ref, ng_ref, y_ref, dmat_ref, state_ref):
    i = pl.program_id(0)
    log_g = [math.log(1.0 - 2.0 ** (-5.0 - h)) for h in range(RET_HEADS)]

    @pl.when(i == 0)
    def _():
        state_ref[...] = jnp.zeros_like(state_ref)
        r_i = lax.broadcasted_iota(jnp.int32, (CHUNK, CHUNK), 0)
        c_i = lax.broadcasted_iota(jnp.int32, (CHUNK, CHUNK), 1)
        diff = (r_i - c_i).astype(F32)
        for h in range(RET_HEADS):
            dmat_ref[h] = jnp.where(r_i >= c_i, jnp.exp(jnp.where(r_i >= c_i, diff, 0.0) * log_g[h]), 0.0)

    qkw = RET_HEADS * RET_DK
    cos = cos_ref[...]
    sin = sin_ref[...]
    q = jnp.concatenate([_rope_lanes(u_ref[:, c * 128:(c + 1) * 128], cos, sin) for c in range(2)], axis=1)
    k = jnp.concatenate([_rope_lanes(u_ref[:, qkw + c * 128:qkw + (c + 1) * 128], cos, sin)
                         for c in range(2)], axis=1) * (RET_DK ** -0.5)
    idx = lax.broadcasted_iota(jnp.int32, (CHUNK, 1), 0).astype(F32)
    k_t = k.T
    idx_row = lax.broadcasted_iota(jnp.int32, (1, CHUNK), 1).astype(F32)
    heads = range(RET_HEADS)
    q_hs = [q[:, h * RET_DK:(h + 1) * RET_DK] for h in heads]
    k_hs = [k[:, h * RET_DK:(h + 1) * RET_DK] for h in heads]
    v_hs = [u_ref[:, 2 * qkw + h * RET_DV: 2 * qkw + (h + 1) * RET_DV].astype(BF16) for h in heads]
    s_prevs = [state_ref[h] for h in heads]
    scs = [_mm_nt(q_hs[h], k_hs[h]) for h in heads]
    y_cross = [_mm(q_hs[h] * jnp.exp((idx + 1.0) * log_g[h]), s_prevs[h]) for h in heads]
    s_adds = [_mm(k_t[h * RET_DK:(h + 1) * RET_DK, :] * jnp.exp((CHUNK - 1 - idx_row) * log_g[h]), v_hs[h])
              for h in heads]
    y_in = [_mm(scs[h] * dmat_ref[h], v_hs[h]) for h in heads]
    for h in heads:
        g_h = u_ref[:, 2 * qkw + RET_WIDTH + h * RET_DV: 2 * qkw + RET_WIDTH + (h + 1) * RET_DV]
        y = y_in[h] + y_cross[h]
        state_ref[h] = s_prevs[h] * math.exp(CHUNK * log_g[h]) + s_adds[h]
        mu = jnp.mean(y, axis=-1, keepdims=True)
        var = jnp.mean(jnp.square(y - mu), axis=-1, keepdims=True)
        yn = (y - mu) * lax.rsqrt(var + EPS) * ng_ref[:, h * RET_DV:(h + 1) * RET_DV]
        y_ref[:, h * RET_DV:(h + 1) * RET_DV] = (_silu(g_h) * yn).astype(BF16)


def _retention(u, cos, sin, ng):
    lp = u.shape[0]
    return pl.pallas_call(
        _ret_kernel,
        grid=(lp // CHUNK,),
        in_specs=[pl.BlockSpec((CHUNK, RET_IN), lambda i: (i, 0)),
                  pl.BlockSpec((CHUNK, 128), lambda i: (i, 0)),
                  pl.BlockSpec((CHUNK, 128), lambda i: (i, 0)),
                  pl.BlockSpec((1, RET_WIDTH), lambda i: (0, 0))],
        out_specs=pl.BlockSpec((CHUNK, RET_WIDTH), lambda i: (i, 0)),
        out_shape=jax.ShapeDtypeStruct((lp, RET_WIDTH), BF16),
        scratch_shapes=[pltpu.VMEM((RET_HEADS, CHUNK, CHUNK), F32),
                        pltpu.VMEM((RET_HEADS, RET_DK, RET_DV), F32)],
        compiler_params=_cparams("arbitrary"),
        name="retention",
    )(u, cos, sin, ng)


def _inv_unit_upper_many(mats, in_block, eye):
    d = [jnp.where(in_block, a, 0.0) for a in mats]
    f = [a - x for a, x in zip(mats, d)]
    d2 = [_mm(x, x) for x in d]
    d4 = [_mm(x, x) for x in d2]
    p = [_mm(eye + x, eye + y) for x, y in zip(d, d2)]
    d8 = [_mm(x, x) for x in d4]
    p = [_mm(x, eye + y) for x, y in zip(p, d4)]
    td = [_mm(x, eye + y) for x, y in zip(p, d8)]
    g = [_mm(x, y) for x, y in zip(f, td)]
    g2 = [_mm(x, x) for x in g]
    tg = [_mm(x, eye + y) for x, y in zip(td, g)]
    return [_mm(x, eye + y) for x, y in zip(tg, g2)]


def _rwkv_kernel(u_ref, mu_ref, w0_ref, a0_ref, kk_ref, ka_ref, rk_ref, ln_ref, lora_ref, tri_ref,
                 ones_ref, y_ref, prev_ref, state_ref, ot_ref):
    i = pl.program_id(0)

    @pl.when(i == 0)
    def _():
        prev_ref[...] = jnp.zeros_like(prev_ref)
        state_ref[...] = jnp.zeros_like(state_ref)

    u = u_ref[...]
    rows = lax.broadcasted_iota(jnp.int32, (CHUNK, 1), 0)
    u_prev = jnp.where(rows == 0, prev_ref[7:8, :], pltpu.roll(u, 1, axis=0))
    prev_ref[...] = u[CHUNK - 8:, :]
    us = u + (u_prev - u) * mu_ref[...]
    w3 = RWKV_WIDTH
    r = us[:, :w3]
    k = us[:, w3:2 * w3]
    v = us[:, 2 * w3:3 * w3]
    lo = us[:, 3 * w3:]
    w = w0_ref[...] + _mm(jnp.tanh(lo), lora_ref[0])
    w = -_softplus(-w) - 0.5
    ld = -jnp.exp(w)
    a = _sigmoid(a0_ref[...] + _mm(lo, lora_ref[1]))
    g = _mm(_sigmoid(lo), lora_ref[2])
    ones = ones_ref[...]
    kk = k * kk_ref[...]
    k2 = k * (1.0 + (a - 1.0) * ka_ref[...])
    kk = kk / jnp.maximum(jnp.sqrt(_dot01_right(kk * kk, ones)), 1e-12)
    b = kk * a

    lc_full = _dot01_left(tri_ref[...], ld)
    v_t = v.T

    n = RWKV_SUB
    r_i = lax.broadcasted_iota(jnp.int32, (n, n), 0)
    c_i = lax.broadcasted_iota(jnp.int32, (n, n), 1)
    in_block = (r_i // RWKV_INV_BLOCK) == (c_i // RWKV_INV_BLOCK)
    eye = jnp.where(r_i == c_i, 1.0, 0.0)
    hd = RWKV_HEAD_DIM
    n_sub = CHUNK // n
    heads = range(RWKV_HEADS)
    hs = [slice(h * hd, (h + 1) * hd) for h in heads]

    sc = []
    for sub in range(n_sub):
        lo_r, hi_r = sub * n, (sub + 1) * n
        lc = lc_full[lo_r:hi_r, :]
        if sub > 0:
            lc = lc - lc_full[lo_r - 1:lo_r, :]
        lc_last = lc[n - 1:n, :]
        e_pos = jnp.exp(lc)
        e_neg = jnp.exp(-lc)
        e_prev = jnp.exp(lc - ld[lo_r:hi_r, :])
        e_end = jnp.exp(lc_last - lc)
        sc.append(dict(
            gam=jnp.exp(lc_last),
            a_t=(-kk[lo_r:hi_r, :] * e_prev).astype(BF16),
            r_t=(r[lo_r:hi_r, :] * e_pos).astype(BF16),
            b_t=(b[lo_r:hi_r, :] * e_neg).astype(BF16),
            k_t=(k2[lo_r:hi_r, :] * e_neg).astype(BF16),
            b_h=(b[lo_r:hi_r, :] * e_end).astype(BF16),
            k_h=(k2[lo_r:hi_r, :] * e_end).astype(BF16),
            v_t=v_t[:, lo_r:hi_r].astype(BF16),
        ))

    pairs = [(sub, h) for sub in range(n_sub) for h in heads]
    bk = [jnp.concatenate([sc[sub]["b_t"][:, hs[h]], sc[sub]["k_t"][:, hs[h]]], axis=0) for sub, h in pairs]
    ar2 = [jnp.concatenate([sc[sub]["a_t"][:, hs[h]], sc[sub]["r_t"][:, hs[h]]], axis=0) for sub, h in pairs]
    r2 = lax.broadcasted_iota(jnp.int32, (n, 2 * n), 0)
    c2 = lax.broadcasted_iota(jnp.int32, (n, 2 * n), 1)
    mask2 = r2 < jnp.where(c2 < n, c2, c2 - n + 1)
    quad = [_mm_nt(x, y) for x, y in zip(bk, ar2)]
    top = [jnp.where(mask2, x[:n, :], 0.0) for x in quad]
    bot = [jnp.where(mask2, x[n:, :], 0.0).astype(BF16) for x in quad]
    vprod = [_mm(sc[sub]["v_t"][hs[h], :], jnp.concatenate([y, sc[sub]["k_h"][:, hs[h]]], axis=1))
             for y, (sub, h) in zip(bot, pairs)]
    t_t = _inv_unit_upper_many([x[:, :n] for x in top], in_block, eye)
    top = [x.astype(BF16) for x in top]

    state = [state_ref[h] for h in heads]
    for sub in range(n_sub):
        c = sc[sub]
        base = sub * RWKV_HEADS
        xr = [_mm_nt(state[h], ar2[base + h]) for h in heads]
        u_t = [_mm(xr[h][:, :n] + vprod[base + h][:, :n], t_t[base + h]) for h in heads]
        ub = [_mm(u_t[h], top[base + h]) for h in heads]
        su = [_mm(u_t[h], c["b_h"][:, hs[h]]) for h in heads]
        for h in heads:
            ot_ref[h * hd:(h + 1) * hd, sub * n:(sub + 1) * n] = (
                xr[h][:, n:] + ub[h][:, n:] + vprod[base + h][:, n:2 * n])
        state = [state[h] * c["gam"][:, hs[h]] + su[h] + vprod[base + h][:, 2 * n:] for h in heads]
    for h in heads:
        state_ref[h] = state[h]

    out = ot_ref[...].T
    inv_n = 1.0 / RWKV_HEAD_DIM
    mean = _dot01_right(out, ones) * inv_n
    cen = out - mean
    var = _dot01_right(cen * cen, ones) * inv_n
    out = cen * lax.rsqrt(var + RWKV_LN_EPS) * ln_ref[...]
    bonus = _dot01_right(r * k2 * rk_ref[...], ones) * v
    y_ref[...] = ((out + bonus) * g).astype(BF16)


def _rwkv(u, mu, w0, a0, k_k, k_a, r_k, ln, lora, tri, ones):
    lp = u.shape[0]
    full = lambda shape: pl.BlockSpec(shape, lambda i: (0,) * len(shape))
    vec = full((1, RWKV_WIDTH))
    return pl.pallas_call(
        _rwkv_kernel,
        grid=(lp // CHUNK,),
        in_specs=[pl.BlockSpec((CHUNK, RWKV_IN), lambda i: (i, 0)), full((1, RWKV_IN)),
                  vec, vec, vec, vec, vec, vec, full((3, 128, RWKV_WIDTH)), full((CHUNK, CHUNK)),
                  full((RWKV_WIDTH, RWKV_WIDTH))],
        out_specs=pl.BlockSpec((CHUNK, RWKV_WIDTH), lambda i: (i, 0)),
        out_shape=jax.ShapeDtypeStruct((lp, RWKV_WIDTH), BF16),
        scratch_shapes=[pltpu.VMEM((8, RWKV_IN), F32),
                        pltpu.VMEM((RWKV_HEADS, RWKV_HEAD_DIM, RWKV_HEAD_DIM), F32),
                        pltpu.VMEM((RWKV_WIDTH, CHUNK), F32)],
        compiler_params=_cparams("arbitrary"),
        name="rwkv7",
    )(u, mu, w0, a0, k_k, k_a, r_k, ln, lora, tri, ones)


def _constants():
    r = jnp.arange(CHUNK)
    tri = (r[:, None] >= r[None, :]).astype(BF16)
    lane = jnp.arange(RWKV_WIDTH)
    ones_bd = ((lane[:, None] // RWKV_HEAD_DIM) == (lane[None, :] // RWKV_HEAD_DIM)).astype(BF16)
    e8 = ((jnp.arange(128)[:, None] == (lane[None, :] // SSD_HEAD_DIM))).astype(BF16)
    return tri, ones_bd, e8


def kernel(x, meta_tokens, ffn1_norm, ffn1_w_gate, ffn1_w_up, ffn1_w_down, mix_norm, w_in, w_out, mla_q_norm, mla_w_q_up, mla_kv_norm, mla_w_kv_up, mla_qk_norm_q, mla_qk_norm_k, ssd_conv_w, ssd_conv_b, ssd_dt_bias, ssd_a_log, ssd_d, ssd_norm, ret_norm, rwkv_mu, rwkv_w0, rwkv_w2, rwkv_a0, rwkv_a2, rwkv_g2, rwkv_k_k, rwkv_k_a, rwkv_r_k, rwkv_ln, ffn2_norm, ffn2_w_gate, ffn2_w_up, ffn2_w_down):
    b, seq, d = x.shape
    assert b == 1 and d == D_MODEL and seq % CHUNK == 0
    nl = w_in.shape[0]
    lp = seq + CHUNK
    assert lp % TOKEN_TILE == 0

    hst = jnp.concatenate([jnp.zeros((PAD, d), x.dtype), meta_tokens.astype(x.dtype), x[0]], axis=0)

    o0 = MLA_IN
    o1 = o0 + SSD_IN
    o2 = o1 + RET_IN
    w_mla = jnp.pad(w_in[:, :, :o0], ((0, 0), (0, 0), (0, MLA_IN_PAD - MLA_IN))).astype(BF16)
    w_ssd = jnp.pad(w_in[:, :, o0:o1], ((0, 0), (0, 0), (0, SSD_IN_PAD - SSD_IN))).astype(BF16)
    w_ret = w_in[:, :, o1:o2].astype(BF16)
    w_rwkv = w_in[:, :, o2:].astype(BF16)
    w_o = w_out.reshape(nl, 4, 512, D_MODEL).astype(BF16)
    wq = jnp.pad(mla_w_q_up.reshape(nl, MLA_Q_LORA, MLA_HEADS, MLA_QK),
                 ((0, 0), (0, 0), (0, 0), (0, MLA_QK_PAD - MLA_QK))
                 ).reshape(nl, MLA_Q_LORA, MLA_HEADS * MLA_QK_PAD).astype(BF16)
    wkv = mla_w_kv_up.reshape(nl, MLA_KV_LORA, MLA_HEADS, 2, MLA_NOPE).transpose(0, 1, 3, 2, 4
                              ).reshape(nl, MLA_KV_LORA, 2 * MLA_HEADS * MLA_NOPE).astype(BF16)
    gq = jnp.pad(mla_qk_norm_q, ((0, 0), (0, MLA_QK_PAD - MLA_QK)))[:, None, :]
    gk = jnp.pad(mla_qk_norm_k, ((0, 0), (0, MLA_QK_PAD - MLA_QK)))[:, None, :]
    dtb = jnp.pad(ssd_dt_bias, ((0, 0), (0, 128 - SSD_HEADS)))[:, None, :]
    alog = jnp.pad(ssd_a_log, ((0, 0), (0, 128 - SSD_HEADS)))[:, None, :]
    dsk = jnp.repeat(ssd_d, SSD_HEAD_DIM, axis=1)[:, None, :]
    lora = jnp.zeros((nl, 3, 128, RWKV_WIDTH), F32)
    lora = lora.at[:, 0, :RWKV_DECAY_LORA].set(rwkv_w2)
    lora = lora.at[:, 1, RWKV_DECAY_LORA:RWKV_DECAY_LORA + RWKV_A_LORA].set(rwkv_a2)
    lora = lora.at[:, 2, RWKV_DECAY_LORA + RWKV_A_LORA:].set(rwkv_g2)
    lora = lora.astype(BF16)

    tri, ones_bd, e8 = _constants()
    cos, sin, sinm = _rope_tables(lp)

    for l in range(nl):
        hst = _ffn(hst, ffn1_norm[l][None, :], ffn1_w_gate, ffn1_w_up, ffn1_w_down, l)
        gmix = mix_norm[l][None, :]
        q, k, v = _mla_prep(hst, gmix, w_mla, mla_q_norm[l][None, :], wq, mla_kv_norm[l][None, :],
                            wkv, gq[l], gk[l], cos, sinm, l)
        y_mla = _attention(q, k, v)
        u_ssd = _inproj(hst, gmix, w_ssd, l)
        y_ssd = _ssd(u_ssd, ssd_conv_w[l], ssd_conv_b[l][None, :], dtb[l], alog[l], dsk[l],
                     ssd_norm[l][None, :], tri, e8)
        u_ret = _inproj(hst, gmix, w_ret, l)
        y_ret = _retention(u_ret, cos, sin, ret_norm[l].reshape(1, RET_WIDTH))
        u_rwkv = _inproj(hst, gmix, w_rwkv, l)
        y_rwkv = _rwkv(u_rwkv, rwkv_mu[l][None, :], rwkv_w0[l][None, :], rwkv_a0[l][None, :],
                       rwkv_k_k[l][None, :], rwkv_k_a[l][None, :], rwkv_r_k[l].reshape(1, RWKV_WIDTH),
                       rwkv_ln[l].reshape(1, RWKV_WIDTH), lora[l], tri, ones_bd)
        hst = _outproj(hst, (y_mla, y_ssd, y_ret, y_rwkv), w_o, l)
        hst = _ffn(hst, ffn2_norm[l][None, :], ffn2_w_gate, ffn2_w_up, ffn2_w_down, l)
    return hst[CHUNK:][None]
```

```python
import functools
import itertools
import math

import jax
import jax.numpy as jnp
from jax import lax
from jax.experimental import pallas as pl
from jax.experimental.pallas import tpu as pltpu

F32 = jnp.float32
BF16 = jnp.bfloat16

D_MODEL = 2048
DEPTH = 4
N_META = 16
CHUNK = 128
PAD = CHUNK - N_META
D_FF = 5632
EPS = 1e-6
ROPE_THETA = 10000.0
NEG_INF = -1e30

MLA_HEADS = 4
MLA_NOPE = 128
MLA_ROPE = 64
MLA_QK = MLA_NOPE + MLA_ROPE
MLA_V = 128
MLA_Q_LORA = 384
MLA_KV_LORA = 128
MLA_QK_PAD = 256
MLA_V_AUG = MLA_V + 16
MLA_IN = MLA_Q_LORA + MLA_KV_LORA + MLA_ROPE
MLA_IN_PAD = 640

SSD_HEADS = 8
SSD_HEAD_DIM = 64
SSD_WIDTH = 512
SSD_GROUPS = 2
SSD_STATE = 128
SSD_CONV = 4
SSD_CONV_CH = SSD_WIDTH + 2 * SSD_GROUPS * SSD_STATE
SSD_IN = SSD_WIDTH + SSD_CONV_CH + SSD_HEADS
SSD_IN_PAD = SSD_WIDTH + SSD_CONV_CH + 128

RET_HEADS = 4
RET_DK = 64
RET_DV = 128
RET_WIDTH = 512
RET_IN = 2 * RET_HEADS * RET_DK + 2 * RET_WIDTH

RWKV_HEADS = 8
RWKV_HEAD_DIM = 64
RWKV_WIDTH = 512
RWKV_DECAY_LORA = 32
RWKV_A_LORA = 32
RWKV_GATE_LORA = 64
RWKV_LN_EPS = 64e-5
RWKV_IN = 3 * RWKV_WIDTH + RWKV_DECAY_LORA + RWKV_A_LORA + RWKV_GATE_LORA
RWKV_SUB = 64
RWKV_INV_BLOCK = 8

MIX_WIDTH = 2048

V7X_VMEM_LIMIT_BYTES = 56 * 1024 * 1024
TOKEN_TILE = 640
FF_TILE = 512
FFN_CONVERT_ROWS = 256
PROJ_CONVERT_ROWS = 16


def _cparams(*sem):
    return pltpu.CompilerParams(dimension_semantics=sem, vmem_limit_bytes=V7X_VMEM_LIMIT_BYTES)


def _sigmoid(x):
    return 1.0 / (1.0 + jnp.exp(-x))


def _silu(x):
    return x * _sigmoid(x)


def _softplus(x):
    return jnp.maximum(x, 0.0) + jnp.log(1.0 + jnp.exp(-jnp.abs(x)))


def _rms(x, g, eps=EPS):
    return x * lax.rsqrt(jnp.mean(x * x, axis=-1, keepdims=True) + eps) * g


def _mm(a, b):
    return jnp.dot(a.astype(BF16), b.astype(BF16), preferred_element_type=F32)


def _mm_nt(a, b):
    return lax.dot_general(a.astype(BF16), b.astype(BF16), (((1,), (1,)), ((), ())),
                           preferred_element_type=F32)


def _split3(x):
    x1 = x.astype(BF16)
    r1 = x - x1.astype(F32)
    x2 = r1.astype(BF16)
    x3 = (r1 - x2.astype(F32)).astype(BF16)
    return x1, x2, x3


def _dot01_right(x, m01):
    p1, p2, p3 = _split3(x)
    return (jnp.dot(p1, m01, preferred_element_type=F32)
            + jnp.dot(p2, m01, preferred_element_type=F32)
            + jnp.dot(p3, m01, preferred_element_type=F32))


def _dot01_left(m01, x):
    p1, p2, p3 = _split3(x)
    return (jnp.dot(m01, p1, preferred_element_type=F32)
            + jnp.dot(m01, p2, preferred_element_type=F32)
            + jnp.dot(m01, p3, preferred_element_type=F32))


def _row_ids(rows, cols, base):
    return base + lax.broadcasted_iota(jnp.int32, (rows, cols), 0)


def _rope_table_kernel(inv_ref, cos_ref, sin_ref, sinm_ref):
    i = pl.program_id(0)
    pos = (_row_ids(CHUNK, 128, i * CHUNK) - PAD).astype(F32)
    lane = lax.broadcasted_iota(jnp.int32, (CHUNK, 128), 1)
    ang = pos * inv_ref[...]
    c = jnp.cos(ang)
    s = jnp.sin(ang)
    s = jnp.where((lane % 64) < 32, -s, s)
    cos_ref[...] = c
    sin_ref[...] = s
    sinm_ref[...] = jnp.where(lane < 64, s, 0.0)


def _rope_tables(lp):
    half = 32
    inv = ROPE_THETA ** (-jnp.arange(half, dtype=F32) / half)
    inv = jnp.tile(inv, 4)[None, :]
    out = jax.ShapeDtypeStruct((lp, 128), F32)
    spec = pl.BlockSpec((CHUNK, 128), lambda i: (i, 0))
    return pl.pallas_call(
        _rope_table_kernel,
        grid=(lp // CHUNK,),
        in_specs=[pl.BlockSpec((1, 128), lambda i: (0, 0))],
        out_specs=[spec, spec, spec],
        out_shape=[out, out, out],
        compiler_params=_cparams("parallel"),
        name="rope_tables",
    )(inv)


def _rope_lanes(x, cos, sin_signed):
    lane = lax.broadcasted_iota(jnp.int32, x.shape, 1)
    fwd = pltpu.roll(x, 32, axis=1)
    bwd = pltpu.roll(x, 96, axis=1)
    rot = jnp.where((lane % 64) < 32, bwd, fwd)
    return x * cos + rot * sin_signed


def _ffn_kernel(*refs, convert_next, convert_proj):
    refs = list(refs)
    x_ref, g_ref, wg_ref, wu_ref, wd_ref = refs[:5]
    n_in = 5 + (3 if convert_next else 0) + (2 if convert_proj else 0)
    ins, outs, xn_ref = refs[5:n_in], refs[n_in:-1], refs[-1]
    o_ref = outs.pop(0)
    if convert_next:
        for src, dst in zip(ins[:3], outs[:3]):
            dst[...] = src[...].astype(BF16)
        ins, outs = ins[3:], outs[3:]
    if convert_proj:
        win_ref, wout_ref = ins
        wmla_ref, wssd_ref, wret_ref, wrwkv_ref, wo_ref = outs
        o0 = MLA_IN
        o1 = o0 + SSD_IN
        o2 = o1 + RET_IN
        rows = win_ref.shape[0]
        wmla_ref[:, :MLA_IN] = win_ref[:, :o0].astype(BF16)
        wmla_ref[:, MLA_IN:] = jnp.zeros((rows, MLA_IN_PAD - MLA_IN), BF16)
        wssd_ref[:, :SSD_IN] = win_ref[:, o0:o1].astype(BF16)
        wssd_ref[:, SSD_IN:] = jnp.zeros((rows, SSD_IN_PAD - SSD_IN), BF16)
        wret_ref[...] = win_ref[:, o1:o2].astype(BF16)
        wrwkv_ref[...] = win_ref[:, o2:].astype(BF16)
        wo_ref[...] = wout_ref[...].astype(BF16)
    j = pl.program_id(1)

    @pl.when(j == 0)
    def _():
        xn_ref[...] = _rms(x_ref[...], g_ref[...]).astype(BF16)
        o_ref[...] = jnp.zeros_like(o_ref)

    xn = xn_ref[...]
    a = jnp.dot(xn, wg_ref[...], preferred_element_type=F32)
    b = jnp.dot(xn, wu_ref[...], preferred_element_type=F32)
    mid = (_silu(a) * b).astype(BF16)
    o_ref[...] += jnp.dot(mid, wd_ref[...], preferred_element_type=F32)

    @pl.when(j == pl.num_programs(1) - 1)
    def _():
        o_ref[...] = x_ref[...] + 0.5 * o_ref[...]


def _ffn(h, g, wgb, wub, wdb, nxt=None, proj=None):
    lp = h.shape[0]
    tm, tf = TOKEN_TILE, FF_TILE
    n_j = D_FF // tf
    n_c = D_MODEL // FFN_CONVERT_ROWS if nxt is not None else 0
    in_specs = [
        pl.BlockSpec((tm, D_MODEL), lambda i, j: (i, 0)),
        pl.BlockSpec((1, D_MODEL), lambda i, j: (0, 0)),
        pl.BlockSpec((D_MODEL, tf), lambda i, j: (0, j)),
        pl.BlockSpec((D_MODEL, tf), lambda i, j: (0, j)),
        pl.BlockSpec((tf, D_MODEL), lambda i, j: (j, 0)),
    ]
    out_specs = [pl.BlockSpec((tm, D_MODEL), lambda i, j: (i, 0))]
    out_shape = [jax.ShapeDtypeStruct((lp, D_MODEL), F32)]
    args = [h, g, wgb, wub, wdb]
    if nxt is not None:
        wg, wu, wd, l = nxt
        cr = FFN_CONVERT_ROWS
        assert lp // tm >= n_c
        ci = lambda i: jnp.minimum(i, n_c - 1)
        cj = lambda i, j: jnp.where(i < n_c, j, n_j - 1)
        in_specs += [
            pl.BlockSpec((None, cr, tf), lambda i, j: (l, ci(i), cj(i, j))),
            pl.BlockSpec((None, cr, tf), lambda i, j: (l, ci(i), cj(i, j))),
            pl.BlockSpec((None, tf, cr), lambda i, j: (l, cj(i, j), ci(i))),
        ]
        out_specs += [
            pl.BlockSpec((cr, tf), lambda i, j: (ci(i), cj(i, j))),
            pl.BlockSpec((cr, tf), lambda i, j: (ci(i), cj(i, j))),
            pl.BlockSpec((tf, cr), lambda i, j: (cj(i, j), ci(i))),
        ]
        out_shape += [
            jax.ShapeDtypeStruct((D_MODEL, D_FF), BF16),
            jax.ShapeDtypeStruct((D_MODEL, D_FF), BF16),
            jax.ShapeDtypeStruct((D_FF, D_MODEL), BF16),
        ]
        args += [wg, wu, wd]
    if proj is not None:
        w_in, w_out, lpj = proj
        pr = PROJ_CONVERT_ROWS
        n_p = D_MODEL // pr
        assert (lp // tm) * n_j >= n_p
        pidx = lambda i, j: jnp.minimum(i * n_j + j, n_p - 1)
        in_specs += [
            pl.BlockSpec((None, pr, w_in.shape[2]), lambda i, j: (lpj, pidx(i, j), 0)),
            pl.BlockSpec((None, pr, D_MODEL), lambda i, j: (lpj, pidx(i, j), 0)),
        ]
        widths = (MLA_IN_PAD, SSD_IN_PAD, RET_IN, RWKV_IN, D_MODEL)
        out_specs += [pl.BlockSpec((pr, n), lambda i, j: (pidx(i, j), 0)) for n in widths]
        out_shape += [jax.ShapeDtypeStruct((D_MODEL, n), BF16) for n in widths]
        args += [w_in, w_out]
    outs = pl.pallas_call(
        functools.partial(_ffn_kernel, convert_next=nxt is not None, convert_proj=proj is not None),
        grid=(lp // tm, n_j),
        in_specs=in_specs,
        out_specs=out_specs,
        out_shape=out_shape,
        scratch_shapes=[pltpu.VMEM((tm, D_MODEL), BF16)],
        compiler_params=_cparams("arbitrary", "arbitrary"),
        name="ffn",
    )(*args)
    n_next = 3 if nxt is not None else 0
    return (outs[0], tuple(outs[1:1 + n_next]) if nxt is not None else None,
            tuple(outs[1 + n_next:]) if proj is not None else None)


def _inproj_kernel(x_ref, g_ref, w_ref, o_ref):
    i = pl.program_id(0)
    tm = x_ref.shape[0]
    xn = _rms(x_ref[...], g_ref[...]).astype(BF16)
    u = jnp.dot(xn, w_ref[...], preferred_element_type=F32)
    row = _row_ids(tm, 1, i * tm)
    o_ref[...] = jnp.where(row >= PAD, u, 0.0)


def _inproj(h, g, w):
    lp = h.shape[0]
    n = w.shape[1]
    tm = TOKEN_TILE
    return pl.pallas_call(
        _inproj_kernel,
        grid=(lp // tm,),
        in_specs=[
            pl.BlockSpec((tm, D_MODEL), lambda i: (i, 0)),
            pl.BlockSpec((1, D_MODEL), lambda i: (0, 0)),
            pl.BlockSpec((D_MODEL, n), lambda i: (0, 0)),
        ],
        out_specs=pl.BlockSpec((tm, n), lambda i: (i, 0)),
        out_shape=jax.ShapeDtypeStruct((lp, n), F32),
        compiler_params=_cparams("parallel"),
        name="inproj",
    )(h, g, w)


def _outproj_kernel(h_ref, y0_ref, y1_ref, y2_ref, y3_ref, w_ref, o_ref):
    acc = h_ref[...]
    for m, y_ref in enumerate((y0_ref, y1_ref, y2_ref, y3_ref)):
        acc = acc + jnp.dot(y_ref[...], w_ref[m], preferred_element_type=F32)
    o_ref[...] = acc


def _outproj(h, ys, w):
    lp = h.shape[0]
    tm = TOKEN_TILE
    yspec = pl.BlockSpec((tm, 512), lambda i: (i, 0))
    return pl.pallas_call(
        _outproj_kernel,
        grid=(lp // tm,),
        in_specs=[pl.BlockSpec((tm, D_MODEL), lambda i: (i, 0)), yspec, yspec, yspec, yspec,
                  pl.BlockSpec((4, 512, D_MODEL), lambda i: (0, 0, 0))],
        out_specs=pl.BlockSpec((tm, D_MODEL), lambda i: (i, 0)),
        out_shape=jax.ShapeDtypeStruct((lp, D_MODEL), F32),
        compiler_params=_cparams("parallel"),
        name="outproj",
    )(h, *ys, w)


def _mla_prep_kernel(x_ref, g_ref, win_ref, qn_ref, wq_ref, kvn_ref, wkv_ref, gq_ref, gk_ref,
                     cos_ref, sinm_ref, q_ref, k_ref, v_ref):
    i = pl.program_id(0)
    tm = x_ref.shape[0]
    xn = _rms(x_ref[...], g_ref[...]).astype(BF16)
    u = jnp.dot(xn, win_ref[...], preferred_element_type=F32)
    row = _row_ids(tm, 1, i * tm)
    u = jnp.where(row >= PAD, u, 0.0)
    cq = u[:, :MLA_Q_LORA]
    ckv = u[:, MLA_Q_LORA:MLA_Q_LORA + MLA_KV_LORA]
    kpe = u[:, MLA_Q_LORA + MLA_KV_LORA:]
    q_all = _mm(_rms(cq, qn_ref[...]), wq_ref[...])
    kv_all = _mm(_rms(ckv, kvn_ref[...]), wkv_ref[...])
    cos = cos_ref[...]
    sinm = sinm_ref[...]
    gq = gq_ref[...]
    gk = gk_ref[...]
    scale = MLA_QK ** -0.5
    for h in range(MLA_HEADS):
        qh = q_all[:, h * MLA_QK_PAD:(h + 1) * MLA_QK_PAD]
        ss = jnp.sum(qh * qh, axis=-1, keepdims=True) * (1.0 / MLA_QK)
        qh = qh * lax.rsqrt(ss + EPS) * gq
        q_rot = _rope_lanes(qh[:, MLA_NOPE:], cos, sinm)
        q_ref[h, :, :MLA_NOPE] = (qh[:, :MLA_NOPE] * scale).astype(BF16)
        q_ref[h, :, MLA_NOPE:] = (q_rot * scale).astype(BF16)

        kn = kv_all[:, h * MLA_NOPE:(h + 1) * MLA_NOPE]
        ss = (jnp.sum(kn * kn, axis=-1, keepdims=True)
              + jnp.sum(kpe * kpe, axis=-1, keepdims=True)) * (1.0 / MLA_QK)
        rs = lax.rsqrt(ss + EPS)
        k_rot = _rope_lanes(kpe * rs * gk[:, MLA_NOPE:], cos, sinm)
        k_ref[h, :, :MLA_NOPE] = (kn * rs * gk[:, :MLA_NOPE]).astype(BF16)
        k_ref[h, :, MLA_NOPE:] = k_rot.astype(BF16)
        v_ref[h, 0, :MLA_V, :] = kv_all[:, 4 * MLA_NOPE + h * MLA_V: 4 * MLA_NOPE + (h + 1) * MLA_V].T.astype(BF16)
        tail = lax.broadcasted_iota(jnp.int32, (MLA_V_AUG - MLA_V, tm), 0)
        v_ref[h, 0, MLA_V:, :] = jnp.where(tail == 0, 1.0, 0.0).astype(BF16)


def _mla_prep(h, g, win, qn, wq, kvn, wkv, gq, gk, cos, sinm, l):
    lp = h.shape[0]
    tm = TOKEN_TILE
    full = lambda shape: pl.BlockSpec(shape, lambda i: (0,) * len(shape))
    layer = lambda shape: pl.BlockSpec((None,) + shape, lambda i: (l,) + (0,) * len(shape))
    return pl.pallas_call(
        _mla_prep_kernel,
        grid=(lp // tm,),
        in_specs=[
            pl.BlockSpec((tm, D_MODEL), lambda i: (i, 0)),
            full((1, D_MODEL)), full((D_MODEL, MLA_IN_PAD)),
            full((1, MLA_Q_LORA)), layer((MLA_Q_LORA, MLA_HEADS * MLA_QK_PAD)),
            full((1, MLA_KV_LORA)), layer((MLA_KV_LORA, 2 * MLA_HEADS * MLA_NOPE)),
            full((1, MLA_QK_PAD)), full((1, MLA_QK_PAD)),
            pl.BlockSpec((tm, 128), lambda i: (i, 0)),
            pl.BlockSpec((tm, 128), lambda i: (i, 0)),
        ],
        out_specs=[
            pl.BlockSpec((MLA_HEADS, tm, MLA_QK_PAD), lambda i: (0, i, 0)),
            pl.BlockSpec((MLA_HEADS, tm, MLA_QK_PAD), lambda i: (0, i, 0)),
            pl.BlockSpec((MLA_HEADS, 1, MLA_V_AUG, tm), lambda i: (0, i, 0, 0)),
        ],
        out_shape=[
            jax.ShapeDtypeStruct((MLA_HEADS, lp, MLA_QK_PAD), BF16),
            jax.ShapeDtypeStruct((MLA_HEADS, lp, MLA_QK_PAD), BF16),
            jax.ShapeDtypeStruct((MLA_HEADS, lp // tm, MLA_V_AUG, tm), BF16),
        ],
        compiler_params=_cparams("parallel"),
        name="mla_prep",
    )(h, g, win, qn, wq, kvn, wkv, gq, gk, cos, sinm)


def _attn_kernel(q_ref, k_ref, vt_ref, o_ref):
    i = pl.program_id(1)
    tq = q_ref.shape[1]
    q = q_ref[0]
    key0 = lax.broadcasted_iota(jnp.int32, (tq, tq), 0)
    qry = i * tq + lax.broadcasted_iota(jnp.int32, (tq, tq), 1)

    def scores(j, masked):
        start = pl.multiple_of(j * tq, tq)
        k = k_ref[0, pl.ds(start, tq), :]
        s = lax.dot_general(k, q, (((1,), (1,)), ((), ())), preferred_element_type=F32)
        if masked:
            key = key0 + j * tq
            s = jnp.where(key <= qry, jnp.where(key >= PAD, s, NEG_INF), NEG_INF)
        return s

    def update(j, s, carry):
        m, acc = carry
        m_new = jnp.maximum(m, jnp.max(s, axis=0, keepdims=True))
        alpha = jnp.exp(m - m_new)
        p = jnp.exp((s - m_new).astype(BF16))
        acc = alpha * acc + jnp.dot(vt_ref[0, j], p, preferred_element_type=F32)
        return m_new, acc

    init = (jnp.full((1, tq), NEG_INF, F32), jnp.zeros((MLA_V_AUG, tq), F32))
    s_first = scores(0, True)

    def body(j, c):
        s_prev, carry = c
        s_new = scores(j, False)
        return s_new, update(j - 1, s_prev, carry)

    s_last, carry = lax.fori_loop(1, i, body, (s_first, init))
    j_last = jnp.maximum(i - 1, 0)
    m, acc = lax.cond(i > 0,
                      lambda c: update(i, scores(i, True), update(j_last, s_last, c)),
                      lambda c: update(j_last, s_last, c), carry)
    o_ref[...] = (acc[:MLA_V] / acc[MLA_V:MLA_V + 1]).T.astype(BF16)


def _attention(q, k, v):
    lp = q.shape[1]
    tq = TOKEN_TILE
    return pl.pallas_call(
        _attn_kernel,
        grid=(MLA_HEADS, lp // tq),
        in_specs=[
            pl.BlockSpec((1, tq, MLA_QK_PAD), lambda h, i: (h, i, 0)),
            pl.BlockSpec((1, lp, MLA_QK_PAD), lambda h, i: (h, 0, 0)),
            pl.BlockSpec((1, lp // tq, MLA_V_AUG, tq), lambda h, i: (h, 0, 0, 0)),
        ],
        out_specs=pl.BlockSpec((tq, MLA_V), lambda h, i: (i, h)),
        out_shape=jax.ShapeDtypeStruct((lp, MLA_HEADS * MLA_V), BF16),
        compiler_params=_cparams("parallel", "arbitrary"),
        name="mla_attention",
    )(q, k, v)


def _shift_rows(x, carry, s):
    rolled = pltpu.roll(x, s, axis=0)
    head = pltpu.roll(carry, s, axis=0)
    r8 = lax.broadcasted_iota(jnp.int32, (8, 1), 0)
    return jnp.concatenate([jnp.where(r8 < s, head, rolled[:8]), rolled[8:]], axis=0)


def _ssd_init(carry_ref, state_ref):
    carry_ref[...] = jnp.zeros_like(carry_ref)
    state_ref[...] = jnp.zeros_like(state_ref)


def _ssd_body(u_ref, cw_ref, cb_ref, dtb_ref, alog_ref, dsk_ref, ng_ref, tri_ref, e8_ref,
              y_ref, carry_ref, state_ref):
    i = pl.program_id(0)
    z = u_ref[:, :SSD_WIDTH]
    xbc = u_ref[:, SSD_WIDTH:SSD_WIDTH + SSD_CONV_CH]
    dt_raw = u_ref[:, SSD_WIDTH + SSD_CONV_CH:]
    carry = carry_ref[...]
    cw = cw_ref[...]
    conv = xbc * cw[3:4, :]
    for s in (1, 2, 3):
        conv = conv + _shift_rows(xbc, carry, s) * cw[3 - s:4 - s, :]
    carry_ref[...] = xbc[CHUNK - 8:, :]
    xbc = _silu(conv + cb_ref[...])
    xs = xbc[:, :SSD_WIDTH]
    bm = xbc[:, SSD_WIDTH:SSD_WIDTH + SSD_GROUPS * SSD_STATE]
    cm = xbc[:, SSD_WIDTH + SSD_GROUPS * SSD_STATE:]

    row = _row_ids(CHUNK, 1, i * CHUNK)
    dt = _softplus(dt_raw + dtb_ref[...]) * jnp.where(row >= PAD, 1.0, 0.0)
    la = dt * (-jnp.exp(alog_ref[...]))
    tri = tri_ref[...]
    e8 = e8_ref[...]
    cs = _dot01_left(tri, la)
    cs_e = _dot01_right(cs, e8)
    dt_e = _dot01_right(dt, e8)
    cs_t = cs.T
    cs_last_e = cs_e[CHUNK - 1:CHUNK, :]
    x = xs * dt_e
    xd = x * jnp.exp(cs_last_e - cs_e)
    ecs = jnp.exp(cs_e)
    dec = jnp.exp(cs_last_e)

    r_i = lax.broadcasted_iota(jnp.int32, (CHUNK, CHUNK), 0)
    c_i = lax.broadcasted_iota(jnp.int32, (CHUNK, CHUNK), 1)
    causal = r_i >= c_i
    per = SSD_HEADS // SSD_GROUPS
    gw = per * SSD_HEAD_DIM
    groups = range(SSD_GROUPS)
    b_gs = [bm[:, g * SSD_STATE:(g + 1) * SSD_STATE] for g in groups]
    c_gs = [cm[:, g * SSD_STATE:(g + 1) * SSD_STATE] for g in groups]
    s_prev = [state_ref[g] for g in groups]
    yield
    scores = [_mm_nt(c_gs[g], b_gs[g]) for g in groups]
    y_off = [_mm(c_gs[g], s_prev[g]) for g in groups]
    s_add = [_mm(b_gs[g].T, xd[:, g * gw:(g + 1) * gw]) for g in groups]
    yield
    lmats = []
    for h in range(SSD_HEADS):
        seg = cs[:, h:h + 1] - cs_t[h:h + 1, :]
        lmats.append(jnp.where(causal, jnp.exp(jnp.where(causal, seg, 0.0)), 0.0))
    y_diag = [_mm(scores[h // per] * lmats[h], x[:, h * SSD_HEAD_DIM:(h + 1) * SSD_HEAD_DIM])
              for h in range(SSD_HEADS)]
    yield
    for g in groups:
        state_ref[g] = s_prev[g] * dec[:, g * gw:(g + 1) * gw] + s_add[g]
    y = (jnp.concatenate(y_diag, axis=1) + jnp.concatenate(y_off, axis=1) * ecs) + dsk_ref[...] * xs
    y = y * _silu(z)
    y_ref[...] = _rms(y, ng_ref[...]).astype(BF16)


def _ssd_kernel(*refs):
    pl.when(pl.program_id(0) == 0)(lambda: _ssd_init(*refs[-2:]))
    for _ in _ssd_body(*refs):
        pass


def _ssd(u, cw, cb, dtb, alog, dsk, ng, tri, e8):
    lp = u.shape[0]
    full = lambda shape: pl.BlockSpec(shape, lambda i: (0,) * len(shape))
    return pl.pallas_call(
        _ssd_kernel,
        grid=(lp // CHUNK,),
        in_specs=[pl.BlockSpec((CHUNK, SSD_IN_PAD), lambda i: (i, 0)),
                  full((SSD_CONV, SSD_CONV_CH)), full((1, SSD_CONV_CH)), full((1, 128)), full((1, 128)),
                  full((1, SSD_WIDTH)), full((1, SSD_WIDTH)), full((CHUNK, CHUNK)), full((128, SSD_WIDTH))],
        out_specs=pl.BlockSpec((CHUNK, SSD_WIDTH), lambda i: (i, 0)),
        out_shape=jax.ShapeDtypeStruct((lp, SSD_WIDTH), BF16),
        scratch_shapes=[pltpu.VMEM((8, SSD_CONV_CH), F32),
                        pltpu.VMEM((SSD_GROUPS, SSD_STATE, SSD_WIDTH // SSD_GROUPS), F32)],
        compiler_params=_cparams("arbitrary"),
        name="ssd",
    )(u, cw, cb, dtb, alog, dsk, ng, tri, e8)


RET_LOG_G = [math.log(1.0 - 2.0 ** (-5.0 - h)) for h in range(RET_HEADS)]


def _ret_init(dmat_ref, state_ref):
    state_ref[...] = jnp.zeros_like(state_ref)
    r_i = lax.broadcasted_iota(jnp.int32, (CHUNK, CHUNK), 0)
    c_i = lax.broadcasted_iota(jnp.int32, (CHUNK, CHUNK), 1)
    diff = (r_i - c_i).astype(F32)
    for h in range(RET_HEADS):
        dmat_ref[h] = jnp.where(r_i >= c_i, jnp.exp(jnp.where(r_i >= c_i, diff, 0.0) * RET_LOG_G[h]), 0.0)


def _ret_body(u_ref, cos_ref, sin_ref, ng_ref, y_ref, dmat_ref, state_ref):
    log_g = RET_LOG_G
    qkw = RET_HEADS * RET_DK
    cos = cos_ref[...]
    sin = sin_ref[...]
    q = jnp.concatenate([_rope_lanes(u_ref[:, c * 128:(c + 1) * 128], cos, sin) for c in range(2)], axis=1)
    k = jnp.concatenate([_rope_lanes(u_ref[:, qkw + c * 128:qkw + (c + 1) * 128], cos, sin)
                         for c in range(2)], axis=1) * (RET_DK ** -0.5)
    idx = lax.broadcasted_iota(jnp.int32, (CHUNK, 1), 0).astype(F32)
    k_t = k.T
    idx_row = lax.broadcasted_iota(jnp.int32, (1, CHUNK), 1).astype(F32)
    heads = range(RET_HEADS)
    q_hs = [q[:, h * RET_DK:(h + 1) * RET_DK] for h in heads]
    k_hs = [k[:, h * RET_DK:(h + 1) * RET_DK] for h in heads]
    v_hs = [u_ref[:, 2 * qkw + h * RET_DV: 2 * qkw + (h + 1) * RET_DV].astype(BF16) for h in heads]
    s_prevs = [state_ref[h] for h in heads]
    yield
    scs = [_mm_nt(q_hs[h], k_hs[h]) for h in heads]
    y_cross = [_mm(q_hs[h] * jnp.exp((idx + 1.0) * log_g[h]), s_prevs[h]) for h in heads]
    yield
    s_adds = [_mm(k_t[h * RET_DK:(h + 1) * RET_DK, :] * jnp.exp((CHUNK - 1 - idx_row) * log_g[h]), v_hs[h])
              for h in heads]
    y_in = [_mm(scs[h] * dmat_ref[h], v_hs[h]) for h in heads]
    yield
    for h in heads:
        g_h = u_ref[:, 2 * qkw + RET_WIDTH + h * RET_DV: 2 * qkw + RET_WIDTH + (h + 1) * RET_DV]
        y = y_in[h] + y_cross[h]
        state_ref[h] = s_prevs[h] * math.exp(CHUNK * log_g[h]) + s_adds[h]
        mu = jnp.mean(y, axis=-1, keepdims=True)
        var = jnp.mean(jnp.square(y - mu), axis=-1, keepdims=True)
        yn = (y - mu) * lax.rsqrt(var + EPS) * ng_ref[:, h * RET_DV:(h + 1) * RET_DV]
        y_ref[:, h * RET_DV:(h + 1) * RET_DV] = (_silu(g_h) * yn).astype(BF16)


def _ret_kernel(*refs):
    pl.when(pl.program_id(0) == 0)(lambda: _ret_init(*refs[-2:]))
    for _ in _ret_body(*refs):
        pass


def _retention(u, cos, sin, ng):
    lp = u.shape[0]
    return pl.pallas_call(
        _ret_kernel,
        grid=(lp // CHUNK,),
        in_specs=[pl.BlockSpec((CHUNK, RET_IN), lambda i: (i, 0)),
                  pl.BlockSpec((CHUNK, 128), lambda i: (i, 0)),
                  pl.BlockSpec((CHUNK, 128), lambda i: (i, 0)),
                  pl.BlockSpec((1, RET_WIDTH), lambda i: (0, 0))],
        out_specs=pl.BlockSpec((CHUNK, RET_WIDTH), lambda i: (i, 0)),
        out_shape=jax.ShapeDtypeStruct((lp, RET_WIDTH), BF16),
        scratch_shapes=[pltpu.VMEM((RET_HEADS, CHUNK, CHUNK), F32),
                        pltpu.VMEM((RET_HEADS, RET_DK, RET_DV), F32)],
        compiler_params=_cparams("arbitrary"),
        name="retention",
    )(u, cos, sin, ng)


def _inv_unit_upper_many(mats, in_block, eye):
    d = [jnp.where(in_block, a, 0.0) for a in mats]
    f = [a - x for a, x in zip(mats, d)]
    d2 = [_mm(x, x) for x in d]
    yield
    d4 = [_mm(x, x) for x in d2]
    p = [_mm(eye + x, eye + y) for x, y in zip(d, d2)]
    yield
    td = [_mm(x, eye + y) for x, y in zip(p, d4)]
    yield
    g = [_mm(x, y) for x, y in zip(f, td)]
    yield
    g2 = [_mm(x, x) for x in g]
    tg = [_mm(x, eye + y) for x, y in zip(td, g)]
    yield
    g4 = [_mm(x, x) for x in g2]
    tg = [_mm(x, eye + y) for x, y in zip(tg, g2)]
    yield
    return [_mm(x, eye + y) for x, y in zip(tg, g4)]


def _rwkv_init(prev_ref, state_ref):
    prev_ref[...] = jnp.zeros_like(prev_ref)
    state_ref[...] = jnp.zeros_like(state_ref)


def _rwkv_body(u_ref, mu_ref, w0_ref, a0_ref, kk_ref, ka_ref, rk_ref, ln_ref, lora_ref, tri_ref,
               ones_ref, y_ref, prev_ref, state_ref, ot_ref):
    u = u_ref[...]
    rows = lax.broadcasted_iota(jnp.int32, (CHUNK, 1), 0)
    u_prev = jnp.where(rows == 0, prev_ref[7:8, :], pltpu.roll(u, 1, axis=0))
    prev_ref[...] = u[CHUNK - 8:, :]
    us = u + (u_prev - u) * mu_ref[...]
    w3 = RWKV_WIDTH
    r = us[:, :w3]
    k = us[:, w3:2 * w3]
    v = us[:, 2 * w3:3 * w3]
    lo = us[:, 3 * w3:]
    w = w0_ref[...] + _mm(jnp.tanh(lo), lora_ref[0])
    w = -_softplus(-w) - 0.5
    ld = -jnp.exp(w)
    a = _sigmoid(a0_ref[...] + _mm(lo, lora_ref[1]))
    g = _mm(_sigmoid(lo), lora_ref[2])
    ones = ones_ref[...]
    kk = k * kk_ref[...]
    k2 = k * (1.0 + (a - 1.0) * ka_ref[...])
    kk = kk / jnp.maximum(jnp.sqrt(_dot01_right(kk * kk, ones)), 1e-12)
    b = kk * a

    lc_full = _dot01_left(tri_ref[...], ld)
    v_t = v.T

    n = RWKV_SUB
    r_i = lax.broadcasted_iota(jnp.int32, (n, n), 0)
    c_i = lax.broadcasted_iota(jnp.int32, (n, n), 1)
    in_block = (r_i // RWKV_INV_BLOCK) == (c_i // RWKV_INV_BLOCK)
    eye = jnp.where(r_i == c_i, 1.0, 0.0)
    hd = RWKV_HEAD_DIM
    n_sub = CHUNK // n
    heads = range(RWKV_HEADS)
    hs = [slice(h * hd, (h + 1) * hd) for h in heads]

    sc = []
    for sub in range(n_sub):
        lo_r, hi_r = sub * n, (sub + 1) * n
        lc = lc_full[lo_r:hi_r, :]
        if sub > 0:
            lc = lc - lc_full[lo_r - 1:lo_r, :]
        lc_last = lc[n - 1:n, :]
        e_pos = jnp.exp(lc)
        e_neg = jnp.exp(-lc)
        e_prev = jnp.exp(lc - ld[lo_r:hi_r, :])
        e_end = jnp.exp(lc_last - lc)
        sc.append(dict(
            gam=jnp.exp(lc_last),
            a_t=(-kk[lo_r:hi_r, :] * e_prev).astype(BF16),
            r_t=(r[lo_r:hi_r, :] * e_pos).astype(BF16),
            b_t=(b[lo_r:hi_r, :] * e_neg).astype(BF16),
            k_t=(k2[lo_r:hi_r, :] * e_neg).astype(BF16),
            b_h=(b[lo_r:hi_r, :] * e_end).astype(BF16),
            k_h=(k2[lo_r:hi_r, :] * e_end).astype(BF16),
            v_t=v_t[:, lo_r:hi_r].astype(BF16),
        ))

    pairs = [(sub, h) for sub in range(n_sub) for h in heads]
    yield
    bk = [jnp.concatenate([sc[sub]["b_t"][:, hs[h]], sc[sub]["k_t"][:, hs[h]]], axis=0) for sub, h in pairs]
    ar2 = [jnp.concatenate([sc[sub]["a_t"][:, hs[h]], sc[sub]["r_t"][:, hs[h]]], axis=0) for sub, h in pairs]
    r2 = lax.broadcasted_iota(jnp.int32, (n, 2 * n), 0)
    c2 = lax.broadcasted_iota(jnp.int32, (n, 2 * n), 1)
    mask2 = r2 < jnp.where(c2 < n, c2, c2 - n + 1)
    quad = [_mm_nt(x, y) for x, y in zip(bk, ar2)]
    top = [jnp.where(mask2, x[:n, :], 0.0) for x in quad]
    bot = [jnp.where(mask2, x[n:, :], 0.0).astype(BF16) for x in quad]
    yield
    vprod = [_mm(sc[sub]["v_t"][hs[h], :], jnp.concatenate([y, sc[sub]["k_h"][:, hs[h]]], axis=1))
             for y, (sub, h) in zip(bot, pairs)]
    yield
    t_t = yield from _inv_unit_upper_many([x[:, :n] for x in top], in_block, eye)
    top = [x.astype(BF16) for x in top]

    state = [state_ref[h] for h in heads]
    for sub in range(n_sub):
        c = sc[sub]
        base = sub * RWKV_HEADS
        yield
        xr = [_mm_nt(state[h], ar2[base + h]) for h in heads]
        yield
        u_t = [_mm(xr[h][:, :n] + vprod[base + h][:, :n], t_t[base + h]) for h in heads]
        yield
        ub = [_mm(u_t[h], top[base + h]) for h in heads]
        su = [_mm(u_t[h], c["b_h"][:, hs[h]]) for h in heads]
        for h in heads:
            ot_ref[h * hd:(h + 1) * hd, sub * n:(sub + 1) * n] = (
                xr[h][:, n:] + ub[h][:, n:] + vprod[base + h][:, n:2 * n])
        state = [state[h] * c["gam"][:, hs[h]] + su[h] + vprod[base + h][:, 2 * n:] for h in heads]
    for h in heads:
        state_ref[h] = state[h]
    yield

    out = ot_ref[...].T
    inv_n = 1.0 / RWKV_HEAD_DIM
    mean = _dot01_right(out, ones) * inv_n
    cen = out - mean
    var = _dot01_right(cen * cen, ones) * inv_n
    out = cen * lax.rsqrt(var + RWKV_LN_EPS) * ln_ref[...]
    bonus = _dot01_right(r * k2 * rk_ref[...], ones) * v
    y_ref[...] = ((out + bonus) * g).astype(BF16)


def _rwkv_kernel(*refs):
    pl.when(pl.program_id(0) == 0)(lambda: _rwkv_init(*refs[-3:-1]))
    for _ in _rwkv_body(*refs):
        pass


N_SSD_IN, N_RET_IN, N_RWKV_IN = 9, 4, 11


def _mixers_kernel(*refs):
    n_in = N_SSD_IN + N_RET_IN + N_RWKV_IN
    ins, (y_ssd, y_ret, y_rwkv), scr = refs[:n_in], refs[n_in:n_in + 3], refs[n_in + 3:]
    ssd_in, ret_in, rwkv_in = ins[:N_SSD_IN], ins[N_SSD_IN:N_SSD_IN + N_RET_IN], ins[N_SSD_IN + N_RET_IN:]
    ssd_scr, ret_scr, rwkv_scr = scr[:2], scr[2:4], scr[4:]

    @pl.when(pl.program_id(0) == 0)
    def _():
        _ssd_init(*ssd_scr)
        _ret_init(*ret_scr)
        _rwkv_init(*rwkv_scr[:2])

    main = _rwkv_body(*rwkv_in, y_rwkv, *rwkv_scr)
    side = itertools.chain(_ssd_body(*ssd_in, y_ssd, *ssd_scr), _ret_body(*ret_in, y_ret, *ret_scr))
    for step, _ in enumerate(main):
        if step % 2 == 0:
            next(side, None)
    for _ in side:
        pass


def _mixers(u_ssd, ssd_params, u_ret, ret_params, u_rwkv, rwkv_params):
    lp = u_ssd.shape[0]
    full = lambda a: pl.BlockSpec(a.shape, lambda i: (0,) * a.ndim)
    rows = lambda a: pl.BlockSpec((CHUNK, a.shape[1]), lambda i: (i, 0))
    cos, sin, ng = ret_params
    args = [u_ssd, *ssd_params, u_ret, cos, sin, ng, u_rwkv, *rwkv_params]
    assert len(args) == N_SSD_IN + N_RET_IN + N_RWKV_IN
    in_specs = ([rows(u_ssd)] + [full(a) for a in ssd_params]
                + [rows(u_ret), rows(cos), rows(sin), full(ng)]
                + [rows(u_rwkv)] + [full(a) for a in rwkv_params])
    out = jax.ShapeDtypeStruct((lp, 512), BF16)
    ospec = pl.BlockSpec((CHUNK, 512), lambda i: (i, 0))
    return pl.pallas_call(
        _mixers_kernel,
        grid=(lp // CHUNK,),
        in_specs=in_specs,
        out_specs=[ospec, ospec, ospec],
        out_shape=[out, out, out],
        scratch_shapes=[pltpu.VMEM((8, SSD_CONV_CH), F32),
                        pltpu.VMEM((SSD_GROUPS, SSD_STATE, SSD_WIDTH // SSD_GROUPS), F32),
                        pltpu.VMEM((RET_HEADS, CHUNK, CHUNK), F32),
                        pltpu.VMEM((RET_HEADS, RET_DK, RET_DV), F32),
                        pltpu.VMEM((8, RWKV_IN), F32),
                        pltpu.VMEM((RWKV_HEADS, RWKV_HEAD_DIM, RWKV_HEAD_DIM), F32),
                        pltpu.VMEM((RWKV_WIDTH, CHUNK), F32)],
        compiler_params=_cparams("arbitrary"),
        name="mixers",
    )(*args)


def _rwkv(u, mu, w0, a0, k_k, k_a, r_k, ln, lora, tri, ones):
    lp = u.shape[0]
    full = lambda shape: pl.BlockSpec(shape, lambda i: (0,) * len(shape))
    vec = full((1, RWKV_WIDTH))
    return pl.pallas_call(
        _rwkv_kernel,
        grid=(lp // CHUNK,),
        in_specs=[pl.BlockSpec((CHUNK, RWKV_IN), lambda i: (i, 0)), full((1, RWKV_IN)),
                  vec, vec, vec, vec, vec, vec, full((3, 128, RWKV_WIDTH)), full((CHUNK, CHUNK)),
                  full((RWKV_WIDTH, RWKV_WIDTH))],
        out_specs=pl.BlockSpec((CHUNK, RWKV_WIDTH), lambda i: (i, 0)),
        out_shape=jax.ShapeDtypeStruct((lp, RWKV_WIDTH), BF16),
        scratch_shapes=[pltpu.VMEM((8, RWKV_IN), F32),
                        pltpu.VMEM((RWKV_HEADS, RWKV_HEAD_DIM, RWKV_HEAD_DIM), F32),
                        pltpu.VMEM((RWKV_WIDTH, CHUNK), F32)],
        compiler_params=_cparams("arbitrary"),
        name="rwkv7",
    )(u, mu, w0, a0, k_k, k_a, r_k, ln, lora, tri, ones)


def _constants():
    r = jnp.arange(CHUNK)
    tri = (r[:, None] >= r[None, :]).astype(BF16)
    lane = jnp.arange(RWKV_WIDTH)
    ones_bd = ((lane[:, None] // RWKV_HEAD_DIM) == (lane[None, :] // RWKV_HEAD_DIM)).astype(BF16)
    e8 = ((jnp.arange(128)[:, None] == (lane[None, :] // SSD_HEAD_DIM))).astype(BF16)
    return tri, ones_bd, e8


def kernel(x, meta_tokens, ffn1_norm, ffn1_w_gate, ffn1_w_up, ffn1_w_down, mix_norm, w_in, w_out, mla_q_norm, mla_w_q_up, mla_kv_norm, mla_w_kv_up, mla_qk_norm_q, mla_qk_norm_k, ssd_conv_w, ssd_conv_b, ssd_dt_bias, ssd_a_log, ssd_d, ssd_norm, ret_norm, rwkv_mu, rwkv_w0, rwkv_w2, rwkv_a0, rwkv_a2, rwkv_g2, rwkv_k_k, rwkv_k_a, rwkv_r_k, rwkv_ln, ffn2_norm, ffn2_w_gate, ffn2_w_up, ffn2_w_down):
    b, seq, d = x.shape
    assert b == 1 and d == D_MODEL and seq % CHUNK == 0
    nl = w_in.shape[0]
    lp = seq + CHUNK
    assert lp % TOKEN_TILE == 0

    hst = jnp.concatenate([jnp.zeros((PAD, d), x.dtype), meta_tokens.astype(x.dtype), x[0]], axis=0)

    wq =jnp.pad(mla_w_q_up.reshape(nl, MLA_Q_LORA, MLA_HEADS, MLA_QK),
                 ((0, 0), (0, 0), (0, 0), (0, MLA_QK_PAD - MLA_QK))
                 ).reshape(nl, MLA_Q_LORA, MLA_HEADS * MLA_QK_PAD).astype(BF16)
    wkv = mla_w_kv_up.reshape(nl, MLA_KV_LORA, MLA_HEADS, 2, MLA_NOPE).transpose(0, 1, 3, 2, 4
                              ).reshape(nl, MLA_KV_LORA, 2 * MLA_HEADS * MLA_NOPE).astype(BF16)
    gq = jnp.pad(mla_qk_norm_q, ((0, 0), (0, MLA_QK_PAD - MLA_QK)))[:, None, :]
    gk = jnp.pad(mla_qk_norm_k, ((0, 0), (0, MLA_QK_PAD - MLA_QK)))[:, None, :]
    dtb = jnp.pad(ssd_dt_bias, ((0, 0), (0, 128 - SSD_HEADS)))[:, None, :]
    alog = jnp.pad(ssd_a_log, ((0, 0), (0, 128 - SSD_HEADS)))[:, None, :]
    dsk = jnp.repeat(ssd_d, SSD_HEAD_DIM, axis=1)[:, None, :]
    lora = jnp.zeros((nl, 3, 128, RWKV_WIDTH), F32)
    lora = lora.at[:, 0, :RWKV_DECAY_LORA].set(rwkv_w2)
    lora = lora.at[:, 1, RWKV_DECAY_LORA:RWKV_DECAY_LORA + RWKV_A_LORA].set(rwkv_a2)
    lora = lora.at[:, 2, RWKV_DECAY_LORA + RWKV_A_LORA:].set(rwkv_g2)
    lora = lora.astype(BF16)

    tri, ones_bd, e8 = _constants()
    cos, sin, sinm = _rope_tables(lp)

    w_ffn = (ffn1_w_gate[0].astype(BF16), ffn1_w_up[0].astype(BF16), ffn1_w_down[0].astype(BF16))
    for l in range(nl):
        hst, w_ffn, (w_mla, w_ssd, w_ret, w_rwkv, w_o) = _ffn(
            hst, ffn1_norm[l][None, :], *w_ffn, nxt=(ffn2_w_gate, ffn2_w_up, ffn2_w_down, l),
            proj=(w_in, w_out, l))
        gmix = mix_norm[l][None, :]
        q, k, v = _mla_prep(hst, gmix, w_mla, mla_q_norm[l][None, :], wq, mla_kv_norm[l][None, :],
                            wkv, gq[l], gk[l], cos, sinm, l)
        y_mla = _attention(q, k, v)
        u_ssd = _inproj(hst, gmix, w_ssd)
        u_ret = _inproj(hst, gmix, w_ret)
        u_rwkv = _inproj(hst, gmix, w_rwkv)
        y_ssd, y_ret, y_rwkv = _mixers(
            u_ssd, (ssd_conv_w[l], ssd_conv_b[l][None, :], dtb[l], alog[l], dsk[l], ssd_norm[l][None, :], tri, e8),
            u_ret, (cos, sin, ret_norm[l].reshape(1, RET_WIDTH)),
            u_rwkv, (rwkv_mu[l][None, :], rwkv_w0[l][None, :], rwkv_a0[l][None, :], rwkv_k_k[l][None, :],
                     rwkv_k_a[l][None, :], rwkv_r_k[l].reshape(1, RWKV_WIDTH), rwkv_ln[l].reshape(1, RWKV_WIDTH),
                     lora[l], tri, ones_bd))
        hst = _outproj(hst, (y_mla, y_ssd, y_ret, y_rwkv), w_o.reshape(4, 512, D_MODEL))
        nxt = (ffn1_w_gate, ffn1_w_up, ffn1_w_down, l + 1) if l + 1 < nl else None
        hst, w_ffn, _ = _ffn(hst, ffn2_norm[l][None, :], *w_ffn, nxt=nxt)
    return hst[CHUNK:][None]
```

```python
import functools
import itertools
import math

import jax
import jax.numpy as jnp
from jax import lax
from jax.experimental import pallas as pl
from jax.experimental.pallas import tpu as pltpu

F32 = jnp.float32
BF16 = jnp.bfloat16

D_MODEL = 2048
DEPTH = 4
N_META = 16
CHUNK = 128
PAD = CHUNK - N_META
D_FF = 5632
EPS = 1e-6
ROPE_THETA = 10000.0
NEG_INF = -1e30

MLA_HEADS = 4
MLA_NOPE = 128
MLA_ROPE = 64
MLA_QK = MLA_NOPE + MLA_ROPE
MLA_V = 128
MLA_Q_LORA = 384
MLA_KV_LORA = 128
MLA_QK_PAD = 256
MLA_V_AUG = MLA_V + 16
MLA_IN = MLA_Q_LORA + MLA_KV_LORA + MLA_ROPE
MLA_IN_PAD = 640

SSD_HEADS = 8
SSD_HEAD_DIM = 64
SSD_WIDTH = 512
SSD_GROUPS = 2
SSD_STATE = 128
SSD_CONV = 4
SSD_CONV_CH = SSD_WIDTH + 2 * SSD_GROUPS * SSD_STATE
SSD_IN = SSD_WIDTH + SSD_CONV_CH + SSD_HEADS
SSD_IN_PAD = SSD_WIDTH + SSD_CONV_CH + 128

RET_HEADS = 4
RET_DK = 64
RET_DV = 128
RET_WIDTH = 512
RET_IN = 2 * RET_HEADS * RET_DK + 2 * RET_WIDTH

RWKV_HEADS = 8
RWKV_HEAD_DIM = 64
RWKV_WIDTH = 512
RWKV_DECAY_LORA = 32
RWKV_A_LORA = 32
RWKV_GATE_LORA = 64
RWKV_LN_EPS = 64e-5
RWKV_IN = 3 * RWKV_WIDTH + RWKV_DECAY_LORA + RWKV_A_LORA + RWKV_GATE_LORA
RWKV_SUB = 64
RWKV_INV_BLOCK = 8

MIX_WIDTH = 2048

V7X_VMEM_LIMIT_BYTES = 56 * 1024 * 1024
TOKEN_TILE = 640
FF_TILE = 512
FFN_CONVERT_ROWS = 256
PROJ_CONVERT_ROWS = 16


def _cparams(*sem):
    return pltpu.CompilerParams(dimension_semantics=sem, vmem_limit_bytes=V7X_VMEM_LIMIT_BYTES)


def _sigmoid(x):
    return 1.0 / (1.0 + jnp.exp(-x))


def _silu(x):
    return x * _sigmoid(x)


def _softplus(x):
    return jnp.maximum(x, 0.0) + jnp.log(1.0 + jnp.exp(-jnp.abs(x)))


def _rms(x, g, eps=EPS):
    return x * lax.rsqrt(jnp.mean(x * x, axis=-1, keepdims=True) + eps) * g


def _mm(a, b):
    return jnp.dot(a.astype(BF16), b.astype(BF16), preferred_element_type=F32)


def _mm_nt(a, b):
    return lax.dot_general(a.astype(BF16), b.astype(BF16), (((1,), (1,)), ((), ())),
                           preferred_element_type=F32)


def _split3(x):
    x1 = x.astype(BF16)
    r1 = x - x1.astype(F32)
    x2 = r1.astype(BF16)
    x3 = (r1 - x2.astype(F32)).astype(BF16)
    return x1, x2, x3


def _dot01_right(x, m01):
    p1, p2, p3 = _split3(x)
    return (jnp.dot(p1, m01, preferred_element_type=F32)
            + jnp.dot(p2, m01, preferred_element_type=F32)
            + jnp.dot(p3, m01, preferred_element_type=F32))


def _dot01_left(m01, x):
    p1, p2, p3 = _split3(x)
    return (jnp.dot(m01, p1, preferred_element_type=F32)
            + jnp.dot(m01, p2, preferred_element_type=F32)
            + jnp.dot(m01, p3, preferred_element_type=F32))


def _row_ids(rows, cols, base):
    return base + lax.broadcasted_iota(jnp.int32, (rows, cols), 0)


def _rope_table_kernel(inv_ref, cos_ref, sin_ref, sinm_ref):
    i = pl.program_id(0)
    pos = (_row_ids(CHUNK, 128, i * CHUNK) - PAD).astype(F32)
    lane = lax.broadcasted_iota(jnp.int32, (CHUNK, 128), 1)
    ang = pos * inv_ref[...]
    c = jnp.cos(ang)
    s = jnp.sin(ang)
    s = jnp.where((lane % 64) < 32, -s, s)
    cos_ref[...] = c
    sin_ref[...] = s
    sinm_ref[...] = jnp.where(lane < 64, s, 0.0)


def _rope_tables(lp):
    half = 32
    inv = ROPE_THETA ** (-jnp.arange(half, dtype=F32) / half)
    inv = jnp.tile(inv, 4)[None, :]
    out = jax.ShapeDtypeStruct((lp, 128), F32)
    spec = pl.BlockSpec((CHUNK, 128), lambda i: (i, 0))
    return pl.pallas_call(
        _rope_table_kernel,
        grid=(lp // CHUNK,),
        in_specs=[pl.BlockSpec((1, 128), lambda i: (0, 0))],
        out_specs=[spec, spec, spec],
        out_shape=[out, out, out],
        compiler_params=_cparams("parallel"),
        name="rope_tables",
    )(inv)


def _rope_lanes(x, cos, sin_signed):
    lane = lax.broadcasted_iota(jnp.int32, x.shape, 1)
    fwd = pltpu.roll(x, 32, axis=1)
    bwd = pltpu.roll(x, 96, axis=1)
    rot = jnp.where((lane % 64) < 32, bwd, fwd)
    return x * cos + rot * sin_signed


def _ffn_kernel(*refs, convert_next, convert_proj):
    refs = list(refs)
    x_ref, g_ref, wg_ref, wu_ref, wd_ref = refs[:5]
    n_in = 5 + (3 if convert_next else 0) + (2 if convert_proj else 0)
    ins, outs, xn_ref = refs[5:n_in], refs[n_in:-1], refs[-1]
    o_ref = outs.pop(0)
    if convert_next:
        for src, dst in zip(ins[:3], outs[:3]):
            dst[...] = src[...].astype(BF16)
        ins, outs = ins[3:], outs[3:]
    if convert_proj:
        win_ref, wout_ref = ins
        wmla_ref, wssd_ref, wret_ref, wrwkv_ref, wo_ref = outs
        o0 = MLA_IN
        o1 = o0 + SSD_IN
        o2 = o1 + RET_IN
        rows = win_ref.shape[0]
        wmla_ref[:, :MLA_IN] = win_ref[:, :o0].astype(BF16)
        wmla_ref[:, MLA_IN:] = jnp.zeros((rows, MLA_IN_PAD - MLA_IN), BF16)
        wssd_ref[:, :SSD_IN] = win_ref[:, o0:o1].astype(BF16)
        wssd_ref[:, SSD_IN:] = jnp.zeros((rows, SSD_IN_PAD - SSD_IN), BF16)
        wret_ref[...] = win_ref[:, o1:o2].astype(BF16)
        wrwkv_ref[...] = win_ref[:, o2:].astype(BF16)
        wo_ref[...] = wout_ref[...].astype(BF16)
    j = pl.program_id(1)

    @pl.when(j == 0)
    def _():
        xn_ref[...] = _rms(x_ref[...], g_ref[...]).astype(BF16)
        o_ref[...] = jnp.zeros_like(o_ref)

    xn = xn_ref[...]
    a = jnp.dot(xn, wg_ref[...], preferred_element_type=F32)
    b = jnp.dot(xn, wu_ref[...], preferred_element_type=F32)
    mid = (_silu(a) * b).astype(BF16)
    o_ref[...] += jnp.dot(mid, wd_ref[...], preferred_element_type=F32)

    @pl.when(j == pl.num_programs(1) - 1)
    def _():
        o_ref[...] = x_ref[...] + 0.5 * o_ref[...]


def _ffn(h, g, wgb, wub, wdb, nxt=None, proj=None):
    lp = h.shape[0]
    tm, tf = TOKEN_TILE, FF_TILE
    n_j = D_FF // tf
    n_c = D_MODEL // FFN_CONVERT_ROWS if nxt is not None else 0
    in_specs = [
        pl.BlockSpec((tm, D_MODEL), lambda i, j: (i, 0)),
        pl.BlockSpec((1, D_MODEL), lambda i, j: (0, 0)),
        pl.BlockSpec((D_MODEL, tf), lambda i, j: (0, j)),
        pl.BlockSpec((D_MODEL, tf), lambda i, j: (0, j)),
        pl.BlockSpec((tf, D_MODEL), lambda i, j: (j, 0)),
    ]
    out_specs = [pl.BlockSpec((tm, D_MODEL), lambda i, j: (i, 0))]
    out_shape = [jax.ShapeDtypeStruct((lp, D_MODEL), F32)]
    args = [h, g, wgb, wub, wdb]
    if nxt is not None:
        wg, wu, wd, l = nxt
        cr = FFN_CONVERT_ROWS
        assert lp // tm >= n_c
        ci = lambda i: jnp.minimum(i, n_c - 1)
        cj = lambda i, j: jnp.where(i < n_c, j, n_j - 1)
        in_specs += [
            pl.BlockSpec((None, cr, tf), lambda i, j: (l, ci(i), cj(i, j))),
            pl.BlockSpec((None, cr, tf), lambda i, j: (l, ci(i), cj(i, j))),
            pl.BlockSpec((None, tf, cr), lambda i, j: (l, cj(i, j), ci(i))),
        ]
        out_specs += [
            pl.BlockSpec((cr, tf), lambda i, j: (ci(i), cj(i, j))),
            pl.BlockSpec((cr, tf), lambda i, j: (ci(i), cj(i, j))),
            pl.BlockSpec((tf, cr), lambda i, j: (cj(i, j), ci(i))),
        ]
        out_shape += [
            jax.ShapeDtypeStruct((D_MODEL, D_FF), BF16),
            jax.ShapeDtypeStruct((D_MODEL, D_FF), BF16),
            jax.ShapeDtypeStruct((D_FF, D_MODEL), BF16),
        ]
        args += [wg, wu, wd]
    if proj is not None:
        w_in, w_out, lpj = proj
        pr = PROJ_CONVERT_ROWS
        n_p = D_MODEL // pr
        assert (lp // tm) * n_j >= n_p
        pidx = lambda i, j: jnp.minimum(i * n_j + j, n_p - 1)
        in_specs += [
            pl.BlockSpec((None, pr, w_in.shape[2]), lambda i, j: (lpj, pidx(i, j), 0)),
            pl.BlockSpec((None, pr, D_MODEL), lambda i, j: (lpj, pidx(i, j), 0)),
        ]
        widths = (MLA_IN_PAD, SSD_IN_PAD, RET_IN, RWKV_IN, D_MODEL)
        out_specs += [pl.BlockSpec((pr, n), lambda i, j: (pidx(i, j), 0)) for n in widths]
        out_shape += [jax.ShapeDtypeStruct((D_MODEL, n), BF16) for n in widths]
        args += [w_in, w_out]
    outs = pl.pallas_call(
        functools.partial(_ffn_kernel, convert_next=nxt is not None, convert_proj=proj is not None),
        grid=(lp // tm, n_j),
        in_specs=in_specs,
        out_specs=out_specs,
        out_shape=out_shape,
        scratch_shapes=[pltpu.VMEM((tm, D_MODEL), BF16)],
        compiler_params=_cparams("arbitrary", "arbitrary"),
        name="ffn",
    )(*args)
    n_next = 3 if nxt is not None else 0
    return (outs[0], tuple(outs[1:1 + n_next]) if nxt is not None else None,
            tuple(outs[1 + n_next:]) if proj is not None else None)


def _inproj_kernel(x_ref, g_ref, w_ref, o_ref):
    i = pl.program_id(0)
    tm = x_ref.shape[0]
    xn = _rms(x_ref[...], g_ref[...]).astype(BF16)
    u = jnp.dot(xn, w_ref[...], preferred_element_type=F32)
    row = _row_ids(tm, 1, i * tm)
    o_ref[...] = jnp.where(row >= PAD, u, 0.0)


def _inproj(h, g, w):
    lp = h.shape[0]
    n = w.shape[1]
    tm = TOKEN_TILE
    return pl.pallas_call(
        _inproj_kernel,
        grid=(lp // tm,),
        in_specs=[
            pl.BlockSpec((tm, D_MODEL), lambda i: (i, 0)),
            pl.BlockSpec((1, D_MODEL), lambda i: (0, 0)),
            pl.BlockSpec((D_MODEL, n), lambda i: (0, 0)),
        ],
        out_specs=pl.BlockSpec((tm, n), lambda i: (i, 0)),
        out_shape=jax.ShapeDtypeStruct((lp, n), F32),
        compiler_params=_cparams("parallel"),
        name="inproj",
    )(h, g, w)


def _outproj_kernel(h_ref, y0_ref, y1_ref, y2_ref, y3_ref, w_ref, o_ref):
    acc = h_ref[...]
    for m, y_ref in enumerate((y0_ref, y1_ref, y2_ref, y3_ref)):
        acc = acc + jnp.dot(y_ref[...], w_ref[m], preferred_element_type=F32)
    o_ref[...] = acc


def _outproj(h, ys, w):
    lp = h.shape[0]
    tm = TOKEN_TILE
    yspec = pl.BlockSpec((tm, 512), lambda i: (i, 0))
    return pl.pallas_call(
        _outproj_kernel,
        grid=(lp // tm,),
        in_specs=[pl.BlockSpec((tm, D_MODEL), lambda i: (i, 0)), yspec, yspec, yspec, yspec,
                  pl.BlockSpec((4, 512, D_MODEL), lambda i: (0, 0, 0))],
        out_specs=pl.BlockSpec((tm, D_MODEL), lambda i: (i, 0)),
        out_shape=jax.ShapeDtypeStruct((lp, D_MODEL), F32),
        compiler_params=_cparams("parallel"),
        name="outproj",
    )(h, *ys, w)


def _mla_prep_kernel(x_ref, g_ref, win_ref, qn_ref, wq_ref, kvn_ref, wkv_ref, gq_ref, gk_ref,
                     cos_ref, sinm_ref, wssd_ref, q_ref, k_ref, v_ref, ussd_ref):
    i = pl.program_id(0)
    tm = x_ref.shape[0]
    xn = _rms(x_ref[...], g_ref[...]).astype(BF16)
    u = jnp.dot(xn, win_ref[...], preferred_element_type=F32)
    row = _row_ids(tm, 1, i * tm)
    u = jnp.where(row >= PAD, u, 0.0)
    cq = u[:, :MLA_Q_LORA]
    ckv = u[:, MLA_Q_LORA:MLA_Q_LORA + MLA_KV_LORA]
    kpe = u[:, MLA_Q_LORA + MLA_KV_LORA:]
    q_all = _mm(_rms(cq, qn_ref[...]), wq_ref[...])
    kv_all = _mm(_rms(ckv, kvn_ref[...]), wkv_ref[...])
    ussd_ref[...] = jnp.where(row >= PAD, jnp.dot(xn, wssd_ref[...], preferred_element_type=F32), 0.0)
    cos = cos_ref[...]
    sinm = sinm_ref[...]
    gq = gq_ref[...]
    gk = gk_ref[...]
    scale = MLA_QK ** -0.5
    for h in range(MLA_HEADS):
        qh = q_all[:, h * MLA_QK_PAD:(h + 1) * MLA_QK_PAD]
        ss = jnp.sum(qh * qh, axis=-1, keepdims=True) * (1.0 / MLA_QK)
        qh = qh * lax.rsqrt(ss + EPS) * gq
        q_rot = _rope_lanes(qh[:, MLA_NOPE:], cos, sinm)
        q_ref[h, :, :MLA_NOPE] = (qh[:, :MLA_NOPE] * scale).astype(BF16)
        q_ref[h, :, MLA_NOPE:] = (q_rot * scale).astype(BF16)

        kn = kv_all[:, h * MLA_NOPE:(h + 1) * MLA_NOPE]
        ss = (jnp.sum(kn * kn, axis=-1, keepdims=True)
              + jnp.sum(kpe * kpe, axis=-1, keepdims=True)) * (1.0 / MLA_QK)
        rs = lax.rsqrt(ss + EPS)
        k_rot = _rope_lanes(kpe * rs * gk[:, MLA_NOPE:], cos, sinm)
        k_ref[h, :, :MLA_NOPE] = (kn * rs * gk[:, :MLA_NOPE]).astype(BF16)
        k_ref[h, :, MLA_NOPE:] = k_rot.astype(BF16)
        v_ref[h, 0, :MLA_V, :] = kv_all[:, 4 * MLA_NOPE + h * MLA_V: 4 * MLA_NOPE + (h + 1) * MLA_V].T.astype(BF16)
        tail = lax.broadcasted_iota(jnp.int32, (MLA_V_AUG - MLA_V, tm), 0)
        v_ref[h, 0, MLA_V:, :] = jnp.where(tail == 0, 1.0, 0.0).astype(BF16)


def _mla_prep(h, g, win, qn, wq, kvn, wkv, gq, gk, cos, sinm, wssd, l):
    lp = h.shape[0]
    tm = TOKEN_TILE
    full = lambda shape: pl.BlockSpec(shape, lambda i: (0,) * len(shape))
    layer = lambda shape: pl.BlockSpec((None,) + shape, lambda i: (l,) + (0,) * len(shape))
    return pl.pallas_call(
        _mla_prep_kernel,
        grid=(lp // tm,),
        in_specs=[
            pl.BlockSpec((tm, D_MODEL), lambda i: (i, 0)),
            full((1, D_MODEL)), full((D_MODEL, MLA_IN_PAD)),
            full((1, MLA_Q_LORA)), layer((MLA_Q_LORA, MLA_HEADS * MLA_QK_PAD)),
            full((1, MLA_KV_LORA)), layer((MLA_KV_LORA, 2 * MLA_HEADS * MLA_NOPE)),
            full((1, MLA_QK_PAD)), full((1, MLA_QK_PAD)),
            pl.BlockSpec((tm, 128), lambda i: (i, 0)),
            pl.BlockSpec((tm, 128), lambda i: (i, 0)),
            full((D_MODEL, SSD_IN_PAD)),
        ],
        out_specs=[
            pl.BlockSpec((MLA_HEADS, tm, MLA_QK_PAD), lambda i: (0, i, 0)),
            pl.BlockSpec((MLA_HEADS, tm, MLA_QK_PAD), lambda i: (0, i, 0)),
            pl.BlockSpec((MLA_HEADS, 1, MLA_V_AUG, tm), lambda i: (0, i, 0, 0)),
            pl.BlockSpec((tm, SSD_IN_PAD), lambda i: (i, 0)),
        ],
        out_shape=[
            jax.ShapeDtypeStruct((MLA_HEADS, lp, MLA_QK_PAD), BF16),
            jax.ShapeDtypeStruct((MLA_HEADS, lp, MLA_QK_PAD), BF16),
            jax.ShapeDtypeStruct((MLA_HEADS, lp // tm, MLA_V_AUG, tm), BF16),
            jax.ShapeDtypeStruct((lp, SSD_IN_PAD), F32),
        ],
        compiler_params=_cparams("parallel"),
        name="mla_prep",
    )(h, g, win, qn, wq, kvn, wkv, gq, gk, cos, sinm, wssd)


def _attn_kernel(q_ref, k_ref, vt_ref, o_ref):
    i = pl.program_id(1)
    tq = q_ref.shape[1]
    q = q_ref[0]
    key0 = lax.broadcasted_iota(jnp.int32, (tq, tq), 0)
    qry = i * tq + lax.broadcasted_iota(jnp.int32, (tq, tq), 1)

    def scores(j, masked):
        start = pl.multiple_of(j * tq, tq)
        k = k_ref[0, pl.ds(start, tq), :]
        s = lax.dot_general(k, q, (((1,), (1,)), ((), ())), preferred_element_type=F32)
        if masked:
            key = key0 + j * tq
            s = jnp.where(key <= qry, jnp.where(key >= PAD, s, NEG_INF), NEG_INF)
        return s

    def update(j, s, carry):
        m, acc = carry
        m_new = jnp.maximum(m, jnp.max(s, axis=0, keepdims=True))
        alpha = jnp.exp(m - m_new)
        p = jnp.exp((s - m_new).astype(BF16))
        acc = alpha * acc + jnp.dot(vt_ref[0, j], p, preferred_element_type=F32)
        return m_new, acc

    init = (jnp.full((1, tq), NEG_INF, F32), jnp.zeros((MLA_V_AUG, tq), F32))
    s_first = scores(0, True)

    def body(j, c):
        s_prev, carry = c
        s_new = scores(j, False)
        return s_new, update(j - 1, s_prev, carry)

    s_last, carry = lax.fori_loop(1, i, body, (s_first, init))
    j_last = jnp.maximum(i - 1, 0)
    m, acc = lax.cond(i > 0,
                      lambda c: update(i, scores(i, True), update(j_last, s_last, c)),
                      lambda c: update(j_last, s_last, c), carry)
    o_ref[...] = (acc[:MLA_V] / acc[MLA_V:MLA_V + 1]).T.astype(BF16)


def _attention(q, k, v):
    lp = q.shape[1]
    tq = TOKEN_TILE
    return pl.pallas_call(
        _attn_kernel,
        grid=(MLA_HEADS, lp // tq),
        in_specs=[
            pl.BlockSpec((1, tq, MLA_QK_PAD), lambda h, i: (h, i, 0)),
            pl.BlockSpec((1, lp, MLA_QK_PAD), lambda h, i: (h, 0, 0)),
            pl.BlockSpec((1, lp // tq, MLA_V_AUG, tq), lambda h, i: (h, 0, 0, 0)),
        ],
        out_specs=pl.BlockSpec((tq, MLA_V), lambda h, i: (i, h)),
        out_shape=jax.ShapeDtypeStruct((lp, MLA_HEADS * MLA_V), BF16),
        compiler_params=_cparams("parallel", "arbitrary"),
        name="mla_attention",
    )(q, k, v)


def _shift_rows(x, carry, s):
    rolled = pltpu.roll(x, s, axis=0)
    head = pltpu.roll(carry, s, axis=0)
    r8 = lax.broadcasted_iota(jnp.int32, (8, 1), 0)
    return jnp.concatenate([jnp.where(r8 < s, head, rolled[:8]), rolled[8:]], axis=0)


def _ssd_init(carry_ref, state_ref):
    carry_ref[...] = jnp.zeros_like(carry_ref)
    state_ref[...] = jnp.zeros_like(state_ref)


def _ssd_body(u_ref, cw_ref, cb_ref, dtb_ref, alog_ref, dsk_ref, ng_ref, tri_ref, e8_ref,
              y_ref, carry_ref, state_ref):
    i = pl.program_id(0)
    z = u_ref[:, :SSD_WIDTH]
    xbc = u_ref[:, SSD_WIDTH:SSD_WIDTH + SSD_CONV_CH]
    dt_raw = u_ref[:, SSD_WIDTH + SSD_CONV_CH:]
    carry = carry_ref[...]
    cw = cw_ref[...]
    conv = xbc * cw[3:4, :]
    for s in (1, 2, 3):
        conv = conv + _shift_rows(xbc, carry, s) * cw[3 - s:4 - s, :]
    carry_ref[...] = xbc[CHUNK - 8:, :]
    xbc = _silu(conv + cb_ref[...])
    xs = xbc[:, :SSD_WIDTH]
    bm = xbc[:, SSD_WIDTH:SSD_WIDTH + SSD_GROUPS * SSD_STATE]
    cm = xbc[:, SSD_WIDTH + SSD_GROUPS * SSD_STATE:]

    row = _row_ids(CHUNK, 1, i * CHUNK)
    dt = _softplus(dt_raw + dtb_ref[...]) * jnp.where(row >= PAD, 1.0, 0.0)
    la = dt * (-jnp.exp(alog_ref[...]))
    tri = tri_ref[...]
    e8 = e8_ref[...]
    cs = _dot01_left(tri, la)
    cs_e = _dot01_right(cs, e8)
    dt_e = _dot01_right(dt, e8)
    cs_t = cs.T
    cs_last_e = cs_e[CHUNK - 1:CHUNK, :]
    x = xs * dt_e
    xd = x * jnp.exp(cs_last_e - cs_e)
    ecs = jnp.exp(cs_e)
    dec = jnp.exp(cs_last_e)

    r_i = lax.broadcasted_iota(jnp.int32, (CHUNK, CHUNK), 0)
    c_i = lax.broadcasted_iota(jnp.int32, (CHUNK, CHUNK), 1)
    causal = r_i >= c_i
    per = SSD_HEADS // SSD_GROUPS
    gw = per * SSD_HEAD_DIM
    groups = range(SSD_GROUPS)
    b_gs = [bm[:, g * SSD_STATE:(g + 1) * SSD_STATE] for g in groups]
    c_gs = [cm[:, g * SSD_STATE:(g + 1) * SSD_STATE] for g in groups]
    s_prev = [state_ref[g] for g in groups]
    yield
    scores = [_mm_nt(c_gs[g], b_gs[g]) for g in groups]
    y_off = [_mm(c_gs[g], s_prev[g]) for g in groups]
    s_add = [_mm(b_gs[g].T, xd[:, g * gw:(g + 1) * gw]) for g in groups]
    yield
    lmats = []
    for h in range(SSD_HEADS):
        seg = cs[:, h:h + 1] - cs_t[h:h + 1, :]
        lmats.append(jnp.where(causal, jnp.exp(jnp.where(causal, seg, 0.0)), 0.0))
    y_diag = [_mm(scores[h // per] * lmats[h], x[:, h * SSD_HEAD_DIM:(h + 1) * SSD_HEAD_DIM])
              for h in range(SSD_HEADS)]
    yield
    for g in groups:
        state_ref[g] = s_prev[g] * dec[:, g * gw:(g + 1) * gw] + s_add[g]
    y = (jnp.concatenate(y_diag, axis=1) + jnp.concatenate(y_off, axis=1) * ecs) + dsk_ref[...] * xs
    y = y * _silu(z)
    y_ref[...] = _rms(y, ng_ref[...]).astype(BF16)


def _ssd_kernel(*refs):
    pl.when(pl.program_id(0) == 0)(lambda: _ssd_init(*refs[-2:]))
    for _ in _ssd_body(*refs):
        pass


def _ssd(u, cw, cb, dtb, alog, dsk, ng, tri, e8):
    lp = u.shape[0]
    full = lambda shape: pl.BlockSpec(shape, lambda i: (0,) * len(shape))
    return pl.pallas_call(
        _ssd_kernel,
        grid=(lp // CHUNK,),
        in_specs=[pl.BlockSpec((CHUNK, SSD_IN_PAD), lambda i: (i, 0)),
                  full((SSD_CONV, SSD_CONV_CH)), full((1, SSD_CONV_CH)), full((1, 128)), full((1, 128)),
                  full((1, SSD_WIDTH)), full((1, SSD_WIDTH)), full((CHUNK, CHUNK)), full((128, SSD_WIDTH))],
        out_specs=pl.BlockSpec((CHUNK, SSD_WIDTH), lambda i: (i, 0)),
        out_shape=jax.ShapeDtypeStruct((lp, SSD_WIDTH), BF16),
        scratch_shapes=[pltpu.VMEM((8, SSD_CONV_CH), F32),
                        pltpu.VMEM((SSD_GROUPS, SSD_STATE, SSD_WIDTH // SSD_GROUPS), F32)],
        compiler_params=_cparams("arbitrary"),
        name="ssd",
    )(u, cw, cb, dtb, alog, dsk, ng, tri, e8)


RET_LOG_G = [math.log(1.0 - 2.0 ** (-5.0 - h)) for h in range(RET_HEADS)]


def _ret_init(dmat_ref, state_ref):
    state_ref[...] = jnp.zeros_like(state_ref)
    r_i = lax.broadcasted_iota(jnp.int32, (CHUNK, CHUNK), 0)
    c_i = lax.broadcasted_iota(jnp.int32, (CHUNK, CHUNK), 1)
    diff = (r_i - c_i).astype(F32)
    for h in range(RET_HEADS):
        dmat_ref[h] = jnp.where(r_i >= c_i, jnp.exp(jnp.where(r_i >= c_i, diff, 0.0) * RET_LOG_G[h]), 0.0)


def _ret_body(u_ref, cos_ref, sin_ref, ng_ref, y_ref, dmat_ref, state_ref):
    log_g = RET_LOG_G
    qkw = RET_HEADS * RET_DK
    cos = cos_ref[...]
    sin = sin_ref[...]
    q = jnp.concatenate([_rope_lanes(u_ref[:, c * 128:(c + 1) * 128], cos, sin) for c in range(2)], axis=1)
    k = jnp.concatenate([_rope_lanes(u_ref[:, qkw + c * 128:qkw + (c + 1) * 128], cos, sin)
                         for c in range(2)], axis=1) * (RET_DK ** -0.5)
    idx = lax.broadcasted_iota(jnp.int32, (CHUNK, 1), 0).astype(F32)
    k_t = k.T
    idx_row = lax.broadcasted_iota(jnp.int32, (1, CHUNK), 1).astype(F32)
    heads = range(RET_HEADS)
    q_hs = [q[:, h * RET_DK:(h + 1) * RET_DK] for h in heads]
    k_hs = [k[:, h * RET_DK:(h + 1) * RET_DK] for h in heads]
    v_hs = [u_ref[:, 2 * qkw + h * RET_DV: 2 * qkw + (h + 1) * RET_DV].astype(BF16) for h in heads]
    s_prevs = [state_ref[h] for h in heads]
    yield
    scs = [_mm_nt(q_hs[h], k_hs[h]) for h in heads]
    y_cross = [_mm(q_hs[h] * jnp.exp((idx + 1.0) * log_g[h]), s_prevs[h]) for h in heads]
    yield
    s_adds = [_mm(k_t[h * RET_DK:(h + 1) * RET_DK, :] * jnp.exp((CHUNK - 1 - idx_row) * log_g[h]), v_hs[h])
              for h in heads]
    y_in = [_mm(scs[h] * dmat_ref[h], v_hs[h]) for h in heads]
    yield
    for h in heads:
        g_h = u_ref[:, 2 * qkw + RET_WIDTH + h * RET_DV: 2 * qkw + RET_WIDTH + (h + 1) * RET_DV]
        y = y_in[h] + y_cross[h]
        state_ref[h] = s_prevs[h] * math.exp(CHUNK * log_g[h]) + s_adds[h]
        mu = jnp.mean(y, axis=-1, keepdims=True)
        var = jnp.mean(jnp.square(y - mu), axis=-1, keepdims=True)
        yn = (y - mu) * lax.rsqrt(var + EPS) * ng_ref[:, h * RET_DV:(h + 1) * RET_DV]
        y_ref[:, h * RET_DV:(h + 1) * RET_DV] = (_silu(g_h) * yn).astype(BF16)


def _ret_kernel(*refs):
    pl.when(pl.program_id(0) == 0)(lambda: _ret_init(*refs[-2:]))
    for _ in _ret_body(*refs):
        pass


def _retention(u, cos, sin, ng):
    lp = u.shape[0]
    return pl.pallas_call(
        _ret_kernel,
        grid=(lp // CHUNK,),
        in_specs=[pl.BlockSpec((CHUNK, RET_IN), lambda i: (i, 0)),
                  pl.BlockSpec((CHUNK, 128), lambda i: (i, 0)),
                  pl.BlockSpec((CHUNK, 128), lambda i: (i, 0)),
                  pl.BlockSpec((1, RET_WIDTH), lambda i: (0, 0))],
        out_specs=pl.BlockSpec((CHUNK, RET_WIDTH), lambda i: (i, 0)),
        out_shape=jax.ShapeDtypeStruct((lp, RET_WIDTH), BF16),
        scratch_shapes=[pltpu.VMEM((RET_HEADS, CHUNK, CHUNK), F32),
                        pltpu.VMEM((RET_HEADS, RET_DK, RET_DV), F32)],
        compiler_params=_cparams("arbitrary"),
        name="retention",
    )(u, cos, sin, ng)


def _inv_unit_upper_many(mats, in_block, eye):
    d = [jnp.where(in_block, a, 0.0) for a in mats]
    f = [a - x for a, x in zip(mats, d)]
    d2 = [_mm(x, x) for x in d]
    yield
    d4 = [_mm(x, x) for x in d2]
    p = [_mm(eye + x, eye + y) for x, y in zip(d, d2)]
    yield
    td = [_mm(x, eye + y) for x, y in zip(p, d4)]
    yield
    g = [_mm(x, y) for x, y in zip(f, td)]
    yield
    g2 = [_mm(x, x) for x in g]
    tg = [_mm(x, eye + y) for x, y in zip(td, g)]
    yield
    g4 = [_mm(x, x) for x in g2]
    tg = [_mm(x, eye + y) for x, y in zip(tg, g2)]
    yield
    return [_mm(x, eye + y) for x, y in zip(tg, g4)]


def _rwkv_init(prev_ref, state_ref):
    prev_ref[...] = jnp.zeros_like(prev_ref)
    state_ref[...] = jnp.zeros_like(state_ref)


def _rwkv_body(u_ref, mu_ref, w0_ref, a0_ref, kk_ref, ka_ref, rk_ref, ln_ref, lora_ref, tri_ref,
               ones_ref, y_ref, prev_ref, state_ref, ot_ref):
    u = u_ref[...]
    rows = lax.broadcasted_iota(jnp.int32, (CHUNK, 1), 0)
    u_prev = jnp.where(rows == 0, prev_ref[7:8, :], pltpu.roll(u, 1, axis=0))
    prev_ref[...] = u[CHUNK - 8:, :]
    us = u + (u_prev - u) * mu_ref[...]
    w3 = RWKV_WIDTH
    r = us[:, :w3]
    k = us[:, w3:2 * w3]
    v = us[:, 2 * w3:3 * w3]
    lo = us[:, 3 * w3:]
    w = w0_ref[...] + _mm(jnp.tanh(lo), lora_ref[0])
    w = -_softplus(-w) - 0.5
    ld = -jnp.exp(w)
    a = _sigmoid(a0_ref[...] + _mm(lo, lora_ref[1]))
    g = _mm(_sigmoid(lo), lora_ref[2])
    ones = ones_ref[...]
    kk = k * kk_ref[...]
    k2 = k * (1.0 + (a - 1.0) * ka_ref[...])
    kk = kk / jnp.maximum(jnp.sqrt(_dot01_right(kk * kk, ones)), 1e-12)
    b = kk * a

    lc_full = _dot01_left(tri_ref[...], ld)
    v_t = v.T

    n = RWKV_SUB
    r_i = lax.broadcasted_iota(jnp.int32, (n, n), 0)
    c_i = lax.broadcasted_iota(jnp.int32, (n, n), 1)
    in_block = (r_i // RWKV_INV_BLOCK) == (c_i // RWKV_INV_BLOCK)
    eye = jnp.where(r_i == c_i, 1.0, 0.0)
    hd = RWKV_HEAD_DIM
    n_sub = CHUNK // n
    heads = range(RWKV_HEADS)
    hs = [slice(h * hd, (h + 1) * hd) for h in heads]

    sc = []
    for sub in range(n_sub):
        lo_r, hi_r = sub * n, (sub + 1) * n
        lc = lc_full[lo_r:hi_r, :]
        if sub > 0:
            lc = lc - lc_full[lo_r - 1:lo_r, :]
        lc_last = lc[n - 1:n, :]
        e_pos = jnp.exp(lc)
        e_neg = jnp.exp(-lc)
        e_prev = jnp.exp(lc - ld[lo_r:hi_r, :])
        e_end = jnp.exp(lc_last - lc)
        sc.append(dict(
            gam=jnp.exp(lc_last),
            a_t=(-kk[lo_r:hi_r, :] * e_prev).astype(BF16),
            r_t=(r[lo_r:hi_r, :] * e_pos).astype(BF16),
            b_t=(b[lo_r:hi_r, :] * e_neg).astype(BF16),
            k_t=(k2[lo_r:hi_r, :] * e_neg).astype(BF16),
            b_h=(b[lo_r:hi_r, :] * e_end).astype(BF16),
            k_h=(k2[lo_r:hi_r, :] * e_end).astype(BF16),
            v_t=v_t[:, lo_r:hi_r].astype(BF16),
        ))

    pairs = [(sub, h) for sub in range(n_sub) for h in heads]
    yield
    bk = [jnp.concatenate([sc[sub]["b_t"][:, hs[h]], sc[sub]["k_t"][:, hs[h]]], axis=0) for sub, h in pairs]
    ar2 = [jnp.concatenate([sc[sub]["a_t"][:, hs[h]], sc[sub]["r_t"][:, hs[h]]], axis=0) for sub, h in pairs]
    r2 = lax.broadcasted_iota(jnp.int32, (n, 2 * n), 0)
    c2 = lax.broadcasted_iota(jnp.int32, (n, 2 * n), 1)
    mask2 = r2 < jnp.where(c2 < n, c2, c2 - n + 1)
    quad = [_mm_nt(x, y) for x, y in zip(bk, ar2)]
    top = [jnp.where(mask2, x[:n, :], 0.0) for x in quad]
    bot = [jnp.where(mask2, x[n:, :], 0.0).astype(BF16) for x in quad]
    yield
    vprod = [_mm(sc[sub]["v_t"][hs[h], :], jnp.concatenate([y, sc[sub]["k_h"][:, hs[h]]], axis=1))
             for y, (sub, h) in zip(bot, pairs)]
    yield
    t_t = yield from _inv_unit_upper_many([x[:, :n] for x in top], in_block, eye)
    top = [x.astype(BF16) for x in top]

    state = [state_ref[h] for h in heads]
    for sub in range(n_sub):
        c = sc[sub]
        base = sub * RWKV_HEADS
        yield
        xr = [_mm_nt(state[h], ar2[base + h]) for h in heads]
        yield
        u_t = [_mm(xr[h][:, :n] + vprod[base + h][:, :n], t_t[base + h]) for h in heads]
        yield
        ub = [_mm(u_t[h], top[base + h]) for h in heads]
        su = [_mm(u_t[h], c["b_h"][:, hs[h]]) for h in heads]
        for h in heads:
            ot_ref[h * hd:(h + 1) * hd, sub * n:(sub + 1) * n] = (
                xr[h][:, n:] + ub[h][:, n:] + vprod[base + h][:, n:2 * n])
        state = [state[h] * c["gam"][:, hs[h]] + su[h] + vprod[base + h][:, 2 * n:] for h in heads]
    for h in heads:
        state_ref[h] = state[h]
    yield

    out = ot_ref[...].T
    inv_n = 1.0 / RWKV_HEAD_DIM
    mean = _dot01_right(out, ones) * inv_n
    cen = out - mean
    var = _dot01_right(cen * cen, ones) * inv_n
    out = cen * lax.rsqrt(var + RWKV_LN_EPS) * ln_ref[...]
    bonus = _dot01_right(r * k2 * rk_ref[...], ones) * v
    y_ref[...] = ((out + bonus) * g).astype(BF16)


def _rwkv_kernel(*refs):
    pl.when(pl.program_id(0) == 0)(lambda: _rwkv_init(*refs[-3:-1]))
    for _ in _rwkv_body(*refs):
        pass


N_SSD_IN, N_RET_IN, N_RWKV_IN = 9, 4, 11


def _mixers_kernel(*refs):
    n_in = N_SSD_IN + N_RET_IN + N_RWKV_IN
    ins, (y_ssd, y_ret, y_rwkv), scr = refs[:n_in], refs[n_in:n_in + 3], refs[n_in + 3:]
    ssd_in, ret_in, rwkv_in = ins[:N_SSD_IN], ins[N_SSD_IN:N_SSD_IN + N_RET_IN], ins[N_SSD_IN + N_RET_IN:]
    ssd_scr, ret_scr, rwkv_scr = scr[:2], scr[2:4], scr[4:]

    @pl.when(pl.program_id(0) == 0)
    def _():
        _ssd_init(*ssd_scr)
        _ret_init(*ret_scr)
        _rwkv_init(*rwkv_scr[:2])

    main = _rwkv_body(*rwkv_in, y_rwkv, *rwkv_scr)
    side = itertools.chain(_ssd_body(*ssd_in, y_ssd, *ssd_scr), _ret_body(*ret_in, y_ret, *ret_scr))
    for step, _ in enumerate(main):
        if step % 2 == 0:
            next(side, None)
    for _ in side:
        pass


def _mixers(u_ssd, ssd_params, u_ret, ret_params, u_rwkv, rwkv_params):
    lp = u_ssd.shape[0]
    full = lambda a: pl.BlockSpec(a.shape, lambda i: (0,) * a.ndim)
    rows = lambda a: pl.BlockSpec((CHUNK, a.shape[1]), lambda i: (i, 0))
    cos, sin, ng = ret_params
    args = [u_ssd, *ssd_params, u_ret, cos, sin, ng, u_rwkv, *rwkv_params]
    assert len(args) == N_SSD_IN + N_RET_IN + N_RWKV_IN
    in_specs = ([rows(u_ssd)] + [full(a) for a in ssd_params]
                + [rows(u_ret), rows(cos), rows(sin), full(ng)]
                + [rows(u_rwkv)] + [full(a) for a in rwkv_params])
    out = jax.ShapeDtypeStruct((lp, 512), BF16)
    ospec = pl.BlockSpec((CHUNK, 512), lambda i: (i, 0))
    return pl.pallas_call(
        _mixers_kernel,
        grid=(lp // CHUNK,),
        in_specs=in_specs,
        out_specs=[ospec, ospec, ospec],
        out_shape=[out, out, out],
        scratch_shapes=[pltpu.VMEM((8, SSD_CONV_CH), F32),
                        pltpu.VMEM((SSD_GROUPS, SSD_STATE, SSD_WIDTH // SSD_GROUPS), F32),
                        pltpu.VMEM((RET_HEADS, CHUNK, CHUNK), F32),
                        pltpu.VMEM((RET_HEADS, RET_DK, RET_DV), F32),
                        pltpu.VMEM((8, RWKV_IN), F32),
                        pltpu.VMEM((RWKV_HEADS, RWKV_HEAD_DIM, RWKV_HEAD_DIM), F32),
                        pltpu.VMEM((RWKV_WIDTH, CHUNK), F32)],
        compiler_params=_cparams("arbitrary"),
        name="mixers",
    )(*args)


def _rwkv(u, mu, w0, a0, k_k, k_a, r_k, ln, lora, tri, ones):
    lp = u.shape[0]
    full = lambda shape: pl.BlockSpec(shape, lambda i: (0,) * len(shape))
    vec = full((1, RWKV_WIDTH))
    return pl.pallas_call(
        _rwkv_kernel,
        grid=(lp // CHUNK,),
        in_specs=[pl.BlockSpec((CHUNK, RWKV_IN), lambda i: (i, 0)), full((1, RWKV_IN)),
                  vec, vec, vec, vec, vec, vec, full((3, 128, RWKV_WIDTH)), full((CHUNK, CHUNK)),
                  full((RWKV_WIDTH, RWKV_WIDTH))],
        out_specs=pl.BlockSpec((CHUNK, RWKV_WIDTH), lambda i: (i, 0)),
        out_shape=jax.ShapeDtypeStruct((lp, RWKV_WIDTH), BF16),
        scratch_shapes=[pltpu.VMEM((8, RWKV_IN), F32),
                        pltpu.VMEM((RWKV_HEADS, RWKV_HEAD_DIM, RWKV_HEAD_DIM), F32),
                        pltpu.VMEM((RWKV_WIDTH, CHUNK), F32)],
        compiler_params=_cparams("arbitrary"),
        name="rwkv7",
    )(u, mu, w0, a0, k_k, k_a, r_k, ln, lora, tri, ones)


def _constants():
    r = jnp.arange(CHUNK)
    tri = (r[:, None] >= r[None, :]).astype(BF16)
    lane = jnp.arange(RWKV_WIDTH)
    ones_bd = ((lane[:, None] // RWKV_HEAD_DIM) == (lane[None, :] // RWKV_HEAD_DIM)).astype(BF16)
    e8 = ((jnp.arange(128)[:, None] == (lane[None, :] // SSD_HEAD_DIM))).astype(BF16)
    return tri, ones_bd, e8


def kernel(x, meta_tokens, ffn1_norm, ffn1_w_gate, ffn1_w_up, ffn1_w_down, mix_norm, w_in, w_out, mla_q_norm, mla_w_q_up, mla_kv_norm, mla_w_kv_up, mla_qk_norm_q, mla_qk_norm_k, ssd_conv_w, ssd_conv_b, ssd_dt_bias, ssd_a_log, ssd_d, ssd_norm, ret_norm, rwkv_mu, rwkv_w0, rwkv_w2, rwkv_a0, rwkv_a2, rwkv_g2, rwkv_k_k, rwkv_k_a, rwkv_r_k, rwkv_ln, ffn2_norm, ffn2_w_gate, ffn2_w_up, ffn2_w_down):
    b, seq, d = x.shape
    assert b == 1 and d == D_MODEL and seq % CHUNK == 0
    nl = w_in.shape[0]
    lp = seq + CHUNK
    assert lp % TOKEN_TILE == 0

    hst = jnp.concatenate([jnp.zeros((PAD, d), x.dtype), meta_tokens.astype(x.dtype), x[0]], axis=0)

    wq =jnp.pad(mla_w_q_up.reshape(nl, MLA_Q_LORA, MLA_HEADS, MLA_QK),
                 ((0, 0), (0, 0), (0, 0), (0, MLA_QK_PAD - MLA_QK))
                 ).reshape(nl, MLA_Q_LORA, MLA_HEADS * MLA_QK_PAD).astype(BF16)
    wkv = mla_w_kv_up.reshape(nl, MLA_KV_LORA, MLA_HEADS, 2, MLA_NOPE).transpose(0, 1, 3, 2, 4
                              ).reshape(nl, MLA_KV_LORA, 2 * MLA_HEADS * MLA_NOPE).astype(BF16)
    gq = jnp.pad(mla_qk_norm_q, ((0, 0), (0, MLA_QK_PAD - MLA_QK)))[:, None, :]
    gk = jnp.pad(mla_qk_norm_k, ((0, 0), (0, MLA_QK_PAD - MLA_QK)))[:, None, :]
    dtb = jnp.pad(ssd_dt_bias, ((0, 0), (0, 128 - SSD_HEADS)))[:, None, :]
    alog = jnp.pad(ssd_a_log, ((0, 0), (0, 128 - SSD_HEADS)))[:, None, :]
    dsk = jnp.repeat(ssd_d, SSD_HEAD_DIM, axis=1)[:, None, :]
    lora = jnp.zeros((nl, 3, 128, RWKV_WIDTH), F32)
    lora = lora.at[:, 0, :RWKV_DECAY_LORA].set(rwkv_w2)
    lora = lora.at[:, 1, RWKV_DECAY_LORA:RWKV_DECAY_LORA + RWKV_A_LORA].set(rwkv_a2)
    lora = lora.at[:, 2, RWKV_DECAY_LORA + RWKV_A_LORA:].set(rwkv_g2)
    lora = lora.astype(BF16)

    tri, ones_bd, e8 = _constants()
    cos, sin, sinm = _rope_tables(lp)

    w_ffn = (ffn1_w_gate[0].astype(BF16), ffn1_w_up[0].astype(BF16), ffn1_w_down[0].astype(BF16))
    for l in range(nl):
        hst, w_ffn, (w_mla, w_ssd, w_ret, w_rwkv, w_o) = _ffn(
            hst, ffn1_norm[l][None, :], *w_ffn, nxt=(ffn2_w_gate, ffn2_w_up, ffn2_w_down, l),
            proj=(w_in, w_out, l))
        gmix = mix_norm[l][None, :]
        q, k, v, u_ssd = _mla_prep(hst, gmix, w_mla, mla_q_norm[l][None, :], wq, mla_kv_norm[l][None, :],
                                   wkv, gq[l], gk[l], cos, sinm, w_ssd, l)
        y_mla = _attention(q, k, v)
        u_ret = _inproj(hst, gmix, w_ret)
        u_rwkv = _inproj(hst, gmix, w_rwkv)
        y_ssd, y_ret, y_rwkv = _mixers(
            u_ssd, (ssd_conv_w[l], ssd_conv_b[l][None, :], dtb[l], alog[l], dsk[l], ssd_norm[l][None, :], tri, e8),
            u_ret, (cos, sin, ret_norm[l].reshape(1, RET_WIDTH)),
            u_rwkv, (rwkv_mu[l][None, :], rwkv_w0[l][None, :], rwkv_a0[l][None, :], rwkv_k_k[l][None, :],
                     rwkv_k_a[l][None, :], rwkv_r_k[l].reshape(1, RWKV_WIDTH), rwkv_ln[l].reshape(1, RWKV_WIDTH),
                     lora[l], tri, ones_bd))
        hst = _outproj(hst, (y_mla, y_ssd, y_ret, y_rwkv), w_o.reshape(4, 512, D_MODEL))
        nxt = (ffn1_w_gate, ffn1_w_up, ffn1_w_down, l + 1) if l + 1 < nl else None
        hst, w_ffn, _ = _ffn(hst, ffn2_norm[l][None, :], *w_ffn, nxt=nxt)
    return hst[CHUNK:][None]
```

```python
import functools
import itertools
import math

import jax
import jax.numpy as jnp
from jax import lax
from jax.experimental import pallas as pl
from jax.experimental.pallas import tpu as pltpu

F32 = jnp.float32
BF16 = jnp.bfloat16

D_MODEL = 2048
DEPTH = 4
N_META = 16
CHUNK = 128
PAD = CHUNK - N_META
D_FF = 5632
EPS = 1e-6
ROPE_THETA = 10000.0
NEG_INF = -1e30

MLA_HEADS = 4
MLA_NOPE = 128
MLA_ROPE = 64
MLA_QK = MLA_NOPE + MLA_ROPE
MLA_V = 128
MLA_Q_LORA = 384
MLA_KV_LORA = 128
MLA_QK_PAD = 256
MLA_V_AUG = MLA_V + 16
MLA_IN = MLA_Q_LORA + MLA_KV_LORA + MLA_ROPE
MLA_IN_PAD = 640

SSD_HEADS = 8
SSD_HEAD_DIM = 64
SSD_WIDTH = 512
SSD_GROUPS = 2
SSD_STATE = 128
SSD_CONV = 4
SSD_CONV_CH = SSD_WIDTH + 2 * SSD_GROUPS * SSD_STATE
SSD_IN = SSD_WIDTH + SSD_CONV_CH + SSD_HEADS
SSD_IN_PAD = SSD_WIDTH + SSD_CONV_CH + 128

RET_HEADS = 4
RET_DK = 64
RET_DV = 128
RET_WIDTH = 512
RET_IN = 2 * RET_HEADS * RET_DK + 2 * RET_WIDTH

RWKV_HEADS = 8
RWKV_HEAD_DIM = 64
RWKV_WIDTH = 512
RWKV_DECAY_LORA = 32
RWKV_A_LORA = 32
RWKV_GATE_LORA = 64
RWKV_LN_EPS = 64e-5
RWKV_IN = 3 * RWKV_WIDTH + RWKV_DECAY_LORA + RWKV_A_LORA + RWKV_GATE_LORA
RWKV_SUB = 64
RWKV_INV_BLOCK = 8

MIX_WIDTH = 2048

V7X_VMEM_LIMIT_BYTES = 56 * 1024 * 1024
TOKEN_TILE = 640
FF_TILE = 512
FFN_CONVERT_ROWS = 256
PROJ_CONVERT_ROWS = 16


def _cparams(*sem):
    return pltpu.CompilerParams(dimension_semantics=sem, vmem_limit_bytes=V7X_VMEM_LIMIT_BYTES)


def _sigmoid(x):
    return 1.0 / (1.0 + jnp.exp(-x))


def _silu(x):
    return x * _sigmoid(x)


def _softplus(x):
    return jnp.maximum(x, 0.0) + jnp.log(1.0 + jnp.exp(-jnp.abs(x)))


def _rms(x, g, eps=EPS):
    return x * lax.rsqrt(jnp.mean(x * x, axis=-1, keepdims=True) + eps) * g


def _mm(a, b):
    return jnp.dot(a.astype(BF16), b.astype(BF16), preferred_element_type=F32)


def _mm_nt(a, b):
    return lax.dot_general(a.astype(BF16), b.astype(BF16), (((1,), (1,)), ((), ())),
                           preferred_element_type=F32)


def _split3(x):
    x1 = x.astype(BF16)
    r1 = x - x1.astype(F32)
    x2 = r1.astype(BF16)
    x3 = (r1 - x2.astype(F32)).astype(BF16)
    return x1, x2, x3


def _dot01_right(x, m01):
    p1, p2, p3 = _split3(x)
    return (jnp.dot(p1, m01, preferred_element_type=F32)
            + jnp.dot(p2, m01, preferred_element_type=F32)
            + jnp.dot(p3, m01, preferred_element_type=F32))


def _dot01_left(m01, x):
    p1, p2, p3 = _split3(x)
    return (jnp.dot(m01, p1, preferred_element_type=F32)
            + jnp.dot(m01, p2, preferred_element_type=F32)
            + jnp.dot(m01, p3, preferred_element_type=F32))


def _row_ids(rows, cols, base):
    return base + lax.broadcasted_iota(jnp.int32, (rows, cols), 0)


def _rope_table_kernel(inv_ref, cos_ref, sin_ref, sinm_ref):
    i = pl.program_id(0)
    pos = (_row_ids(CHUNK, 128, i * CHUNK) - PAD).astype(F32)
    lane = lax.broadcasted_iota(jnp.int32, (CHUNK, 128), 1)
    ang = pos * inv_ref[...]
    c = jnp.cos(ang)
    s = jnp.sin(ang)
    s = jnp.where((lane % 64) < 32, -s, s)
    cos_ref[...] = c
    sin_ref[...] = s
    sinm_ref[...] = jnp.where(lane < 64, s, 0.0)


def _rope_tables(lp):
    half = 32
    inv = ROPE_THETA ** (-jnp.arange(half, dtype=F32) / half)
    inv = jnp.tile(inv, 4)[None, :]
    out = jax.ShapeDtypeStruct((lp, 128), F32)
    spec = pl.BlockSpec((CHUNK, 128), lambda i: (i, 0))
    return pl.pallas_call(
        _rope_table_kernel,
        grid=(lp // CHUNK,),
        in_specs=[pl.BlockSpec((1, 128), lambda i: (0, 0))],
        out_specs=[spec, spec, spec],
        out_shape=[out, out, out],
        compiler_params=_cparams("parallel"),
        name="rope_tables",
    )(inv)


def _rope_lanes(x, cos, sin_signed):
    lane = lax.broadcasted_iota(jnp.int32, x.shape, 1)
    fwd = pltpu.roll(x, 32, axis=1)
    bwd = pltpu.roll(x, 96, axis=1)
    rot = jnp.where((lane % 64) < 32, bwd, fwd)
    return x * cos + rot * sin_signed


def _ffn_kernel(*refs, convert_next, convert_proj):
    refs = list(refs)
    x_ref, g_ref, wg_ref, wu_ref, wd_ref = refs[:5]
    n_in = 5 + (3 if convert_next else 0) + (2 if convert_proj else 0)
    ins, outs, xn_ref = refs[5:n_in], refs[n_in:-1], refs[-1]
    o_ref = outs.pop(0)
    if convert_next:
        for src, dst in zip(ins[:3], outs[:3]):
            dst[...] = src[...].astype(BF16)
        ins, outs = ins[3:], outs[3:]
    if convert_proj:
        win_ref, wout_ref = ins
        wmla_ref, wssd_ref, wret_ref, wrwkv_ref, wo_ref = outs
        o0 = MLA_IN
        o1 = o0 + SSD_IN
        o2 = o1 + RET_IN
        rows = win_ref.shape[0]
        wmla_ref[:, :MLA_IN] = win_ref[:, :o0].astype(BF16)
        wmla_ref[:, MLA_IN:] = jnp.zeros((rows, MLA_IN_PAD - MLA_IN), BF16)
        wssd_ref[:, :SSD_IN] = win_ref[:, o0:o1].astype(BF16)
        wssd_ref[:, SSD_IN:] = jnp.zeros((rows, SSD_IN_PAD - SSD_IN), BF16)
        wret_ref[...] = win_ref[:, o1:o2].astype(BF16)
        wrwkv_ref[...] = win_ref[:, o2:].astype(BF16)
        wo_ref[...] = wout_ref[...].astype(BF16)
    j = pl.program_id(1)

    @pl.when(j == 0)
    def _():
        xn_ref[...] = _rms(x_ref[...], g_ref[...]).astype(BF16)
        o_ref[...] = jnp.zeros_like(o_ref)

    xn = xn_ref[...]
    a = jnp.dot(xn, wg_ref[...], preferred_element_type=F32)
    b = jnp.dot(xn, wu_ref[...], preferred_element_type=F32)
    mid = (_silu(a) * b).astype(BF16)
    o_ref[...] += jnp.dot(mid, wd_ref[...], preferred_element_type=F32)

    @pl.when(j == pl.num_programs(1) - 1)
    def _():
        o_ref[...] = x_ref[...] + 0.5 * o_ref[...]


def _ffn(h, g, wgb, wub, wdb, nxt=None, proj=None):
    lp = h.shape[0]
    tm, tf = TOKEN_TILE, FF_TILE
    n_j = D_FF // tf
    n_c = D_MODEL // FFN_CONVERT_ROWS if nxt is not None else 0
    in_specs = [
        pl.BlockSpec((tm, D_MODEL), lambda i, j: (i, 0)),
        pl.BlockSpec((1, D_MODEL), lambda i, j: (0, 0)),
        pl.BlockSpec((D_MODEL, tf), lambda i, j: (0, j)),
        pl.BlockSpec((D_MODEL, tf), lambda i, j: (0, j)),
        pl.BlockSpec((tf, D_MODEL), lambda i, j: (j, 0)),
    ]
    out_specs = [pl.BlockSpec((tm, D_MODEL), lambda i, j: (i, 0))]
    out_shape = [jax.ShapeDtypeStruct((lp, D_MODEL), F32)]
    args = [h, g, wgb, wub, wdb]
    if nxt is not None:
        wg, wu, wd, l = nxt
        cr = FFN_CONVERT_ROWS
        assert lp // tm >= n_c
        ci = lambda i: jnp.minimum(i, n_c - 1)
        cj = lambda i, j: jnp.where(i < n_c, j, n_j - 1)
        in_specs += [
            pl.BlockSpec((None, cr, tf), lambda i, j: (l, ci(i), cj(i, j))),
            pl.BlockSpec((None, cr, tf), lambda i, j: (l, ci(i), cj(i, j))),
            pl.BlockSpec((None, tf, cr), lambda i, j: (l, cj(i, j), ci(i))),
        ]
        out_specs += [
            pl.BlockSpec((cr, tf), lambda i, j: (ci(i), cj(i, j))),
            pl.BlockSpec((cr, tf), lambda i, j: (ci(i), cj(i, j))),
            pl.BlockSpec((tf, cr), lambda i, j: (cj(i, j), ci(i))),
        ]
        out_shape += [
            jax.ShapeDtypeStruct((D_MODEL, D_FF), BF16),
            jax.ShapeDtypeStruct((D_MODEL, D_FF), BF16),
            jax.ShapeDtypeStruct((D_FF, D_MODEL), BF16),
        ]
        args += [wg, wu, wd]
    if proj is not None:
        w_in, w_out, lpj = proj
        pr = PROJ_CONVERT_ROWS
        n_p = D_MODEL // pr
        assert (lp // tm) * n_j >= n_p
        pidx = lambda i, j: jnp.minimum(i * n_j + j, n_p - 1)
        in_specs += [
            pl.BlockSpec((None, pr, w_in.shape[2]), lambda i, j: (lpj, pidx(i, j), 0)),
            pl.BlockSpec((None, pr, D_MODEL), lambda i, j: (lpj, pidx(i, j), 0)),
        ]
        widths = (MLA_IN_PAD, SSD_IN_PAD, RET_IN, RWKV_IN, D_MODEL)
        out_specs += [pl.BlockSpec((pr, n), lambda i, j: (pidx(i, j), 0)) for n in widths]
        out_shape += [jax.ShapeDtypeStruct((D_MODEL, n), BF16) for n in widths]
        args += [w_in, w_out]
    outs = pl.pallas_call(
        functools.partial(_ffn_kernel, convert_next=nxt is not None, convert_proj=proj is not None),
        grid=(lp // tm, n_j),
        in_specs=in_specs,
        out_specs=out_specs,
        out_shape=out_shape,
        scratch_shapes=[pltpu.VMEM((tm, D_MODEL), BF16)],
        compiler_params=_cparams("arbitrary", "arbitrary"),
        name="ffn",
    )(*args)
    n_next = 3 if nxt is not None else 0
    return (outs[0], tuple(outs[1:1 + n_next]) if nxt is not None else None,
            tuple(outs[1 + n_next:]) if proj is not None else None)


def _inproj_kernel(x_ref, g_ref, w_ref, o_ref):
    i = pl.program_id(0)
    tm = x_ref.shape[0]
    xn = _rms(x_ref[...], g_ref[...]).astype(BF16)
    u = jnp.dot(xn, w_ref[...], preferred_element_type=F32)
    row = _row_ids(tm, 1, i * tm)
    o_ref[...] = jnp.where(row >= PAD, u, 0.0)


def _inproj(h, g, w):
    lp = h.shape[0]
    n = w.shape[1]
    tm = TOKEN_TILE
    return pl.pallas_call(
        _inproj_kernel,
        grid=(lp // tm,),
        in_specs=[
            pl.BlockSpec((tm, D_MODEL), lambda i: (i, 0)),
            pl.BlockSpec((1, D_MODEL), lambda i: (0, 0)),
            pl.BlockSpec((D_MODEL, n), lambda i: (0, 0)),
        ],
        out_specs=pl.BlockSpec((tm, n), lambda i: (i, 0)),
        out_shape=jax.ShapeDtypeStruct((lp, n), F32),
        compiler_params=_cparams("parallel"),
        name="inproj",
    )(h, g, w)


def _outproj_kernel(h_ref, y0_ref, y1_ref, y2_ref, y3_ref, w_ref, o_ref):
    acc = h_ref[...]
    for m, y_ref in enumerate((y0_ref, y1_ref, y2_ref, y3_ref)):
        acc = acc + jnp.dot(y_ref[...], w_ref[m], preferred_element_type=F32)
    o_ref[...] = acc


def _outproj(h, ys, w):
    lp = h.shape[0]
    tm = TOKEN_TILE
    yspec = pl.BlockSpec((tm, 512), lambda i: (i, 0))
    return pl.pallas_call(
        _outproj_kernel,
        grid=(lp // tm,),
        in_specs=[pl.BlockSpec((tm, D_MODEL), lambda i: (i, 0)), yspec, yspec, yspec, yspec,
                  pl.BlockSpec((4, 512, D_MODEL), lambda i: (0, 0, 0))],
        out_specs=pl.BlockSpec((tm, D_MODEL), lambda i: (i, 0)),
        out_shape=jax.ShapeDtypeStruct((lp, D_MODEL), F32),
        compiler_params=_cparams("parallel"),
        name="outproj",
    )(h, *ys, w)


def _mla_prep_kernel(x_ref, g_ref, win_ref, qn_ref, wq_ref, kvn_ref, wkv_ref, gq_ref, gk_ref,
                     cos_ref, sinm_ref, wssd_ref, q_ref, k_ref, v_ref, ussd_ref):
    i = pl.program_id(0)
    tm = x_ref.shape[0]
    xn = _rms(x_ref[...], g_ref[...]).astype(BF16)
    u = jnp.dot(xn, win_ref[...], preferred_element_type=F32)
    row = _row_ids(tm, 1, i * tm)
    u = jnp.where(row >= PAD, u, 0.0)
    cq = u[:, :MLA_Q_LORA]
    ckv = u[:, MLA_Q_LORA:MLA_Q_LORA + MLA_KV_LORA]
    kpe = u[:, MLA_Q_LORA + MLA_KV_LORA:]
    q_all = _mm(_rms(cq, qn_ref[...]), wq_ref[...])
    kv_all = _mm(_rms(ckv, kvn_ref[...]), wkv_ref[...])
    ussd_ref[...] = jnp.where(row >= PAD, jnp.dot(xn, wssd_ref[...], preferred_element_type=F32), 0.0)
    cos = cos_ref[...]
    sinm = sinm_ref[...]
    gq = gq_ref[...]
    gk = gk_ref[...]
    scale = MLA_QK ** -0.5
    for h in range(MLA_HEADS):
        qh = q_all[:, h * MLA_QK_PAD:(h + 1) * MLA_QK_PAD]
        ss = jnp.sum(qh * qh, axis=-1, keepdims=True) * (1.0 / MLA_QK)
        qh = qh * lax.rsqrt(ss + EPS) * gq
        q_rot = _rope_lanes(qh[:, MLA_NOPE:], cos, sinm)
        q_ref[h, 0, :MLA_NOPE, :] = (qh[:, :MLA_NOPE] * scale).T.astype(BF16)
        q_ref[h, 0, MLA_NOPE:, :] = (q_rot * scale).T.astype(BF16)

        kn = kv_all[:, h * MLA_NOPE:(h + 1) * MLA_NOPE]
        ss = (jnp.sum(kn * kn, axis=-1, keepdims=True)
              + jnp.sum(kpe * kpe, axis=-1, keepdims=True)) * (1.0 / MLA_QK)
        rs = lax.rsqrt(ss + EPS)
        k_rot = _rope_lanes(kpe * rs * gk[:, MLA_NOPE:], cos, sinm)
        k_ref[h, :, :MLA_NOPE] = (kn * rs * gk[:, :MLA_NOPE]).astype(BF16)
        k_ref[h, :, MLA_NOPE:] = k_rot.astype(BF16)
        v_ref[h, 0, :MLA_V, :] = kv_all[:, 4 * MLA_NOPE + h * MLA_V: 4 * MLA_NOPE + (h + 1) * MLA_V].T.astype(BF16)
        tail = lax.broadcasted_iota(jnp.int32, (MLA_V_AUG - MLA_V, tm), 0)
        v_ref[h, 0, MLA_V:, :] = jnp.where(tail == 0, 1.0, 0.0).astype(BF16)


def _mla_prep(h, g, win, qn, wq, kvn, wkv, gq, gk, cos, sinm, wssd, l):
    lp = h.shape[0]
    tm = TOKEN_TILE
    full = lambda shape: pl.BlockSpec(shape, lambda i: (0,) * len(shape))
    layer = lambda shape: pl.BlockSpec((None,) + shape, lambda i: (l,) + (0,) * len(shape))
    return pl.pallas_call(
        _mla_prep_kernel,
        grid=(lp // tm,),
        in_specs=[
            pl.BlockSpec((tm, D_MODEL), lambda i: (i, 0)),
            full((1, D_MODEL)), full((D_MODEL, MLA_IN_PAD)),
            full((1, MLA_Q_LORA)), layer((MLA_Q_LORA, MLA_HEADS * MLA_QK_PAD)),
            full((1, MLA_KV_LORA)), layer((MLA_KV_LORA, 2 * MLA_HEADS * MLA_NOPE)),
            full((1, MLA_QK_PAD)), full((1, MLA_QK_PAD)),
            pl.BlockSpec((tm, 128), lambda i: (i, 0)),
            pl.BlockSpec((tm, 128), lambda i: (i, 0)),
            full((D_MODEL, SSD_IN_PAD)),
        ],
        out_specs=[
            pl.BlockSpec((MLA_HEADS, 1, MLA_QK_PAD, tm), lambda i: (0, i, 0, 0)),
            pl.BlockSpec((MLA_HEADS, tm, MLA_QK_PAD), lambda i: (0, i, 0)),
            pl.BlockSpec((MLA_HEADS, 1, MLA_V_AUG, tm), lambda i: (0, i, 0, 0)),
            pl.BlockSpec((tm, SSD_IN_PAD), lambda i: (i, 0)),
        ],
        out_shape=[
            jax.ShapeDtypeStruct((MLA_HEADS, lp // tm, MLA_QK_PAD, tm), BF16),
            jax.ShapeDtypeStruct((MLA_HEADS, lp, MLA_QK_PAD), BF16),
            jax.ShapeDtypeStruct((MLA_HEADS, lp // tm, MLA_V_AUG, tm), BF16),
            jax.ShapeDtypeStruct((lp, SSD_IN_PAD), F32),
        ],
        compiler_params=_cparams("parallel"),
        name="mla_prep",
    )(h, g, win, qn, wq, kvn, wkv, gq, gk, cos, sinm, wssd)


def _attn_kernel(q_ref, k_ref, vt_ref, o_ref):
    i = pl.program_id(1)
    tq = q_ref.shape[3]
    qt = q_ref[0, 0]
    key0 = lax.broadcasted_iota(jnp.int32, (tq, tq), 0)
    qry = i * tq + lax.broadcasted_iota(jnp.int32, (tq, tq), 1)

    def scores(j, masked):
        start = pl.multiple_of(j * tq, tq)
        k = k_ref[0, pl.ds(start, tq), :]
        s = jnp.dot(k, qt, preferred_element_type=F32)
        if masked:
            key = key0 + j * tq
            s = jnp.where(key <= qry, jnp.where(key >= PAD, s, NEG_INF), NEG_INF)
        return s

    def update(j, s, carry):
        m, acc = carry
        m_new = jnp.maximum(m, jnp.max(s, axis=0, keepdims=True))
        alpha = jnp.exp(m - m_new)
        p = jnp.exp((s - m_new).astype(BF16))
        acc = alpha * acc + jnp.dot(vt_ref[0, j], p, preferred_element_type=F32)
        return m_new, acc

    init = (jnp.full((1, tq), NEG_INF, F32), jnp.zeros((MLA_V_AUG, tq), F32))
    s_first = scores(0, True)

    def body(j, c):
        s_prev, carry = c
        s_new = scores(j, False)
        return s_new, update(j - 1, s_prev, carry)

    s_last, carry = lax.fori_loop(1, i, body, (s_first, init))
    j_last = jnp.maximum(i - 1, 0)
    m, acc = lax.cond(i > 0,
                      lambda c: update(i, scores(i, True), update(j_last, s_last, c)),
                      lambda c: update(j_last, s_last, c), carry)
    o_ref[...] = (acc[:MLA_V] / acc[MLA_V:MLA_V + 1]).T.astype(BF16)


def _attention(q, k, v):
    lp = k.shape[1]
    tq = TOKEN_TILE
    return pl.pallas_call(
        _attn_kernel,
        grid=(MLA_HEADS, lp // tq),
        in_specs=[
            pl.BlockSpec((1, 1, MLA_QK_PAD, tq), lambda h, i: (h, i, 0, 0)),
            pl.BlockSpec((1, lp, MLA_QK_PAD), lambda h, i: (h, 0, 0)),
            pl.BlockSpec((1, lp // tq, MLA_V_AUG, tq), lambda h, i: (h, 0, 0, 0)),
        ],
        out_specs=pl.BlockSpec((tq, MLA_V), lambda h, i: (i, h)),
        out_shape=jax.ShapeDtypeStruct((lp, MLA_HEADS * MLA_V), BF16),
        compiler_params=_cparams("parallel", "arbitrary"),
        name="mla_attention",
    )(q, k, v)


def _shift_rows(x, carry, s):
    rolled = pltpu.roll(x, s, axis=0)
    head = pltpu.roll(carry, s, axis=0)
    r8 = lax.broadcasted_iota(jnp.int32, (8, 1), 0)
    return jnp.concatenate([jnp.where(r8 < s, head, rolled[:8]), rolled[8:]], axis=0)


def _ssd_init(carry_ref, state_ref):
    carry_ref[...] = jnp.zeros_like(carry_ref)
    state_ref[...] = jnp.zeros_like(state_ref)


def _ssd_body(u_ref, cw_ref, cb_ref, dtb_ref, alog_ref, dsk_ref, ng_ref, tri_ref, e8_ref,
              y_ref, carry_ref, state_ref):
    i = pl.program_id(0)
    z = u_ref[:, :SSD_WIDTH]
    xbc = u_ref[:, SSD_WIDTH:SSD_WIDTH + SSD_CONV_CH]
    dt_raw = u_ref[:, SSD_WIDTH + SSD_CONV_CH:]
    carry = carry_ref[...]
    cw = cw_ref[...]
    conv = xbc * cw[3:4, :]
    for s in (1, 2, 3):
        conv = conv + _shift_rows(xbc, carry, s) * cw[3 - s:4 - s, :]
    carry_ref[...] = xbc[CHUNK - 8:, :]
    xbc = _silu(conv + cb_ref[...])
    xs = xbc[:, :SSD_WIDTH]
    bm = xbc[:, SSD_WIDTH:SSD_WIDTH + SSD_GROUPS * SSD_STATE]
    cm = xbc[:, SSD_WIDTH + SSD_GROUPS * SSD_STATE:]

    row = _row_ids(CHUNK, 1, i * CHUNK)
    dt = _softplus(dt_raw + dtb_ref[...]) * jnp.where(row >= PAD, 1.0, 0.0)
    la = dt * (-jnp.exp(alog_ref[...]))
    tri = tri_ref[...]
    e8 = e8_ref[...]
    cs = _dot01_left(tri, la)
    cs_e = _dot01_right(cs, e8)
    dt_e = _dot01_right(dt, e8)
    cs_t = cs.T
    cs_last_e = cs_e[CHUNK - 1:CHUNK, :]
    x = xs * dt_e
    xd = x * jnp.exp(cs_last_e - cs_e)
    ecs = jnp.exp(cs_e)
    dec = jnp.exp(cs_last_e)

    r_i = lax.broadcasted_iota(jnp.int32, (CHUNK, CHUNK), 0)
    c_i = lax.broadcasted_iota(jnp.int32, (CHUNK, CHUNK), 1)
    causal = r_i >= c_i
    per = SSD_HEADS // SSD_GROUPS
    gw = per * SSD_HEAD_DIM
    groups = range(SSD_GROUPS)
    b_gs = [bm[:, g * SSD_STATE:(g + 1) * SSD_STATE] for g in groups]
    c_gs = [cm[:, g * SSD_STATE:(g + 1) * SSD_STATE] for g in groups]
    s_prev = [state_ref[g] for g in groups]
    yield
    scores = [_mm_nt(c_gs[g], b_gs[g]) for g in groups]
    y_off = [_mm(c_gs[g], s_prev[g]) for g in groups]
    s_add = [_mm(b_gs[g].T, xd[:, g * gw:(g + 1) * gw]) for g in groups]
    yield
    lmats = []
    for h in range(SSD_HEADS):
        seg = cs[:, h:h + 1] - cs_t[h:h + 1, :]
        lmats.append(jnp.where(causal, jnp.exp(jnp.where(causal, seg, 0.0)), 0.0))
    y_diag = [_mm(scores[h // per] * lmats[h], x[:, h * SSD_HEAD_DIM:(h + 1) * SSD_HEAD_DIM])
              for h in range(SSD_HEADS)]
    yield
    for g in groups:
        state_ref[g] = s_prev[g] * dec[:, g * gw:(g + 1) * gw] + s_add[g]
    y = (jnp.concatenate(y_diag, axis=1) + jnp.concatenate(y_off, axis=1) * ecs) + dsk_ref[...] * xs
    y = y * _silu(z)
    y_ref[...] = _rms(y, ng_ref[...]).astype(BF16)


def _ssd_kernel(*refs):
    pl.when(pl.program_id(0) == 0)(lambda: _ssd_init(*refs[-2:]))
    for _ in _ssd_body(*refs):
        pass


def _ssd(u, cw, cb, dtb, alog, dsk, ng, tri, e8):
    lp = u.shape[0]
    full = lambda shape: pl.BlockSpec(shape, lambda i: (0,) * len(shape))
    return pl.pallas_call(
        _ssd_kernel,
        grid=(lp // CHUNK,),
        in_specs=[pl.BlockSpec((CHUNK, SSD_IN_PAD), lambda i: (i, 0)),
                  full((SSD_CONV, SSD_CONV_CH)), full((1, SSD_CONV_CH)), full((1, 128)), full((1, 128)),
                  full((1, SSD_WIDTH)), full((1, SSD_WIDTH)), full((CHUNK, CHUNK)), full((128, SSD_WIDTH))],
        out_specs=pl.BlockSpec((CHUNK, SSD_WIDTH), lambda i: (i, 0)),
        out_shape=jax.ShapeDtypeStruct((lp, SSD_WIDTH), BF16),
        scratch_shapes=[pltpu.VMEM((8, SSD_CONV_CH), F32),
                        pltpu.VMEM((SSD_GROUPS, SSD_STATE, SSD_WIDTH // SSD_GROUPS), F32)],
        compiler_params=_cparams("arbitrary"),
        name="ssd",
    )(u, cw, cb, dtb, alog, dsk, ng, tri, e8)


RET_LOG_G = [math.log(1.0 - 2.0 ** (-5.0 - h)) for h in range(RET_HEADS)]


def _ret_init(dmat_ref, state_ref):
    state_ref[...] = jnp.zeros_like(state_ref)
    r_i = lax.broadcasted_iota(jnp.int32, (CHUNK, CHUNK), 0)
    c_i = lax.broadcasted_iota(jnp.int32, (CHUNK, CHUNK), 1)
    diff = (r_i - c_i).astype(F32)
    for h in range(RET_HEADS):
        dmat_ref[h] = jnp.where(r_i >= c_i, jnp.exp(jnp.where(r_i >= c_i, diff, 0.0) * RET_LOG_G[h]), 0.0)


def _ret_body(u_ref, cos_ref, sin_ref, ng_ref, y_ref, dmat_ref, state_ref):
    log_g = RET_LOG_G
    qkw = RET_HEADS * RET_DK
    cos = cos_ref[...]
    sin = sin_ref[...]
    q = jnp.concatenate([_rope_lanes(u_ref[:, c * 128:(c + 1) * 128], cos, sin) for c in range(2)], axis=1)
    k = jnp.concatenate([_rope_lanes(u_ref[:, qkw + c * 128:qkw + (c + 1) * 128], cos, sin)
                         for c in range(2)], axis=1) * (RET_DK ** -0.5)
    idx = lax.broadcasted_iota(jnp.int32, (CHUNK, 1), 0).astype(F32)
    k_t = k.T
    idx_row = lax.broadcasted_iota(jnp.int32, (1, CHUNK), 1).astype(F32)
    heads = range(RET_HEADS)
    q_hs = [q[:, h * RET_DK:(h + 1) * RET_DK] for h in heads]
    k_hs = [k[:, h * RET_DK:(h + 1) * RET_DK] for h in heads]
    v_hs = [u_ref[:, 2 * qkw + h * RET_DV: 2 * qkw + (h + 1) * RET_DV].astype(BF16) for h in heads]
    s_prevs = [state_ref[h] for h in heads]
    yield
    scs = [_mm_nt(q_hs[h], k_hs[h]) for h in heads]
    y_cross = [_mm(q_hs[h] * jnp.exp((idx + 1.0) * log_g[h]), s_prevs[h]) for h in heads]
    yield
    s_adds = [_mm(k_t[h * RET_DK:(h + 1) * RET_DK, :] * jnp.exp((CHUNK - 1 - idx_row) * log_g[h]), v_hs[h])
              for h in heads]
    y_in = [_mm(scs[h] * dmat_ref[h], v_hs[h]) for h in heads]
    yield
    for h in heads:
        g_h = u_ref[:, 2 * qkw + RET_WIDTH + h * RET_DV: 2 * qkw + RET_WIDTH + (h + 1) * RET_DV]
        y = y_in[h] + y_cross[h]
        state_ref[h] = s_prevs[h] * math.exp(CHUNK * log_g[h]) + s_adds[h]
        mu = jnp.mean(y, axis=-1, keepdims=True)
        var = jnp.mean(jnp.square(y - mu), axis=-1, keepdims=True)
        yn = (y - mu) * lax.rsqrt(var + EPS) * ng_ref[:, h * RET_DV:(h + 1) * RET_DV]
        y_ref[:, h * RET_DV:(h + 1) * RET_DV] = (_silu(g_h) * yn).astype(BF16)


def _ret_kernel(*refs):
    pl.when(pl.program_id(0) == 0)(lambda: _ret_init(*refs[-2:]))
    for _ in _ret_body(*refs):
        pass


def _retention(u, cos, sin, ng):
    lp = u.shape[0]
    return pl.pallas_call(
        _ret_kernel,
        grid=(lp // CHUNK,),
        in_specs=[pl.BlockSpec((CHUNK, RET_IN), lambda i: (i, 0)),
                  pl.BlockSpec((CHUNK, 128), lambda i: (i, 0)),
                  pl.BlockSpec((CHUNK, 128), lambda i: (i, 0)),
                  pl.BlockSpec((1, RET_WIDTH), lambda i: (0, 0))],
        out_specs=pl.BlockSpec((CHUNK, RET_WIDTH), lambda i: (i, 0)),
        out_shape=jax.ShapeDtypeStruct((lp, RET_WIDTH), BF16),
        scratch_shapes=[pltpu.VMEM((RET_HEADS, CHUNK, CHUNK), F32),
                        pltpu.VMEM((RET_HEADS, RET_DK, RET_DV), F32)],
        compiler_params=_cparams("arbitrary"),
        name="retention",
    )(u, cos, sin, ng)


def _inv_unit_upper_many(mats, in_block, eye):
    d = [jnp.where(in_block, a, 0.0) for a in mats]
    f = [a - x for a, x in zip(mats, d)]
    d2 = [_mm(x, x) for x in d]
    yield
    d4 = [_mm(x, x) for x in d2]
    p = [_mm(eye + x, eye + y) for x, y in zip(d, d2)]
    yield
    td = [_mm(x, eye + y) for x, y in zip(p, d4)]
    yield
    g = [_mm(x, y) for x, y in zip(f, td)]
    yield
    g2 = [_mm(x, x) for x in g]
    tg = [_mm(x, eye + y) for x, y in zip(td, g)]
    yield
    g4 = [_mm(x, x) for x in g2]
    tg = [_mm(x, eye + y) for x, y in zip(tg, g2)]
    yield
    return [_mm(x, eye + y) for x, y in zip(tg, g4)]


def _rwkv_init(prev_ref, state_ref):
    prev_ref[...] = jnp.zeros_like(prev_ref)
    state_ref[...] = jnp.zeros_like(state_ref)


def _rwkv_body(u_ref, mu_ref, w0_ref, a0_ref, kk_ref, ka_ref, rk_ref, ln_ref, lora_ref, tri_ref,
               ones_ref, y_ref, prev_ref, state_ref, ot_ref):
    u = u_ref[...]
    rows = lax.broadcasted_iota(jnp.int32, (CHUNK, 1), 0)
    u_prev = jnp.where(rows == 0, prev_ref[7:8, :], pltpu.roll(u, 1, axis=0))
    prev_ref[...] = u[CHUNK - 8:, :]
    us = u + (u_prev - u) * mu_ref[...]
    w3 = RWKV_WIDTH
    r = us[:, :w3]
    k = us[:, w3:2 * w3]
    v = us[:, 2 * w3:3 * w3]
    lo = us[:, 3 * w3:]
    w = w0_ref[...] + _mm(jnp.tanh(lo), lora_ref[0])
    w = -_softplus(-w) - 0.5
    ld = -jnp.exp(w)
    a = _sigmoid(a0_ref[...] + _mm(lo, lora_ref[1]))
    g = _mm(_sigmoid(lo), lora_ref[2])
    ones = ones_ref[...]
    kk = k * kk_ref[...]
    k2 = k * (1.0 + (a - 1.0) * ka_ref[...])
    kk = kk / jnp.maximum(jnp.sqrt(_dot01_right(kk * kk, ones)), 1e-12)
    b = kk * a

    lc_full = _dot01_left(tri_ref[...], ld)
    v_t = v.T

    n = RWKV_SUB
    r_i = lax.broadcasted_iota(jnp.int32, (n, n), 0)
    c_i = lax.broadcasted_iota(jnp.int32, (n, n), 1)
    in_block = (r_i // RWKV_INV_BLOCK) == (c_i // RWKV_INV_BLOCK)
    eye = jnp.where(r_i == c_i, 1.0, 0.0)
    hd = RWKV_HEAD_DIM
    n_sub = CHUNK // n
    heads = range(RWKV_HEADS)
    hs = [slice(h * hd, (h + 1) * hd) for h in heads]

    sc = []
    for sub in range(n_sub):
        lo_r, hi_r = sub * n, (sub + 1) * n
        lc = lc_full[lo_r:hi_r, :]
        if sub > 0:
            lc = lc - lc_full[lo_r - 1:lo_r, :]
        lc_last = lc[n - 1:n, :]
        e_pos = jnp.exp(lc)
        e_neg = jnp.exp(-lc)
        e_prev = jnp.exp(lc - ld[lo_r:hi_r, :])
        e_end = jnp.exp(lc_last - lc)
        sc.append(dict(
            gam=jnp.exp(lc_last),
            a_t=(-kk[lo_r:hi_r, :] * e_prev).astype(BF16),
            r_t=(r[lo_r:hi_r, :] * e_pos).astype(BF16),
            b_t=(b[lo_r:hi_r, :] * e_neg).astype(BF16),
            k_t=(k2[lo_r:hi_r, :] * e_neg).astype(BF16),
            b_h=(b[lo_r:hi_r, :] * e_end).astype(BF16),
            k_h=(k2[lo_r:hi_r, :] * e_end).astype(BF16),
            v_t=v_t[:, lo_r:hi_r].astype(BF16),
        ))

    pairs = [(sub, h) for sub in range(n_sub) for h in heads]
    yield
    bk = [jnp.concatenate([sc[sub]["b_t"][:, hs[h]], sc[sub]["k_t"][:, hs[h]]], axis=0) for sub, h in pairs]
    ar2 = [jnp.concatenate([sc[sub]["a_t"][:, hs[h]], sc[sub]["r_t"][:, hs[h]]], axis=0) for sub, h in pairs]
    r2 = lax.broadcasted_iota(jnp.int32, (n, 2 * n), 0)
    c2 = lax.broadcasted_iota(jnp.int32, (n, 2 * n), 1)
    mask2 = r2 < jnp.where(c2 < n, c2, c2 - n + 1)
    quad = [_mm_nt(x, y) for x, y in zip(bk, ar2)]
    top = [jnp.where(mask2, x[:n, :], 0.0) for x in quad]
    bot = [jnp.where(mask2, x[n:, :], 0.0).astype(BF16) for x in quad]
    yield
    vprod = [_mm(sc[sub]["v_t"][hs[h], :], jnp.concatenate([y, sc[sub]["k_h"][:, hs[h]]], axis=1))
             for y, (sub, h) in zip(bot, pairs)]
    yield
    t_t = yield from _inv_unit_upper_many([x[:, :n] for x in top], in_block, eye)
    top = [x.astype(BF16) for x in top]

    state = [state_ref[h] for h in heads]
    for sub in range(n_sub):
        c = sc[sub]
        base = sub * RWKV_HEADS
        yield
        xr = [_mm_nt(state[h], ar2[base + h]) for h in heads]
        yield
        u_t = [_mm(xr[h][:, :n] + vprod[base + h][:, :n], t_t[base + h]) for h in heads]
        yield
        ub = [_mm(u_t[h], top[base + h]) for h in heads]
        su = [_mm(u_t[h], c["b_h"][:, hs[h]]) for h in heads]
        for h in heads:
            ot_ref[h * hd:(h + 1) * hd, sub * n:(sub + 1) * n] = (
                xr[h][:, n:] + ub[h][:, n:] + vprod[base + h][:, n:2 * n])
        state = [state[h] * c["gam"][:, hs[h]] + su[h] + vprod[base + h][:, 2 * n:] for h in heads]
    for h in heads:
        state_ref[h] = state[h]
    yield

    out = ot_ref[...].T
    inv_n = 1.0 / RWKV_HEAD_DIM
    mean = _dot01_right(out, ones) * inv_n
    cen = out - mean
    var = _dot01_right(cen * cen, ones) * inv_n
    out = cen * lax.rsqrt(var + RWKV_LN_EPS) * ln_ref[...]
    bonus = _dot01_right(r * k2 * rk_ref[...], ones) * v
    y_ref[...] = ((out + bonus) * g).astype(BF16)


def _rwkv_kernel(*refs):
    pl.when(pl.program_id(0) == 0)(lambda: _rwkv_init(*refs[-3:-1]))
    for _ in _rwkv_body(*refs):
        pass


N_SSD_IN, N_RET_IN, N_RWKV_IN = 9, 4, 11


def _mixers_kernel(*refs):
    n_in = N_SSD_IN + N_RET_IN + N_RWKV_IN
    ins, (y_ssd, y_ret, y_rwkv), scr = refs[:n_in], refs[n_in:n_in + 3], refs[n_in + 3:]
    ssd_in, ret_in, rwkv_in = ins[:N_SSD_IN], ins[N_SSD_IN:N_SSD_IN + N_RET_IN], ins[N_SSD_IN + N_RET_IN:]
    ssd_scr, ret_scr, rwkv_scr = scr[:2], scr[2:4], scr[4:]

    @pl.when(pl.program_id(0) == 0)
    def _():
        _ssd_init(*ssd_scr)
        _ret_init(*ret_scr)
        _rwkv_init(*rwkv_scr[:2])

    main = _rwkv_body(*rwkv_in, y_rwkv, *rwkv_scr)
    side = itertools.chain(_ssd_body(*ssd_in, y_ssd, *ssd_scr), _ret_body(*ret_in, y_ret, *ret_scr))
    for step, _ in enumerate(main):
        if step % 2 == 0:
            next(side, None)
    for _ in side:
        pass


def _mixers(u_ssd, ssd_params, u_ret, ret_params, u_rwkv, rwkv_params):
    lp = u_ssd.shape[0]
    full = lambda a: pl.BlockSpec(a.shape, lambda i: (0,) * a.ndim)
    rows = lambda a: pl.BlockSpec((CHUNK, a.shape[1]), lambda i: (i, 0))
    cos, sin, ng = ret_params
    args = [u_ssd, *ssd_params, u_ret, cos, sin, ng, u_rwkv, *rwkv_params]
    assert len(args) == N_SSD_IN + N_RET_IN + N_RWKV_IN
    in_specs = ([rows(u_ssd)] + [full(a) for a in ssd_params]
                + [rows(u_ret), rows(cos), rows(sin), full(ng)]
                + [rows(u_rwkv)] + [full(a) for a in rwkv_params])
    out = jax.ShapeDtypeStruct((lp, 512), BF16)
    ospec = pl.BlockSpec((CHUNK, 512), lambda i: (i, 0))
    return pl.pallas_call(
        _mixers_kernel,
        grid=(lp // CHUNK,),
        in_specs=in_specs,
        out_specs=[ospec, ospec, ospec],
        out_shape=[out, out, out],
        scratch_shapes=[pltpu.VMEM((8, SSD_CONV_CH), F32),
                        pltpu.VMEM((SSD_GROUPS, SSD_STATE, SSD_WIDTH // SSD_GROUPS), F32),
                        pltpu.VMEM((RET_HEADS, CHUNK, CHUNK), F32),
                        pltpu.VMEM((RET_HEADS, RET_DK, RET_DV), F32),
                        pltpu.VMEM((8, RWKV_IN), F32),
                        pltpu.VMEM((RWKV_HEADS, RWKV_HEAD_DIM, RWKV_HEAD_DIM), F32),
                        pltpu.VMEM((RWKV_WIDTH, CHUNK), F32)],
        compiler_params=_cparams("arbitrary"),
        name="mixers",
    )(*args)


def _rwkv(u, mu, w0, a0, k_k, k_a, r_k, ln, lora, tri, ones):
    lp = u.shape[0]
    full = lambda shape: pl.BlockSpec(shape, lambda i: (0,) * len(shape))
    vec = full((1, RWKV_WIDTH))
    return pl.pallas_call(
        _rwkv_kernel,
        grid=(lp // CHUNK,),
        in_specs=[pl.BlockSpec((CHUNK, RWKV_IN), lambda i: (i, 0)), full((1, RWKV_IN)),
                  vec, vec, vec, vec, vec, vec, full((3, 128, RWKV_WIDTH)), full((CHUNK, CHUNK)),
                  full((RWKV_WIDTH, RWKV_WIDTH))],
        out_specs=pl.BlockSpec((CHUNK, RWKV_WIDTH), lambda i: (i, 0)),
        out_shape=jax.ShapeDtypeStruct((lp, RWKV_WIDTH), BF16),
        scratch_shapes=[pltpu.VMEM((8, RWKV_IN), F32),
                        pltpu.VMEM((RWKV_HEADS, RWKV_HEAD_DIM, RWKV_HEAD_DIM), F32),
                        pltpu.VMEM((RWKV_WIDTH, CHUNK), F32)],
        compiler_params=_cparams("arbitrary"),
        name="rwkv7",
    )(u, mu, w0, a0, k_k, k_a, r_k, ln, lora, tri, ones)


def _constants():
    r = jnp.arange(CHUNK)
    tri = (r[:, None] >= r[None, :]).astype(BF16)
    lane = jnp.arange(RWKV_WIDTH)
    ones_bd = ((lane[:, None] // RWKV_HEAD_DIM) == (lane[None, :] // RWKV_HEAD_DIM)).astype(BF16)
    e8 = ((jnp.arange(128)[:, None] == (lane[None, :] // SSD_HEAD_DIM))).astype(BF16)
    return tri, ones_bd, e8


def kernel(x, meta_tokens, ffn1_norm, ffn1_w_gate, ffn1_w_up, ffn1_w_down, mix_norm, w_in, w_out, mla_q_norm, mla_w_q_up, mla_kv_norm, mla_w_kv_up, mla_qk_norm_q, mla_qk_norm_k, ssd_conv_w, ssd_conv_b, ssd_dt_bias, ssd_a_log, ssd_d, ssd_norm, ret_norm, rwkv_mu, rwkv_w0, rwkv_w2, rwkv_a0, rwkv_a2, rwkv_g2, rwkv_k_k, rwkv_k_a, rwkv_r_k, rwkv_ln, ffn2_norm, ffn2_w_gate, ffn2_w_up, ffn2_w_down):
    b, seq, d = x.shape
    assert b == 1 and d == D_MODEL and seq % CHUNK == 0
    nl = w_in.shape[0]
    lp = seq + CHUNK
    assert lp % TOKEN_TILE == 0

    hst = jnp.concatenate([jnp.zeros((PAD, d), x.dtype), meta_tokens.astype(x.dtype), x[0]], axis=0)

    wq =jnp.pad(mla_w_q_up.reshape(nl, MLA_Q_LORA, MLA_HEADS, MLA_QK),
                 ((0, 0), (0, 0), (0, 0), (0, MLA_QK_PAD - MLA_QK))
                 ).reshape(nl, MLA_Q_LORA, MLA_HEADS * MLA_QK_PAD).astype(BF16)
    wkv = mla_w_kv_up.reshape(nl, MLA_KV_LORA, MLA_HEADS, 2, MLA_NOPE).transpose(0, 1, 3, 2, 4
                              ).reshape(nl, MLA_KV_LORA, 2 * MLA_HEADS * MLA_NOPE).astype(BF16)
    gq = jnp.pad(mla_qk_norm_q, ((0, 0), (0, MLA_QK_PAD - MLA_QK)))[:, None, :]
    gk = jnp.pad(mla_qk_norm_k, ((0, 0), (0, MLA_QK_PAD - MLA_QK)))[:, None, :]
    dtb = jnp.pad(ssd_dt_bias, ((0, 0), (0, 128 - SSD_HEADS)))[:, None, :]
    alog = jnp.pad(ssd_a_log, ((0, 0), (0, 128 - SSD_HEADS)))[:, None, :]
    dsk = jnp.repeat(ssd_d, SSD_HEAD_DIM, axis=1)[:, None, :]
    lora = jnp.zeros((nl, 3, 128, RWKV_WIDTH), F32)
    lora = lora.at[:, 0, :RWKV_DECAY_LORA].set(rwkv_w2)
    lora = lora.at[:, 1, RWKV_DECAY_LORA:RWKV_DECAY_LORA + RWKV_A_LORA].set(rwkv_a2)
    lora = lora.at[:, 2, RWKV_DECAY_LORA + RWKV_A_LORA:].set(rwkv_g2)
    lora = lora.astype(BF16)

    tri, ones_bd, e8 = _constants()
    cos, sin, sinm = _rope_tables(lp)

    w_ffn = (ffn1_w_gate[0].astype(BF16), ffn1_w_up[0].astype(BF16), ffn1_w_down[0].astype(BF16))
    for l in range(nl):
        hst, w_ffn, (w_mla, w_ssd, w_ret, w_rwkv, w_o) = _ffn(
            hst, ffn1_norm[l][None, :], *w_ffn, nxt=(ffn2_w_gate, ffn2_w_up, ffn2_w_down, l),
            proj=(w_in, w_out, l))
        gmix = mix_norm[l][None, :]
        q, k, v, u_ssd = _mla_prep(hst, gmix, w_mla, mla_q_norm[l][None, :], wq, mla_kv_norm[l][None, :],
                                   wkv, gq[l], gk[l], cos, sinm, w_ssd, l)
        y_mla = _attention(q, k, v)
        u_ret = _inproj(hst, gmix, w_ret)
        u_rwkv = _inproj(hst, gmix, w_rwkv)
        y_ssd, y_ret, y_rwkv = _mixers(
            u_ssd, (ssd_conv_w[l], ssd_conv_b[l][None, :], dtb[l], alog[l], dsk[l], ssd_norm[l][None, :], tri, e8),
            u_ret, (cos, sin, ret_norm[l].reshape(1, RET_WIDTH)),
            u_rwkv, (rwkv_mu[l][None, :], rwkv_w0[l][None, :], rwkv_a0[l][None, :], rwkv_k_k[l][None, :],
                     rwkv_k_a[l][None, :], rwkv_r_k[l].reshape(1, RWKV_WIDTH), rwkv_ln[l].reshape(1, RWKV_WIDTH),
                     lora[l], tri, ones_bd))
        hst = _outproj(hst, (y_mla, y_ssd, y_ret, y_rwkv), w_o.reshape(4, 512, D_MODEL))
        nxt = (ffn1_w_gate, ffn1_w_up, ffn1_w_down, l + 1) if l + 1 < nl else None
        hst, w_ffn, _ = _ffn(hst, ffn2_norm[l][None, :], *w_ffn, nxt=nxt)
    return hst[CHUNK:][None]
```

```python
import functools
import itertools
import math

import jax
import jax.numpy as jnp
from jax import lax
from jax.experimental import pallas as pl
from jax.experimental.pallas import tpu as pltpu

F32 = jnp.float32
BF16 = jnp.bfloat16

D_MODEL = 2048
DEPTH = 4
N_META = 16
CHUNK = 128
PAD = CHUNK - N_META
D_FF = 5632
EPS = 1e-6
ROPE_THETA = 10000.0
NEG_INF = -1e30

MLA_HEADS = 4
MLA_NOPE = 128
MLA_ROPE = 64
MLA_QK = MLA_NOPE + MLA_ROPE
MLA_V = 128
MLA_Q_LORA = 384
MLA_KV_LORA = 128
MLA_QK_PAD = 256
MLA_V_AUG = MLA_V + 16
MLA_IN = MLA_Q_LORA + MLA_KV_LORA + MLA_ROPE
MLA_IN_PAD = 640

SSD_HEADS = 8
SSD_HEAD_DIM = 64
SSD_WIDTH = 512
SSD_GROUPS = 2
SSD_STATE = 128
SSD_CONV = 4
SSD_CONV_CH = SSD_WIDTH + 2 * SSD_GROUPS * SSD_STATE
SSD_IN = SSD_WIDTH + SSD_CONV_CH + SSD_HEADS
SSD_IN_PAD = SSD_WIDTH + SSD_CONV_CH + 128

RET_HEADS = 4
RET_DK = 64
RET_DV = 128
RET_WIDTH = 512
RET_IN = 2 * RET_HEADS * RET_DK + 2 * RET_WIDTH

RWKV_HEADS = 8
RWKV_HEAD_DIM = 64
RWKV_WIDTH = 512
RWKV_DECAY_LORA = 32
RWKV_A_LORA = 32
RWKV_GATE_LORA = 64
RWKV_LN_EPS = 64e-5
RWKV_IN = 3 * RWKV_WIDTH + RWKV_DECAY_LORA + RWKV_A_LORA + RWKV_GATE_LORA
RWKV_SUB = 64
RWKV_INV_BLOCK = 8

MIX_WIDTH = 2048
N_MIXERS = 4
MIXER_WIDTH = MIX_WIDTH // N_MIXERS

V7X_VMEM_LIMIT_BYTES = 56 * 1024 * 1024
TOKEN_TILE = 640
FF_TILE = 512
FFN_CONVERT_ROWS = 256
PROJ_CONVERT_ROWS = 16


def _cparams(*sem):
    return pltpu.CompilerParams(dimension_semantics=sem, vmem_limit_bytes=V7X_VMEM_LIMIT_BYTES)


def _sigmoid(x):
    return 1.0 / (1.0 + jnp.exp(-x))


def _silu(x):
    return x * _sigmoid(x)


def _softplus(x):
    return jnp.maximum(x, 0.0) + jnp.log(1.0 + jnp.exp(-jnp.abs(x)))


def _rms(x, g, eps=EPS):
    return x * lax.rsqrt(jnp.mean(x * x, axis=-1, keepdims=True) + eps) * g


def _mm(a, b):
    return jnp.dot(a.astype(BF16), b.astype(BF16), preferred_element_type=F32)


def _mm_nt(a, b):
    return lax.dot_general(a.astype(BF16), b.astype(BF16), (((1,), (1,)), ((), ())),
                           preferred_element_type=F32)


def _split3(x):
    x1 = x.astype(BF16)
    r1 = x - x1.astype(F32)
    x2 = r1.astype(BF16)
    x3 = (r1 - x2.astype(F32)).astype(BF16)
    return x1, x2, x3


def _dot01_right(x, m01, pieces=3):
    out = None
    for p in _split3(x)[:pieces]:
        d = jnp.dot(p, m01, preferred_element_type=F32)
        out = d if out is None else out + d
    return out


def _dot01_left(m01, x):
    p1, p2, p3 = _split3(x)
    return (jnp.dot(m01, p1, preferred_element_type=F32)
            + jnp.dot(m01, p2, preferred_element_type=F32)
            + jnp.dot(m01, p3, preferred_element_type=F32))


def _row_ids(rows, cols, base):
    return base + lax.broadcasted_iota(jnp.int32, (rows, cols), 0)


def _rope_table_kernel(inv_ref, cos_ref, sin_ref, sinm_ref):
    i = pl.program_id(0)
    pos = (_row_ids(CHUNK, 128, i * CHUNK) - PAD).astype(F32)
    lane = lax.broadcasted_iota(jnp.int32, (CHUNK, 128), 1)
    ang = pos * inv_ref[...]
    c = jnp.cos(ang)
    s = jnp.sin(ang)
    s = jnp.where((lane % 64) < 32, -s, s)
    cos_ref[...] = c
    sin_ref[...] = s
    sinm_ref[...] = jnp.where(lane < 64, s, 0.0)


def _rope_tables(lp):
    half = 32
    inv = ROPE_THETA ** (-jnp.arange(half, dtype=F32) / half)
    inv = jnp.tile(inv, 4)[None, :]
    out = jax.ShapeDtypeStruct((lp, 128), F32)
    spec = pl.BlockSpec((CHUNK, 128), lambda i: (i, 0))
    return pl.pallas_call(
        _rope_table_kernel,
        grid=(lp // CHUNK,),
        in_specs=[pl.BlockSpec((1, 128), lambda i: (0, 0))],
        out_specs=[spec, spec, spec],
        out_shape=[out, out, out],
        compiler_params=_cparams("parallel"),
        name="rope_tables",
    )(inv)


def _rope_lanes(x, cos, sin_signed):
    lane = lax.broadcasted_iota(jnp.int32, x.shape, 1)
    fwd = pltpu.roll(x, 32, axis=1)
    bwd = pltpu.roll(x, 96, axis=1)
    rot = jnp.where((lane % 64) < 32, bwd, fwd)
    return x * cos + rot * sin_signed


def _ffn_kernel(*refs, convert_next, convert_proj):
    refs = list(refs)
    x_ref, g_ref, wg_ref, wu_ref, wd_ref = refs[:5]
    n_in = 5 + (3 if convert_next else 0) + (2 if convert_proj else 0)
    ins, outs, xn_ref = refs[5:n_in], refs[n_in:-1], refs[-1]
    o_ref = outs.pop(0)
    if convert_next:
        for src, dst in zip(ins[:3], outs[:3]):
            dst[...] = src[...].astype(BF16)
        ins, outs = ins[3:], outs[3:]
    if convert_proj:
        win_ref, wout_ref = ins
        wmla_ref, wssd_ref, wret_ref, wrwkv_ref, wo_ref = outs
        o0 = MLA_IN
        o1 = o0 + SSD_IN
        o2 = o1 + RET_IN
        rows = win_ref.shape[0]
        wmla_ref[:, :MLA_IN] = win_ref[:, :o0].astype(BF16)
        wmla_ref[:, MLA_IN:] = jnp.zeros((rows, MLA_IN_PAD - MLA_IN), BF16)
        wssd_ref[:, :SSD_IN] = win_ref[:, o0:o1].astype(BF16)
        wssd_ref[:, SSD_IN:] = jnp.zeros((rows, SSD_IN_PAD - SSD_IN), BF16)
        wret_ref[...] = win_ref[:, o1:o2].astype(BF16)
        wrwkv_ref[...] = win_ref[:, o2:].astype(BF16)
        wo_ref[...] = wout_ref[...].astype(BF16)
    j = pl.program_id(1)

    @pl.when(j == 0)
    def _():
        xn_ref[...] = _rms(x_ref[...], g_ref[...]).astype(BF16)
        o_ref[...] = jnp.zeros_like(o_ref)

    xn = xn_ref[...]
    a = jnp.dot(xn, wg_ref[...], preferred_element_type=F32)
    b = jnp.dot(xn, wu_ref[...], preferred_element_type=F32)
    mid = (_silu(a) * b).astype(BF16)
    o_ref[...] += jnp.dot(mid, wd_ref[...], preferred_element_type=F32)

    @pl.when(j == pl.num_programs(1) - 1)
    def _():
        o_ref[...] = x_ref[...] + 0.5 * o_ref[...]


def _ffn(h, g, wgb, wub, wdb, nxt=None, proj=None):
    lp = h.shape[0]
    tm, tf = TOKEN_TILE, FF_TILE
    n_j = D_FF // tf
    n_c = D_MODEL // FFN_CONVERT_ROWS if nxt is not None else 0
    in_specs = [
        pl.BlockSpec((tm, D_MODEL), lambda i, j: (i, 0)),
        pl.BlockSpec((1, D_MODEL), lambda i, j: (0, 0)),
        pl.BlockSpec((D_MODEL, tf), lambda i, j: (0, j)),
        pl.BlockSpec((D_MODEL, tf), lambda i, j: (0, j)),
        pl.BlockSpec((tf, D_MODEL), lambda i, j: (j, 0)),
    ]
    out_specs = [pl.BlockSpec((tm, D_MODEL), lambda i, j: (i, 0))]
    out_shape = [jax.ShapeDtypeStruct((lp, D_MODEL), F32)]
    args = [h, g, wgb, wub, wdb]
    if nxt is not None:
        wg, wu, wd, l = nxt
        cr = FFN_CONVERT_ROWS
        assert lp // tm >= n_c
        ci = lambda i: jnp.minimum(i, n_c - 1)
        cj = lambda i, j: jnp.where(i < n_c, j, n_j - 1)
        in_specs += [
            pl.BlockSpec((None, cr, tf), lambda i, j: (l, ci(i), cj(i, j))),
            pl.BlockSpec((None, cr, tf), lambda i, j: (l, ci(i), cj(i, j))),
            pl.BlockSpec((None, tf, cr), lambda i, j: (l, cj(i, j), ci(i))),
        ]
        out_specs += [
            pl.BlockSpec((cr, tf), lambda i, j: (ci(i), cj(i, j))),
            pl.BlockSpec((cr, tf), lambda i, j: (ci(i), cj(i, j))),
            pl.BlockSpec((tf, cr), lambda i, j: (cj(i, j), ci(i))),
        ]
        out_shape += [
            jax.ShapeDtypeStruct((D_MODEL, D_FF), BF16),
            jax.ShapeDtypeStruct((D_MODEL, D_FF), BF16),
            jax.ShapeDtypeStruct((D_FF, D_MODEL), BF16),
        ]
        args += [wg, wu, wd]
    if proj is not None:
        w_in, w_out, lpj = proj
        pr = PROJ_CONVERT_ROWS
        n_p = D_MODEL // pr
        assert (lp // tm) * n_j >= n_p
        pidx = lambda i, j: jnp.minimum(i * n_j + j, n_p - 1)
        in_specs += [
            pl.BlockSpec((None, pr, w_in.shape[2]), lambda i, j: (lpj, pidx(i, j), 0)),
            pl.BlockSpec((None, pr, D_MODEL), lambda i, j: (lpj, pidx(i, j), 0)),
        ]
        widths = (MLA_IN_PAD, SSD_IN_PAD, RET_IN, RWKV_IN, D_MODEL)
        out_specs += [pl.BlockSpec((pr, n), lambda i, j: (pidx(i, j), 0)) for n in widths]
        out_shape += [jax.ShapeDtypeStruct((D_MODEL, n), BF16) for n in widths]
        args += [w_in, w_out]
    outs = pl.pallas_call(
        functools.partial(_ffn_kernel, convert_next=nxt is not None, convert_proj=proj is not None),
        grid=(lp // tm, n_j),
        in_specs=in_specs,
        out_specs=out_specs,
        out_shape=out_shape,
        scratch_shapes=[pltpu.VMEM((tm, D_MODEL), BF16)],
        compiler_params=_cparams("arbitrary", "arbitrary"),
        name="ffn",
    )(*args)
    n_next = 3 if nxt is not None else 0
    return (outs[0], tuple(outs[1:1 + n_next]) if nxt is not None else None,
            tuple(outs[1 + n_next:]) if proj is not None else None)


def _inproj_kernel(x_ref, g_ref, w_ref, o_ref):
    i = pl.program_id(0)
    tm = x_ref.shape[0]
    xn = _rms(x_ref[...], g_ref[...]).astype(BF16)
    u = jnp.dot(xn, w_ref[...], preferred_element_type=F32)
    row = _row_ids(tm, 1, i * tm)
    o_ref[...] = jnp.where(row >= PAD, u, 0.0)


def _inproj(h, g, w):
    lp = h.shape[0]
    n = w.shape[1]
    tm = TOKEN_TILE
    return pl.pallas_call(
        _inproj_kernel,
        grid=(lp // tm,),
        in_specs=[
            pl.BlockSpec((tm, D_MODEL), lambda i: (i, 0)),
            pl.BlockSpec((1, D_MODEL), lambda i: (0, 0)),
            pl.BlockSpec((D_MODEL, n), lambda i: (0, 0)),
        ],
        out_specs=pl.BlockSpec((tm, n), lambda i: (i, 0)),
        out_shape=jax.ShapeDtypeStruct((lp, n), F32),
        compiler_params=_cparams("parallel"),
        name="inproj",
    )(h, g, w)


def _outproj_kernel(h_ref, y0_ref, y1_ref, y2_ref, y3_ref, w_ref, o_ref):
    acc = h_ref[...]
    for m, y_ref in enumerate((y0_ref, y1_ref, y2_ref, y3_ref)):
        acc = acc + jnp.dot(y_ref[...], w_ref[m], preferred_element_type=F32)
    o_ref[...] = acc


def _outproj(h, ys, w):
    lp = h.shape[0]
    tm = TOKEN_TILE
    yspec = pl.BlockSpec((tm, MIXER_WIDTH), lambda i: (i, 0))
    return pl.pallas_call(
        _outproj_kernel,
        grid=(lp // tm,),
        in_specs=[pl.BlockSpec((tm, D_MODEL), lambda i: (i, 0)), yspec, yspec, yspec, yspec,
                  pl.BlockSpec((N_MIXERS, MIXER_WIDTH, D_MODEL), lambda i: (0, 0, 0))],
        out_specs=pl.BlockSpec((tm, D_MODEL), lambda i: (i, 0)),
        out_shape=jax.ShapeDtypeStruct((lp, D_MODEL), F32),
        compiler_params=_cparams("parallel"),
        name="outproj",
    )(h, *ys, w)


def _mla_prep_kernel(x_ref, g_ref, win_ref, qn_ref, wq_ref, kvn_ref, wkv_ref, gq_ref, gk_ref,
                     cos_ref, sinm_ref, wssd_ref, q_ref, k_ref, v_ref, ussd_ref):
    i = pl.program_id(0)
    tm = x_ref.shape[0]
    xn = _rms(x_ref[...], g_ref[...]).astype(BF16)
    u = jnp.dot(xn, win_ref[...], preferred_element_type=F32)
    row = _row_ids(tm, 1, i * tm)
    u = jnp.where(row >= PAD, u, 0.0)
    cq = u[:, :MLA_Q_LORA]
    ckv = u[:, MLA_Q_LORA:MLA_Q_LORA + MLA_KV_LORA]
    kpe = u[:, MLA_Q_LORA + MLA_KV_LORA:]
    q_all = _mm(_rms(cq, qn_ref[...]), wq_ref[...])
    kv_all = _mm(_rms(ckv, kvn_ref[...]), wkv_ref[...])
    ussd_ref[...] = jnp.where(row >= PAD, jnp.dot(xn, wssd_ref[...], preferred_element_type=F32), 0.0)
    cos = cos_ref[...]
    sinm = sinm_ref[...]
    gq = gq_ref[...]
    gk = gk_ref[...]
    scale = MLA_QK ** -0.5
    for h in range(MLA_HEADS):
        qh = q_all[:, h * MLA_QK_PAD:(h + 1) * MLA_QK_PAD]
        ss = jnp.sum(qh * qh, axis=-1, keepdims=True) * (1.0 / MLA_QK)
        qh = qh * lax.rsqrt(ss + EPS) * gq
        q_rot = _rope_lanes(qh[:, MLA_NOPE:], cos, sinm)
        q_ref[h, 0, :MLA_NOPE, :] = (qh[:, :MLA_NOPE] * scale).T.astype(BF16)
        q_ref[h, 0, MLA_NOPE:, :] = (q_rot * scale).T.astype(BF16)

        kn = kv_all[:, h * MLA_NOPE:(h + 1) * MLA_NOPE]
        ss = (jnp.sum(kn * kn, axis=-1, keepdims=True)
              + jnp.sum(kpe * kpe, axis=-1, keepdims=True)) * (1.0 / MLA_QK)
        rs = lax.rsqrt(ss + EPS)
        k_rot = _rope_lanes(kpe * rs * gk[:, MLA_NOPE:], cos, sinm)
        k_ref[h, :, :MLA_NOPE] = (kn * rs * gk[:, :MLA_NOPE]).astype(BF16)
        k_ref[h, :, MLA_NOPE:] = k_rot.astype(BF16)
        v_ref[h, 0, :MLA_V, :] = kv_all[:, 4 * MLA_NOPE + h * MLA_V: 4 * MLA_NOPE + (h + 1) * MLA_V].T.astype(BF16)
        tail = lax.broadcasted_iota(jnp.int32, (MLA_V_AUG - MLA_V, tm), 0)
        v_ref[h, 0, MLA_V:, :] = jnp.where(tail == 0, 1.0, 0.0).astype(BF16)


def _mla_prep(h, g, win, qn, wq, kvn, wkv, gq, gk, cos, sinm, wssd, l):
    lp = h.shape[0]
    tm = TOKEN_TILE
    full = lambda shape: pl.BlockSpec(shape, lambda i: (0,) * len(shape))
    layer = lambda shape: pl.BlockSpec((None,) + shape, lambda i: (l,) + (0,) * len(shape))
    return pl.pallas_call(
        _mla_prep_kernel,
        grid=(lp // tm,),
        in_specs=[
            pl.BlockSpec((tm, D_MODEL), lambda i: (i, 0)),
            full((1, D_MODEL)), full((D_MODEL, MLA_IN_PAD)),
            full((1, MLA_Q_LORA)), layer((MLA_Q_LORA, MLA_HEADS * MLA_QK_PAD)),
            full((1, MLA_KV_LORA)), layer((MLA_KV_LORA, 2 * MLA_HEADS * MLA_NOPE)),
            full((1, MLA_QK_PAD)), full((1, MLA_QK_PAD)),
            pl.BlockSpec((tm, 128), lambda i: (i, 0)),
            pl.BlockSpec((tm, 128), lambda i: (i, 0)),
            full((D_MODEL, SSD_IN_PAD)),
        ],
        out_specs=[
            pl.BlockSpec((MLA_HEADS, 1, MLA_QK_PAD, tm), lambda i: (0, i, 0, 0)),
            pl.BlockSpec((MLA_HEADS, tm, MLA_QK_PAD), lambda i: (0, i, 0)),
            pl.BlockSpec((MLA_HEADS, 1, MLA_V_AUG, tm), lambda i: (0, i, 0, 0)),
            pl.BlockSpec((tm, SSD_IN_PAD), lambda i: (i, 0)),
        ],
        out_shape=[
            jax.ShapeDtypeStruct((MLA_HEADS, lp // tm, MLA_QK_PAD, tm), BF16),
            jax.ShapeDtypeStruct((MLA_HEADS, lp, MLA_QK_PAD), BF16),
            jax.ShapeDtypeStruct((MLA_HEADS, lp // tm, MLA_V_AUG, tm), BF16),
            jax.ShapeDtypeStruct((lp, SSD_IN_PAD), F32),
        ],
        compiler_params=_cparams("parallel"),
        name="mla_prep",
    )(h, g, win, qn, wq, kvn, wkv, gq, gk, cos, sinm, wssd)


def _attn_kernel(q_ref, k_ref, vt_ref, o_ref):
    i = pl.program_id(1)
    tq = q_ref.shape[3]
    qt = q_ref[0, 0]
    key0 = lax.broadcasted_iota(jnp.int32, (tq, tq), 0)
    qry = i * tq + lax.broadcasted_iota(jnp.int32, (tq, tq), 1)

    def scores(j, masked):
        start = pl.multiple_of(j * tq, tq)
        k = k_ref[0, pl.ds(start, tq), :]
        s = jnp.dot(k, qt, preferred_element_type=F32)
        if masked:
            key = key0 + j * tq
            s = jnp.where(key <= qry, jnp.where(key >= PAD, s, NEG_INF), NEG_INF)
        return s

    def update(j, s, carry):
        m, acc = carry
        m_new = jnp.maximum(m, jnp.max(s, axis=0, keepdims=True))
        alpha = jnp.exp(m - m_new)
        p = jnp.exp((s - m_new).astype(BF16))
        acc = alpha * acc + jnp.dot(vt_ref[0, j], p, preferred_element_type=F32)
        return m_new, acc

    init = (jnp.full((1, tq), NEG_INF, F32), jnp.zeros((MLA_V_AUG, tq), F32))
    s_first = scores(0, True)

    def body(j, c):
        s_prev, carry = c
        s_new = scores(j, False)
        return s_new, update(j - 1, s_prev, carry)

    s_last, carry = lax.fori_loop(1, i, body, (s_first, init))
    j_last = jnp.maximum(i - 1, 0)
    m, acc = lax.cond(i > 0,
                      lambda c: update(i, scores(i, True), update(j_last, s_last, c)),
                      lambda c: update(j_last, s_last, c), carry)
    o_ref[...] = (acc[:MLA_V] / acc[MLA_V:MLA_V + 1]).T.astype(BF16)


def _attention(q, k, v):
    lp = k.shape[1]
    tq = TOKEN_TILE
    return pl.pallas_call(
        _attn_kernel,
        grid=(MLA_HEADS, lp // tq),
        in_specs=[
            pl.BlockSpec((1, 1, MLA_QK_PAD, tq), lambda h, i: (h, i, 0, 0)),
            pl.BlockSpec((1, lp, MLA_QK_PAD), lambda h, i: (h, 0, 0)),
            pl.BlockSpec((1, lp // tq, MLA_V_AUG, tq), lambda h, i: (h, 0, 0, 0)),
        ],
        out_specs=pl.BlockSpec((tq, MLA_V), lambda h, i: (i, h)),
        out_shape=jax.ShapeDtypeStruct((lp, MLA_HEADS * MLA_V), BF16),
        compiler_params=_cparams("parallel", "arbitrary"),
        name="mla_attention",
    )(q, k, v)


def _shift_rows(x, carry, s):
    rolled = pltpu.roll(x, s, axis=0)
    head = pltpu.roll(carry, s, axis=0)
    r8 = lax.broadcasted_iota(jnp.int32, (8, 1), 0)
    return jnp.concatenate([jnp.where(r8 < s, head, rolled[:8]), rolled[8:]], axis=0)


def _ssd_init(carry_ref, state_ref):
    carry_ref[...] = jnp.zeros_like(carry_ref)
    state_ref[...] = jnp.zeros_like(state_ref)


def _ssd_body(u_ref, cw_ref, cb_ref, dtb_ref, alog_ref, dsk_ref, ng_ref, tri_ref, e8_ref,
              y_ref, carry_ref, state_ref):
    i = pl.program_id(0)
    z = u_ref[:, :SSD_WIDTH]
    xbc = u_ref[:, SSD_WIDTH:SSD_WIDTH + SSD_CONV_CH]
    dt_raw = u_ref[:, SSD_WIDTH + SSD_CONV_CH:]
    carry = carry_ref[...]
    cw = cw_ref[...]
    conv = xbc * cw[3:4, :]
    for s in (1, 2, 3):
        conv = conv + _shift_rows(xbc, carry, s) * cw[3 - s:4 - s, :]
    carry_ref[...] = xbc[CHUNK - 8:, :]
    xbc = _silu(conv + cb_ref[...])
    xs = xbc[:, :SSD_WIDTH]
    bm = xbc[:, SSD_WIDTH:SSD_WIDTH + SSD_GROUPS * SSD_STATE]
    cm = xbc[:, SSD_WIDTH + SSD_GROUPS * SSD_STATE:]

    row = _row_ids(CHUNK, 1, i * CHUNK)
    dt = _softplus(dt_raw + dtb_ref[...]) * jnp.where(row >= PAD, 1.0, 0.0)
    la = dt * (-jnp.exp(alog_ref[...]))
    tri = tri_ref[...]
    e8 = e8_ref[...]
    cs = _dot01_left(tri, la)
    cs_e = _dot01_right(cs, e8)
    dt_e = _dot01_right(dt, e8)
    cs_t = cs.T
    cs_last_e = cs_e[CHUNK - 1:CHUNK, :]
    x = xs * dt_e
    xd = x * jnp.exp(cs_last_e - cs_e)
    ecs = jnp.exp(cs_e)
    dec = jnp.exp(cs_last_e)

    r_i = lax.broadcasted_iota(jnp.int32, (CHUNK, CHUNK), 0)
    c_i = lax.broadcasted_iota(jnp.int32, (CHUNK, CHUNK), 1)
    causal = r_i >= c_i
    per = SSD_HEADS // SSD_GROUPS
    gw = per * SSD_HEAD_DIM
    groups = range(SSD_GROUPS)
    b_gs = [bm[:, g * SSD_STATE:(g + 1) * SSD_STATE] for g in groups]
    c_gs = [cm[:, g * SSD_STATE:(g + 1) * SSD_STATE] for g in groups]
    s_prev = [state_ref[g] for g in groups]
    yield
    scores = [_mm_nt(c_gs[g], b_gs[g]) for g in groups]
    y_off = [_mm(c_gs[g], s_prev[g]) for g in groups]
    s_add = [_mm(b_gs[g].T, xd[:, g * gw:(g + 1) * gw]) for g in groups]
    yield
    lmats = []
    for h in range(SSD_HEADS):
        seg = cs[:, h:h + 1] - cs_t[h:h + 1, :]
        lmats.append(jnp.where(causal, jnp.exp(jnp.where(causal, seg, 0.0)), 0.0))
    y_diag = [_mm(scores[h // per] * lmats[h], x[:, h * SSD_HEAD_DIM:(h + 1) * SSD_HEAD_DIM])
              for h in range(SSD_HEADS)]
    yield
    for g in groups:
        state_ref[g] = s_prev[g] * dec[:, g * gw:(g + 1) * gw] + s_add[g]
    y = (jnp.concatenate(y_diag, axis=1) + jnp.concatenate(y_off, axis=1) * ecs) + dsk_ref[...] * xs
    y = y * _silu(z)
    y_ref[...] = _rms(y, ng_ref[...]).astype(BF16)


RET_LOG_G = [math.log(1.0 - 2.0 ** (-5.0 - h)) for h in range(RET_HEADS)]


def _ret_init(dmat_ref, state_ref):
    state_ref[...] = jnp.zeros_like(state_ref)
    r_i = lax.broadcasted_iota(jnp.int32, (CHUNK, CHUNK), 0)
    c_i = lax.broadcasted_iota(jnp.int32, (CHUNK, CHUNK), 1)
    diff = (r_i - c_i).astype(F32)
    for h in range(RET_HEADS):
        dmat_ref[h] = jnp.where(r_i >= c_i, jnp.exp(jnp.where(r_i >= c_i, diff, 0.0) * RET_LOG_G[h]), 0.0)


def _ret_body(u_ref, cos_ref, sin_ref, ng_ref, y_ref, dmat_ref, state_ref):
    log_g = RET_LOG_G
    qkw = RET_HEADS * RET_DK
    cos = cos_ref[...]
    sin = sin_ref[...]
    q = jnp.concatenate([_rope_lanes(u_ref[:, c * 128:(c + 1) * 128], cos, sin) for c in range(2)], axis=1)
    k = jnp.concatenate([_rope_lanes(u_ref[:, qkw + c * 128:qkw + (c + 1) * 128], cos, sin)
                         for c in range(2)], axis=1) * (RET_DK ** -0.5)
    idx = lax.broadcasted_iota(jnp.int32, (CHUNK, 1), 0).astype(F32)
    k_t = k.T
    idx_row = lax.broadcasted_iota(jnp.int32, (1, CHUNK), 1).astype(F32)
    heads = range(RET_HEADS)
    q_hs = [q[:, h * RET_DK:(h + 1) * RET_DK] for h in heads]
    k_hs = [k[:, h * RET_DK:(h + 1) * RET_DK] for h in heads]
    v_hs = [u_ref[:, 2 * qkw + h * RET_DV: 2 * qkw + (h + 1) * RET_DV].astype(BF16) for h in heads]
    s_prevs = [state_ref[h] for h in heads]
    yield
    scs = [_mm_nt(q_hs[h], k_hs[h]) for h in heads]
    y_cross = [_mm(q_hs[h] * jnp.exp((idx + 1.0) * log_g[h]), s_prevs[h]) for h in heads]
    yield
    s_adds = [_mm(k_t[h * RET_DK:(h + 1) * RET_DK, :] * jnp.exp((CHUNK - 1 - idx_row) * log_g[h]), v_hs[h])
              for h in heads]
    y_in = [_mm(scs[h] * dmat_ref[h], v_hs[h]) for h in heads]
    yield
    for h in heads:
        g_h = u_ref[:, 2 * qkw + RET_WIDTH + h * RET_DV: 2 * qkw + RET_WIDTH + (h + 1) * RET_DV]
        y = y_in[h] + y_cross[h]
        state_ref[h] = s_prevs[h] * math.exp(CHUNK * log_g[h]) + s_adds[h]
        mu = jnp.mean(y, axis=-1, keepdims=True)
        var = jnp.mean(jnp.square(y - mu), axis=-1, keepdims=True)
        yn = (y - mu) * lax.rsqrt(var + EPS) * ng_ref[:, h * RET_DV:(h + 1) * RET_DV]
        y_ref[:, h * RET_DV:(h + 1) * RET_DV] = (_silu(g_h) * yn).astype(BF16)


def _inv_unit_upper_many(mats, in_block, eye):
    d = [jnp.where(in_block, a, 0.0) for a in mats]
    f = [a - x for a, x in zip(mats, d)]
    d2 = [_mm(x, x) for x in d]
    yield
    d4 = [_mm(x, x) for x in d2]
    p = [_mm(eye + x, eye + y) for x, y in zip(d, d2)]
    yield
    td = [_mm(x, eye + y) for x, y in zip(p, d4)]
    yield
    g = [_mm(x, y) for x, y in zip(f, td)]
    yield
    g2 = [_mm(x, x) for x in g]
    tg = [_mm(x, eye + y) for x, y in zip(td, g)]
    yield
    g4 = [_mm(x, x) for x in g2]
    tg = [_mm(x, eye + y) for x, y in zip(tg, g2)]
    yield
    return [_mm(x, eye + y) for x, y in zip(tg, g4)]


def _rwkv_init(prev_ref, state_ref):
    prev_ref[...] = jnp.zeros_like(prev_ref)
    state_ref[...] = jnp.zeros_like(state_ref)


def _rwkv_body(u_ref, mu_ref, w0_ref, a0_ref, kk_ref, ka_ref, rk_ref, ln_ref, lora_ref, tri_ref,
               ones_ref, y_ref, prev_ref, state_ref, ot_ref):
    u = u_ref[...]
    rows = lax.broadcasted_iota(jnp.int32, (CHUNK, 1), 0)
    u_prev = jnp.where(rows == 0, prev_ref[7:8, :], pltpu.roll(u, 1, axis=0))
    prev_ref[...] = u[CHUNK - 8:, :]
    us = u + (u_prev - u) * mu_ref[...]
    w3 = RWKV_WIDTH
    r = us[:, :w3]
    k = us[:, w3:2 * w3]
    v = us[:, 2 * w3:3 * w3]
    lo = us[:, 3 * w3:]
    w = w0_ref[...] + _mm(jnp.tanh(lo), lora_ref[0])
    w = -_softplus(-w) - 0.5
    ld = -jnp.exp(w)
    a = _sigmoid(a0_ref[...] + _mm(lo, lora_ref[1]))
    g = _mm(_sigmoid(lo), lora_ref[2])
    ones = ones_ref[...]
    kk = k * kk_ref[...]
    k2 = k * (1.0 + (a - 1.0) * ka_ref[...])
    kk = kk / jnp.maximum(jnp.sqrt(_dot01_right(kk * kk, ones, pieces=2)), 1e-12)
    b = kk * a

    lc_full = _dot01_left(tri_ref[...], ld)
    v_t = v.T

    n = RWKV_SUB
    r_i = lax.broadcasted_iota(jnp.int32, (n, n), 0)
    c_i = lax.broadcasted_iota(jnp.int32, (n, n), 1)
    in_block = (r_i // RWKV_INV_BLOCK) == (c_i // RWKV_INV_BLOCK)
    eye = jnp.where(r_i == c_i, 1.0, 0.0)
    hd = RWKV_HEAD_DIM
    n_sub = CHUNK // n
    heads = range(RWKV_HEADS)
    hs = [slice(h * hd, (h + 1) * hd) for h in heads]

    sc = []
    for sub in range(n_sub):
        lo_r, hi_r = sub * n, (sub + 1) * n
        lc = lc_full[lo_r:hi_r, :]
        if sub > 0:
            lc = lc - lc_full[lo_r - 1:lo_r, :]
        lc_last = lc[n - 1:n, :]
        e_pos = jnp.exp(lc)
        e_neg = jnp.exp(-lc)
        e_prev = jnp.exp(lc - ld[lo_r:hi_r, :])
        e_end = jnp.exp(lc_last - lc)
        sc.append(dict(
            gam=jnp.exp(lc_last),
            a_t=(-kk[lo_r:hi_r, :] * e_prev).astype(BF16),
            r_t=(r[lo_r:hi_r, :] * e_pos).astype(BF16),
            b_t=(b[lo_r:hi_r, :] * e_neg).astype(BF16),
            k_t=(k2[lo_r:hi_r, :] * e_neg).astype(BF16),
            b_h=(b[lo_r:hi_r, :] * e_end).astype(BF16),
            k_h=(k2[lo_r:hi_r, :] * e_end).astype(BF16),
            v_t=v_t[:, lo_r:hi_r].astype(BF16),
        ))

    pairs = [(sub, h) for sub in range(n_sub) for h in heads]
    yield
    bk = [jnp.concatenate([sc[sub]["b_t"][:, hs[h]], sc[sub]["k_t"][:, hs[h]]], axis=0) for sub, h in pairs]
    ar2 = [jnp.concatenate([sc[sub]["a_t"][:, hs[h]], sc[sub]["r_t"][:, hs[h]]], axis=0) for sub, h in pairs]
    r2 = lax.broadcasted_iota(jnp.int32, (n, 2 * n), 0)
    c2 = lax.broadcasted_iota(jnp.int32, (n, 2 * n), 1)
    mask2 = r2 < jnp.where(c2 < n, c2, c2 - n + 1)
    quad = [_mm_nt(x, y) for x, y in zip(bk, ar2)]
    top = [jnp.where(mask2, x[:n, :], 0.0) for x in quad]
    bot = [jnp.where(mask2, x[n:, :], 0.0).astype(BF16) for x in quad]
    yield
    vprod = [_mm(sc[sub]["v_t"][hs[h], :], jnp.concatenate([y, sc[sub]["k_h"][:, hs[h]]], axis=1))
             for y, (sub, h) in zip(bot, pairs)]
    yield
    t_t = yield from _inv_unit_upper_many([x[:, :n] for x in top], in_block, eye)
    top = [x.astype(BF16) for x in top]

    state = [state_ref[h] for h in heads]
    for sub in range(n_sub):
        c = sc[sub]
        base = sub * RWKV_HEADS
        yield
        xr = [_mm_nt(state[h], ar2[base + h]) for h in heads]
        yield
        u_t = [_mm(xr[h][:, :n] + vprod[base + h][:, :n], t_t[base + h]) for h in heads]
        yield
        ub = [_mm(u_t[h], top[base + h]) for h in heads]
        su = [_mm(u_t[h], c["b_h"][:, hs[h]]) for h in heads]
        for h in heads:
            ot_ref[h * hd:(h + 1) * hd, sub * n:(sub + 1) * n] = (
                xr[h][:, n:] + ub[h][:, n:] + vprod[base + h][:, n:2 * n])
        state = [state[h] * c["gam"][:, hs[h]] + su[h] + vprod[base + h][:, 2 * n:] for h in heads]
    for h in heads:
        state_ref[h] = state[h]
    yield

    out = ot_ref[...].T
    inv_n = 1.0 / RWKV_HEAD_DIM
    mean = _dot01_right(out, ones, pieces=2) * inv_n
    cen = out - mean
    var = _dot01_right(cen * cen, ones, pieces=2) * inv_n
    out = cen * lax.rsqrt(var + RWKV_LN_EPS) * ln_ref[...]
    bonus = _dot01_right(r * k2 * rk_ref[...], ones, pieces=2) * v
    y_ref[...] = ((out + bonus) * g).astype(BF16)


N_SSD_IN, N_RET_IN, N_RWKV_IN = 9, 4, 11


def _mixers_kernel(*refs):
    n_in = N_SSD_IN + N_RET_IN + N_RWKV_IN
    ins, (y_ssd, y_ret, y_rwkv), scr = refs[:n_in], refs[n_in:n_in + 3], refs[n_in + 3:]
    ssd_in, ret_in, rwkv_in = ins[:N_SSD_IN], ins[N_SSD_IN:N_SSD_IN + N_RET_IN], ins[N_SSD_IN + N_RET_IN:]
    ssd_scr, ret_scr, rwkv_scr = scr[:2], scr[2:4], scr[4:]

    @pl.when(pl.program_id(0) == 0)
    def _():
        _ssd_init(*ssd_scr)
        _ret_init(*ret_scr)
        _rwkv_init(*rwkv_scr[:2])

    main = _rwkv_body(*rwkv_in, y_rwkv, *rwkv_scr)
    side = itertools.chain(_ssd_body(*ssd_in, y_ssd, *ssd_scr), _ret_body(*ret_in, y_ret, *ret_scr))
    for step, _ in enumerate(main):
        if step % 2 == 0:
            next(side, None)
    for _ in side:
        pass


def _mixers(u_ssd, ssd_params, u_ret, ret_params, u_rwkv, rwkv_params):
    lp = u_ssd.shape[0]
    full = lambda a: pl.BlockSpec(a.shape, lambda i: (0,) * a.ndim)
    rows = lambda a: pl.BlockSpec((CHUNK, a.shape[1]), lambda i: (i, 0))
    cos, sin, ng = ret_params
    args = [u_ssd, *ssd_params, u_ret, cos, sin, ng, u_rwkv, *rwkv_params]
    assert len(args) == N_SSD_IN + N_RET_IN + N_RWKV_IN
    in_specs = ([rows(u_ssd)] + [full(a) for a in ssd_params]
                + [rows(u_ret), rows(cos), rows(sin), full(ng)]
                + [rows(u_rwkv)] + [full(a) for a in rwkv_params])
    out = jax.ShapeDtypeStruct((lp, MIXER_WIDTH), BF16)
    ospec = pl.BlockSpec((CHUNK, MIXER_WIDTH), lambda i: (i, 0))
    return pl.pallas_call(
        _mixers_kernel,
        grid=(lp // CHUNK,),
        in_specs=in_specs,
        out_specs=[ospec, ospec, ospec],
        out_shape=[out, out, out],
        scratch_shapes=[pltpu.VMEM((8, SSD_CONV_CH), F32),
                        pltpu.VMEM((SSD_GROUPS, SSD_STATE, SSD_WIDTH // SSD_GROUPS), F32),
                        pltpu.VMEM((RET_HEADS, CHUNK, CHUNK), F32),
                        pltpu.VMEM((RET_HEADS, RET_DK, RET_DV), F32),
                        pltpu.VMEM((8, RWKV_IN), F32),
                        pltpu.VMEM((RWKV_HEADS, RWKV_HEAD_DIM, RWKV_HEAD_DIM), F32),
                        pltpu.VMEM((RWKV_WIDTH, CHUNK), F32)],
        compiler_params=_cparams("arbitrary"),
        name="mixers",
    )(*args)


def _constants():
    r = jnp.arange(CHUNK)
    tri = (r[:, None] >= r[None, :]).astype(BF16)
    lane = jnp.arange(RWKV_WIDTH)
    ones_bd = ((lane[:, None] // RWKV_HEAD_DIM) == (lane[None, :] // RWKV_HEAD_DIM)).astype(BF16)
    e8 = ((jnp.arange(128)[:, None] == (lane[None, :] // SSD_HEAD_DIM))).astype(BF16)
    return tri, ones_bd, e8


def kernel(x, meta_tokens, ffn1_norm, ffn1_w_gate, ffn1_w_up, ffn1_w_down, mix_norm, w_in, w_out, mla_q_norm, mla_w_q_up, mla_kv_norm, mla_w_kv_up, mla_qk_norm_q, mla_qk_norm_k, ssd_conv_w, ssd_conv_b, ssd_dt_bias, ssd_a_log, ssd_d, ssd_norm, ret_norm, rwkv_mu, rwkv_w0, rwkv_w2, rwkv_a0, rwkv_a2, rwkv_g2, rwkv_k_k, rwkv_k_a, rwkv_r_k, rwkv_ln, ffn2_norm, ffn2_w_gate, ffn2_w_up, ffn2_w_down):
    b, seq, d = x.shape
    assert b == 1 and d == D_MODEL and seq % CHUNK == 0
    nl = w_in.shape[0]
    lp = seq + CHUNK
    assert lp % TOKEN_TILE == 0

    hst = jnp.concatenate([jnp.zeros((PAD, d), x.dtype), meta_tokens.astype(x.dtype), x[0]], axis=0)

    wq =jnp.pad(mla_w_q_up.reshape(nl, MLA_Q_LORA, MLA_HEADS, MLA_QK),
                 ((0, 0), (0, 0), (0, 0), (0, MLA_QK_PAD - MLA_QK))
                 ).reshape(nl, MLA_Q_LORA, MLA_HEADS * MLA_QK_PAD).astype(BF16)
    wkv = mla_w_kv_up.reshape(nl, MLA_KV_LORA, MLA_HEADS, 2, MLA_NOPE).transpose(0, 1, 3, 2, 4
                              ).reshape(nl, MLA_KV_LORA, 2 * MLA_HEADS * MLA_NOPE).astype(BF16)
    gq = jnp.pad(mla_qk_norm_q, ((0, 0), (0, MLA_QK_PAD - MLA_QK)))[:, None, :]
    gk = jnp.pad(mla_qk_norm_k, ((0, 0), (0, MLA_QK_PAD - MLA_QK)))[:, None, :]
    dtb = jnp.pad(ssd_dt_bias, ((0, 0), (0, 128 - SSD_HEADS)))[:, None, :]
    alog = jnp.pad(ssd_a_log, ((0, 0), (0, 128 - SSD_HEADS)))[:, None, :]
    dsk = jnp.repeat(ssd_d, SSD_HEAD_DIM, axis=1)[:, None, :]
    lora = jnp.zeros((nl, 3, 128, RWKV_WIDTH), F32)
    lora = lora.at[:, 0, :RWKV_DECAY_LORA].set(rwkv_w2)
    lora = lora.at[:, 1, RWKV_DECAY_LORA:RWKV_DECAY_LORA + RWKV_A_LORA].set(rwkv_a2)
    lora = lora.at[:, 2, RWKV_DECAY_LORA + RWKV_A_LORA:].set(rwkv_g2)
    lora = lora.astype(BF16)

    tri, ones_bd, e8 = _constants()
    cos, sin, sinm = _rope_tables(lp)

    w_ffn = (ffn1_w_gate[0].astype(BF16), ffn1_w_up[0].astype(BF16), ffn1_w_down[0].astype(BF16))
    for l in range(nl):
        hst, w_ffn, (w_mla, w_ssd, w_ret, w_rwkv, w_o) = _ffn(
            hst, ffn1_norm[l][None, :], *w_ffn, nxt=(ffn2_w_gate, ffn2_w_up, ffn2_w_down, l),
            proj=(w_in, w_out, l))
        gmix = mix_norm[l][None, :]
        q, k, v, u_ssd = _mla_prep(hst, gmix, w_mla, mla_q_norm[l][None, :], wq, mla_kv_norm[l][None, :],
                                   wkv, gq[l], gk[l], cos, sinm, w_ssd, l)
        y_mla = _attention(q, k, v)
        u_ret = _inproj(hst, gmix, w_ret)
        u_rwkv = _inproj(hst, gmix, w_rwkv)
        y_ssd, y_ret, y_rwkv = _mixers(
            u_ssd, (ssd_conv_w[l], ssd_conv_b[l][None, :], dtb[l], alog[l], dsk[l], ssd_norm[l][None, :], tri, e8),
            u_ret, (cos, sin, ret_norm[l].reshape(1, RET_WIDTH)),
            u_rwkv, (rwkv_mu[l][None, :], rwkv_w0[l][None, :], rwkv_a0[l][None, :], rwkv_k_k[l][None, :],
                     rwkv_k_a[l][None, :], rwkv_r_k[l].reshape(1, RWKV_WIDTH), rwkv_ln[l].reshape(1, RWKV_WIDTH),
                     lora[l], tri, ones_bd))
        hst = _outproj(hst, (y_mla, y_ssd, y_ret, y_rwkv), w_o.reshape(N_MIXERS, MIXER_WIDTH, D_MODEL))
        nxt = (ffn1_w_gate, ffn1_w_up, ffn1_w_down, l + 1) if l + 1 < nl else None
        hst, w_ffn, _ = _ffn(hst, ffn2_norm[l][None, :], *w_ffn, nxt=nxt)
    return hst[CHUNK:][None]
```

```python
import functools
import itertools
import math

import jax
import jax.numpy as jnp
from jax import lax
from jax.experimental import pallas as pl
from jax.experimental.pallas import tpu as pltpu

F32 = jnp.float32
BF16 = jnp.bfloat16

D_MODEL = 2048
DEPTH = 4
N_META = 16
CHUNK = 128
PAD = CHUNK - N_META
D_FF = 5632
EPS = 1e-6
ROPE_THETA = 10000.0
NEG_INF = -1e30

MLA_HEADS = 4
MLA_NOPE = 128
MLA_ROPE = 64
MLA_QK = MLA_NOPE + MLA_ROPE
MLA_V = 128
MLA_Q_LORA = 384
MLA_KV_LORA = 128
MLA_QK_PAD = 256
MLA_V_AUG = MLA_V + 16
MLA_IN = MLA_Q_LORA + MLA_KV_LORA + MLA_ROPE
MLA_IN_PAD = 640

SSD_HEADS = 8
SSD_HEAD_DIM = 64
SSD_WIDTH = 512
SSD_GROUPS = 2
SSD_STATE = 128
SSD_CONV = 4
SSD_CONV_CH = SSD_WIDTH + 2 * SSD_GROUPS * SSD_STATE
SSD_IN = SSD_WIDTH + SSD_CONV_CH + SSD_HEADS
SSD_IN_PAD = SSD_WIDTH + SSD_CONV_CH + 128

RET_HEADS = 4
RET_DK = 64
RET_DV = 128
RET_WIDTH = 512
RET_IN = 2 * RET_HEADS * RET_DK + 2 * RET_WIDTH

RWKV_HEADS = 8
RWKV_HEAD_DIM = 64
RWKV_WIDTH = 512
RWKV_DECAY_LORA = 32
RWKV_A_LORA = 32
RWKV_GATE_LORA = 64
RWKV_LN_EPS = 64e-5
RWKV_IN = 3 * RWKV_WIDTH + RWKV_DECAY_LORA + RWKV_A_LORA + RWKV_GATE_LORA
RWKV_SUB = 64
RWKV_INV_BLOCK = 8

MIX_WIDTH = 2048
N_MIXERS = 4
MIXER_WIDTH = MIX_WIDTH // N_MIXERS

V7X_VMEM_LIMIT_BYTES = 56 * 1024 * 1024
TOKEN_TILE = 640
FF_TILE = 512
FFN_CONVERT_ROWS = 512
PROJ_CONVERT_ROWS = 32


def _cparams(*sem):
    return pltpu.CompilerParams(dimension_semantics=sem, vmem_limit_bytes=V7X_VMEM_LIMIT_BYTES)


def _sigmoid(x):
    return 1.0 / (1.0 + jnp.exp(-x))


def _silu(x):
    return x * _sigmoid(x)


def _softplus(x):
    return jnp.maximum(x, 0.0) + jnp.log(1.0 + jnp.exp(-jnp.abs(x)))


def _rms(x, g, eps=EPS):
    return x * lax.rsqrt(jnp.mean(x * x, axis=-1, keepdims=True) + eps) * g


def _mm(a, b):
    return jnp.dot(a.astype(BF16), b.astype(BF16), preferred_element_type=F32)


def _mm_nt(a, b):
    return lax.dot_general(a.astype(BF16), b.astype(BF16), (((1,), (1,)), ((), ())),
                           preferred_element_type=F32)


def _split3(x):
    x1 = x.astype(BF16)
    r1 = x - x1.astype(F32)
    x2 = r1.astype(BF16)
    x3 = (r1 - x2.astype(F32)).astype(BF16)
    return x1, x2, x3


def _dot01_right(x, m01, pieces=3):
    out = None
    for p in _split3(x)[:pieces]:
        d = jnp.dot(p, m01, preferred_element_type=F32)
        out = d if out is None else out + d
    return out


def _dot01_left(m01, x):
    p1, p2, p3 = _split3(x)
    return (jnp.dot(m01, p1, preferred_element_type=F32)
            + jnp.dot(m01, p2, preferred_element_type=F32)
            + jnp.dot(m01, p3, preferred_element_type=F32))


def _row_ids(rows, cols, base):
    return base + lax.broadcasted_iota(jnp.int32, (rows, cols), 0)


def _rope_table_kernel(inv_ref, cos_ref, sin_ref, sinm_ref):
    i = pl.program_id(0)
    pos = (_row_ids(CHUNK, 128, i * CHUNK) - PAD).astype(F32)
    lane = lax.broadcasted_iota(jnp.int32, (CHUNK, 128), 1)
    ang = pos * inv_ref[...]
    c = jnp.cos(ang)
    s = jnp.sin(ang)
    s = jnp.where((lane % 64) < 32, -s, s)
    cos_ref[...] = c
    sin_ref[...] = s
    sinm_ref[...] = jnp.where(lane < 64, s, 0.0)


def _rope_tables(lp):
    half = 32
    inv = ROPE_THETA ** (-jnp.arange(half, dtype=F32) / half)
    inv = jnp.tile(inv, 4)[None, :]
    out = jax.ShapeDtypeStruct((lp, 128), F32)
    spec = pl.BlockSpec((CHUNK, 128), lambda i: (i, 0))
    return pl.pallas_call(
        _rope_table_kernel,
        grid=(lp // CHUNK,),
        in_specs=[pl.BlockSpec((1, 128), lambda i: (0, 0))],
        out_specs=[spec, spec, spec],
        out_shape=[out, out, out],
        compiler_params=_cparams("parallel"),
        name="rope_tables",
    )(inv)


def _rope_lanes(x, cos, sin_signed):
    lane = lax.broadcasted_iota(jnp.int32, x.shape, 1)
    fwd = pltpu.roll(x, 32, axis=1)
    bwd = pltpu.roll(x, 96, axis=1)
    rot = jnp.where((lane % 64) < 32, bwd, fwd)
    return x * cos + rot * sin_signed


def _ffn_kernel(*refs, convert_next, convert_proj):
    refs = list(refs)
    x_ref, g_ref, wg_ref, wu_ref, wd_ref = refs[:5]
    n_in = 5 + (3 if convert_next else 0) + (2 if convert_proj else 0)
    ins, outs, xn_ref = refs[5:n_in], refs[n_in:-1], refs[-1]
    o_ref = outs.pop(0)
    if convert_next:
        for src, dst in zip(ins[:3], outs[:3]):
            dst[...] = src[...].astype(BF16)
        ins, outs = ins[3:], outs[3:]
    if convert_proj:
        win_ref, wout_ref = ins
        wmla_ref, wssd_ref, wret_ref, wrwkv_ref, wo_ref = outs
        o0 = MLA_IN
        o1 = o0 + SSD_IN
        o2 = o1 + RET_IN
        rows = win_ref.shape[0]
        wmla_ref[:, :MLA_IN] = win_ref[:, :o0].astype(BF16)
        wmla_ref[:, MLA_IN:] = jnp.zeros((rows, MLA_IN_PAD - MLA_IN), BF16)
        wssd_ref[:, :SSD_IN] = win_ref[:, o0:o1].astype(BF16)
        wssd_ref[:, SSD_IN:] = jnp.zeros((rows, SSD_IN_PAD - SSD_IN), BF16)
        wret_ref[...] = win_ref[:, o1:o2].astype(BF16)
        wrwkv_ref[...] = win_ref[:, o2:].astype(BF16)
        wo_ref[...] = wout_ref[...].astype(BF16)
    j = pl.program_id(1)

    @pl.when(j == 0)
    def _():
        xn_ref[...] = _rms(x_ref[...], g_ref[...]).astype(BF16)
        o_ref[...] = jnp.zeros_like(o_ref)

    xn = xn_ref[...]
    a = jnp.dot(xn, wg_ref[...], preferred_element_type=F32)
    b = jnp.dot(xn, wu_ref[...], preferred_element_type=F32)
    mid = (_silu(a) * b).astype(BF16)
    o_ref[...] += jnp.dot(mid, wd_ref[...], preferred_element_type=F32)

    @pl.when(j == pl.num_programs(1) - 1)
    def _():
        o_ref[...] = x_ref[...] + 0.5 * o_ref[...]


def _ffn(h, g, wgb, wub, wdb, nxt=None, proj=None):
    lp = h.shape[0]
    tm, tf = TOKEN_TILE, FF_TILE
    n_j = D_FF // tf
    n_c = D_MODEL // FFN_CONVERT_ROWS if nxt is not None else 0
    in_specs = [
        pl.BlockSpec((tm, D_MODEL), lambda i, j: (i, 0)),
        pl.BlockSpec((1, D_MODEL), lambda i, j: (0, 0)),
        pl.BlockSpec((D_MODEL, tf), lambda i, j: (0, j)),
        pl.BlockSpec((D_MODEL, tf), lambda i, j: (0, j)),
        pl.BlockSpec((tf, D_MODEL), lambda i, j: (j, 0)),
    ]
    out_specs = [pl.BlockSpec((tm, D_MODEL), lambda i, j: (i, 0))]
    out_shape = [jax.ShapeDtypeStruct((lp, D_MODEL), F32)]
    args = [h, g, wgb, wub, wdb]
    if nxt is not None:
        wg, wu, wd, l = nxt
        cr = FFN_CONVERT_ROWS
        assert lp // tm >= n_c
        ci = lambda i: jnp.minimum(i, n_c - 1)
        cj = lambda i, j: jnp.where(i < n_c, j, n_j - 1)
        in_specs += [
            pl.BlockSpec((None, cr, tf), lambda i, j: (l, ci(i), cj(i, j))),
            pl.BlockSpec((None, cr, tf), lambda i, j: (l, ci(i), cj(i, j))),
            pl.BlockSpec((None, tf, cr), lambda i, j: (l, cj(i, j), ci(i))),
        ]
        out_specs += [
            pl.BlockSpec((cr, tf), lambda i, j: (ci(i), cj(i, j))),
            pl.BlockSpec((cr, tf), lambda i, j: (ci(i), cj(i, j))),
            pl.BlockSpec((tf, cr), lambda i, j: (cj(i, j), ci(i))),
        ]
        out_shape += [
            jax.ShapeDtypeStruct((D_MODEL, D_FF), BF16),
            jax.ShapeDtypeStruct((D_MODEL, D_FF), BF16),
            jax.ShapeDtypeStruct((D_FF, D_MODEL), BF16),
        ]
        args += [wg, wu, wd]
    if proj is not None:
        w_in, w_out, lpj = proj
        pr = PROJ_CONVERT_ROWS
        n_p = D_MODEL // pr
        assert (lp // tm) * n_j >= n_p
        pidx = lambda i, j: jnp.minimum(i * n_j + j, n_p - 1)
        in_specs += [
            pl.BlockSpec((None, pr, w_in.shape[2]), lambda i, j: (lpj, pidx(i, j), 0)),
            pl.BlockSpec((None, pr, D_MODEL), lambda i, j: (lpj, pidx(i, j), 0)),
        ]
        widths = (MLA_IN_PAD, SSD_IN_PAD, RET_IN, RWKV_IN, D_MODEL)
        out_specs += [pl.BlockSpec((pr, n), lambda i, j: (pidx(i, j), 0)) for n in widths]
        out_shape += [jax.ShapeDtypeStruct((D_MODEL, n), BF16) for n in widths]
        args += [w_in, w_out]
    outs = pl.pallas_call(
        functools.partial(_ffn_kernel, convert_next=nxt is not None, convert_proj=proj is not None),
        grid=(lp // tm, n_j),
        in_specs=in_specs,
        out_specs=out_specs,
        out_shape=out_shape,
        scratch_shapes=[pltpu.VMEM((tm, D_MODEL), BF16)],
        compiler_params=_cparams("arbitrary", "arbitrary"),
        name="ffn",
    )(*args)
    n_next = 3 if nxt is not None else 0
    return (outs[0], tuple(outs[1:1 + n_next]) if nxt is not None else None,
            tuple(outs[1 + n_next:]) if proj is not None else None)


def _inproj_kernel(x_ref, g_ref, w_ref, o_ref):
    i = pl.program_id(0)
    tm = x_ref.shape[0]
    xn = _rms(x_ref[...], g_ref[...]).astype(BF16)
    u = jnp.dot(xn, w_ref[...], preferred_element_type=F32)
    row = _row_ids(tm, 1, i * tm)
    o_ref[...] = jnp.where(row >= PAD, u, 0.0)


def _inproj(h, g, w):
    lp = h.shape[0]
    n = w.shape[1]
    tm = TOKEN_TILE
    return pl.pallas_call(
        _inproj_kernel,
        grid=(lp // tm,),
        in_specs=[
            pl.BlockSpec((tm, D_MODEL), lambda i: (i, 0)),
            pl.BlockSpec((1, D_MODEL), lambda i: (0, 0)),
            pl.BlockSpec((D_MODEL, n), lambda i: (0, 0)),
        ],
        out_specs=pl.BlockSpec((tm, n), lambda i: (i, 0)),
        out_shape=jax.ShapeDtypeStruct((lp, n), F32),
        compiler_params=_cparams("parallel"),
        name="inproj",
    )(h, g, w)


def _outproj_kernel(h_ref, y0_ref, y1_ref, y2_ref, y3_ref, w_ref, o_ref):
    acc = h_ref[...]
    for m, y_ref in enumerate((y0_ref, y1_ref, y2_ref, y3_ref)):
        acc = acc + jnp.dot(y_ref[...], w_ref[m], preferred_element_type=F32)
    o_ref[...] = acc


def _outproj(h, ys, w):
    lp = h.shape[0]
    tm = TOKEN_TILE
    yspec = pl.BlockSpec((tm, MIXER_WIDTH), lambda i: (i, 0))
    return pl.pallas_call(
        _outproj_kernel,
        grid=(lp // tm,),
        in_specs=[pl.BlockSpec((tm, D_MODEL), lambda i: (i, 0)), yspec, yspec, yspec, yspec,
                  pl.BlockSpec((N_MIXERS, MIXER_WIDTH, D_MODEL), lambda i: (0, 0, 0))],
        out_specs=pl.BlockSpec((tm, D_MODEL), lambda i: (i, 0)),
        out_shape=jax.ShapeDtypeStruct((lp, D_MODEL), F32),
        compiler_params=_cparams("parallel"),
        name="outproj",
    )(h, *ys, w)


def _mla_prep_kernel(x_ref, g_ref, win_ref, qn_ref, wq_ref, kvn_ref, wkv_ref, gq_ref, gk_ref,
                     cos_ref, sinm_ref, wssd_ref, q_ref, k_ref, v_ref, ussd_ref):
    i = pl.program_id(0)
    tm = x_ref.shape[0]
    xn = _rms(x_ref[...], g_ref[...]).astype(BF16)
    u = jnp.dot(xn, win_ref[...], preferred_element_type=F32)
    row = _row_ids(tm, 1, i * tm)
    u = jnp.where(row >= PAD, u, 0.0)
    cq = u[:, :MLA_Q_LORA]
    ckv = u[:, MLA_Q_LORA:MLA_Q_LORA + MLA_KV_LORA]
    kpe = u[:, MLA_Q_LORA + MLA_KV_LORA:]
    q_all = _mm(_rms(cq, qn_ref[...]), wq_ref[...])
    kv_all = _mm(_rms(ckv, kvn_ref[...]), wkv_ref[...])
    ussd_ref[...] = jnp.where(row >= PAD, jnp.dot(xn, wssd_ref[...], preferred_element_type=F32), 0.0)
    cos = cos_ref[...]
    sinm = sinm_ref[...]
    gq = gq_ref[...]
    gk = gk_ref[...]
    scale = MLA_QK ** -0.5
    for h in range(MLA_HEADS):
        qh = q_all[:, h * MLA_QK_PAD:(h + 1) * MLA_QK_PAD]
        ss = jnp.sum(qh * qh, axis=-1, keepdims=True) * (1.0 / MLA_QK)
        qh = qh * lax.rsqrt(ss + EPS) * gq
        q_rot = _rope_lanes(qh[:, MLA_NOPE:], cos, sinm)
        q_ref[h, 0, :MLA_NOPE, :] = (qh[:, :MLA_NOPE] * scale).T.astype(BF16)
        q_ref[h, 0, MLA_NOPE:, :] = (q_rot * scale).T.astype(BF16)

        kn = kv_all[:, h * MLA_NOPE:(h + 1) * MLA_NOPE]
        ss = (jnp.sum(kn * kn, axis=-1, keepdims=True)
              + jnp.sum(kpe * kpe, axis=-1, keepdims=True)) * (1.0 / MLA_QK)
        rs = lax.rsqrt(ss + EPS)
        k_rot = _rope_lanes(kpe * rs * gk[:, MLA_NOPE:], cos, sinm)
        k_ref[h, :, :MLA_NOPE] = (kn * rs * gk[:, :MLA_NOPE]).astype(BF16)
        k_ref[h, :, MLA_NOPE:] = k_rot.astype(BF16)
        v_ref[h, 0, :MLA_V, :] = kv_all[:, 4 * MLA_NOPE + h * MLA_V: 4 * MLA_NOPE + (h + 1) * MLA_V].T.astype(BF16)
        tail = lax.broadcasted_iota(jnp.int32, (MLA_V_AUG - MLA_V, tm), 0)
        v_ref[h, 0, MLA_V:, :] = jnp.where(tail == 0, 1.0, 0.0).astype(BF16)


def _mla_prep(h, g, win, qn, wq, kvn, wkv, gq, gk, cos, sinm, wssd, l):
    lp = h.shape[0]
    tm = TOKEN_TILE
    full = lambda shape: pl.BlockSpec(shape, lambda i: (0,) * len(shape))
    layer = lambda shape: pl.BlockSpec((None,) + shape, lambda i: (l,) + (0,) * len(shape))
    return pl.pallas_call(
        _mla_prep_kernel,
        grid=(lp // tm,),
        in_specs=[
            pl.BlockSpec((tm, D_MODEL), lambda i: (i, 0)),
            full((1, D_MODEL)), full((D_MODEL, MLA_IN_PAD)),
            full((1, MLA_Q_LORA)), layer((MLA_Q_LORA, MLA_HEADS * MLA_QK_PAD)),
            full((1, MLA_KV_LORA)), layer((MLA_KV_LORA, 2 * MLA_HEADS * MLA_NOPE)),
            full((1, MLA_QK_PAD)), full((1, MLA_QK_PAD)),
            pl.BlockSpec((tm, 128), lambda i: (i, 0)),
            pl.BlockSpec((tm, 128), lambda i: (i, 0)),
            full((D_MODEL, SSD_IN_PAD)),
        ],
        out_specs=[
            pl.BlockSpec((MLA_HEADS, 1, MLA_QK_PAD, tm), lambda i: (0, i, 0, 0)),
            pl.BlockSpec((MLA_HEADS, tm, MLA_QK_PAD), lambda i: (0, i, 0)),
            pl.BlockSpec((MLA_HEADS, 1, MLA_V_AUG, tm), lambda i: (0, i, 0, 0)),
            pl.BlockSpec((tm, SSD_IN_PAD), lambda i: (i, 0)),
        ],
        out_shape=[
            jax.ShapeDtypeStruct((MLA_HEADS, lp // tm, MLA_QK_PAD, tm), BF16),
            jax.ShapeDtypeStruct((MLA_HEADS, lp, MLA_QK_PAD), BF16),
            jax.ShapeDtypeStruct((MLA_HEADS, lp // tm, MLA_V_AUG, tm), BF16),
            jax.ShapeDtypeStruct((lp, SSD_IN_PAD), F32),
        ],
        compiler_params=_cparams("parallel"),
        name="mla_prep",
    )(h, g, win, qn, wq, kvn, wkv, gq, gk, cos, sinm, wssd)


def _attn_kernel(q_ref, k_ref, vt_ref, o_ref):
    i = pl.program_id(1)
    tq = q_ref.shape[3]
    qt = q_ref[0, 0]
    key0 = lax.broadcasted_iota(jnp.int32, (tq, tq), 0)
    qry = i * tq + lax.broadcasted_iota(jnp.int32, (tq, tq), 1)

    def scores(j, masked):
        start = pl.multiple_of(j * tq, tq)
        k = k_ref[0, pl.ds(start, tq), :]
        s = jnp.dot(k, qt, preferred_element_type=F32)
        if masked:
            key = key0 + j * tq
            s = jnp.where(key <= qry, jnp.where(key >= PAD, s, NEG_INF), NEG_INF)
        return s

    def update(j, s, carry):
        m, acc = carry
        m_new = jnp.maximum(m, jnp.max(s, axis=0, keepdims=True))
        alpha = jnp.exp(m - m_new)
        p = jnp.exp((s - m_new).astype(BF16))
        acc = alpha * acc + jnp.dot(vt_ref[0, j], p, preferred_element_type=F32)
        return m_new, acc

    init = (jnp.full((1, tq), NEG_INF, F32), jnp.zeros((MLA_V_AUG, tq), F32))
    s_first = scores(0, True)

    def body(j, c):
        s_prev, carry = c
        s_new = scores(j, False)
        return s_new, update(j - 1, s_prev, carry)

    s_last, carry = lax.fori_loop(1, i, body, (s_first, init))
    j_last = jnp.maximum(i - 1, 0)
    m, acc = lax.cond(i > 0,
                      lambda c: update(i, scores(i, True), update(j_last, s_last, c)),
                      lambda c: update(j_last, s_last, c), carry)
    o_ref[...] = (acc[:MLA_V] / acc[MLA_V:MLA_V + 1]).T.astype(BF16)


def _attention(q, k, v):
    lp = k.shape[1]
    tq = TOKEN_TILE
    return pl.pallas_call(
        _attn_kernel,
        grid=(MLA_HEADS, lp // tq),
        in_specs=[
            pl.BlockSpec((1, 1, MLA_QK_PAD, tq), lambda h, i: (h, i, 0, 0)),
            pl.BlockSpec((1, lp, MLA_QK_PAD), lambda h, i: (h, 0, 0)),
            pl.BlockSpec((1, lp // tq, MLA_V_AUG, tq), lambda h, i: (h, 0, 0, 0)),
        ],
        out_specs=pl.BlockSpec((tq, MLA_V), lambda h, i: (i, h)),
        out_shape=jax.ShapeDtypeStruct((lp, MLA_HEADS * MLA_V), BF16),
        compiler_params=_cparams("parallel", "arbitrary"),
        name="mla_attention",
    )(q, k, v)


def _shift_rows(x, carry, s):
    rolled = pltpu.roll(x, s, axis=0)
    head = pltpu.roll(carry, s, axis=0)
    r8 = lax.broadcasted_iota(jnp.int32, (8, 1), 0)
    return jnp.concatenate([jnp.where(r8 < s, head, rolled[:8]), rolled[8:]], axis=0)


def _ssd_init(carry_ref, state_ref):
    carry_ref[...] = jnp.zeros_like(carry_ref)
    state_ref[...] = jnp.zeros_like(state_ref)


def _ssd_body(u_ref, cw_ref, cb_ref, dtb_ref, alog_ref, dsk_ref, ng_ref, tri_ref, e8_ref,
              y_ref, carry_ref, state_ref):
    i = pl.program_id(0)
    z = u_ref[:, :SSD_WIDTH]
    xbc = u_ref[:, SSD_WIDTH:SSD_WIDTH + SSD_CONV_CH]
    dt_raw = u_ref[:, SSD_WIDTH + SSD_CONV_CH:]
    carry = carry_ref[...]
    cw = cw_ref[...]
    conv = xbc * cw[3:4, :]
    for s in (1, 2, 3):
        conv = conv + _shift_rows(xbc, carry, s) * cw[3 - s:4 - s, :]
    carry_ref[...] = xbc[CHUNK - 8:, :]
    xbc = _silu(conv + cb_ref[...])
    xs = xbc[:, :SSD_WIDTH]
    bm = xbc[:, SSD_WIDTH:SSD_WIDTH + SSD_GROUPS * SSD_STATE]
    cm = xbc[:, SSD_WIDTH + SSD_GROUPS * SSD_STATE:]

    row = _row_ids(CHUNK, 1, i * CHUNK)
    dt = _softplus(dt_raw + dtb_ref[...]) * jnp.where(row >= PAD, 1.0, 0.0)
    la = dt * (-jnp.exp(alog_ref[...]))
    tri = tri_ref[...]
    e8 = e8_ref[...]
    cs = _dot01_left(tri, la)
    cs_e = _dot01_right(cs, e8, pieces=2)
    dt_e = _dot01_right(dt, e8, pieces=2)
    cs_t = cs.T
    cs_last_e = cs_e[CHUNK - 1:CHUNK, :]
    x = xs * dt_e
    xd = x * jnp.exp(cs_last_e - cs_e)
    ecs = jnp.exp(cs_e)
    dec = jnp.exp(cs_last_e)

    r_i = lax.broadcasted_iota(jnp.int32, (CHUNK, CHUNK), 0)
    c_i = lax.broadcasted_iota(jnp.int32, (CHUNK, CHUNK), 1)
    causal = r_i >= c_i
    per = SSD_HEADS // SSD_GROUPS
    gw = per * SSD_HEAD_DIM
    groups = range(SSD_GROUPS)
    b_gs = [bm[:, g * SSD_STATE:(g + 1) * SSD_STATE] for g in groups]
    c_gs = [cm[:, g * SSD_STATE:(g + 1) * SSD_STATE] for g in groups]
    s_prev = [state_ref[g] for g in groups]
    yield
    scores = [_mm_nt(c_gs[g], b_gs[g]) for g in groups]
    y_off = [_mm(c_gs[g], s_prev[g]) for g in groups]
    s_add = [_mm(b_gs[g].T, xd[:, g * gw:(g + 1) * gw]) for g in groups]
    yield
    lmats = []
    for h in range(SSD_HEADS):
        seg = cs[:, h:h + 1] - cs_t[h:h + 1, :]
        lmats.append(jnp.where(causal, jnp.exp(jnp.where(causal, seg, 0.0)), 0.0))
    y_diag = [_mm(scores[h // per] * lmats[h], x[:, h * SSD_HEAD_DIM:(h + 1) * SSD_HEAD_DIM])
              for h in range(SSD_HEADS)]
    yield
    for g in groups:
        state_ref[g] = s_prev[g] * dec[:, g * gw:(g + 1) * gw] + s_add[g]
    y = (jnp.concatenate(y_diag, axis=1) + jnp.concatenate(y_off, axis=1) * ecs) + dsk_ref[...] * xs
    y = y * _silu(z)
    y_ref[...] = _rms(y, ng_ref[...]).astype(BF16)


RET_LOG_G = [math.log(1.0 - 2.0 ** (-5.0 - h)) for h in range(RET_HEADS)]


def _ret_init(dmat_ref, state_ref):
    state_ref[...] = jnp.zeros_like(state_ref)
    r_i = lax.broadcasted_iota(jnp.int32, (CHUNK, CHUNK), 0)
    c_i = lax.broadcasted_iota(jnp.int32, (CHUNK, CHUNK), 1)
    diff = (r_i - c_i).astype(F32)
    for h in range(RET_HEADS):
        dmat_ref[h] = jnp.where(r_i >= c_i, jnp.exp(jnp.where(r_i >= c_i, diff, 0.0) * RET_LOG_G[h]), 0.0)


def _ret_body(u_ref, cos_ref, sin_ref, ng_ref, y_ref, dmat_ref, state_ref):
    log_g = RET_LOG_G
    qkw = RET_HEADS * RET_DK
    cos = cos_ref[...]
    sin = sin_ref[...]
    q = jnp.concatenate([_rope_lanes(u_ref[:, c * 128:(c + 1) * 128], cos, sin) for c in range(2)], axis=1)
    k = jnp.concatenate([_rope_lanes(u_ref[:, qkw + c * 128:qkw + (c + 1) * 128], cos, sin)
                         for c in range(2)], axis=1) * (RET_DK ** -0.5)
    idx = lax.broadcasted_iota(jnp.int32, (CHUNK, 1), 0).astype(F32)
    k_t = k.T
    idx_row = lax.broadcasted_iota(jnp.int32, (1, CHUNK), 1).astype(F32)
    heads = range(RET_HEADS)
    q_hs = [q[:, h * RET_DK:(h + 1) * RET_DK] for h in heads]
    k_hs = [k[:, h * RET_DK:(h + 1) * RET_DK] for h in heads]
    v_hs = [u_ref[:, 2 * qkw + h * RET_DV: 2 * qkw + (h + 1) * RET_DV].astype(BF16) for h in heads]
    s_prevs = [state_ref[h] for h in heads]
    yield
    scs = [_mm_nt(q_hs[h], k_hs[h]) for h in heads]
    y_cross = [_mm(q_hs[h] * jnp.exp((idx + 1.0) * log_g[h]), s_prevs[h]) for h in heads]
    yield
    s_adds = [_mm(k_t[h * RET_DK:(h + 1) * RET_DK, :] * jnp.exp((CHUNK - 1 - idx_row) * log_g[h]), v_hs[h])
              for h in heads]
    y_in = [_mm(scs[h] * dmat_ref[h], v_hs[h]) for h in heads]
    yield
    for h in heads:
        g_h = u_ref[:, 2 * qkw + RET_WIDTH + h * RET_DV: 2 * qkw + RET_WIDTH + (h + 1) * RET_DV]
        y = y_in[h] + y_cross[h]
        state_ref[h] = s_prevs[h] * math.exp(CHUNK * log_g[h]) + s_adds[h]
        mu = jnp.mean(y, axis=-1, keepdims=True)
        var = jnp.mean(jnp.square(y - mu), axis=-1, keepdims=True)
        yn = (y - mu) * lax.rsqrt(var + EPS) * ng_ref[:, h * RET_DV:(h + 1) * RET_DV]
        y_ref[:, h * RET_DV:(h + 1) * RET_DV] = (_silu(g_h) * yn).astype(BF16)


def _inv_unit_upper_many(mats, in_block, eye):
    d = [jnp.where(in_block, a, 0.0) for a in mats]
    f = [a - x for a, x in zip(mats, d)]
    d2 = [_mm(x, x) for x in d]
    yield
    d4 = [_mm(x, x) for x in d2]
    p = [_mm(eye + x, eye + y) for x, y in zip(d, d2)]
    yield
    td = [_mm(x, eye + y) for x, y in zip(p, d4)]
    yield
    g = [_mm(x, y) for x, y in zip(f, td)]
    yield
    g2 = [_mm(x, x) for x in g]
    tg = [_mm(x, eye + y) for x, y in zip(td, g)]
    yield
    g4 = [_mm(x, x) for x in g2]
    tg = [_mm(x, eye + y) for x, y in zip(tg, g2)]
    yield
    return [_mm(x, eye + y) for x, y in zip(tg, g4)]


def _rwkv_init(prev_ref, state_ref):
    prev_ref[...] = jnp.zeros_like(prev_ref)
    state_ref[...] = jnp.zeros_like(state_ref)


def _rwkv_body(u_ref, mu_ref, w0_ref, a0_ref, kk_ref, ka_ref, rk_ref, ln_ref, lora_ref, tri_ref,
               ones_ref, y_ref, prev_ref, state_ref, ot_ref):
    u = u_ref[...]
    rows = lax.broadcasted_iota(jnp.int32, (CHUNK, 1), 0)
    u_prev = jnp.where(rows == 0, prev_ref[7:8, :], pltpu.roll(u, 1, axis=0))
    prev_ref[...] = u[CHUNK - 8:, :]
    us = u + (u_prev - u) * mu_ref[...]
    w3 = RWKV_WIDTH
    r = us[:, :w3]
    k = us[:, w3:2 * w3]
    v = us[:, 2 * w3:3 * w3]
    lo = us[:, 3 * w3:]
    w = w0_ref[...] + _mm(jnp.tanh(lo), lora_ref[0])
    w = -_softplus(-w) - 0.5
    ld = -jnp.exp(w)
    a = _sigmoid(a0_ref[...] + _mm(lo, lora_ref[1]))
    g = _mm(_sigmoid(lo), lora_ref[2])
    ones = ones_ref[...]
    kk = k * kk_ref[...]
    k2 = k * (1.0 + (a - 1.0) * ka_ref[...])
    kk = kk / jnp.maximum(jnp.sqrt(_dot01_right(kk * kk, ones, pieces=2)), 1e-12)
    b = kk * a

    lc_full = _dot01_left(tri_ref[...], ld)
    v_t = v.T

    n = RWKV_SUB
    r_i = lax.broadcasted_iota(jnp.int32, (n, n), 0)
    c_i = lax.broadcasted_iota(jnp.int32, (n, n), 1)
    in_block = (r_i // RWKV_INV_BLOCK) == (c_i // RWKV_INV_BLOCK)
    eye = jnp.where(r_i == c_i, 1.0, 0.0)
    hd = RWKV_HEAD_DIM
    n_sub = CHUNK // n
    heads = range(RWKV_HEADS)
    hs = [slice(h * hd, (h + 1) * hd) for h in heads]

    sc = []
    for sub in range(n_sub):
        lo_r, hi_r = sub * n, (sub + 1) * n
        lc = lc_full[lo_r:hi_r, :]
        if sub > 0:
            lc = lc - lc_full[lo_r - 1:lo_r, :]
        lc_last = lc[n - 1:n, :]
        e_pos = jnp.exp(lc)
        e_neg = jnp.exp(-lc)
        e_prev = jnp.exp(lc - ld[lo_r:hi_r, :])
        e_end = jnp.exp(lc_last - lc)
        sc.append(dict(
            gam=jnp.exp(lc_last),
            a_t=(-kk[lo_r:hi_r, :] * e_prev).astype(BF16),
            r_t=(r[lo_r:hi_r, :] * e_pos).astype(BF16),
            b_t=(b[lo_r:hi_r, :] * e_neg).astype(BF16),
            k_t=(k2[lo_r:hi_r, :] * e_neg).astype(BF16),
            b_h=(b[lo_r:hi_r, :] * e_end).astype(BF16),
            k_h=(k2[lo_r:hi_r, :] * e_end).astype(BF16),
            v_t=v_t[:, lo_r:hi_r].astype(BF16),
        ))

    pairs = [(sub, h) for sub in range(n_sub) for h in heads]
    yield
    bk = [jnp.concatenate([sc[sub]["b_t"][:, hs[h]], sc[sub]["k_t"][:, hs[h]]], axis=0) for sub, h in pairs]
    ar2 = [jnp.concatenate([sc[sub]["a_t"][:, hs[h]], sc[sub]["r_t"][:, hs[h]]], axis=0) for sub, h in pairs]
    r2 = lax.broadcasted_iota(jnp.int32, (n, 2 * n), 0)
    c2 = lax.broadcasted_iota(jnp.int32, (n, 2 * n), 1)
    mask2 = r2 < jnp.where(c2 < n, c2, c2 - n + 1)
    quad = [_mm_nt(x, y) for x, y in zip(bk, ar2)]
    top = [jnp.where(mask2, x[:n, :], 0.0) for x in quad]
    bot = [jnp.where(mask2, x[n:, :], 0.0).astype(BF16) for x in quad]
    yield
    vprod = [_mm(sc[sub]["v_t"][hs[h], :], jnp.concatenate([y, sc[sub]["k_h"][:, hs[h]]], axis=1))
             for y, (sub, h) in zip(bot, pairs)]
    yield
    t_t = yield from _inv_unit_upper_many([x[:, :n] for x in top], in_block, eye)
    top = [x.astype(BF16) for x in top]

    state = [state_ref[h] for h in heads]
    for sub in range(n_sub):
        c = sc[sub]
        base = sub * RWKV_HEADS
        yield
        xr = [_mm_nt(state[h], ar2[base + h]) for h in heads]
        yield
        u_t = [_mm(xr[h][:, :n] + vprod[base + h][:, :n], t_t[base + h]) for h in heads]
        yield
        ub = [_mm(u_t[h], top[base + h]) for h in heads]
        su = [_mm(u_t[h], c["b_h"][:, hs[h]]) for h in heads]
        for h in heads:
            ot_ref[h * hd:(h + 1) * hd, sub * n:(sub + 1) * n] = (
                xr[h][:, n:] + ub[h][:, n:] + vprod[base + h][:, n:2 * n])
        state = [state[h] * c["gam"][:, hs[h]] + su[h] + vprod[base + h][:, 2 * n:] for h in heads]
    for h in heads:
        state_ref[h] = state[h]
    yield

    out = ot_ref[...].T
    inv_n = 1.0 / RWKV_HEAD_DIM
    mean = _dot01_right(out, ones, pieces=2) * inv_n
    cen = out - mean
    var = _dot01_right(cen * cen, ones, pieces=2) * inv_n
    out = cen * lax.rsqrt(var + RWKV_LN_EPS) * ln_ref[...]
    bonus = _dot01_right(r * k2 * rk_ref[...], ones, pieces=2) * v
    y_ref[...] = ((out + bonus) * g).astype(BF16)


N_SSD_IN, N_RET_IN, N_RWKV_IN = 9, 4, 11


def _mixers_kernel(*refs):
    n_in = N_SSD_IN + N_RET_IN + N_RWKV_IN
    ins, (y_ssd, y_ret, y_rwkv), scr = refs[:n_in], refs[n_in:n_in + 3], refs[n_in + 3:]
    ssd_in, ret_in, rwkv_in = ins[:N_SSD_IN], ins[N_SSD_IN:N_SSD_IN + N_RET_IN], ins[N_SSD_IN + N_RET_IN:]
    ssd_scr, ret_scr, rwkv_scr = scr[:2], scr[2:4], scr[4:]

    @pl.when(pl.program_id(0) == 0)
    def _():
        _ssd_init(*ssd_scr)
        _ret_init(*ret_scr)
        _rwkv_init(*rwkv_scr[:2])

    main = _rwkv_body(*rwkv_in, y_rwkv, *rwkv_scr)
    side = itertools.chain(_ssd_body(*ssd_in, y_ssd, *ssd_scr), _ret_body(*ret_in, y_ret, *ret_scr))
    for step, _ in enumerate(main):
        if step % 2 == 0:
            next(side, None)
    for _ in side:
        pass


def _mixers(u_ssd, ssd_params, u_ret, ret_params, u_rwkv, rwkv_params):
    lp = u_ssd.shape[0]
    full = lambda a: pl.BlockSpec(a.shape, lambda i: (0,) * a.ndim)
    rows = lambda a: pl.BlockSpec((CHUNK, a.shape[1]), lambda i: (i, 0))
    cos, sin, ng = ret_params
    args = [u_ssd, *ssd_params, u_ret, cos, sin, ng, u_rwkv, *rwkv_params]
    assert len(args) == N_SSD_IN + N_RET_IN + N_RWKV_IN
    in_specs = ([rows(u_ssd)] + [full(a) for a in ssd_params]
                + [rows(u_ret), rows(cos), rows(sin), full(ng)]
                + [rows(u_rwkv)] + [full(a) for a in rwkv_params])
    out = jax.ShapeDtypeStruct((lp, MIXER_WIDTH), BF16)
    ospec = pl.BlockSpec((CHUNK, MIXER_WIDTH), lambda i: (i, 0))
    return pl.pallas_call(
        _mixers_kernel,
        grid=(lp // CHUNK,),
        in_specs=in_specs,
        out_specs=[ospec, ospec, ospec],
        out_shape=[out, out, out],
        scratch_shapes=[pltpu.VMEM((8, SSD_CONV_CH), F32),
                        pltpu.VMEM((SSD_GROUPS, SSD_STATE, SSD_WIDTH // SSD_GROUPS), F32),
                        pltpu.VMEM((RET_HEADS, CHUNK, CHUNK), F32),
                        pltpu.VMEM((RET_HEADS, RET_DK, RET_DV), F32),
                        pltpu.VMEM((8, RWKV_IN), F32),
                        pltpu.VMEM((RWKV_HEADS, RWKV_HEAD_DIM, RWKV_HEAD_DIM), F32),
                        pltpu.VMEM((RWKV_WIDTH, CHUNK), F32)],
        compiler_params=_cparams("arbitrary"),
        name="mixers",
    )(*args)


def _constants():
    r = jnp.arange(CHUNK)
    tri = (r[:, None] >= r[None, :]).astype(BF16)
    lane = jnp.arange(RWKV_WIDTH)
    ones_bd = ((lane[:, None] // RWKV_HEAD_DIM) == (lane[None, :] // RWKV_HEAD_DIM)).astype(BF16)
    e8 = ((jnp.arange(128)[:, None] == (lane[None, :] // SSD_HEAD_DIM))).astype(BF16)
    return tri, ones_bd, e8


def kernel(x, meta_tokens, ffn1_norm, ffn1_w_gate, ffn1_w_up, ffn1_w_down, mix_norm, w_in, w_out, mla_q_norm, mla_w_q_up, mla_kv_norm, mla_w_kv_up, mla_qk_norm_q, mla_qk_norm_k, ssd_conv_w, ssd_conv_b, ssd_dt_bias, ssd_a_log, ssd_d, ssd_norm, ret_norm, rwkv_mu, rwkv_w0, rwkv_w2, rwkv_a0, rwkv_a2, rwkv_g2, rwkv_k_k, rwkv_k_a, rwkv_r_k, rwkv_ln, ffn2_norm, ffn2_w_gate, ffn2_w_up, ffn2_w_down):
    b, seq, d = x.shape
    assert b == 1 and d == D_MODEL and seq % CHUNK == 0
    nl = w_in.shape[0]
    lp = seq + CHUNK
    assert lp % TOKEN_TILE == 0

    hst = jnp.concatenate([jnp.zeros((PAD, d), x.dtype), meta_tokens.astype(x.dtype), x[0]], axis=0)

    wq =jnp.pad(mla_w_q_up.reshape(nl, MLA_Q_LORA, MLA_HEADS, MLA_QK),
                 ((0, 0), (0, 0), (0, 0), (0, MLA_QK_PAD - MLA_QK))
                 ).reshape(nl, MLA_Q_LORA, MLA_HEADS * MLA_QK_PAD).astype(BF16)
    wkv = mla_w_kv_up.reshape(nl, MLA_KV_LORA, MLA_HEADS, 2, MLA_NOPE).transpose(0, 1, 3, 2, 4
                              ).reshape(nl, MLA_KV_LORA, 2 * MLA_HEADS * MLA_NOPE).astype(BF16)
    gq = jnp.pad(mla_qk_norm_q, ((0, 0), (0, MLA_QK_PAD - MLA_QK)))[:, None, :]
    gk = jnp.pad(mla_qk_norm_k, ((0, 0), (0, MLA_QK_PAD - MLA_QK)))[:, None, :]
    dtb = jnp.pad(ssd_dt_bias, ((0, 0), (0, 128 - SSD_HEADS)))[:, None, :]
    alog = jnp.pad(ssd_a_log, ((0, 0), (0, 128 - SSD_HEADS)))[:, None, :]
    dsk = jnp.repeat(ssd_d, SSD_HEAD_DIM, axis=1)[:, None, :]
    lora = jnp.zeros((nl, 3, 128, RWKV_WIDTH), F32)
    lora = lora.at[:, 0, :RWKV_DECAY_LORA].set(rwkv_w2)
    lora = lora.at[:, 1, RWKV_DECAY_LORA:RWKV_DECAY_LORA + RWKV_A_LORA].set(rwkv_a2)
    lora = lora.at[:, 2, RWKV_DECAY_LORA + RWKV_A_LORA:].set(rwkv_g2)
    lora = lora.astype(BF16)

    tri, ones_bd, e8 = _constants()
    cos, sin, sinm = _rope_tables(lp)

    w_ffn = (ffn1_w_gate[0].astype(BF16), ffn1_w_up[0].astype(BF16), ffn1_w_down[0].astype(BF16))
    for l in range(nl):
        hst, w_ffn, (w_mla, w_ssd, w_ret, w_rwkv, w_o) = _ffn(
            hst, ffn1_norm[l][None, :], *w_ffn, nxt=(ffn2_w_gate, ffn2_w_up, ffn2_w_down, l),
            proj=(w_in, w_out, l))
        gmix = mix_norm[l][None, :]
        q, k, v, u_ssd = _mla_prep(hst, gmix, w_mla, mla_q_norm[l][None, :], wq, mla_kv_norm[l][None, :],
                                   wkv, gq[l], gk[l], cos, sinm, w_ssd, l)
        y_mla = _attention(q, k, v)
        u_ret = _inproj(hst, gmix, w_ret)
        u_rwkv = _inproj(hst, gmix, w_rwkv)
        y_ssd, y_ret, y_rwkv = _mixers(
            u_ssd, (ssd_conv_w[l], ssd_conv_b[l][None, :], dtb[l], alog[l], dsk[l], ssd_norm[l][None, :], tri, e8),
            u_ret, (cos, sin, ret_norm[l].reshape(1, RET_WIDTH)),
            u_rwkv, (rwkv_mu[l][None, :], rwkv_w0[l][None, :], rwkv_a0[l][None, :], rwkv_k_k[l][None, :],
                     rwkv_k_a[l][None, :], rwkv_r_k[l].reshape(1, RWKV_WIDTH), rwkv_ln[l].reshape(1, RWKV_WIDTH),
                     lora[l], tri, ones_bd))
        hst = _outproj(hst, (y_mla, y_ssd, y_ret, y_rwkv), w_o.reshape(N_MIXERS, MIXER_WIDTH, D_MODEL))
        nxt = (ffn1_w_gate, ffn1_w_up, ffn1_w_down, l + 1) if l + 1 < nl else None
        hst, w_ffn, _ = _ffn(hst, ffn2_norm[l][None, :], *w_ffn, nxt=nxt)
    return hst[CHUNK:][None]
```

```python
import functools
import itertools
import math

import jax
import jax.numpy as jnp
from jax import lax
from jax.experimental import pallas as pl
from jax.experimental.pallas import tpu as pltpu

F32 = jnp.float32
BF16 = jnp.bfloat16

D_MODEL = 2048
DEPTH = 4
N_META = 16
CHUNK = 128
PAD = CHUNK - N_META
D_FF = 5632
EPS = 1e-6
ROPE_THETA = 10000.0
NEG_INF = -1e30

MLA_HEADS = 4
MLA_NOPE = 128
MLA_ROPE = 64
MLA_QK = MLA_NOPE + MLA_ROPE
MLA_V = 128
MLA_Q_LORA = 384
MLA_KV_LORA = 128
MLA_QK_PAD = 256
MLA_V_AUG = MLA_V + 16
MLA_IN = MLA_Q_LORA + MLA_KV_LORA + MLA_ROPE
MLA_IN_PAD = 640

SSD_HEADS = 8
SSD_HEAD_DIM = 64
SSD_WIDTH = 512
SSD_GROUPS = 2
SSD_STATE = 128
SSD_CONV = 4
SSD_CONV_CH = SSD_WIDTH + 2 * SSD_GROUPS * SSD_STATE
SSD_IN = SSD_WIDTH + SSD_CONV_CH + SSD_HEADS
SSD_IN_PAD = SSD_WIDTH + SSD_CONV_CH + 128

RET_HEADS = 4
RET_DK = 64
RET_DV = 128
RET_WIDTH = 512
RET_IN = 2 * RET_HEADS * RET_DK + 2 * RET_WIDTH

RWKV_HEADS = 8
RWKV_HEAD_DIM = 64
RWKV_WIDTH = 512
RWKV_DECAY_LORA = 32
RWKV_A_LORA = 32
RWKV_GATE_LORA = 64
RWKV_LN_EPS = 64e-5
RWKV_IN = 3 * RWKV_WIDTH + RWKV_DECAY_LORA + RWKV_A_LORA + RWKV_GATE_LORA
RWKV_SUB = 64
RWKV_INV_BLOCK = 8

MIX_WIDTH = 2048
N_MIXERS = 4
MIXER_WIDTH = MIX_WIDTH // N_MIXERS

V7X_VMEM_LIMIT_BYTES = 56 * 1024 * 1024
TOKEN_TILE = 640
FF_TILE = 512
FFN_CONVERT_ROWS = 256
PROJ_CONVERT_ROWS = 16


def _cparams(*sem):
    return pltpu.CompilerParams(dimension_semantics=sem, vmem_limit_bytes=V7X_VMEM_LIMIT_BYTES)


def _sigmoid(x):
    return 1.0 / (1.0 + jnp.exp(-x))


def _silu(x):
    return x * _sigmoid(x)


def _softplus(x):
    return jnp.maximum(x, 0.0) + jnp.log(1.0 + jnp.exp(-jnp.abs(x)))


def _rms(x, g, eps=EPS):
    return x * lax.rsqrt(jnp.mean(x * x, axis=-1, keepdims=True) + eps) * g


def _mm(a, b):
    return jnp.dot(a.astype(BF16), b.astype(BF16), preferred_element_type=F32)


def _mm_nt(a, b):
    return lax.dot_general(a.astype(BF16), b.astype(BF16), (((1,), (1,)), ((), ())),
                           preferred_element_type=F32)


def _split3(x):
    x1 = x.astype(BF16)
    r1 = x - x1.astype(F32)
    x2 = r1.astype(BF16)
    x3 = (r1 - x2.astype(F32)).astype(BF16)
    return x1, x2, x3


def _dot01_right(x, m01, pieces=3):
    out = None
    for p in _split3(x)[:pieces]:
        d = jnp.dot(p, m01, preferred_element_type=F32)
        out = d if out is None else out + d
    return out


def _dot01_left(m01, x):
    p1, p2, p3 = _split3(x)
    return (jnp.dot(m01, p1, preferred_element_type=F32)
            + jnp.dot(m01, p2, preferred_element_type=F32)
            + jnp.dot(m01, p3, preferred_element_type=F32))


def _row_ids(rows, cols, base):
    return base + lax.broadcasted_iota(jnp.int32, (rows, cols), 0)


def _rope_table_kernel(inv_ref, cos_ref, sin_ref, sinm_ref):
    i = pl.program_id(0)
    pos = (_row_ids(CHUNK, 128, i * CHUNK) - PAD).astype(F32)
    lane = lax.broadcasted_iota(jnp.int32, (CHUNK, 128), 1)
    ang = pos * inv_ref[...]
    c = jnp.cos(ang)
    s = jnp.sin(ang)
    s = jnp.where((lane % 64) < 32, -s, s)
    cos_ref[...] = c
    sin_ref[...] = s
    sinm_ref[...] = jnp.where(lane < 64, s, 0.0)


def _rope_tables(lp):
    half = 32
    inv = ROPE_THETA ** (-jnp.arange(half, dtype=F32) / half)
    inv = jnp.tile(inv, 4)[None, :]
    out = jax.ShapeDtypeStruct((lp, 128), F32)
    spec = pl.BlockSpec((CHUNK, 128), lambda i: (i, 0))
    return pl.pallas_call(
        _rope_table_kernel,
        grid=(lp // CHUNK,),
        in_specs=[pl.BlockSpec((1, 128), lambda i: (0, 0))],
        out_specs=[spec, spec, spec],
        out_shape=[out, out, out],
        compiler_params=_cparams("parallel"),
        name="rope_tables",
    )(inv)


def _rope_lanes(x, cos, sin_signed):
    lane = lax.broadcasted_iota(jnp.int32, x.shape, 1)
    fwd = pltpu.roll(x, 32, axis=1)
    bwd = pltpu.roll(x, 96, axis=1)
    rot = jnp.where((lane % 64) < 32, bwd, fwd)
    return x * cos + rot * sin_signed


def _ffn_kernel(*refs, convert_next, convert_proj):
    refs = list(refs)
    x_ref, g_ref, wg_ref, wu_ref, wd_ref = refs[:5]
    n_in = 5 + (3 if convert_next else 0) + (2 if convert_proj else 0)
    ins, outs, xn_ref = refs[5:n_in], refs[n_in:-1], refs[-1]
    o_ref = outs.pop(0)
    if convert_next:
        for src, dst in zip(ins[:3], outs[:3]):
            dst[...] = src[...].astype(BF16)
        ins, outs = ins[3:], outs[3:]
    if convert_proj:
        win_ref, wout_ref = ins
        wmla_ref, wssd_ref, wret_ref, wrwkv_ref, wo_ref = outs
        o0 = MLA_IN
        o1 = o0 + SSD_IN
        o2 = o1 + RET_IN
        rows = win_ref.shape[0]
        wmla_ref[:, :MLA_IN] = win_ref[:, :o0].astype(BF16)
        wmla_ref[:, MLA_IN:] = jnp.zeros((rows, MLA_IN_PAD - MLA_IN), BF16)
        wssd_ref[:, :SSD_IN] = win_ref[:, o0:o1].astype(BF16)
        wssd_ref[:, SSD_IN:] = jnp.zeros((rows, SSD_IN_PAD - SSD_IN), BF16)
        wret_ref[...] = win_ref[:, o1:o2].astype(BF16)
        wrwkv_ref[...] = win_ref[:, o2:].astype(BF16)
        wo_ref[...] = wout_ref[...].astype(BF16)
    j = pl.program_id(1)

    @pl.when(j == 0)
    def _():
        xn_ref[...] = _rms(x_ref[...], g_ref[...]).astype(BF16)
        o_ref[...] = jnp.zeros_like(o_ref)

    xn = xn_ref[...]
    a = jnp.dot(xn, wg_ref[...], preferred_element_type=F32)
    b = jnp.dot(xn, wu_ref[...], preferred_element_type=F32)
    mid = (_silu(a) * b).astype(BF16)
    o_ref[...] += jnp.dot(mid, wd_ref[...], preferred_element_type=F32)

    @pl.when(j == pl.num_programs(1) - 1)
    def _():
        o_ref[...] = x_ref[...] + 0.5 * o_ref[...]


def _ffn(h, g, wgb, wub, wdb, nxt=None, proj=None):
    lp = h.shape[0]
    tm, tf = TOKEN_TILE, FF_TILE
    n_j = D_FF // tf
    n_c = D_MODEL // FFN_CONVERT_ROWS if nxt is not None else 0
    in_specs = [
        pl.BlockSpec((tm, D_MODEL), lambda i, j: (i, 0)),
        pl.BlockSpec((1, D_MODEL), lambda i, j: (0, 0)),
        pl.BlockSpec((D_MODEL, tf), lambda i, j: (0, j)),
        pl.BlockSpec((D_MODEL, tf), lambda i, j: (0, j)),
        pl.BlockSpec((tf, D_MODEL), lambda i, j: (j, 0)),
    ]
    out_specs = [pl.BlockSpec((tm, D_MODEL), lambda i, j: (i, 0))]
    out_shape = [jax.ShapeDtypeStruct((lp, D_MODEL), F32)]
    args = [h, g, wgb, wub, wdb]
    if nxt is not None:
        wg, wu, wd, l = nxt
        cr = FFN_CONVERT_ROWS
        assert lp // tm >= n_c
        ci = lambda i: jnp.minimum(i, n_c - 1)
        cj = lambda i, j: jnp.where(i < n_c, j, n_j - 1)
        in_specs += [
            pl.BlockSpec((None, cr, tf), lambda i, j: (l, ci(i), cj(i, j))),
            pl.BlockSpec((None, cr, tf), lambda i, j: (l, ci(i), cj(i, j))),
            pl.BlockSpec((None, tf, cr), lambda i, j: (l, cj(i, j), ci(i))),
        ]
        out_specs += [
            pl.BlockSpec((cr, tf), lambda i, j: (ci(i), cj(i, j))),
            pl.BlockSpec((cr, tf), lambda i, j: (ci(i), cj(i, j))),
            pl.BlockSpec((tf, cr), lambda i, j: (cj(i, j), ci(i))),
        ]
        out_shape += [
            jax.ShapeDtypeStruct((D_MODEL, D_FF), BF16),
            jax.ShapeDtypeStruct((D_MODEL, D_FF), BF16),
            jax.ShapeDtypeStruct((D_FF, D_MODEL), BF16),
        ]
        args += [wg, wu, wd]
    if proj is not None:
        w_in, w_out, lpj = proj
        pr = PROJ_CONVERT_ROWS
        n_p = D_MODEL // pr
        assert (lp // tm) * n_j >= n_p
        pidx = lambda i, j: jnp.minimum(i * n_j + j, n_p - 1)
        in_specs += [
            pl.BlockSpec((None, pr, w_in.shape[2]), lambda i, j: (lpj, pidx(i, j), 0)),
            pl.BlockSpec((None, pr, D_MODEL), lambda i, j: (lpj, pidx(i, j), 0)),
        ]
        widths = (MLA_IN_PAD, SSD_IN_PAD, RET_IN, RWKV_IN, D_MODEL)
        out_specs += [pl.BlockSpec((pr, n), lambda i, j: (pidx(i, j), 0)) for n in widths]
        out_shape += [jax.ShapeDtypeStruct((D_MODEL, n), BF16) for n in widths]
        args += [w_in, w_out]
    outs = pl.pallas_call(
        functools.partial(_ffn_kernel, convert_next=nxt is not None, convert_proj=proj is not None),
        grid=(lp // tm, n_j),
        in_specs=in_specs,
        out_specs=out_specs,
        out_shape=out_shape,
        scratch_shapes=[pltpu.VMEM((tm, D_MODEL), BF16)],
        compiler_params=_cparams("arbitrary", "arbitrary"),
        name="ffn",
    )(*args)
    n_next = 3 if nxt is not None else 0
    return (outs[0], tuple(outs[1:1 + n_next]) if nxt is not None else None,
            tuple(outs[1 + n_next:]) if proj is not None else None)


def _inproj_kernel(x_ref, g_ref, w_ref, o_ref):
    i = pl.program_id(0)
    tm = x_ref.shape[0]
    xn = _rms(x_ref[...], g_ref[...]).astype(BF16)
    u = jnp.dot(xn, w_ref[...], preferred_element_type=F32)
    row = _row_ids(tm, 1, i * tm)
    o_ref[...] = jnp.where(row >= PAD, u, 0.0)


def _inproj(h, g, w):
    lp = h.shape[0]
    n = w.shape[1]
    tm = TOKEN_TILE
    return pl.pallas_call(
        _inproj_kernel,
        grid=(lp // tm,),
        in_specs=[
            pl.BlockSpec((tm, D_MODEL), lambda i: (i, 0)),
            pl.BlockSpec((1, D_MODEL), lambda i: (0, 0)),
            pl.BlockSpec((D_MODEL, n), lambda i: (0, 0)),
        ],
        out_specs=pl.BlockSpec((tm, n), lambda i: (i, 0)),
        out_shape=jax.ShapeDtypeStruct((lp, n), F32),
        compiler_params=_cparams("parallel"),
        name="inproj",
    )(h, g, w)


def _outproj_kernel(h_ref, y0_ref, y1_ref, y2_ref, y3_ref, w_ref, o_ref):
    acc = h_ref[...]
    for m, y_ref in enumerate((y0_ref, y1_ref, y2_ref, y3_ref)):
        acc = acc + jnp.dot(y_ref[...], w_ref[m], preferred_element_type=F32)
    o_ref[...] = acc


def _outproj(h, ys, w):
    lp = h.shape[0]
    tm = TOKEN_TILE
    yspec = pl.BlockSpec((tm, MIXER_WIDTH), lambda i: (i, 0))
    return pl.pallas_call(
        _outproj_kernel,
        grid=(lp // tm,),
        in_specs=[pl.BlockSpec((tm, D_MODEL), lambda i: (i, 0)), yspec, yspec, yspec, yspec,
                  pl.BlockSpec((N_MIXERS, MIXER_WIDTH, D_MODEL), lambda i: (0, 0, 0))],
        out_specs=pl.BlockSpec((tm, D_MODEL), lambda i: (i, 0)),
        out_shape=jax.ShapeDtypeStruct((lp, D_MODEL), F32),
        compiler_params=_cparams("parallel"),
        name="outproj",
    )(h, *ys, w)


def _mla_prep_kernel(x_ref, g_ref, win_ref, qn_ref, wq_ref, kvn_ref, wkv_ref, gq_ref, gk_ref,
                     cos_ref, sinm_ref, wssd_ref, q_ref, k_ref, v_ref, ussd_ref):
    i = pl.program_id(0)
    tm = x_ref.shape[0]
    xn = _rms(x_ref[...], g_ref[...]).astype(BF16)
    u = jnp.dot(xn, win_ref[...], preferred_element_type=F32)
    row = _row_ids(tm, 1, i * tm)
    u = jnp.where(row >= PAD, u, 0.0)
    cq = u[:, :MLA_Q_LORA]
    ckv = u[:, MLA_Q_LORA:MLA_Q_LORA + MLA_KV_LORA]
    kpe = u[:, MLA_Q_LORA + MLA_KV_LORA:]
    q_all = _mm(_rms(cq, qn_ref[...]), wq_ref[...])
    kv_all = _mm(_rms(ckv, kvn_ref[...]), wkv_ref[...])
    ussd_ref[...] = jnp.where(row >= PAD, jnp.dot(xn, wssd_ref[...], preferred_element_type=F32), 0.0)
    cos = cos_ref[...]
    sinm = sinm_ref[...]
    gq = gq_ref[...]
    gk = gk_ref[...]
    scale = MLA_QK ** -0.5
    for h in range(MLA_HEADS):
        qh = q_all[:, h * MLA_QK_PAD:(h + 1) * MLA_QK_PAD]
        ss = jnp.sum(qh * qh, axis=-1, keepdims=True) * (1.0 / MLA_QK)
        qh = qh * lax.rsqrt(ss + EPS) * gq
        q_rot = _rope_lanes(qh[:, MLA_NOPE:], cos, sinm)
        q_ref[h, 0, :MLA_NOPE, :] = (qh[:, :MLA_NOPE] * scale).T.astype(BF16)
        q_ref[h, 0, MLA_NOPE:, :] = (q_rot * scale).T.astype(BF16)

        kn = kv_all[:, h * MLA_NOPE:(h + 1) * MLA_NOPE]
        ss = (jnp.sum(kn * kn, axis=-1, keepdims=True)
              + jnp.sum(kpe * kpe, axis=-1, keepdims=True)) * (1.0 / MLA_QK)
        rs = lax.rsqrt(ss + EPS)
        k_rot = _rope_lanes(kpe * rs * gk[:, MLA_NOPE:], cos, sinm)
        k_ref[h, :, :MLA_NOPE] = (kn * rs * gk[:, :MLA_NOPE]).astype(BF16)
        k_ref[h, :, MLA_NOPE:] = k_rot.astype(BF16)
        v_ref[h, 0, :MLA_V, :] = kv_all[:, 4 * MLA_NOPE + h * MLA_V: 4 * MLA_NOPE + (h + 1) * MLA_V].T.astype(BF16)
        tail = lax.broadcasted_iota(jnp.int32, (MLA_V_AUG - MLA_V, tm), 0)
        v_ref[h, 0, MLA_V:, :] = jnp.where(tail == 0, 1.0, 0.0).astype(BF16)


def _mla_prep(h, g, win, qn, wq, kvn, wkv, gq, gk, cos, sinm, wssd, l):
    lp = h.shape[0]
    tm = TOKEN_TILE
    full = lambda shape: pl.BlockSpec(shape, lambda i: (0,) * len(shape))
    layer = lambda shape: pl.BlockSpec((None,) + shape, lambda i: (l,) + (0,) * len(shape))
    return pl.pallas_call(
        _mla_prep_kernel,
        grid=(lp // tm,),
        in_specs=[
            pl.BlockSpec((tm, D_MODEL), lambda i: (i, 0)),
            full((1, D_MODEL)), full((D_MODEL, MLA_IN_PAD)),
            full((1, MLA_Q_LORA)), layer((MLA_Q_LORA, MLA_HEADS * MLA_QK_PAD)),
            full((1, MLA_KV_LORA)), layer((MLA_KV_LORA, 2 * MLA_HEADS * MLA_NOPE)),
            full((1, MLA_QK_PAD)), full((1, MLA_QK_PAD)),
            pl.BlockSpec((tm, 128), lambda i: (i, 0)),
            pl.BlockSpec((tm, 128), lambda i: (i, 0)),
            full((D_MODEL, SSD_IN_PAD)),
        ],
        out_specs=[
            pl.BlockSpec((MLA_HEADS, 1, MLA_QK_PAD, tm), lambda i: (0, i, 0, 0)),
            pl.BlockSpec((MLA_HEADS, tm, MLA_QK_PAD), lambda i: (0, i, 0)),
            pl.BlockSpec((MLA_HEADS, 1, MLA_V_AUG, tm), lambda i: (0, i, 0, 0)),
            pl.BlockSpec((tm, SSD_IN_PAD), lambda i: (i, 0)),
        ],
        out_shape=[
            jax.ShapeDtypeStruct((MLA_HEADS, lp // tm, MLA_QK_PAD, tm), BF16),
            jax.ShapeDtypeStruct((MLA_HEADS, lp, MLA_QK_PAD), BF16),
            jax.ShapeDtypeStruct((MLA_HEADS, lp // tm, MLA_V_AUG, tm), BF16),
            jax.ShapeDtypeStruct((lp, SSD_IN_PAD), F32),
        ],
        compiler_params=_cparams("parallel"),
        name="mla_prep",
    )(h, g, win, qn, wq, kvn, wkv, gq, gk, cos, sinm, wssd)


def _attn_kernel(q_ref, k_ref, vt_ref, o_ref):
    i = pl.program_id(1)
    tq = q_ref.shape[3]
    qt = q_ref[0, 0]
    key0 = lax.broadcasted_iota(jnp.int32, (tq, tq), 0)
    qry = i * tq + lax.broadcasted_iota(jnp.int32, (tq, tq), 1)

    def scores(j, masked):
        start = pl.multiple_of(j * tq, tq)
        k = k_ref[0, pl.ds(start, tq), :]
        s = jnp.dot(k, qt, preferred_element_type=F32)
        if masked:
            key = key0 + j * tq
            s = jnp.where(key <= qry, jnp.where(key >= PAD, s, NEG_INF), NEG_INF)
        return s

    def update(j, s, carry):
        m, acc = carry
        m_new = jnp.maximum(m, jnp.max(s, axis=0, keepdims=True))
        alpha = jnp.exp(m - m_new)
        p = jnp.exp((s - m_new).astype(BF16))
        acc = alpha * acc + jnp.dot(vt_ref[0, j], p, preferred_element_type=F32)
        return m_new, acc

    init = (jnp.full((1, tq), NEG_INF, F32), jnp.zeros((MLA_V_AUG, tq), F32))
    s_first = scores(0, True)

    def body(j, c):
        s_prev, carry = c
        s_new = scores(j, False)
        return s_new, update(j - 1, s_prev, carry)

    s_last, carry = lax.fori_loop(1, i, body, (s_first, init))
    j_last = jnp.maximum(i - 1, 0)
    m, acc = lax.cond(i > 0,
                      lambda c: update(i, scores(i, True), update(j_last, s_last, c)),
                      lambda c: update(j_last, s_last, c), carry)
    o_ref[...] = (acc[:MLA_V] / acc[MLA_V:MLA_V + 1]).T.astype(BF16)


def _attention(q, k, v):
    lp = k.shape[1]
    tq = TOKEN_TILE
    return pl.pallas_call(
        _attn_kernel,
        grid=(MLA_HEADS, lp // tq),
        in_specs=[
            pl.BlockSpec((1, 1, MLA_QK_PAD, tq), lambda h, i: (h, i, 0, 0)),
            pl.BlockSpec((1, lp, MLA_QK_PAD), lambda h, i: (h, 0, 0)),
            pl.BlockSpec((1, lp // tq, MLA_V_AUG, tq), lambda h, i: (h, 0, 0, 0)),
        ],
        out_specs=pl.BlockSpec((tq, MLA_V), lambda h, i: (i, h)),
        out_shape=jax.ShapeDtypeStruct((lp, MLA_HEADS * MLA_V), BF16),
        compiler_params=_cparams("parallel", "arbitrary"),
        name="mla_attention",
    )(q, k, v)


def _shift_rows(x, carry, s):
    rolled = pltpu.roll(x, s, axis=0)
    head = pltpu.roll(carry, s, axis=0)
    r8 = lax.broadcasted_iota(jnp.int32, (8, 1), 0)
    return jnp.concatenate([jnp.where(r8 < s, head, rolled[:8]), rolled[8:]], axis=0)


def _ssd_init(carry_ref, state_ref):
    carry_ref[...] = jnp.zeros_like(carry_ref)
    state_ref[...] = jnp.zeros_like(state_ref)


def _ssd_body(u_ref, cw_ref, cb_ref, dtb_ref, alog_ref, dsk_ref, ng_ref, tri_ref, e8_ref,
              y_ref, carry_ref, state_ref):
    i = pl.program_id(0)
    z = u_ref[:, :SSD_WIDTH]
    xbc = u_ref[:, SSD_WIDTH:SSD_WIDTH + SSD_CONV_CH]
    dt_raw = u_ref[:, SSD_WIDTH + SSD_CONV_CH:]
    carry = carry_ref[...]
    cw = cw_ref[...]
    conv = xbc * cw[3:4, :]
    for s in (1, 2, 3):
        conv = conv + _shift_rows(xbc, carry, s) * cw[3 - s:4 - s, :]
    carry_ref[...] = xbc[CHUNK - 8:, :]
    xbc = _silu(conv + cb_ref[...])
    xs = xbc[:, :SSD_WIDTH]
    bm = xbc[:, SSD_WIDTH:SSD_WIDTH + SSD_GROUPS * SSD_STATE]
    cm = xbc[:, SSD_WIDTH + SSD_GROUPS * SSD_STATE:]

    row = _row_ids(CHUNK, 1, i * CHUNK)
    dt = _softplus(dt_raw + dtb_ref[...]) * jnp.where(row >= PAD, 1.0, 0.0)
    la = dt * (-jnp.exp(alog_ref[...]))
    tri = tri_ref[...]
    e8 = e8_ref[...]
    cs = _dot01_left(tri, la)
    cs_e = _dot01_right(cs, e8, pieces=2)
    dt_e = _dot01_right(dt, e8, pieces=2)
    cs_t = cs.T
    cs_last_e = cs_e[CHUNK - 1:CHUNK, :]
    x = xs * dt_e
    xd = x * jnp.exp(cs_last_e - cs_e)
    ecs = jnp.exp(cs_e)
    dec = jnp.exp(cs_last_e)

    r_i = lax.broadcasted_iota(jnp.int32, (CHUNK, CHUNK), 0)
    c_i = lax.broadcasted_iota(jnp.int32, (CHUNK, CHUNK), 1)
    causal = r_i >= c_i
    per = SSD_HEADS // SSD_GROUPS
    gw = per * SSD_HEAD_DIM
    groups = range(SSD_GROUPS)
    b_gs = [bm[:, g * SSD_STATE:(g + 1) * SSD_STATE] for g in groups]
    c_gs = [cm[:, g * SSD_STATE:(g + 1) * SSD_STATE] for g in groups]
    s_prev = [state_ref[g] for g in groups]
    yield
    scores = [_mm_nt(c_gs[g], b_gs[g]) for g in groups]
    y_off = [_mm(c_gs[g], s_prev[g]) for g in groups]
    s_add = [_mm(b_gs[g].T, xd[:, g * gw:(g + 1) * gw]) for g in groups]
    yield
    lmats = []
    for h in range(SSD_HEADS):
        seg = cs[:, h:h + 1] - cs_t[h:h + 1, :]
        lmats.append(jnp.where(causal, jnp.exp(jnp.where(causal, seg, 0.0)), 0.0))
    y_diag = [_mm(scores[h // per] * lmats[h], x[:, h * SSD_HEAD_DIM:(h + 1) * SSD_HEAD_DIM])
              for h in range(SSD_HEADS)]
    yield
    for g in groups:
        state_ref[g] = s_prev[g] * dec[:, g * gw:(g + 1) * gw] + s_add[g]
    y = (jnp.concatenate(y_diag, axis=1) + jnp.concatenate(y_off, axis=1) * ecs) + dsk_ref[...] * xs
    y = y * _silu(z)
    y_ref[...] = _rms(y, ng_ref[...]).astype(BF16)


RET_LOG_G = [math.log(1.0 - 2.0 ** (-5.0 - h)) for h in range(RET_HEADS)]


def _ret_init(dmat_ref, state_ref):
    state_ref[...] = jnp.zeros_like(state_ref)
    r_i = lax.broadcasted_iota(jnp.int32, (CHUNK, CHUNK), 0)
    c_i = lax.broadcasted_iota(jnp.int32, (CHUNK, CHUNK), 1)
    diff = (r_i - c_i).astype(F32)
    for h in range(RET_HEADS):
        dmat_ref[h] = jnp.where(r_i >= c_i, jnp.exp(jnp.where(r_i >= c_i, diff, 0.0) * RET_LOG_G[h]), 0.0)


def _ret_body(u_ref, cos_ref, sin_ref, ng_ref, y_ref, dmat_ref, state_ref):
    log_g = RET_LOG_G
    qkw = RET_HEADS * RET_DK
    cos = cos_ref[...]
    sin = sin_ref[...]
    q = jnp.concatenate([_rope_lanes(u_ref[:, c * 128:(c + 1) * 128], cos, sin) for c in range(2)], axis=1)
    k = jnp.concatenate([_rope_lanes(u_ref[:, qkw + c * 128:qkw + (c + 1) * 128], cos, sin)
                         for c in range(2)], axis=1) * (RET_DK ** -0.5)
    idx = lax.broadcasted_iota(jnp.int32, (CHUNK, 1), 0).astype(F32)
    k_t = k.T
    idx_row = lax.broadcasted_iota(jnp.int32, (1, CHUNK), 1).astype(F32)
    heads = range(RET_HEADS)
    q_hs = [q[:, h * RET_DK:(h + 1) * RET_DK] for h in heads]
    k_hs = [k[:, h * RET_DK:(h + 1) * RET_DK] for h in heads]
    v_hs = [u_ref[:, 2 * qkw + h * RET_DV: 2 * qkw + (h + 1) * RET_DV].astype(BF16) for h in heads]
    s_prevs = [state_ref[h] for h in heads]
    yield
    scs = [_mm_nt(q_hs[h], k_hs[h]) for h in heads]
    y_cross = [_mm(q_hs[h] * jnp.exp((idx + 1.0) * log_g[h]), s_prevs[h]) for h in heads]
    yield
    s_adds = [_mm(k_t[h * RET_DK:(h + 1) * RET_DK, :] * jnp.exp((CHUNK - 1 - idx_row) * log_g[h]), v_hs[h])
              for h in heads]
    y_in = [_mm(scs[h] * dmat_ref[h], v_hs[h]) for h in heads]
    yield
    for h in heads:
        g_h = u_ref[:, 2 * qkw + RET_WIDTH + h * RET_DV: 2 * qkw + RET_WIDTH + (h + 1) * RET_DV]
        y = y_in[h] + y_cross[h]
        state_ref[h] = s_prevs[h] * math.exp(CHUNK * log_g[h]) + s_adds[h]
        mu = jnp.mean(y, axis=-1, keepdims=True)
        var = jnp.mean(jnp.square(y - mu), axis=-1, keepdims=True)
        yn = (y - mu) * lax.rsqrt(var + EPS) * ng_ref[:, h * RET_DV:(h + 1) * RET_DV]
        y_ref[:, h * RET_DV:(h + 1) * RET_DV] = (_silu(g_h) * yn).astype(BF16)


def _inv_unit_upper_many(mats, in_block, eye):
    d = [jnp.where(in_block, a, 0.0) for a in mats]
    f = [a - x for a, x in zip(mats, d)]
    d2 = [_mm(x, x) for x in d]
    yield
    d4 = [_mm(x, x) for x in d2]
    p = [_mm(eye + x, eye + y) for x, y in zip(d, d2)]
    yield
    td = [_mm(x, eye + y) for x, y in zip(p, d4)]
    yield
    g = [_mm(x, y) for x, y in zip(f, td)]
    yield
    g2 = [_mm(x, x) for x in g]
    tg = [_mm(x, eye + y) for x, y in zip(td, g)]
    yield
    g4 = [_mm(x, x) for x in g2]
    tg = [_mm(x, eye + y) for x, y in zip(tg, g2)]
    yield
    return [_mm(x, eye + y) for x, y in zip(tg, g4)]


def _rwkv_init(prev_ref, state_ref):
    prev_ref[...] = jnp.zeros_like(prev_ref)
    state_ref[...] = jnp.zeros_like(state_ref)


def _rwkv_body(u_ref, mu_ref, w0_ref, a0_ref, kk_ref, ka_ref, rk_ref, ln_ref, lora_ref, tri_ref,
               ones_ref, y_ref, prev_ref, state_ref, ot_ref):
    u = u_ref[...]
    rows = lax.broadcasted_iota(jnp.int32, (CHUNK, 1), 0)
    u_prev = jnp.where(rows == 0, prev_ref[7:8, :], pltpu.roll(u, 1, axis=0))
    prev_ref[...] = u[CHUNK - 8:, :]
    us = u + (u_prev - u) * mu_ref[...]
    w3 = RWKV_WIDTH
    r = us[:, :w3]
    k = us[:, w3:2 * w3]
    v = us[:, 2 * w3:3 * w3]
    lo = us[:, 3 * w3:]
    w = w0_ref[...] + _mm(jnp.tanh(lo), lora_ref[0])
    w = -_softplus(-w) - 0.5
    ld = -jnp.exp(w)
    a = _sigmoid(a0_ref[...] + _mm(lo, lora_ref[1]))
    g = _mm(_sigmoid(lo), lora_ref[2])
    ones = ones_ref[...]
    kk = k * kk_ref[...]
    k2 = k * (1.0 + (a - 1.0) * ka_ref[...])
    kk = kk / jnp.maximum(jnp.sqrt(_dot01_right(kk * kk, ones, pieces=2)), 1e-12)
    b = kk * a

    lc_full = _dot01_left(tri_ref[...], ld)
    v_t = v.T

    n = RWKV_SUB
    r_i = lax.broadcasted_iota(jnp.int32, (n, n), 0)
    c_i = lax.broadcasted_iota(jnp.int32, (n, n), 1)
    in_block = (r_i // RWKV_INV_BLOCK) == (c_i // RWKV_INV_BLOCK)
    eye = jnp.where(r_i == c_i, 1.0, 0.0)
    hd = RWKV_HEAD_DIM
    n_sub = CHUNK // n
    heads = range(RWKV_HEADS)
    hs = [slice(h * hd, (h + 1) * hd) for h in heads]

    sc = []
    for sub in range(n_sub):
        lo_r, hi_r = sub * n, (sub + 1) * n
        lc = lc_full[lo_r:hi_r, :]
        if sub > 0:
            lc = lc - lc_full[lo_r - 1:lo_r, :]
        lc_last = lc[n - 1:n, :]
        e_pos = jnp.exp(lc)
        e_neg = jnp.exp(-lc)
        e_prev = jnp.exp(lc - ld[lo_r:hi_r, :])
        e_end = jnp.exp(lc_last - lc)
        sc.append(dict(
            gam=jnp.exp(lc_last),
            a_t=(-kk[lo_r:hi_r, :] * e_prev).astype(BF16),
            r_t=(r[lo_r:hi_r, :] * e_pos).astype(BF16),
            b_t=(b[lo_r:hi_r, :] * e_neg).astype(BF16),
            k_t=(k2[lo_r:hi_r, :] * e_neg).astype(BF16),
            b_h=(b[lo_r:hi_r, :] * e_end).astype(BF16),
            k_h=(k2[lo_r:hi_r, :] * e_end).astype(BF16),
            v_t=v_t[:, lo_r:hi_r].astype(BF16),
        ))

    pairs = [(sub, h) for sub in range(n_sub) for h in heads]
    yield
    bk = [jnp.concatenate([sc[sub]["b_t"][:, hs[h]], sc[sub]["k_t"][:, hs[h]]], axis=0) for sub, h in pairs]
    ar2 = [jnp.concatenate([sc[sub]["a_t"][:, hs[h]], sc[sub]["r_t"][:, hs[h]]], axis=0) for sub, h in pairs]
    r2 = lax.broadcasted_iota(jnp.int32, (n, 2 * n), 0)
    c2 = lax.broadcasted_iota(jnp.int32, (n, 2 * n), 1)
    mask2 = r2 < jnp.where(c2 < n, c2, c2 - n + 1)
    quad = [_mm_nt(x, y) for x, y in zip(bk, ar2)]
    top = [jnp.where(mask2, x[:n, :], 0.0) for x in quad]
    bot = [jnp.where(mask2, x[n:, :], 0.0).astype(BF16) for x in quad]
    yield
    vprod = [_mm(sc[sub]["v_t"][hs[h], :], jnp.concatenate([y, sc[sub]["k_h"][:, hs[h]]], axis=1))
             for y, (sub, h) in zip(bot, pairs)]
    yield
    t_t = yield from _inv_unit_upper_many([x[:, :n] for x in top], in_block, eye)
    top = [x.astype(BF16) for x in top]

    state = [state_ref[h] for h in heads]
    for sub in range(n_sub):
        c = sc[sub]
        base = sub * RWKV_HEADS
        yield
        xr = [_mm_nt(state[h], ar2[base + h]) for h in heads]
        yield
        u_t = [_mm(xr[h][:, :n] + vprod[base + h][:, :n], t_t[base + h]) for h in heads]
        yield
        ub = [_mm(u_t[h], top[base + h]) for h in heads]
        su = [_mm(u_t[h], c["b_h"][:, hs[h]]) for h in heads]
        for h in heads:
            ot_ref[h * hd:(h + 1) * hd, sub * n:(sub + 1) * n] = (
                xr[h][:, n:] + ub[h][:, n:] + vprod[base + h][:, n:2 * n])
        state = [state[h] * c["gam"][:, hs[h]] + su[h] + vprod[base + h][:, 2 * n:] for h in heads]
    for h in heads:
        state_ref[h] = state[h]
    yield

    out = ot_ref[...].T
    inv_n = 1.0 / RWKV_HEAD_DIM
    mean = _dot01_right(out, ones, pieces=2) * inv_n
    cen = out - mean
    var = _dot01_right(cen * cen, ones, pieces=2) * inv_n
    out = cen * lax.rsqrt(var + RWKV_LN_EPS) * ln_ref[...]
    bonus = _dot01_right(r * k2 * rk_ref[...], ones, pieces=2) * v
    y_ref[...] = ((out + bonus) * g).astype(BF16)


N_SSD_IN, N_RET_IN, N_RWKV_IN = 9, 4, 11


def _mixers_kernel(*refs):
    n_in = N_SSD_IN + N_RET_IN + N_RWKV_IN
    ins, (y_ssd, y_ret, y_rwkv), scr = refs[:n_in], refs[n_in:n_in + 3], refs[n_in + 3:]
    ssd_in, ret_in, rwkv_in = ins[:N_SSD_IN], ins[N_SSD_IN:N_SSD_IN + N_RET_IN], ins[N_SSD_IN + N_RET_IN:]
    ssd_scr, ret_scr, rwkv_scr = scr[:2], scr[2:4], scr[4:]

    @pl.when(pl.program_id(0) == 0)
    def _():
        _ssd_init(*ssd_scr)
        _ret_init(*ret_scr)
        _rwkv_init(*rwkv_scr[:2])

    main = _rwkv_body(*rwkv_in, y_rwkv, *rwkv_scr)
    side = itertools.chain(_ssd_body(*ssd_in, y_ssd, *ssd_scr), _ret_body(*ret_in, y_ret, *ret_scr))
    for step, _ in enumerate(main):
        if step % 2 == 0:
            next(side, None)
    for _ in side:
        pass


def _mixers(u_ssd, ssd_params, u_ret, ret_params, u_rwkv, rwkv_params):
    lp = u_ssd.shape[0]
    full = lambda a: pl.BlockSpec(a.shape, lambda i: (0,) * a.ndim)
    rows = lambda a: pl.BlockSpec((CHUNK, a.shape[1]), lambda i: (i, 0))
    cos, sin, ng = ret_params
    args = [u_ssd, *ssd_params, u_ret, cos, sin, ng, u_rwkv, *rwkv_params]
    assert len(args) == N_SSD_IN + N_RET_IN + N_RWKV_IN
    in_specs = ([rows(u_ssd)] + [full(a) for a in ssd_params]
                + [rows(u_ret), rows(cos), rows(sin), full(ng)]
                + [rows(u_rwkv)] + [full(a) for a in rwkv_params])
    out = jax.ShapeDtypeStruct((lp, MIXER_WIDTH), BF16)
    ospec = pl.BlockSpec((CHUNK, MIXER_WIDTH), lambda i: (i, 0))
    return pl.pallas_call(
        _mixers_kernel,
        grid=(lp // CHUNK,),
        in_specs=in_specs,
        out_specs=[ospec, ospec, ospec],
        out_shape=[out, out, out],
        scratch_shapes=[pltpu.VMEM((8, SSD_CONV_CH), F32),
                        pltpu.VMEM((SSD_GROUPS, SSD_STATE, SSD_WIDTH // SSD_GROUPS), F32),
                        pltpu.VMEM((RET_HEADS, CHUNK, CHUNK), F32),
                        pltpu.VMEM((RET_HEADS, RET_DK, RET_DV), F32),
                        pltpu.VMEM((8, RWKV_IN), F32),
                        pltpu.VMEM((RWKV_HEADS, RWKV_HEAD_DIM, RWKV_HEAD_DIM), F32),
                        pltpu.VMEM((RWKV_WIDTH, CHUNK), F32)],
        compiler_params=_cparams("arbitrary"),
        name="mixers",
    )(*args)


def _constants():
    r = jnp.arange(CHUNK)
    tri = (r[:, None] >= r[None, :]).astype(BF16)
    lane = jnp.arange(RWKV_WIDTH)
    ones_bd = ((lane[:, None] // RWKV_HEAD_DIM) == (lane[None, :] // RWKV_HEAD_DIM)).astype(BF16)
    e8 = ((jnp.arange(128)[:, None] == (lane[None, :] // SSD_HEAD_DIM))).astype(BF16)
    return tri, ones_bd, e8


def kernel(x, meta_tokens, ffn1_norm, ffn1_w_gate, ffn1_w_up, ffn1_w_down, mix_norm, w_in, w_out, mla_q_norm, mla_w_q_up, mla_kv_norm, mla_w_kv_up, mla_qk_norm_q, mla_qk_norm_k, ssd_conv_w, ssd_conv_b, ssd_dt_bias, ssd_a_log, ssd_d, ssd_norm, ret_norm, rwkv_mu, rwkv_w0, rwkv_w2, rwkv_a0, rwkv_a2, rwkv_g2, rwkv_k_k, rwkv_k_a, rwkv_r_k, rwkv_ln, ffn2_norm, ffn2_w_gate, ffn2_w_up, ffn2_w_down):
    b, seq, d = x.shape
    assert b == 1 and d == D_MODEL and seq % CHUNK == 0
    nl = w_in.shape[0]
    lp = seq + CHUNK
    assert lp % TOKEN_TILE == 0

    hst = jnp.concatenate([jnp.zeros((PAD, d), x.dtype), meta_tokens.astype(x.dtype), x[0]], axis=0)

    wq =jnp.pad(mla_w_q_up.reshape(nl, MLA_Q_LORA, MLA_HEADS, MLA_QK),
                 ((0, 0), (0, 0), (0, 0), (0, MLA_QK_PAD - MLA_QK))
                 ).reshape(nl, MLA_Q_LORA, MLA_HEADS * MLA_QK_PAD).astype(BF16)
    wkv = mla_w_kv_up.reshape(nl, MLA_KV_LORA, MLA_HEADS, 2, MLA_NOPE).transpose(0, 1, 3, 2, 4
                              ).reshape(nl, MLA_KV_LORA, 2 * MLA_HEADS * MLA_NOPE).astype(BF16)
    gq = jnp.pad(mla_qk_norm_q, ((0, 0), (0, MLA_QK_PAD - MLA_QK)))[:, None, :]
    gk = jnp.pad(mla_qk_norm_k, ((0, 0), (0, MLA_QK_PAD - MLA_QK)))[:, None, :]
    dtb = jnp.pad(ssd_dt_bias, ((0, 0), (0, 128 - SSD_HEADS)))[:, None, :]
    alog = jnp.pad(ssd_a_log, ((0, 0), (0, 128 - SSD_HEADS)))[:, None, :]
    dsk = jnp.repeat(ssd_d, SSD_HEAD_DIM, axis=1)[:, None, :]
    lora = jnp.zeros((nl, 3, 128, RWKV_WIDTH), F32)
    lora = lora.at[:, 0, :RWKV_DECAY_LORA].set(rwkv_w2)
    lora = lora.at[:, 1, RWKV_DECAY_LORA:RWKV_DECAY_LORA + RWKV_A_LORA].set(rwkv_a2)
    lora = lora.at[:, 2, RWKV_DECAY_LORA + RWKV_A_LORA:].set(rwkv_g2)
    lora = lora.astype(BF16)

    tri, ones_bd, e8 = _constants()
    cos, sin, sinm = _rope_tables(lp)

    w_ffn = (ffn1_w_gate[0].astype(BF16), ffn1_w_up[0].astype(BF16), ffn1_w_down[0].astype(BF16))
    for l in range(nl):
        hst, w_ffn, (w_mla, w_ssd, w_ret, w_rwkv, w_o) = _ffn(
            hst, ffn1_norm[l][None, :], *w_ffn, nxt=(ffn2_w_gate, ffn2_w_up, ffn2_w_down, l),
            proj=(w_in, w_out, l))
        gmix = mix_norm[l][None, :]
        q, k, v, u_ssd = _mla_prep(hst, gmix, w_mla, mla_q_norm[l][None, :], wq, mla_kv_norm[l][None, :],
                                   wkv, gq[l], gk[l], cos, sinm, w_ssd, l)
        y_mla = _attention(q, k, v)
        u_ret = _inproj(hst, gmix, w_ret)
        u_rwkv = _inproj(hst, gmix, w_rwkv)
        y_ssd, y_ret, y_rwkv = _mixers(
            u_ssd, (ssd_conv_w[l], ssd_conv_b[l][None, :], dtb[l], alog[l], dsk[l], ssd_norm[l][None, :], tri, e8),
            u_ret, (cos, sin, ret_norm[l].reshape(1, RET_WIDTH)),
            u_rwkv, (rwkv_mu[l][None, :], rwkv_w0[l][None, :], rwkv_a0[l][None, :], rwkv_k_k[l][None, :],
                     rwkv_k_a[l][None, :], rwkv_r_k[l].reshape(1, RWKV_WIDTH), rwkv_ln[l].reshape(1, RWKV_WIDTH),
                     lora[l], tri, ones_bd))
        hst = _outproj(hst, (y_mla, y_ssd, y_ret, y_rwkv), w_o.reshape(N_MIXERS, MIXER_WIDTH, D_MODEL))
        nxt = (ffn1_w_gate, ffn1_w_up, ffn1_w_down, l + 1) if l + 1 < nl else None
        hst, w_ffn, _ = _ffn(hst, ffn2_norm[l][None, :], *w_ffn, nxt=nxt)
    return hst[CHUNK:][None]
```

```python
import functools
import itertools
import math

import jax
import jax.numpy as jnp
from jax import lax
from jax.experimental import pallas as pl
from jax.experimental.pallas import tpu as pltpu

F32 = jnp.float32
BF16 = jnp.bfloat16

D_MODEL = 2048
DEPTH = 4
N_META = 16
CHUNK = 128
PAD = CHUNK - N_META
D_FF = 5632
EPS = 1e-6
ROPE_THETA = 10000.0
NEG_INF = -1e30

MLA_HEADS = 4
MLA_NOPE = 128
MLA_ROPE = 64
MLA_QK = MLA_NOPE + MLA_ROPE
MLA_V = 128
MLA_Q_LORA = 384
MLA_KV_LORA = 128
MLA_QK_PAD = 256
MLA_V_AUG = MLA_V + 16
MLA_IN = MLA_Q_LORA + MLA_KV_LORA + MLA_ROPE
MLA_IN_PAD = 640

SSD_HEADS = 8
SSD_HEAD_DIM = 64
SSD_WIDTH = 512
SSD_GROUPS = 2
SSD_STATE = 128
SSD_CONV = 4
SSD_CONV_CH = SSD_WIDTH + 2 * SSD_GROUPS * SSD_STATE
SSD_IN = SSD_WIDTH + SSD_CONV_CH + SSD_HEADS
SSD_IN_PAD = SSD_WIDTH + SSD_CONV_CH + 128

RET_HEADS = 4
RET_DK = 64
RET_DV = 128
RET_WIDTH = 512
RET_IN = 2 * RET_HEADS * RET_DK + 2 * RET_WIDTH

RWKV_HEADS = 8
RWKV_HEAD_DIM = 64
RWKV_WIDTH = 512
RWKV_DECAY_LORA = 32
RWKV_A_LORA = 32
RWKV_GATE_LORA = 64
RWKV_LN_EPS = 64e-5
RWKV_IN = 3 * RWKV_WIDTH + RWKV_DECAY_LORA + RWKV_A_LORA + RWKV_GATE_LORA
RWKV_SUB = 64
RWKV_INV_BLOCK = 8

MIX_WIDTH = 2048
N_MIXERS = 4
MIXER_WIDTH = MIX_WIDTH // N_MIXERS

V7X_VMEM_LIMIT_BYTES = 56 * 1024 * 1024
TOKEN_TILE = 640
FF_TILE = 512
FFN_CONVERT_ROWS = 256
PROJ_CONVERT_ROWS = 16


def _cparams(*sem):
    return pltpu.CompilerParams(dimension_semantics=sem, vmem_limit_bytes=V7X_VMEM_LIMIT_BYTES)


def _sigmoid(x):
    return 1.0 / (1.0 + jnp.exp(-x))


def _silu(x):
    return x * _sigmoid(x)


def _softplus(x):
    return jnp.maximum(x, 0.0) + jnp.log(1.0 + jnp.exp(-jnp.abs(x)))


def _rms(x, g, eps=EPS):
    return x * lax.rsqrt(jnp.mean(x * x, axis=-1, keepdims=True) + eps) * g


def _mm(a, b):
    return jnp.dot(a.astype(BF16), b.astype(BF16), preferred_element_type=F32)


def _mm_nt(a, b):
    return lax.dot_general(a.astype(BF16), b.astype(BF16), (((1,), (1,)), ((), ())),
                           preferred_element_type=F32)


def _split3(x):
    x1 = x.astype(BF16)
    r1 = x - x1.astype(F32)
    x2 = r1.astype(BF16)
    x3 = (r1 - x2.astype(F32)).astype(BF16)
    return x1, x2, x3


def _dot01_right(x, m01, pieces=3):
    out = None
    for p in _split3(x)[:pieces]:
        d = jnp.dot(p, m01, preferred_element_type=F32)
        out = d if out is None else out + d
    return out


def _dot01_left(m01, x):
    p1, p2, p3 = _split3(x)
    return (jnp.dot(m01, p1, preferred_element_type=F32)
            + jnp.dot(m01, p2, preferred_element_type=F32)
            + jnp.dot(m01, p3, preferred_element_type=F32))


def _row_ids(rows, cols, base):
    return base + lax.broadcasted_iota(jnp.int32, (rows, cols), 0)


def _rope_table_kernel(inv_ref, cos_ref, sin_ref, sinm_ref):
    i = pl.program_id(0)
    pos = (_row_ids(CHUNK, 128, i * CHUNK) - PAD).astype(F32)
    lane = lax.broadcasted_iota(jnp.int32, (CHUNK, 128), 1)
    ang = pos * inv_ref[...]
    c = jnp.cos(ang)
    s = jnp.sin(ang)
    s = jnp.where((lane % 64) < 32, -s, s)
    cos_ref[...] = c
    sin_ref[...] = s
    sinm_ref[...] = jnp.where(lane < 64, s, 0.0)


def _rope_tables(lp):
    half = 32
    inv = ROPE_THETA ** (-jnp.arange(half, dtype=F32) / half)
    inv = jnp.tile(inv, 4)[None, :]
    out = jax.ShapeDtypeStruct((lp, 128), F32)
    spec = pl.BlockSpec((CHUNK, 128), lambda i: (i, 0))
    return pl.pallas_call(
        _rope_table_kernel,
        grid=(lp // CHUNK,),
        in_specs=[pl.BlockSpec((1, 128), lambda i: (0, 0))],
        out_specs=[spec, spec, spec],
        out_shape=[out, out, out],
        compiler_params=_cparams("parallel"),
        name="rope_tables",
    )(inv)


def _rope_lanes(x, cos, sin_signed):
    lane = lax.broadcasted_iota(jnp.int32, x.shape, 1)
    fwd = pltpu.roll(x, 32, axis=1)
    bwd = pltpu.roll(x, 96, axis=1)
    rot = jnp.where((lane % 64) < 32, bwd, fwd)
    return x * cos + rot * sin_signed


def _ffn_kernel(*refs, convert_next, convert_proj):
    refs = list(refs)
    x_ref, g_ref, wg_ref, wu_ref, wd_ref = refs[:5]
    n_in = 5 + (3 if convert_next else 0) + (2 if convert_proj else 0)
    ins, outs, xn_ref = refs[5:n_in], refs[n_in:-1], refs[-1]
    o_ref = outs.pop(0)
    if convert_next:
        for src, dst in zip(ins[:3], outs[:3]):
            dst[...] = src[...].astype(BF16)
        ins, outs = ins[3:], outs[3:]
    if convert_proj:
        win_ref, wout_ref = ins
        wmla_ref, wssd_ref, wret_ref, wrwkv_ref, wo_ref = outs
        o0 = MLA_IN
        o1 = o0 + SSD_IN
        o2 = o1 + RET_IN
        rows = win_ref.shape[0]
        wmla_ref[:, :MLA_IN] = win_ref[:, :o0].astype(BF16)
        wmla_ref[:, MLA_IN:] = jnp.zeros((rows, MLA_IN_PAD - MLA_IN), BF16)
        wssd_ref[:, :SSD_IN] = win_ref[:, o0:o1].astype(BF16)
        wssd_ref[:, SSD_IN:] = jnp.zeros((rows, SSD_IN_PAD - SSD_IN), BF16)
        wret_ref[...] = win_ref[:, o1:o2].astype(BF16)
        wrwkv_ref[...] = win_ref[:, o2:].astype(BF16)
        wo_ref[...] = wout_ref[...].astype(BF16)
    j = pl.program_id(1)

    @pl.when(j == 0)
    def _():
        xn_ref[...] = _rms(x_ref[...], g_ref[...]).astype(BF16)
        o_ref[...] = jnp.zeros_like(o_ref)

    xn = xn_ref[...]
    a = jnp.dot(xn, wg_ref[...], preferred_element_type=F32)
    b = jnp.dot(xn, wu_ref[...], preferred_element_type=F32)
    mid = (_silu(a) * b).astype(BF16)
    o_ref[...] += jnp.dot(mid, wd_ref[...], preferred_element_type=F32)

    @pl.when(j == pl.num_programs(1) - 1)
    def _():
        o_ref[...] = x_ref[...] + 0.5 * o_ref[...]


def _ffn(h, g, wgb, wub, wdb, nxt=None, proj=None):
    lp = h.shape[0]
    tm, tf = TOKEN_TILE, FF_TILE
    n_j = D_FF // tf
    n_c = D_MODEL // FFN_CONVERT_ROWS if nxt is not None else 0
    in_specs = [
        pl.BlockSpec((tm, D_MODEL), lambda i, j: (i, 0)),
        pl.BlockSpec((1, D_MODEL), lambda i, j: (0, 0)),
        pl.BlockSpec((D_MODEL, tf), lambda i, j: (0, j)),
        pl.BlockSpec((D_MODEL, tf), lambda i, j: (0, j)),
        pl.BlockSpec((tf, D_MODEL), lambda i, j: (j, 0)),
    ]
    out_specs = [pl.BlockSpec((tm, D_MODEL), lambda i, j: (i, 0))]
    out_shape = [jax.ShapeDtypeStruct((lp, D_MODEL), F32)]
    args = [h, g, wgb, wub, wdb]
    if nxt is not None:
        wg, wu, wd, l = nxt
        cr = FFN_CONVERT_ROWS
        assert lp // tm >= n_c
        ci = lambda i: jnp.minimum(i, n_c - 1)
        cj = lambda i, j: jnp.where(i < n_c, j, n_j - 1)
        in_specs += [
            pl.BlockSpec((None, cr, tf), lambda i, j: (l, ci(i), cj(i, j))),
            pl.BlockSpec((None, cr, tf), lambda i, j: (l, ci(i), cj(i, j))),
            pl.BlockSpec((None, tf, cr), lambda i, j: (l, cj(i, j), ci(i))),
        ]
        out_specs += [
            pl.BlockSpec((cr, tf), lambda i, j: (ci(i), cj(i, j))),
            pl.BlockSpec((cr, tf), lambda i, j: (ci(i), cj(i, j))),
            pl.BlockSpec((tf, cr), lambda i, j: (cj(i, j), ci(i))),
        ]
        out_shape += [
            jax.ShapeDtypeStruct((D_MODEL, D_FF), BF16),
            jax.ShapeDtypeStruct((D_MODEL, D_FF), BF16),
            jax.ShapeDtypeStruct((D_FF, D_MODEL), BF16),
        ]
        args += [wg, wu, wd]
    if proj is not None:
        w_in, w_out, lpj = proj
        pr = PROJ_CONVERT_ROWS
        n_p = D_MODEL // pr
        assert (lp // tm) * n_j >= n_p
        pidx = lambda i, j: jnp.minimum(i * n_j + j, n_p - 1)
        in_specs += [
            pl.BlockSpec((None, pr, w_in.shape[2]), lambda i, j: (lpj, pidx(i, j), 0)),
            pl.BlockSpec((None, pr, D_MODEL), lambda i, j: (lpj, pidx(i, j), 0)),
        ]
        widths = (MLA_IN_PAD, SSD_IN_PAD, RET_IN, RWKV_IN, D_MODEL)
        out_specs += [pl.BlockSpec((pr, n), lambda i, j: (pidx(i, j), 0)) for n in widths]
        out_shape += [jax.ShapeDtypeStruct((D_MODEL, n), BF16) for n in widths]
        args += [w_in, w_out]
    outs = pl.pallas_call(
        functools.partial(_ffn_kernel, convert_next=nxt is not None, convert_proj=proj is not None),
        grid=(lp // tm, n_j),
        in_specs=in_specs,
        out_specs=out_specs,
        out_shape=out_shape,
        scratch_shapes=[pltpu.VMEM((tm, D_MODEL), BF16)],
        compiler_params=_cparams("arbitrary", "arbitrary"),
        name="ffn",
    )(*args)
    n_next = 3 if nxt is not None else 0
    return (outs[0], tuple(outs[1:1 + n_next]) if nxt is not None else None,
            tuple(outs[1 + n_next:]) if proj is not None else None)


def _inproj_kernel(x_ref, g_ref, wa_ref, wb_ref, oa_ref, ob_ref):
    i = pl.program_id(0)
    tm = x_ref.shape[0]
    xn = _rms(x_ref[...], g_ref[...]).astype(BF16)
    row = _row_ids(tm, 1, i * tm)
    for w_ref, o_ref in ((wa_ref, oa_ref), (wb_ref, ob_ref)):
        o_ref[...] = jnp.where(row >= PAD, jnp.dot(xn, w_ref[...], preferred_element_type=F32), 0.0)


def _inproj(h, g, wa, wb):
    lp = h.shape[0]
    na, nb = wa.shape[1], wb.shape[1]
    tm = TOKEN_TILE // 2
    return pl.pallas_call(
        _inproj_kernel,
        grid=(lp // tm,),
        in_specs=[
            pl.BlockSpec((tm, D_MODEL), lambda i: (i, 0)),
            pl.BlockSpec((1, D_MODEL), lambda i: (0, 0)),
            pl.BlockSpec((D_MODEL, na), lambda i: (0, 0)),
            pl.BlockSpec((D_MODEL, nb), lambda i: (0, 0)),
        ],
        out_specs=[pl.BlockSpec((tm, na), lambda i: (i, 0)), pl.BlockSpec((tm, nb), lambda i: (i, 0))],
        out_shape=[jax.ShapeDtypeStruct((lp, na), F32), jax.ShapeDtypeStruct((lp, nb), F32)],
        compiler_params=_cparams("parallel"),
        name="inproj",
    )(h, g, wa, wb)


def _outproj_kernel(h_ref, y0_ref, y1_ref, y2_ref, y3_ref, w_ref, o_ref):
    acc = h_ref[...]
    for m, y_ref in enumerate((y0_ref, y1_ref, y2_ref, y3_ref)):
        acc = acc + jnp.dot(y_ref[...], w_ref[m], preferred_element_type=F32)
    o_ref[...] = acc


def _outproj(h, ys, w):
    lp = h.shape[0]
    tm = TOKEN_TILE
    yspec = pl.BlockSpec((tm, MIXER_WIDTH), lambda i: (i, 0))
    return pl.pallas_call(
        _outproj_kernel,
        grid=(lp // tm,),
        in_specs=[pl.BlockSpec((tm, D_MODEL), lambda i: (i, 0)), yspec, yspec, yspec, yspec,
                  pl.BlockSpec((N_MIXERS, MIXER_WIDTH, D_MODEL), lambda i: (0, 0, 0))],
        out_specs=pl.BlockSpec((tm, D_MODEL), lambda i: (i, 0)),
        out_shape=jax.ShapeDtypeStruct((lp, D_MODEL), F32),
        compiler_params=_cparams("parallel"),
        name="outproj",
    )(h, *ys, w)


def _mla_prep_kernel(x_ref, g_ref, win_ref, qn_ref, wq_ref, kvn_ref, wkv_ref, gq_ref, gk_ref,
                     cos_ref, sinm_ref, wssd_ref, q_ref, k_ref, v_ref, ussd_ref):
    i = pl.program_id(0)
    tm = x_ref.shape[0]
    xn = _rms(x_ref[...], g_ref[...]).astype(BF16)
    u = jnp.dot(xn, win_ref[...], preferred_element_type=F32)
    row = _row_ids(tm, 1, i * tm)
    u = jnp.where(row >= PAD, u, 0.0)
    cq = u[:, :MLA_Q_LORA]
    ckv = u[:, MLA_Q_LORA:MLA_Q_LORA + MLA_KV_LORA]
    kpe = u[:, MLA_Q_LORA + MLA_KV_LORA:]
    q_all = _mm(_rms(cq, qn_ref[...]), wq_ref[...])
    kv_all = _mm(_rms(ckv, kvn_ref[...]), wkv_ref[...])
    ussd_ref[...] = jnp.where(row >= PAD, jnp.dot(xn, wssd_ref[...], preferred_element_type=F32), 0.0)
    cos = cos_ref[...]
    sinm = sinm_ref[...]
    gq = gq_ref[...]
    gk = gk_ref[...]
    scale = MLA_QK ** -0.5
    for h in range(MLA_HEADS):
        qh = q_all[:, h * MLA_QK_PAD:(h + 1) * MLA_QK_PAD]
        ss = jnp.sum(qh * qh, axis=-1, keepdims=True) * (1.0 / MLA_QK)
        qh = qh * lax.rsqrt(ss + EPS) * gq
        q_rot = _rope_lanes(qh[:, MLA_NOPE:], cos, sinm)
        q_ref[h, 0, :MLA_NOPE, :] = (qh[:, :MLA_NOPE] * scale).T.astype(BF16)
        q_ref[h, 0, MLA_NOPE:, :] = (q_rot * scale).T.astype(BF16)

        kn = kv_all[:, h * MLA_NOPE:(h + 1) * MLA_NOPE]
        ss = (jnp.sum(kn * kn, axis=-1, keepdims=True)
              + jnp.sum(kpe * kpe, axis=-1, keepdims=True)) * (1.0 / MLA_QK)
        rs = lax.rsqrt(ss + EPS)
        k_rot = _rope_lanes(kpe * rs * gk[:, MLA_NOPE:], cos, sinm)
        k_ref[h, :, :MLA_NOPE] = (kn * rs * gk[:, :MLA_NOPE]).astype(BF16)
        k_ref[h, :, MLA_NOPE:] = k_rot.astype(BF16)
        v_ref[h, 0, :MLA_V, :] = kv_all[:, 4 * MLA_NOPE + h * MLA_V: 4 * MLA_NOPE + (h + 1) * MLA_V].T.astype(BF16)
        tail = lax.broadcasted_iota(jnp.int32, (MLA_V_AUG - MLA_V, tm), 0)
        v_ref[h, 0, MLA_V:, :] = jnp.where(tail == 0, 1.0, 0.0).astype(BF16)


def _mla_prep(h, g, win, qn, wq, kvn, wkv, gq, gk, cos, sinm, wssd, l):
    lp = h.shape[0]
    tm = TOKEN_TILE
    full = lambda shape: pl.BlockSpec(shape, lambda i: (0,) * len(shape))
    layer = lambda shape: pl.BlockSpec((None,) + shape, lambda i: (l,) + (0,) * len(shape))
    return pl.pallas_call(
        _mla_prep_kernel,
        grid=(lp // tm,),
        in_specs=[
            pl.BlockSpec((tm, D_MODEL), lambda i: (i, 0)),
            full((1, D_MODEL)), full((D_MODEL, MLA_IN_PAD)),
            full((1, MLA_Q_LORA)), layer((MLA_Q_LORA, MLA_HEADS * MLA_QK_PAD)),
            full((1, MLA_KV_LORA)), layer((MLA_KV_LORA, 2 * MLA_HEADS * MLA_NOPE)),
            full((1, MLA_QK_PAD)), full((1, MLA_QK_PAD)),
            pl.BlockSpec((tm, 128), lambda i: (i, 0)),
            pl.BlockSpec((tm, 128), lambda i: (i, 0)),
            full((D_MODEL, SSD_IN_PAD)),
        ],
        out_specs=[
            pl.BlockSpec((MLA_HEADS, 1, MLA_QK_PAD, tm), lambda i: (0, i, 0, 0)),
            pl.BlockSpec((MLA_HEADS, tm, MLA_QK_PAD), lambda i: (0, i, 0)),
            pl.BlockSpec((MLA_HEADS, 1, MLA_V_AUG, tm), lambda i: (0, i, 0, 0)),
            pl.BlockSpec((tm, SSD_IN_PAD), lambda i: (i, 0)),
        ],
        out_shape=[
            jax.ShapeDtypeStruct((MLA_HEADS, lp // tm, MLA_QK_PAD, tm), BF16),
            jax.ShapeDtypeStruct((MLA_HEADS, lp, MLA_QK_PAD), BF16),
            jax.ShapeDtypeStruct((MLA_HEADS, lp // tm, MLA_V_AUG, tm), BF16),
            jax.ShapeDtypeStruct((lp, SSD_IN_PAD), F32),
        ],
        compiler_params=_cparams("parallel"),
        name="mla_prep",
    )(h, g, win, qn, wq, kvn, wkv, gq, gk, cos, sinm, wssd)


def _attn_kernel(q_ref, k_ref, vt_ref, o_ref):
    i = pl.program_id(1)
    tq = q_ref.shape[3]
    qt = q_ref[0, 0]
    key0 = lax.broadcasted_iota(jnp.int32, (tq, tq), 0)
    qry = i * tq + lax.broadcasted_iota(jnp.int32, (tq, tq), 1)

    def scores(j, masked):
        start = pl.multiple_of(j * tq, tq)
        k = k_ref[0, pl.ds(start, tq), :]
        s = jnp.dot(k, qt, preferred_element_type=F32)
        if masked:
            key = key0 + j * tq
            s = jnp.where(key <= qry, jnp.where(key >= PAD, s, NEG_INF), NEG_INF)
        return s

    def update(j, s, carry):
        m, acc = carry
        m_new = jnp.maximum(m, jnp.max(s, axis=0, keepdims=True))
        alpha = jnp.exp(m - m_new)
        p = jnp.exp((s - m_new).astype(BF16))
        acc = alpha * acc + jnp.dot(vt_ref[0, j], p, preferred_element_type=F32)
        return m_new, acc

    init = (jnp.full((1, tq), NEG_INF, F32), jnp.zeros((MLA_V_AUG, tq), F32))
    s_first = scores(0, True)

    def body(j, c):
        s_prev, carry = c
        s_new = scores(j, False)
        return s_new, update(j - 1, s_prev, carry)

    s_last, carry = lax.fori_loop(1, i, body, (s_first, init))
    j_last = jnp.maximum(i - 1, 0)
    m, acc = lax.cond(i > 0,
                      lambda c: update(i, scores(i, True), update(j_last, s_last, c)),
                      lambda c: update(j_last, s_last, c), carry)
    o_ref[...] = (acc[:MLA_V] / acc[MLA_V:MLA_V + 1]).T.astype(BF16)


def _attention(q, k, v):
    lp = k.shape[1]
    tq = TOKEN_TILE
    return pl.pallas_call(
        _attn_kernel,
        grid=(MLA_HEADS, lp // tq),
        in_specs=[
            pl.BlockSpec((1, 1, MLA_QK_PAD, tq), lambda h, i: (h, i, 0, 0)),
            pl.BlockSpec((1, lp, MLA_QK_PAD), lambda h, i: (h, 0, 0)),
            pl.BlockSpec((1, lp // tq, MLA_V_AUG, tq), lambda h, i: (h, 0, 0, 0)),
        ],
        out_specs=pl.BlockSpec((tq, MLA_V), lambda h, i: (i, h)),
        out_shape=jax.ShapeDtypeStruct((lp, MLA_HEADS * MLA_V), BF16),
        compiler_params=_cparams("parallel", "arbitrary"),
        name="mla_attention",
    )(q, k, v)


def _shift_rows(x, carry, s):
    rolled = pltpu.roll(x, s, axis=0)
    head = pltpu.roll(carry, s, axis=0)
    r8 = lax.broadcasted_iota(jnp.int32, (8, 1), 0)
    return jnp.concatenate([jnp.where(r8 < s, head, rolled[:8]), rolled[8:]], axis=0)


def _ssd_init(carry_ref, state_ref):
    carry_ref[...] = jnp.zeros_like(carry_ref)
    state_ref[...] = jnp.zeros_like(state_ref)


def _ssd_body(u_ref, cw_ref, cb_ref, dtb_ref, alog_ref, dsk_ref, ng_ref, tri_ref, e8_ref,
              y_ref, carry_ref, state_ref):
    i = pl.program_id(0)
    z = u_ref[:, :SSD_WIDTH]
    xbc = u_ref[:, SSD_WIDTH:SSD_WIDTH + SSD_CONV_CH]
    dt_raw = u_ref[:, SSD_WIDTH + SSD_CONV_CH:]
    carry = carry_ref[...]
    cw = cw_ref[...]
    conv = xbc * cw[3:4, :]
    for s in (1, 2, 3):
        conv = conv + _shift_rows(xbc, carry, s) * cw[3 - s:4 - s, :]
    carry_ref[...] = xbc[CHUNK - 8:, :]
    xbc = _silu(conv + cb_ref[...])
    xs = xbc[:, :SSD_WIDTH]
    bm = xbc[:, SSD_WIDTH:SSD_WIDTH + SSD_GROUPS * SSD_STATE]
    cm = xbc[:, SSD_WIDTH + SSD_GROUPS * SSD_STATE:]

    row = _row_ids(CHUNK, 1, i * CHUNK)
    dt = _softplus(dt_raw + dtb_ref[...]) * jnp.where(row >= PAD, 1.0, 0.0)
    la = dt * (-jnp.exp(alog_ref[...]))
    tri = tri_ref[...]
    e8 = e8_ref[...]
    cs = _dot01_left(tri, la)
    cs_e = _dot01_right(cs, e8, pieces=2)
    dt_e = _dot01_right(dt, e8, pieces=2)
    cs_t = cs.T
    cs_last_e = cs_e[CHUNK - 1:CHUNK, :]
    x = xs * dt_e
    xd = x * jnp.exp(cs_last_e - cs_e)
    ecs = jnp.exp(cs_e)
    dec = jnp.exp(cs_last_e)

    r_i = lax.broadcasted_iota(jnp.int32, (CHUNK, CHUNK), 0)
    c_i = lax.broadcasted_iota(jnp.int32, (CHUNK, CHUNK), 1)
    causal = r_i >= c_i
    per = SSD_HEADS // SSD_GROUPS
    gw = per * SSD_HEAD_DIM
    groups = range(SSD_GROUPS)
    b_gs = [bm[:, g * SSD_STATE:(g + 1) * SSD_STATE] for g in groups]
    c_gs = [cm[:, g * SSD_STATE:(g + 1) * SSD_STATE] for g in groups]
    s_prev = [state_ref[g] for g in groups]
    yield
    scores = [_mm_nt(c_gs[g], b_gs[g]) for g in groups]
    y_off = [_mm(c_gs[g], s_prev[g]) for g in groups]
    s_add = [_mm(b_gs[g].T, xd[:, g * gw:(g + 1) * gw]) for g in groups]
    yield
    lmats = []
    for h in range(SSD_HEADS):
        seg = cs[:, h:h + 1] - cs_t[h:h + 1, :]
        lmats.append(jnp.where(causal, jnp.exp(jnp.where(causal, seg, 0.0)), 0.0))
    y_diag = [_mm(scores[h // per] * lmats[h], x[:, h * SSD_HEAD_DIM:(h + 1) * SSD_HEAD_DIM])
              for h in range(SSD_HEADS)]
    yield
    for g in groups:
        state_ref[g] = s_prev[g] * dec[:, g * gw:(g + 1) * gw] + s_add[g]
    y = (jnp.concatenate(y_diag, axis=1) + jnp.concatenate(y_off, axis=1) * ecs) + dsk_ref[...] * xs
    y = y * _silu(z)
    y_ref[...] = _rms(y, ng_ref[...]).astype(BF16)


RET_LOG_G = [math.log(1.0 - 2.0 ** (-5.0 - h)) for h in range(RET_HEADS)]


def _ret_init(dmat_ref, state_ref):
    state_ref[...] = jnp.zeros_like(state_ref)
    r_i = lax.broadcasted_iota(jnp.int32, (CHUNK, CHUNK), 0)
    c_i = lax.broadcasted_iota(jnp.int32, (CHUNK, CHUNK), 1)
    diff = (r_i - c_i).astype(F32)
    for h in range(RET_HEADS):
        dmat_ref[h] = jnp.where(r_i >= c_i, jnp.exp(jnp.where(r_i >= c_i, diff, 0.0) * RET_LOG_G[h]), 0.0)


def _ret_body(u_ref, cos_ref, sin_ref, ng_ref, y_ref, dmat_ref, state_ref):
    log_g = RET_LOG_G
    qkw = RET_HEADS * RET_DK
    cos = cos_ref[...]
    sin = sin_ref[...]
    q = jnp.concatenate([_rope_lanes(u_ref[:, c * 128:(c + 1) * 128], cos, sin) for c in range(2)], axis=1)
    k = jnp.concatenate([_rope_lanes(u_ref[:, qkw + c * 128:qkw + (c + 1) * 128], cos, sin)
                         for c in range(2)], axis=1) * (RET_DK ** -0.5)
    idx = lax.broadcasted_iota(jnp.int32, (CHUNK, 1), 0).astype(F32)
    k_t = k.T
    idx_row = lax.broadcasted_iota(jnp.int32, (1, CHUNK), 1).astype(F32)
    heads = range(RET_HEADS)
    q_hs = [q[:, h * RET_DK:(h + 1) * RET_DK] for h in heads]
    k_hs = [k[:, h * RET_DK:(h + 1) * RET_DK] for h in heads]
    v_hs = [u_ref[:, 2 * qkw + h * RET_DV: 2 * qkw + (h + 1) * RET_DV].astype(BF16) for h in heads]
    s_prevs = [state_ref[h] for h in heads]
    yield
    scs = [_mm_nt(q_hs[h], k_hs[h]) for h in heads]
    y_cross = [_mm(q_hs[h] * jnp.exp((idx + 1.0) * log_g[h]), s_prevs[h]) for h in heads]
    yield
    s_adds = [_mm(k_t[h * RET_DK:(h + 1) * RET_DK, :] * jnp.exp((CHUNK - 1 - idx_row) * log_g[h]), v_hs[h])
              for h in heads]
    y_in = [_mm(scs[h] * dmat_ref[h], v_hs[h]) for h in heads]
    yield
    for h in heads:
        g_h = u_ref[:, 2 * qkw + RET_WIDTH + h * RET_DV: 2 * qkw + RET_WIDTH + (h + 1) * RET_DV]
        y = y_in[h] + y_cross[h]
        state_ref[h] = s_prevs[h] * math.exp(CHUNK * log_g[h]) + s_adds[h]
        mu = jnp.mean(y, axis=-1, keepdims=True)
        var = jnp.mean(jnp.square(y - mu), axis=-1, keepdims=True)
        yn = (y - mu) * lax.rsqrt(var + EPS) * ng_ref[:, h * RET_DV:(h + 1) * RET_DV]
        y_ref[:, h * RET_DV:(h + 1) * RET_DV] = (_silu(g_h) * yn).astype(BF16)


def _inv_unit_upper_many(mats, in_block, eye):
    d = [jnp.where(in_block, a, 0.0) for a in mats]
    f = [a - x for a, x in zip(mats, d)]
    d2 = [_mm(x, x) for x in d]
    yield
    d4 = [_mm(x, x) for x in d2]
    p = [_mm(eye + x, eye + y) for x, y in zip(d, d2)]
    yield
    td = [_mm(x, eye + y) for x, y in zip(p, d4)]
    yield
    g = [_mm(x, y) for x, y in zip(f, td)]
    yield
    g2 = [_mm(x, x) for x in g]
    tg = [_mm(x, eye + y) for x, y in zip(td, g)]
    yield
    g4 = [_mm(x, x) for x in g2]
    tg = [_mm(x, eye + y) for x, y in zip(tg, g2)]
    yield
    return [_mm(x, eye + y) for x, y in zip(tg, g4)]


def _rwkv_init(prev_ref, state_ref):
    prev_ref[...] = jnp.zeros_like(prev_ref)
    state_ref[...] = jnp.zeros_like(state_ref)


def _rwkv_body(u_ref, mu_ref, w0_ref, a0_ref, kk_ref, ka_ref, rk_ref, ln_ref, lora_ref, tri_ref,
               ones_ref, y_ref, prev_ref, state_ref, ot_ref):
    u = u_ref[...]
    rows = lax.broadcasted_iota(jnp.int32, (CHUNK, 1), 0)
    u_prev = jnp.where(rows == 0, prev_ref[7:8, :], pltpu.roll(u, 1, axis=0))
    prev_ref[...] = u[CHUNK - 8:, :]
    us = u + (u_prev - u) * mu_ref[...]
    w3 = RWKV_WIDTH
    r = us[:, :w3]
    k = us[:, w3:2 * w3]
    v = us[:, 2 * w3:3 * w3]
    lo = us[:, 3 * w3:]
    w = w0_ref[...] + _mm(jnp.tanh(lo), lora_ref[0])
    w = -_softplus(-w) - 0.5
    ld = -jnp.exp(w)
    a = _sigmoid(a0_ref[...] + _mm(lo, lora_ref[1]))
    g = _mm(_sigmoid(lo), lora_ref[2])
    ones = ones_ref[...]
    kk = k * kk_ref[...]
    k2 = k * (1.0 + (a - 1.0) * ka_ref[...])
    kk = kk / jnp.maximum(jnp.sqrt(_dot01_right(kk * kk, ones, pieces=2)), 1e-12)
    b = kk * a

    lc_full = _dot01_left(tri_ref[...], ld)
    v_t = v.T

    n = RWKV_SUB
    r_i = lax.broadcasted_iota(jnp.int32, (n, n), 0)
    c_i = lax.broadcasted_iota(jnp.int32, (n, n), 1)
    in_block = (r_i // RWKV_INV_BLOCK) == (c_i // RWKV_INV_BLOCK)
    eye = jnp.where(r_i == c_i, 1.0, 0.0)
    hd = RWKV_HEAD_DIM
    n_sub = CHUNK // n
    heads = range(RWKV_HEADS)
    hs = [slice(h * hd, (h + 1) * hd) for h in heads]

    sc = []
    for sub in range(n_sub):
        lo_r, hi_r = sub * n, (sub + 1) * n
        lc = lc_full[lo_r:hi_r, :]
        if sub > 0:
            lc = lc - lc_full[lo_r - 1:lo_r, :]
        lc_last = lc[n - 1:n, :]
        e_pos = jnp.exp(lc)
        e_neg = jnp.exp(-lc)
        e_prev = jnp.exp(lc - ld[lo_r:hi_r, :])
        e_end = jnp.exp(lc_last - lc)
        sc.append(dict(
            gam=jnp.exp(lc_last),
            a_t=(-kk[lo_r:hi_r, :] * e_prev).astype(BF16),
            r_t=(r[lo_r:hi_r, :] * e_pos).astype(BF16),
            b_t=(b[lo_r:hi_r, :] * e_neg).astype(BF16),
            k_t=(k2[lo_r:hi_r, :] * e_neg).astype(BF16),
            b_h=(b[lo_r:hi_r, :] * e_end).astype(BF16),
            k_h=(k2[lo_r:hi_r, :] * e_end).astype(BF16),
            v_t=v_t[:, lo_r:hi_r].astype(BF16),
        ))

    pairs = [(sub, h) for sub in range(n_sub) for h in heads]
    yield
    bk = [jnp.concatenate([sc[sub]["b_t"][:, hs[h]], sc[sub]["k_t"][:, hs[h]]], axis=0) for sub, h in pairs]
    ar2 = [jnp.concatenate([sc[sub]["a_t"][:, hs[h]], sc[sub]["r_t"][:, hs[h]]], axis=0) for sub, h in pairs]
    r2 = lax.broadcasted_iota(jnp.int32, (n, 2 * n), 0)
    c2 = lax.broadcasted_iota(jnp.int32, (n, 2 * n), 1)
    mask2 = r2 < jnp.where(c2 < n, c2, c2 - n + 1)
    quad = [_mm_nt(x, y) for x, y in zip(bk, ar2)]
    top = [jnp.where(mask2, x[:n, :], 0.0) for x in quad]
    bot = [jnp.where(mask2, x[n:, :], 0.0).astype(BF16) for x in quad]
    yield
    vprod = [_mm(sc[sub]["v_t"][hs[h], :], jnp.concatenate([y, sc[sub]["k_h"][:, hs[h]]], axis=1))
             for y, (sub, h) in zip(bot, pairs)]
    yield
    t_t = yield from _inv_unit_upper_many([x[:, :n] for x in top], in_block, eye)
    top = [x.astype(BF16) for x in top]

    state = [state_ref[h] for h in heads]
    for sub in range(n_sub):
        c = sc[sub]
        base = sub * RWKV_HEADS
        yield
        xr = [_mm_nt(state[h], ar2[base + h]) for h in heads]
        yield
        u_t = [_mm(xr[h][:, :n] + vprod[base + h][:, :n], t_t[base + h]) for h in heads]
        yield
        ub = [_mm(u_t[h], top[base + h]) for h in heads]
        su = [_mm(u_t[h], c["b_h"][:, hs[h]]) for h in heads]
        for h in heads:
            ot_ref[h * hd:(h + 1) * hd, sub * n:(sub + 1) * n] = (
                xr[h][:, n:] + ub[h][:, n:] + vprod[base + h][:, n:2 * n])
        state = [state[h] * c["gam"][:, hs[h]] + su[h] + vprod[base + h][:, 2 * n:] for h in heads]
    for h in heads:
        state_ref[h] = state[h]
    yield

    out = ot_ref[...].T
    inv_n = 1.0 / RWKV_HEAD_DIM
    mean = _dot01_right(out, ones, pieces=2) * inv_n
    cen = out - mean
    var = _dot01_right(cen * cen, ones, pieces=2) * inv_n
    out = cen * lax.rsqrt(var + RWKV_LN_EPS) * ln_ref[...]
    bonus = _dot01_right(r * k2 * rk_ref[...], ones, pieces=2) * v
    y_ref[...] = ((out + bonus) * g).astype(BF16)


N_SSD_IN, N_RET_IN, N_RWKV_IN = 9, 4, 11


def _mixers_kernel(*refs):
    n_in = N_SSD_IN + N_RET_IN + N_RWKV_IN
    ins, (y_ssd, y_ret, y_rwkv), scr = refs[:n_in], refs[n_in:n_in + 3], refs[n_in + 3:]
    ssd_in, ret_in, rwkv_in = ins[:N_SSD_IN], ins[N_SSD_IN:N_SSD_IN + N_RET_IN], ins[N_SSD_IN + N_RET_IN:]
    ssd_scr, ret_scr, rwkv_scr = scr[:2], scr[2:4], scr[4:]

    @pl.when(pl.program_id(0) == 0)
    def _():
        _ssd_init(*ssd_scr)
        _ret_init(*ret_scr)
        _rwkv_init(*rwkv_scr[:2])

    main = _rwkv_body(*rwkv_in, y_rwkv, *rwkv_scr)
    side = itertools.chain(_ssd_body(*ssd_in, y_ssd, *ssd_scr), _ret_body(*ret_in, y_ret, *ret_scr))
    for step, _ in enumerate(main):
        if step % 2 == 0:
            next(side, None)
    for _ in side:
        pass


def _mixers(u_ssd, ssd_params, u_ret, ret_params, u_rwkv, rwkv_params):
    lp = u_ssd.shape[0]
    full = lambda a: pl.BlockSpec(a.shape, lambda i: (0,) * a.ndim)
    rows = lambda a: pl.BlockSpec((CHUNK, a.shape[1]), lambda i: (i, 0))
    cos, sin, ng = ret_params
    args = [u_ssd, *ssd_params, u_ret, cos, sin, ng, u_rwkv, *rwkv_params]
    assert len(args) == N_SSD_IN + N_RET_IN + N_RWKV_IN
    in_specs = ([rows(u_ssd)] + [full(a) for a in ssd_params]
                + [rows(u_ret), rows(cos), rows(sin), full(ng)]
                + [rows(u_rwkv)] + [full(a) for a in rwkv_params])
    out = jax.ShapeDtypeStruct((lp, MIXER_WIDTH), BF16)
    ospec = pl.BlockSpec((CHUNK, MIXER_WIDTH), lambda i: (i, 0))
    return pl.pallas_call(
        _mixers_kernel,
        grid=(lp // CHUNK,),
        in_specs=in_specs,
        out_specs=[ospec, ospec, ospec],
        out_shape=[out, out, out],
        scratch_shapes=[pltpu.VMEM((8, SSD_CONV_CH), F32),
                        pltpu.VMEM((SSD_GROUPS, SSD_STATE, SSD_WIDTH // SSD_GROUPS), F32),
                        pltpu.VMEM((RET_HEADS, CHUNK, CHUNK), F32),
                        pltpu.VMEM((RET_HEADS, RET_DK, RET_DV), F32),
                        pltpu.VMEM((8, RWKV_IN), F32),
                        pltpu.VMEM((RWKV_HEADS, RWKV_HEAD_DIM, RWKV_HEAD_DIM), F32),
                        pltpu.VMEM((RWKV_WIDTH, CHUNK), F32)],
        compiler_params=_cparams("arbitrary"),
        name="mixers",
    )(*args)


def _constants():
    r = jnp.arange(CHUNK)
    tri = (r[:, None] >= r[None, :]).astype(BF16)
    lane = jnp.arange(RWKV_WIDTH)
    ones_bd = ((lane[:, None] // RWKV_HEAD_DIM) == (lane[None, :] // RWKV_HEAD_DIM)).astype(BF16)
    e8 = ((jnp.arange(128)[:, None] == (lane[None, :] // SSD_HEAD_DIM))).astype(BF16)
    return tri, ones_bd, e8


def kernel(x, meta_tokens, ffn1_norm, ffn1_w_gate, ffn1_w_up, ffn1_w_down, mix_norm, w_in, w_out, mla_q_norm, mla_w_q_up, mla_kv_norm, mla_w_kv_up, mla_qk_norm_q, mla_qk_norm_k, ssd_conv_w, ssd_conv_b, ssd_dt_bias, ssd_a_log, ssd_d, ssd_norm, ret_norm, rwkv_mu, rwkv_w0, rwkv_w2, rwkv_a0, rwkv_a2, rwkv_g2, rwkv_k_k, rwkv_k_a, rwkv_r_k, rwkv_ln, ffn2_norm, ffn2_w_gate, ffn2_w_up, ffn2_w_down):
    b, seq, d = x.shape
    assert b == 1 and d == D_MODEL and seq % CHUNK == 0
    nl = w_in.shape[0]
    lp = seq + CHUNK
    assert lp % TOKEN_TILE == 0

    hst = jnp.concatenate([jnp.zeros((PAD, d), x.dtype), meta_tokens.astype(x.dtype), x[0]], axis=0)

    wq =jnp.pad(mla_w_q_up.reshape(nl, MLA_Q_LORA, MLA_HEADS, MLA_QK),
                 ((0, 0), (0, 0), (0, 0), (0, MLA_QK_PAD - MLA_QK))
                 ).reshape(nl, MLA_Q_LORA, MLA_HEADS * MLA_QK_PAD).astype(BF16)
    wkv = mla_w_kv_up.reshape(nl, MLA_KV_LORA, MLA_HEADS, 2, MLA_NOPE).transpose(0, 1, 3, 2, 4
                              ).reshape(nl, MLA_KV_LORA, 2 * MLA_HEADS * MLA_NOPE).astype(BF16)
    gq = jnp.pad(mla_qk_norm_q, ((0, 0), (0, MLA_QK_PAD - MLA_QK)))[:, None, :]
    gk = jnp.pad(mla_qk_norm_k, ((0, 0), (0, MLA_QK_PAD - MLA_QK)))[:, None, :]
    dtb = jnp.pad(ssd_dt_bias, ((0, 0), (0, 128 - SSD_HEADS)))[:, None, :]
    alog = jnp.pad(ssd_a_log, ((0, 0), (0, 128 - SSD_HEADS)))[:, None, :]
    dsk = jnp.repeat(ssd_d, SSD_HEAD_DIM, axis=1)[:, None, :]
    lora = jnp.zeros((nl, 3, 128, RWKV_WIDTH), F32)
    lora = lora.at[:, 0, :RWKV_DECAY_LORA].set(rwkv_w2)
    lora = lora.at[:, 1, RWKV_DECAY_LORA:RWKV_DECAY_LORA + RWKV_A_LORA].set(rwkv_a2)
    lora = lora.at[:, 2, RWKV_DECAY_LORA + RWKV_A_LORA:].set(rwkv_g2)
    lora = lora.astype(BF16)

    tri, ones_bd, e8 = _constants()
    cos, sin, sinm = _rope_tables(lp)

    w_ffn = (ffn1_w_gate[0].astype(BF16), ffn1_w_up[0].astype(BF16), ffn1_w_down[0].astype(BF16))
    for l in range(nl):
        hst, w_ffn, (w_mla, w_ssd, w_ret, w_rwkv, w_o) = _ffn(
            hst, ffn1_norm[l][None, :], *w_ffn, nxt=(ffn2_w_gate, ffn2_w_up, ffn2_w_down, l),
            proj=(w_in, w_out, l))
        gmix = mix_norm[l][None, :]
        q, k, v, u_ssd = _mla_prep(hst, gmix, w_mla, mla_q_norm[l][None, :], wq, mla_kv_norm[l][None, :],
                                   wkv, gq[l], gk[l], cos, sinm, w_ssd, l)
        y_mla = _attention(q, k, v)
        u_ret, u_rwkv = _inproj(hst, gmix, w_ret, w_rwkv)
        y_ssd, y_ret, y_rwkv = _mixers(
            u_ssd, (ssd_conv_w[l], ssd_conv_b[l][None, :], dtb[l], alog[l], dsk[l], ssd_norm[l][None, :], tri, e8),
            u_ret, (cos, sin, ret_norm[l].reshape(1, RET_WIDTH)),
            u_rwkv, (rwkv_mu[l][None, :], rwkv_w0[l][None, :], rwkv_a0[l][None, :], rwkv_k_k[l][None, :],
                     rwkv_k_a[l][None, :], rwkv_r_k[l].reshape(1, RWKV_WIDTH), rwkv_ln[l].reshape(1, RWKV_WIDTH),
                     lora[l], tri, ones_bd))
        hst = _outproj(hst, (y_mla, y_ssd, y_ret, y_rwkv), w_o.reshape(N_MIXERS, MIXER_WIDTH, D_MODEL))
        nxt = (ffn1_w_gate, ffn1_w_up, ffn1_w_down, l + 1) if l + 1 < nl else None
        hst, w_ffn, _ = _ffn(hst, ffn2_norm[l][None, :], *w_ffn, nxt=nxt)
    return hst[CHUNK:][None]
```
